```python
import math
import jax, jax.numpy as jnp
from jax import lax
import numpy as np

D_MODEL = 1024
BATCH = 8
SEQ = 2048
DEPTH = 2
DEC_BATCH = 128
DEC_SEQ = 1
PAST_LEN = 16384
PAGE_SIZE = 128

S5_GROUP = 16
S5_STATE = 64
S5_WIDTH = D_MODEL
S5_GROUPS = S5_WIDTH // S5_GROUP
SSD_WIDTH = D_MODEL
SSD_HEAD_DIM = 64
SSD_HEADS = SSD_WIDTH // SSD_HEAD_DIM
SSD_GROUPS = 4
SSD_HEADS_PER_GROUP = SSD_HEADS // SSD_GROUPS
SSD_STATE = 128
SSD_CONV = 4
SSD_CONV_DIM = SSD_WIDTH + 2 * SSD_GROUPS * SSD_STATE
RET_HEADS = 4
RET_QK_DIM = 128
RET_V_DIM = 256
RET_QK_WIDTH = RET_HEADS * RET_QK_DIM
RET_WIDTH = RET_HEADS * RET_V_DIM
ROPE_BASE = 10000.0
N_BRANCH = 3
D_FF = 4 * D_MODEL
CHUNK = 128
EPS = 1e-6
IN_SIZES = (S5_WIDTH, SSD_WIDTH, SSD_CONV_DIM, SSD_HEADS, RET_QK_WIDTH, RET_QK_WIDTH, RET_WIDTH, RET_WIDTH, N_BRANCH * D_MODEL)
N_IN = sum(IN_SIZES)

kernel_name = "hybrid_s5_ssd_retention_gated_decoder_step"

F32 = jnp.float32


def rmsnorm(x, w):
    xf = x.astype(F32)
    y = xf * lax.rsqrt(jnp.mean(xf * xf, axis=-1, keepdims=True) + EPS)
    return (y * w.astype(F32)).astype(x.dtype)


def _linear_combine(e_prev, e_next):
    a1, b1 = e_prev
    a2, b2 = e_next
    return a2 * a1, a2 * b1 + b2


def s5_mixer(u, a_re, a_im, log_dt, b_re, b_im, c_re, c_im, d, h0):
    bsz, L, _ = u.shape
    uf = u.astype(F32).reshape(bsz, L, S5_GROUPS, S5_GROUP)
    lam = lax.complex(a_re.astype(F32), a_im.astype(F32))
    lam_dt = lam * jnp.exp(log_dt.astype(F32))[:, None]
    a_bar = jnp.exp(lam_dt)
    b_bar = ((a_bar - 1.0) / lam)[..., None] * lax.complex(b_re.astype(F32), b_im.astype(F32))
    bu = jnp.einsum('blgc,gpc->blgp', uf.astype(jnp.complex64), b_bar)
    a_seq = jnp.broadcast_to(a_bar, bu.shape)
    _, h = lax.associative_scan(_linear_combine, (a_seq, bu), axis=1)
    if h0 is not None:
        steps = jnp.arange(1, L + 1, dtype=F32)[:, None, None]
        h = h + jnp.exp(steps * lam_dt)[None] * h0[:, None]
    c = lax.complex(c_re.astype(F32), c_im.astype(F32))
    y = jnp.real(jnp.einsum('gcp,blgp->blgc', c, h)) + d.astype(F32).reshape(S5_GROUPS, S5_GROUP) * uf
    return y.reshape(bsz, L, S5_WIDTH), h[:, -1]


def chunked_decay_scan(q, k, v, log_a, s0):
    q, k, v, log_a = q.astype(F32), k.astype(F32), v.astype(F32), log_a.astype(F32)
    b, L, g, n = q.shape
    r, p = v.shape[3], v.shape[4]
    Q = CHUNK if L % CHUNK == 0 else L
    nc = L // Q
    if s0 is None:
        s0 = jnp.zeros((b, g, r, p, n), F32)
    s0 = s0.astype(F32)

    def to_chunks(t):
        return jnp.moveaxis(t.reshape((b, nc, Q) + t.shape[2:]), 1, 0)

    causal = jnp.tril(jnp.ones((Q, Q), dtype=bool))[None, :, :, None, None]

    def body(s, inp):
        qc, kc, vc, ac = inp
        acum = jnp.cumsum(ac, axis=1)
        seg = acum[:, :, None] - acum[:, None]
        lmat = jnp.exp(jnp.where(causal, seg, -jnp.inf))
        scores = jnp.einsum('bign,bjgn->bgij', qc, kc)
        y = jnp.einsum('bgij,bijgr,bjgrp->bigrp', scores, lmat, vc)
        y = y + jnp.einsum('bign,bgrpn->bigrp', qc, s) * jnp.exp(acum)[..., None]
        w_end = jnp.exp(acum[:, -1:] - acum)
        s = jnp.exp(acum[:, -1])[..., None, None] * s + jnp.einsum('bjgn,bjgrp->bgrpn', kc, vc * w_end[..., None])
        return s, y

    s_last, ys = lax.scan(body, s0, (to_chunks(q), to_chunks(k), to_chunks(v), to_chunks(log_a)))
    y = jnp.moveaxis(ys, 0, 1).reshape(b, L, g, r, p)
    return y, s_last


def causal_conv_silu(xbc, buf, w, bias):
    L = xbc.shape[1]
    full = jnp.concatenate([buf.astype(xbc.dtype), xbc], axis=1)
    y = bias + full[:, 0:L] * w[0]
    for tap in range(1, SSD_CONV):
        y = y + full[:, tap:tap + L] * w[tap]
    return jax.nn.silu(y), full[:, L:]


def gated_group_rmsnorm(y, z, w):
    yf = y.astype(F32) * jax.nn.silu(z.astype(F32))
    shp = yf.shape
    yg = yf.reshape(shp[:-1] + (SSD_GROUPS, SSD_WIDTH // SSD_GROUPS))
    yg = yg * lax.rsqrt(jnp.mean(yg * yg, axis=-1, keepdims=True) + EPS)
    return yg.reshape(shp) * w.astype(F32)


def rotary(x, pos):
    d = x.shape[-1]
    half = d // 2
    inv = ROPE_BASE ** (-jnp.arange(half, dtype=F32) / half)
    ang = pos.astype(F32)[:, None] * inv[None]
    cos = jnp.cos(ang)[None, :, None, :]
    sin = jnp.sin(ang)[None, :, None, :]
    xf = x.astype(F32)
    x1, x2 = xf[..., :half], xf[..., half:]
    return jnp.concatenate([x1 * cos - x2 * sin, x2 * cos + x1 * sin], axis=-1)


def mixer_layer(h, pos, s5_h0, ssd_s0, conv_buf, ret_s0, p):
    bsz, L, _ = h.shape
    dt_ = h.dtype
    xn = rmsnorm(h, p['norm1_w'])
    proj = xn @ p['w_in']
    split_at = np.cumsum(IN_SIZES)[:-1].tolist()
    u, z, xbc, dt, q, k, v, g, gates = jnp.split(proj, split_at, axis=-1)

    y_s5, s5_last = s5_mixer(u, p['s5_a_re'], p['s5_a_im'], p['s5_log_dt'], p['s5_b_re'], p['s5_b_im'],
                             p['s5_c_re'], p['s5_c_im'], p['s5_d'], s5_h0)
    glu = jax.nn.gelu(y_s5.astype(dt_)) @ p['w_s5_out']
    y_a = glu[..., :D_MODEL] * jax.nn.sigmoid(glu[..., D_MODEL:])

    xbc, conv_new = causal_conv_silu(xbc, conv_buf, p['ssd_conv_w'], p['ssd_conv_b'])
    xs, bs, cs = jnp.split(xbc, [SSD_WIDTH, SSD_WIDTH + SSD_GROUPS * SSD_STATE], axis=-1)
    delta = jax.nn.softplus(dt.astype(F32) + p['ssd_dt_bias'].astype(F32))
    a_head = -jnp.exp(p['ssd_a_log'].astype(F32))
    log_a = (delta * a_head).reshape(bsz, L, SSD_GROUPS, SSD_HEADS_PER_GROUP)
    xs4 = xs.astype(F32).reshape(bsz, L, SSD_HEADS, SSD_HEAD_DIM)
    vv = (xs4 * delta[..., None]).reshape(bsz, L, SSD_GROUPS, SSD_HEADS_PER_GROUP, SSD_HEAD_DIM)
    s0 = None if ssd_s0 is None else ssd_s0.reshape(bsz, SSD_GROUPS, SSD_HEADS_PER_GROUP, SSD_HEAD_DIM, SSD_STATE)
    y_ssd, ssd_s = chunked_decay_scan(cs.reshape(bsz, L, SSD_GROUPS, SSD_STATE),
                                      bs.reshape(bsz, L, SSD_GROUPS, SSD_STATE), vv, log_a, s0)
    y_ssd = y_ssd.reshape(bsz, L, SSD_HEADS, SSD_HEAD_DIM) + p['ssd_d'].astype(F32)[:, None] * xs4
    y_b = gated_group_rmsnorm(y_ssd.reshape(bsz, L, SSD_WIDTH), z, p['ssd_norm_w']).astype(dt_) @ p['w_ssd_out']
    ssd_s = ssd_s.reshape(bsz, SSD_HEADS, SSD_HEAD_DIM, SSD_STATE)

    qr = rotary(q.reshape(bsz, L, RET_HEADS, RET_QK_DIM), pos)
    kr = rotary(k.reshape(bsz, L, RET_HEADS, RET_QK_DIM), pos) * (RET_QK_DIM ** -0.5)
    log_gamma = jnp.log1p(-jnp.exp2(-5.0 - jnp.arange(RET_HEADS, dtype=F32)))
    log_a_c = jnp.broadcast_to(log_gamma[:, None], (bsz, L, RET_HEADS, 1))
    r0 = None if ret_s0 is None else ret_s0.reshape(bsz, RET_HEADS, 1, RET_V_DIM, RET_QK_DIM)
    y_ret, ret_s = chunked_decay_scan(qr, kr, v.reshape(bsz, L, RET_HEADS, 1, RET_V_DIM), log_a_c, r0)
    y_ret = y_ret.reshape(bsz, L, RET_HEADS, RET_V_DIM)
    y_ret = y_ret * lax.rsqrt(jnp.mean(y_ret * y_ret, axis=-1, keepdims=True) + EPS)
    y_ret = y_ret.reshape(bsz, L, RET_WIDTH) * jax.nn.silu(g.astype(F32))
    y_c = y_ret.astype(dt_) @ p['w_ret_out']
    ret_s = ret_s.reshape(bsz, RET_HEADS, RET_V_DIM, RET_QK_DIM)

    gate = jax.nn.sigmoid(gates.reshape(bsz, L, N_BRANCH, D_MODEL))
    merged = gate[:, :, 0] * y_a + gate[:, :, 1] * y_b + gate[:, :, 2] * y_c
    h = h + (merged @ p['w_out']).astype(dt_)

    hn = rmsnorm(h, p['norm2_w'])
    h = h + (jnp.square(jax.nn.relu(hn @ p['w_mlp_up'])) @ p['w_mlp_down']).astype(dt_)
    return h, s5_last, ssd_s, conv_new, ret_s


def run_trunk(x, pos, st, params, final_norm_w):
    h = x
    out_re, out_im, out_ssd, out_conv, out_ret = [], [], [], [], []
    for l in range(DEPTH):
        p = {name: arr[l] for name, arr in params.items()}
        if st is None:
            s5_h0, ssd_s0, ret_s0 = None, None, None
            conv_buf = jnp.zeros((x.shape[0], SSD_CONV - 1, SSD_CONV_DIM), x.dtype)
        else:
            s5_h0 = lax.complex(st[0][l].astype(F32), st[1][l].astype(F32))
            ssd_s0, conv_buf, ret_s0 = st[2][l], st[3][l], st[4][l]
        h, s5_last, ssd_s, conv_new, ret_s = mixer_layer(h, pos, s5_h0, ssd_s0, conv_buf, ret_s0, p)
        out_re.append(jnp.real(s5_last))
        out_im.append(jnp.imag(s5_last))
        out_ssd.append(ssd_s)
        out_conv.append(conv_new)
        out_ret.append(ret_s)
    y = rmsnorm(h, final_norm_w)
    return y, jnp.stack(out_re), jnp.stack(out_im), jnp.stack(out_ssd), jnp.stack(out_conv), jnp.stack(out_ret)


def setup_inputs(seed: int = 0) -> dict:
    key = jax.random.key(seed)
    ks = iter(jax.random.split(key, 40))

    def nrm(shape, scale):
        return jax.random.normal(next(ks), shape, F32) * scale

    def log_uniform(shape, lo, hi):
        uu = jax.random.uniform(next(ks), shape, F32)
        return jnp.exp(uu * (math.log(hi) - math.log(lo)) + math.log(lo))

    Dp = DEPTH
    inp = {}
    inp['x_prompt'] = nrm((BATCH, SEQ, D_MODEL), 1.0)
    inp['x_sample'] = nrm((DEC_BATCH, DEC_SEQ, D_MODEL), 1.0)
    inp['state_s5_re'] = nrm((Dp, DEC_BATCH, S5_GROUPS, S5_STATE), 0.1)
    inp['state_s5_im'] = nrm((Dp, DEC_BATCH, S5_GROUPS, S5_STATE), 0.1)
    inp['state_ssd'] = nrm((Dp, DEC_BATCH, SSD_HEADS, SSD_HEAD_DIM, SSD_STATE), 0.1)
    inp['state_conv'] = nrm((Dp, DEC_BATCH, SSD_CONV - 1, SSD_CONV_DIM), 1.0)
    inp['state_ret'] = nrm((Dp, DEC_BATCH, RET_HEADS, RET_V_DIM, RET_QK_DIM), 0.3)
    inp['norm1_w'] = 1.0 + nrm((Dp, D_MODEL), 0.02)
    inp['w_in'] = nrm((Dp, D_MODEL, N_IN), D_MODEL ** -0.5)
    inp['s5_a_re'] = -0.5 + nrm((Dp, S5_GROUPS, S5_STATE), 0.01)
    inp['s5_a_im'] = math.pi * jnp.arange(S5_STATE, dtype=F32) + nrm((Dp, S5_GROUPS, S5_STATE), 0.01)
    inp['s5_log_dt'] = jnp.log(log_uniform((Dp, S5_GROUPS), 1e-3, 1e-1))
    inp['s5_b_re'] = nrm((Dp, S5_GROUPS, S5_STATE, S5_GROUP), (2 * S5_GROUP) ** -0.5)
    inp['s5_b_im'] = nrm((Dp, S5_GROUPS, S5_STATE, S5_GROUP), (2 * S5_GROUP) ** -0.5)
    inp['s5_c_re'] = nrm((Dp, S5_GROUPS, S5_GROUP, S5_STATE), S5_STATE ** -0.5)
    inp['s5_c_im'] = nrm((Dp, S5_GROUPS, S5_GROUP, S5_STATE), S5_STATE ** -0.5)
    inp['s5_d'] = nrm((Dp, S5_WIDTH), 1.0)
    inp['w_s5_out'] = nrm((Dp, S5_WIDTH, 2 * D_MODEL), S5_WIDTH ** -0.5)
    inp['ssd_conv_w'] = nrm((Dp, SSD_CONV, SSD_CONV_DIM), SSD_CONV ** -0.5)
    inp['ssd_conv_b'] = nrm((Dp, SSD_CONV_DIM), 0.01)
    dt0 = log_uniform((Dp, SSD_HEADS), 1e-3, 1e-1)
    inp['ssd_dt_bias'] = dt0 + jnp.log(-jnp.expm1(-dt0))
    inp['ssd_a_log'] = jnp.log(jax.random.uniform(next(ks), (Dp, SSD_HEADS), F32, 1.0, 16.0))
    inp['ssd_d'] = 1.0 + nrm((Dp, SSD_HEADS), 0.1)
    inp['ssd_norm_w'] = 1.0 + nrm((Dp, SSD_WIDTH), 0.02)
    inp['w_ssd_out'] = nrm((Dp, SSD_WIDTH, D_MODEL), SSD_WIDTH ** -0.5)
    inp['w_ret_out'] = nrm((Dp, RET_WIDTH, D_MODEL), RET_WIDTH ** -0.5)
    inp['w_out'] = nrm((Dp, D_MODEL, D_MODEL), D_MODEL ** -0.5)
    inp['norm2_w'] = 1.0 + nrm((Dp, D_MODEL), 0.02)
    inp['w_mlp_up'] = nrm((Dp, D_MODEL, D_FF), D_MODEL ** -0.5)
    inp['w_mlp_down'] = nrm((Dp, D_FF, D_MODEL), D_FF ** -0.5)
    inp['final_norm_w'] = 1.0 + nrm((D_MODEL,), 0.02)
    return inp


def reference(x_prompt, x_sample, state_s5_re, state_s5_im, state_ssd, state_conv, state_ret,
              norm1_w, w_in, s5_a_re, s5_a_im, s5_log_dt, s5_b_re, s5_b_im, s5_c_re, s5_c_im, s5_d,
              w_s5_out, ssd_conv_w, ssd_conv_b, ssd_dt_bias, ssd_a_log, ssd_d, ssd_norm_w, w_ssd_out,
              w_ret_out, w_out, norm2_w, w_mlp_up, w_mlp_down, final_norm_w):
    params = dict(norm1_w=norm1_w, w_in=w_in, s5_a_re=s5_a_re, s5_a_im=s5_a_im, s5_log_dt=s5_log_dt,
                  s5_b_re=s5_b_re, s5_b_im=s5_b_im, s5_c_re=s5_c_re, s5_c_im=s5_c_im, s5_d=s5_d,
                  w_s5_out=w_s5_out, ssd_conv_w=ssd_conv_w, ssd_conv_b=ssd_conv_b, ssd_dt_bias=ssd_dt_bias,
                  ssd_a_log=ssd_a_log, ssd_d=ssd_d, ssd_norm_w=ssd_norm_w, w_ssd_out=w_ssd_out,
                  w_ret_out=w_ret_out, w_out=w_out, norm2_w=norm2_w, w_mlp_up=w_mlp_up, w_mlp_down=w_mlp_down)
    pos_p = jnp.arange(x_prompt.shape[1], dtype=jnp.int32)
    pos_s = PAST_LEN + jnp.arange(x_sample.shape[1], dtype=jnp.int32)
    y_prompt, re_p, im_p, ssd_p, conv_p, ret_p = run_trunk(x_prompt, pos_p, None, params, final_norm_w)
    y_sample, re_s, im_s, ssd_s, conv_s, ret_s = run_trunk(
        x_sample, pos_s, (state_s5_re, state_s5_im, state_ssd, state_conv, state_ret), params, final_norm_w)
    return (y_prompt, y_sample, re_p, im_p, ssd_p, conv_p, ret_p, re_s, im_s, ssd_s, conv_s, ret_s)
```

```python
import functools
import math

import jax
import jax.numpy as jnp
import numpy as np
from jax import lax
from jax.experimental import pallas as pl
from jax.experimental.pallas import tpu as pltpu

F32 = jnp.float32
BF16 = jnp.bfloat16

D_MODEL = 1024
S5_GROUP = 16
S5_STATE = 64
S5_GROUPS = D_MODEL // S5_GROUP
S5_NSTATE = S5_GROUPS * S5_STATE
S5_SUPER = 8
SSD_HEADS = 16
SSD_HEAD_DIM = 64
SSD_GROUPS = 4
SSD_STATE = 128
SSD_CONV = 4
SSD_CONV_DIM = D_MODEL + 2 * SSD_GROUPS * SSD_STATE
RET_HEADS = 4
RET_QK_DIM = 128
RET_V_DIM = 256
ROPE_BASE = 10000.0
D_FF = 4 * D_MODEL
CHUNK = 128
EPS = 1e-6
PAST_LEN = 16384
LANES = 128
SUBLANES = 8
VMEM_LIMIT = 52 * 1024 * 1024

OFF_XBC, OFF_Z, OFF_Q, OFF_K, OFF_V, OFF_G, OFF_GATES = 0, 2048, 3072, 3584, 4096, 5120, 6144
N_MAIN = 9216


def _cparams(sem):
    return pltpu.CompilerParams(dimension_semantics=sem, vmem_limit_bytes=VMEM_LIMIT)


def _sigmoid(x):
    return 1.0 / (1.0 + jnp.exp(-x))


def _silu(x):
    return x * _sigmoid(x)


def _softplus(x):
    return jnp.maximum(x, 0.0) + jnp.log1p(jnp.exp(-jnp.abs(x)))


def _gelu_tanh(x):
    return 0.5 * x * (1.0 + jnp.tanh(math.sqrt(2.0 / math.pi) * (x + 0.044715 * (x * x * x))))


def _rmsnorm(x, w):
    return x * lax.rsqrt(jnp.mean(x * x, axis=-1, keepdims=True) + EPS) * w


def _cumsum_rows(x):
    n = x.shape[0]
    row = lax.broadcasted_iota(jnp.int32, x.shape, 0)
    k = 1
    while k < n:
        x = x + jnp.where(row >= k, pltpu.roll(x, k, 0), 0.0)
        k *= 2
    return x


def _group_rms(y, width):
    outs = []
    for s in range(0, y.shape[-1], width):
        yg = y[:, s:s + width]
        outs.append(yg * lax.rsqrt(jnp.mean(yg * yg, axis=-1, keepdims=True) + EPS))
    return jnp.concatenate(outs, axis=-1)


def _dot(a, b):
    return jnp.dot(a, b, preferred_element_type=F32)


def _split_dot(x, e):
    hi = x.astype(BF16)
    lo = (x - hi.astype(F32)).astype(BF16)
    return _dot(hi, e) + _dot(lo, e)


def _proj_kernel(x_ref, nw_ref, w_ref, wdt_ref, o_ref, odt_ref, xn_ref):
    @pl.when(pl.program_id(2) == 0)
    def _():
        xn = _rmsnorm(x_ref[...], nw_ref[...]).astype(BF16)
        xn_ref[...] = xn
        odt_ref[...] = _dot(xn, wdt_ref[...])

    o_ref[...] = _dot(xn_ref[...], w_ref[...]).astype(o_ref.dtype)


def _proj_main(x3, nw, w, wdt, tm, tn):
    nb, L, _ = x3.shape
    n = w.shape[1]
    return pl.pallas_call(
        _proj_kernel,
        grid=(nb, L // tm, n // tn),
        in_specs=[pl.BlockSpec((None, tm, D_MODEL), lambda b, i, j: (b, i, 0)),
                  pl.BlockSpec((1, D_MODEL), lambda b, i, j: (0, 0)),
                  pl.BlockSpec((D_MODEL, tn), lambda b, i, j: (0, j)),
                  pl.BlockSpec((D_MODEL, LANES), lambda b, i, j: (0, 0))],
        out_specs=[pl.BlockSpec((None, tm, tn), lambda b, i, j: (b, i, j)),
                   pl.BlockSpec((None, tm, LANES), lambda b, i, j: (b, i, 0))],
        out_shape=[jax.ShapeDtypeStruct((nb, L, n), F32), jax.ShapeDtypeStruct((nb, L, LANES), F32)],
        scratch_shapes=[pltpu.VMEM((tm, D_MODEL), BF16)],
        compiler_params=_cparams(("parallel", "parallel", "arbitrary")),
        name="proj_main",
    )(x3, nw, w, wdt)


def _proj_u_kernel(x_ref, nw_ref, w_ref, o_ref):
    xn = _rmsnorm(x_ref[...], nw_ref[...]).astype(BF16)
    o_ref[...] = _dot(xn, w_ref[...])


def _proj_u(x3, nw, wu, tm):
    nb, L, _ = x3.shape
    return pl.pallas_call(
        _proj_u_kernel,
        grid=(nb, L // tm),
        in_specs=[pl.BlockSpec((None, tm, D_MODEL), lambda b, i: (b, i, 0)),
                  pl.BlockSpec((1, D_MODEL), lambda b, i: (0, 0)),
                  pl.BlockSpec((D_MODEL, D_MODEL), lambda b, i: (0, 0))],
        out_specs=pl.BlockSpec((tm, D_MODEL), lambda b, i: (i, b)),
        out_shape=jax.ShapeDtypeStruct((L, nb * D_MODEL), F32),
        compiler_params=_cparams(("parallel", "parallel")),
        name="proj_u",
    )(x3, nw, wu)


def _s5_disc_kernel(are_ref, aim_ref, ldt_ref, bre_ref, bim_ref, abr_ref, abi_ref, bbr_ref, bbi_ref):
    ar, ai = are_ref[...], aim_ref[...]
    dt = jnp.exp(ldt_ref[...])
    mag = jnp.exp(ar * dt)
    abr = mag * jnp.cos(ai * dt)
    abi = mag * jnp.sin(ai * dt)
    abr_ref[...] = abr
    abi_ref[...] = abi
    nr, ni = abr - 1.0, abi
    den = 1.0 / (ar * ar + ai * ai)
    cr = (nr * ar + ni * ai) * den
    ci = (ni * ar - nr * ai) * den
    br, bi = bre_ref[...], bim_ref[...]
    bbr_ref[...] = cr * br - ci * bi
    bbi_ref[...] = cr * bi + ci * br


def _s5_discretise(a_re, a_im, log_dt, b_re, b_im):
    g, p = a_re.shape
    gs = b_re.shape[-1]
    full3 = lambda s: pl.BlockSpec(s, lambda: (0, 0, 0))
    return pl.pallas_call(
        _s5_disc_kernel,
        in_specs=[full3((g, 1, p)), full3((g, 1, p)), full3((g, 1, 1)), full3((g, gs, p)), full3((g, gs, p))],
        out_specs=[full3((g, 1, p)), full3((g, 1, p)), full3((g, gs, p)), full3((g, gs, p))],
        out_shape=[jax.ShapeDtypeStruct((g, 1, p), F32)] * 2 + [jax.ShapeDtypeStruct((g, gs, p), F32)] * 2,
        name="s5_discretise",
    )(a_re.reshape(g, 1, p), a_im.reshape(g, 1, p), log_dt.reshape(g, 1, 1),
      jnp.swapaxes(b_re, 1, 2), jnp.swapaxes(b_im, 1, 2))


def _s5_kernel(u_ref, wbr_ref, wbi_ref, ar_ref, ai_ref, wcr_ref, wci_ref, d_ref, h0r_ref, h0i_ref,
               y_ref, hr_ref, hi_ref, bur, bui, *, nbatch, steps, lane_w):
    @pl.when(pl.program_id(0) == 0)
    def _():
        hr_ref[...] = h0r_ref[...]
        hi_ref[...] = h0i_ref[...]

    cw = S5_SUPER * S5_GROUP
    sw = S5_SUPER * S5_STATE
    u = u_ref[...]
    ub = u.astype(BF16)
    for j in range(S5_GROUPS // S5_SUPER):
        uj = ub[:, j * cw:(j + 1) * cw]
        bur[:, j * sw:(j + 1) * sw] = _dot(uj, wbr_ref[j])
        bui[:, j * sw:(j + 1) * sw] = _dot(uj, wbi_ref[j])

    for lc in range(S5_NSTATE // lane_w):
        sl = slice(lc * lane_w, (lc + 1) * lane_w)
        are = jnp.broadcast_to(ar_ref[:, sl], (nbatch, lane_w))
        aim = jnp.broadcast_to(ai_ref[:, sl], (nbatch, lane_w))

        def body(t, carry, sl=sl, are=are, aim=aim):
            hr, hi = carry
            rows = pl.ds(pl.multiple_of(t * nbatch, SUBLANES), nbatch)
            nr = are * hr - aim * hi + bur[rows, sl]
            ni = are * hi + aim * hr + bui[rows, sl]
            bur[rows, sl] = nr
            bui[rows, sl] = ni
            return nr, ni

        hr, hi = lax.fori_loop(0, steps, body, (hr_ref[:, sl], hi_ref[:, sl]), unroll=min(steps, 4))
        hr_ref[:, sl] = hr
        hi_ref[:, sl] = hi

    for j in range(S5_GROUPS // S5_SUPER):
        yj = (_dot(bur[:, j * sw:(j + 1) * sw].astype(BF16), wcr_ref[j])
              + _dot(bui[:, j * sw:(j + 1) * sw].astype(BF16), wci_ref[j]))
        yj = yj + d_ref[:, j * cw:(j + 1) * cw] * u[:, j * cw:(j + 1) * cw]
        y_ref[:, j * cw:(j + 1) * cw] = _gelu_tanh(yj).astype(y_ref.dtype)


def _s5_scan(u2, wbr, wbi, abr, abi, wcr, wci, d, h0r, h0i, nbatch, steps):
    rows = u2.shape[0]
    blk = nbatch * steps
    lane_w = max(LANES, S5_NSTATE // nbatch)
    nsup = S5_GROUPS // S5_SUPER
    cw, sw = S5_SUPER * S5_GROUP, S5_SUPER * S5_STATE
    c2 = lambda s: pl.BlockSpec(s, lambda c: (0, 0))
    c3 = lambda s: pl.BlockSpec(s, lambda c: (0, 0, 0))
    kern = functools.partial(_s5_kernel, nbatch=nbatch, steps=steps, lane_w=lane_w)
    return pl.pallas_call(
        kern,
        grid=(rows // blk,),
        in_specs=[pl.BlockSpec((blk, D_MODEL), lambda c: (c, 0)),
                  c3((nsup, cw, sw)), c3((nsup, cw, sw)),
                  c2((1, S5_NSTATE)), c2((1, S5_NSTATE)),
                  c3((nsup, sw, cw)), c3((nsup, sw, cw)),
                  c2((1, D_MODEL)),
                  c2((nbatch, S5_NSTATE)), c2((nbatch, S5_NSTATE))],
        out_specs=[pl.BlockSpec((blk, D_MODEL), lambda c: (c, 0)),
                   c2((nbatch, S5_NSTATE)), c2((nbatch, S5_NSTATE))],
        out_shape=[jax.ShapeDtypeStruct((rows, D_MODEL), BF16),
                   jax.ShapeDtypeStruct((nbatch, S5_NSTATE), F32),
                   jax.ShapeDtypeStruct((nbatch, S5_NSTATE), F32)],
        scratch_shapes=[pltpu.VMEM((blk, S5_NSTATE), F32), pltpu.VMEM((blk, S5_NSTATE), F32)],
        compiler_params=_cparams(("arbitrary",)),
        name="s5_scan",
    )(u2, wbr, wbi, abr, abi, wcr, wci, d, h0r, h0i)


def _s5_weights(abr, abi, bbr, bbi, c_re, c_im):
    nsup = S5_GROUPS // S5_SUPER
    eye = jnp.eye(S5_SUPER, dtype=F32)

    def bexp(b):
        b = b.reshape(nsup, S5_SUPER, S5_GROUP, S5_STATE)
        return (b[:, :, :, None, :] * eye[None, :, None, :, None]).reshape(
            nsup, S5_SUPER * S5_GROUP, S5_SUPER * S5_STATE).astype(BF16)

    def cexp(c):
        c = jnp.swapaxes(c, 1, 2).reshape(nsup, S5_SUPER, S5_STATE, S5_GROUP)
        return (c[:, :, :, None, :] * eye[None, :, None, :, None]).reshape(
            nsup, S5_SUPER * S5_STATE, S5_SUPER * S5_GROUP).astype(BF16)

    return (bexp(bbr), bexp(bbi), abr.reshape(1, S5_NSTATE), abi.reshape(1, S5_NSTATE),
            cexp(c_re), cexp(-c_im))


def _ssd_gate_norm(y, xs, z, dfull, nw):
    y = (y + dfull * xs) * _silu(z)
    return _group_rms(y, D_MODEL // SSD_GROUPS) * nw


def _ssd_chunk_kernel(xbc_ref, z_ref, dt_ref, cw_ref, cb_ref, b16_ref, al16_ref, bfull_ref, afull_ref,
                      dfull_ref, nw_ref, eh_ref, o_ref, s_ref, xpad, st):
    c = pl.program_id(1)
    nc = pl.num_programs(1)
    q = CHUNK
    pad = SUBLANES

    @pl.when(c == 0)
    def _():
        xpad[0:pad, :] = jnp.zeros((pad, SSD_CONV_DIM), F32)
        st[...] = jnp.zeros(st.shape, F32)

    @pl.when(c > 0)
    def _():
        xpad[0:pad, :] = xpad[q:q + pad, :]

    xpad[pad:pad + q, :] = xbc_ref[...]
    acc = cb_ref[...] + xpad[pad - 3:pad - 3 + q, :] * cw_ref[0:1, :]
    for tap in range(1, SSD_CONV):
        acc = acc + xpad[pad - 3 + tap:pad - 3 + tap + q, :] * cw_ref[tap:tap + 1, :]
    xc = _silu(acc)
    xs = xc[:, :D_MODEL]
    nbc = SSD_GROUPS * SSD_STATE
    bs = xc[:, D_MODEL:D_MODEL + nbc]
    cs = xc[:, D_MODEL + nbc:]

    delta16 = _softplus(dt_ref[...] + b16_ref[...])
    acum16 = _cumsum_rows(delta16 * (-jnp.exp(al16_ref[...])))
    acum_t = acum16.T
    delta_t = delta16.T
    delta_f = _softplus(_split_dot(dt_ref[...], eh_ref[...]) + bfull_ref[...])
    acum_f = _cumsum_rows(delta_f * afull_ref[...])
    ea_f = jnp.exp(acum_f)
    last_f = acum_f[q - 1:q, :]
    xw = (xs * delta_f * jnp.exp(last_f - acum_f)).astype(BF16)

    ri = lax.broadcasted_iota(jnp.int32, (q, q), 0)
    ci = lax.broadcasted_iota(jnp.int32, (q, q), 1)
    causal = ri >= ci
    hw = D_MODEL // SSD_GROUPS
    hpg = SSD_HEADS // SSD_GROUPS
    xsb = xs.astype(BF16)
    ys = []
    for g in range(SSD_GROUPS):
        bg = bs[:, g * SSD_STATE:(g + 1) * SSD_STATE]
        cg = cs[:, g * SSD_STATE:(g + 1) * SSD_STATE].astype(BF16)
        bgt = bg.T.astype(BF16)
        scores = _dot(cg, bgt)
        sg = st[:, g * hw:(g + 1) * hw]
        y_inter = _dot(cg, sg.astype(BF16)) * ea_f[:, g * hw:(g + 1) * hw]
        for r in range(hpg):
            hd = g * hpg + r
            seg = acum16[:, hd:hd + 1] - acum_t[hd:hd + 1, :]
            lm = jnp.exp(jnp.where(causal, seg, -jnp.inf)) * delta_t[hd:hd + 1, :]
            p = (scores * lm).astype(BF16)
            ys.append(_dot(p, xsb[:, hd * SSD_HEAD_DIM:(hd + 1) * SSD_HEAD_DIM])
                      + y_inter[:, r * SSD_HEAD_DIM:(r + 1) * SSD_HEAD_DIM])
        st[:, g * hw:(g + 1) * hw] = (jnp.exp(last_f[:, g * hw:(g + 1) * hw]) * sg
                                      + _dot(bgt, xw[:, g * hw:(g + 1) * hw]))
    y = jnp.concatenate(ys, axis=-1)
    o_ref[...] = _ssd_gate_norm(y, xs, z_ref[...], dfull_ref[...], nw_ref[...]).astype(o_ref.dtype)

    @pl.when(c == nc - 1)
    def _():
        for k in range(D_MODEL // LANES):
            s_ref[k * LANES:(k + 1) * LANES, :] = st[:, k * LANES:(k + 1) * LANES].T


def _ssd_prompt(proj, dt, cw, cb, b16, al16, bfull, afull, dfull, nw, eh):
    nb, L, _ = proj.shape
    q = CHUNK
    c2 = lambda s: pl.BlockSpec(s, lambda b, c: (0, 0))
    return pl.pallas_call(
        _ssd_chunk_kernel,
        grid=(nb, L // q),
        in_specs=[pl.BlockSpec((None, q, SSD_CONV_DIM), lambda b, c: (b, c, OFF_XBC // SSD_CONV_DIM)),
                  pl.BlockSpec((None, q, D_MODEL), lambda b, c: (b, c, OFF_Z // D_MODEL)),
                  pl.BlockSpec((None, q, LANES), lambda b, c: (b, c, 0)),
                  c2((SSD_CONV, SSD_CONV_DIM)), c2((1, SSD_CONV_DIM)),
                  c2((1, LANES)), c2((1, LANES)),
                  c2((1, D_MODEL)), c2((1, D_MODEL)), c2((1, D_MODEL)), c2((1, D_MODEL)),
                  c2((LANES, D_MODEL))],
        out_specs=[pl.BlockSpec((None, q, D_MODEL), lambda b, c: (b, c, 0)),
                   pl.BlockSpec((None, D_MODEL, SSD_STATE), lambda b, c: (b, 0, 0))],
        out_shape=[jax.ShapeDtypeStruct((nb, L, D_MODEL), BF16),
                   jax.ShapeDtypeStruct((nb, D_MODEL, SSD_STATE), F32)],
        scratch_shapes=[pltpu.VMEM((q + 2 * SUBLANES, SSD_CONV_DIM), F32),
                        pltpu.VMEM((SSD_STATE, D_MODEL), F32)],
        compiler_params=_cparams(("parallel", "arbitrary")),
        name="ssd_chunk",
    )(proj, proj, dt, cw, cb, b16, al16, bfull, afull, dfull, nw, eh)


def _rotary(x, cos2, sin2):
    outs = []
    for h in range(RET_HEADS):
        xh = x[:, h * RET_QK_DIM:(h + 1) * RET_QK_DIM]
        outs.append(xh * cos2 + pltpu.roll(xh, RET_QK_DIM // 2, 1) * sin2)
    return outs


def _ret_gate_norm(y, g):
    return _group_rms(y, RET_V_DIM) * _silu(g)


def _ret_chunk_kernel(q_ref, k_ref, v_ref, g_ref, cos_ref, sin_ref, lm_ref, gp_ref, we_ref, gq_ref,
                      o_ref, s_ref, st):
    c = pl.program_id(1)
    nc = pl.num_programs(1)

    @pl.when(c == 0)
    def _():
        st[...] = jnp.zeros(st.shape, F32)

    cos2, sin2 = cos_ref[...], sin_ref[...]
    qr = _rotary(q_ref[...], cos2, sin2)
    kr = _rotary(k_ref[...], cos2, sin2)
    v = v_ref[...]
    vw = (v * we_ref[...]).astype(BF16)
    vb = v.astype(BF16)
    ys = []
    for h in range(RET_HEADS):
        sl = slice(h * RET_V_DIM, (h + 1) * RET_V_DIM)
        qh = qr[h].astype(BF16)
        kht = (kr[h] * (RET_QK_DIM ** -0.5)).T.astype(BF16)
        p = (_dot(qh, kht) * lm_ref[h]).astype(BF16)
        sh = st[:, sl]
        ys.append(_dot(p, vb[:, sl]) + _dot(qh, sh.astype(BF16)) * gp_ref[:, sl])
        st[:, sl] = gq_ref[:, sl] * sh + _dot(kht, vw[:, sl])
    y = jnp.concatenate(ys, axis=-1)
    o_ref[...] = _ret_gate_norm(y, g_ref[...]).astype(o_ref.dtype)

    @pl.when(c == nc - 1)
    def _():
        for k in range(D_MODEL // LANES):
            s_ref[k * LANES:(k + 1) * LANES, :] = st[:, k * LANES:(k + 1) * LANES].T


def _ret_consts(q):
    hidx = jnp.arange(RET_HEADS, dtype=F32)
    log_gamma = jnp.log1p(-jnp.exp2(-5.0 - hidx))
    i = jnp.arange(q, dtype=F32)
    seg = (i[:, None] - i[None, :])[None] * log_gamma[:, None, None]
    lm = jnp.exp(jnp.where((i[:, None] >= i[None, :])[None], seg, -jnp.inf))
    rep = lambda t: jnp.repeat(t, RET_V_DIM, axis=-1)
    gp = rep(jnp.exp((i[:, None] + 1.0) * log_gamma[None, :]))
    we = rep(jnp.exp((q - 1.0 - i[:, None]) * log_gamma[None, :]))
    gq = rep(jnp.exp(q * log_gamma)[None, :])
    return log_gamma, lm, gp, we, gq


def _rope_tables(pos):
    half = RET_QK_DIM // 2
    inv = ROPE_BASE ** (-jnp.arange(half, dtype=F32) / half)
    ang = pos.astype(F32)[:, None] * inv[None]
    cos, sin = jnp.cos(ang), jnp.sin(ang)
    return jnp.concatenate([cos, cos], axis=-1), jnp.concatenate([-sin, sin], axis=-1)


def _ret_prompt(proj, cos2, sin2, lm, gp, we, gq):
    nb, L, _ = proj.shape
    q = CHUNK
    qkw = RET_HEADS * RET_QK_DIM
    c2 = lambda s: pl.BlockSpec(s, lambda b, c: (0, 0))
    return pl.pallas_call(
        _ret_chunk_kernel,
        grid=(nb, L // q),
        in_specs=[pl.BlockSpec((None, q, qkw), lambda b, c: (b, c, OFF_Q // qkw)),
                  pl.BlockSpec((None, q, qkw), lambda b, c: (b, c, OFF_K // qkw)),
                  pl.BlockSpec((None, q, D_MODEL), lambda b, c: (b, c, OFF_V // D_MODEL)),
                  pl.BlockSpec((None, q, D_MODEL), lambda b, c: (b, c, OFF_G // D_MODEL)),
                  pl.BlockSpec((q, RET_QK_DIM), lambda b, c: (c, 0)),
                  pl.BlockSpec((q, RET_QK_DIM), lambda b, c: (c, 0)),
                  pl.BlockSpec((RET_HEADS, q, q), lambda b, c: (0, 0, 0)),
                  c2((q, D_MODEL)), c2((q, D_MODEL)), c2((1, D_MODEL))],
        out_specs=[pl.BlockSpec((None, q, D_MODEL), lambda b, c: (b, c, 0)),
                   pl.BlockSpec((None, D_MODEL, RET_QK_DIM), lambda b, c: (b, 0, 0))],
        out_shape=[jax.ShapeDtypeStruct((nb, L, D_MODEL), BF16),
                   jax.ShapeDtypeStruct((nb, D_MODEL, RET_QK_DIM), F32)],
        scratch_shapes=[pltpu.VMEM((RET_QK_DIM, D_MODEL), F32)],
        compiler_params=_cparams(("parallel", "arbitrary")),
        name="ret_chunk",
    )(proj, proj, proj, proj, cos2, sin2, lm, gp, we, gq)


def _sample_pre_kernel(xbc_ref, dt_ref, q_ref, k_ref, conv_ref, cw_ref, cb_ref, b16_ref, al16_ref,
                       bfull_ref, afull_ref, eh_ref, cos_ref, sin_ref,
                       vv_ref, e16_ref, bs_ref, cs_ref, xs_ref, convn_ref, qr_ref, kr_ref):
    cd = SSD_CONV_DIM
    x = xbc_ref[...]
    acc = cb_ref[...] + x * cw_ref[SSD_CONV - 1:SSD_CONV, :]
    for tap in range(SSD_CONV - 1):
        acc = acc + conv_ref[:, tap * cd:(tap + 1) * cd] * cw_ref[tap:tap + 1, :]
    convn_ref[:, 0:(SSD_CONV - 2) * cd] = conv_ref[:, cd:(SSD_CONV - 1) * cd]
    convn_ref[:, (SSD_CONV - 2) * cd:] = x
    xc = _silu(acc)
    xs = xc[:, :D_MODEL]
    nbc = SSD_GROUPS * SSD_STATE
    xs_ref[...] = xs
    bs_ref[...] = xc[:, D_MODEL:D_MODEL + nbc]
    cs_ref[...] = xc[:, D_MODEL + nbc:]
    delta16 = _softplus(dt_ref[...] + b16_ref[...])
    e16_ref[...] = jnp.exp(delta16 * (-jnp.exp(al16_ref[...])))
    delta_f = _softplus(_split_dot(dt_ref[...], eh_ref[...]) + bfull_ref[...])
    vv_ref[...] = xs * delta_f
    cos2, sin2 = cos_ref[...], sin_ref[...]
    qr_ref[...] = jnp.concatenate(_rotary(q_ref[...], cos2, sin2), axis=-1)
    kr_ref[...] = jnp.concatenate(_rotary(k_ref[...], cos2, sin2), axis=-1) * (RET_QK_DIM ** -0.5)


def _sample_pre(proj2, dt2, conv2, cw, cb, b16, al16, bfull, afull, eh, cos2, sin2):
    n = proj2.shape[0]
    qkw = RET_HEADS * RET_QK_DIM
    nbc = SSD_GROUPS * SSD_STATE
    full = lambda s: pl.BlockSpec(s, lambda i: (0, 0))
    col = lambda w, off: pl.BlockSpec((n, w), lambda i: (0, off // w))
    shp = lambda w: jax.ShapeDtypeStruct((n, w), F32)
    return pl.pallas_call(
        _sample_pre_kernel,
        grid=(1,),
        in_specs=[col(SSD_CONV_DIM, OFF_XBC), full((n, LANES)), col(qkw, OFF_Q), col(qkw, OFF_K),
                  full((n, (SSD_CONV - 1) * SSD_CONV_DIM)),
                  full((SSD_CONV, SSD_CONV_DIM)), full((1, SSD_CONV_DIM)),
                  full((1, LANES)), full((1, LANES)), full((1, D_MODEL)), full((1, D_MODEL)),
                  full((LANES, D_MODEL)), full((1, RET_QK_DIM)), full((1, RET_QK_DIM))],
        out_specs=[full((n, D_MODEL)), full((n, LANES)), full((n, nbc)), full((n, nbc)), full((n, D_MODEL)),
                   full((n, (SSD_CONV - 1) * SSD_CONV_DIM)), full((n, qkw)), full((n, qkw))],
        out_shape=[shp(D_MODEL), shp(LANES), shp(nbc), shp(nbc), shp(D_MODEL),
                   shp((SSD_CONV - 1) * SSD_CONV_DIM), shp(qkw), shp(qkw)],
        compiler_params=_cparams(("arbitrary",)),
        name="sample_pre",
    )(proj2, dt2, proj2, proj2, conv2, cw, cb, b16, al16, bfull, afull, eh, cos2, sin2)


STATE_ROWS = D_MODEL
STATE_BLK = 64


def _state_step_kernel(e_ref, s_ref, v_ref, k_ref, q_ref, sn_ref, y_ref, *, bb):
    i = pl.program_id(0)
    nblk = STATE_ROWS // STATE_BLK
    gw = STATE_ROWS // 4
    lane = lax.broadcasted_iota(jnp.int32, (LANES, LANES), 1)
    zpad = jnp.zeros((LANES - SUBLANES, LANES), F32)

    def per_token(t, carry):
        b = i * bb + t
        vt = jnp.concatenate([v_ref[t], zpad], axis=0).T
        kk = k_ref[t]
        qq = q_ref[t]
        ycat = jnp.zeros((LANES, LANES), F32)
        for cch in range(STATE_ROWS // LANES):
            cols = []
            for half in range(LANES // STATE_BLK):
                blk = cch * (LANES // STATE_BLK) + half
                r0 = blk * STATE_BLK
                g = r0 // gw
                vcol = vt[half * STATE_BLK:(half + 1) * STATE_BLK, cch:cch + 1]
                sn = e_ref[b * nblk + blk] * s_ref[t, r0:r0 + STATE_BLK, :] + vcol * kk[g:g + 1, :]
                sn_ref[t, r0:r0 + STATE_BLK, :] = sn
                cols.append(jnp.sum(sn * qq[g:g + 1, :], axis=-1, keepdims=True))
            ycol = jnp.concatenate(cols, axis=0)
            ycat = jnp.where(lane == cch, ycol, ycat)
        y_ref[t] = ycat.T[0:SUBLANES, :]
        return carry

    lax.fori_loop(0, bb, per_token, 0)


def _state_step(e_flat, s3, v3, k3, q3, bb=8):
    n = s3.shape[0]
    blk3 = lambda a, b_: pl.BlockSpec((bb, a, b_), lambda i: (i, 0, 0))
    return pl.pallas_call(
        functools.partial(_state_step_kernel, bb=bb),
        grid=(n // bb,),
        in_specs=[pl.BlockSpec(memory_space=pltpu.SMEM),
                  blk3(STATE_ROWS, LANES), blk3(SUBLANES, LANES), blk3(4, LANES), blk3(4, LANES)],
        out_specs=[blk3(STATE_ROWS, LANES), blk3(SUBLANES, LANES)],
        out_shape=[jax.ShapeDtypeStruct((n, STATE_ROWS, LANES), F32),
                   jax.ShapeDtypeStruct((n, SUBLANES, LANES), F32)],
        compiler_params=_cparams(("parallel",)),
        name="state_step",
    )(e_flat, s3, v3, k3, q3)


def _sample_post_kernel(ys_ref, xs_ref, z_ref, yr_ref, g_ref, dfull_ref, nw_ref, os_ref, or_ref):
    os_ref[...] = _ssd_gate_norm(ys_ref[...], xs_ref[...], z_ref[...], dfull_ref[...], nw_ref[...]).astype(os_ref.dtype)
    or_ref[...] = _ret_gate_norm(yr_ref[...], g_ref[...]).astype(or_ref.dtype)


def _sample_post(y_ssd, xs, proj2, y_ret, dfull, nw):
    n = y_ssd.shape[0]
    full = lambda s: pl.BlockSpec(s, lambda i: (0, 0))
    col = lambda off: pl.BlockSpec((n, D_MODEL), lambda i: (0, off // D_MODEL))
    return pl.pallas_call(
        _sample_post_kernel,
        grid=(1,),
        in_specs=[full((n, D_MODEL)), full((n, D_MODEL)), col(OFF_Z), full((n, D_MODEL)), col(OFF_G),
                  full((1, D_MODEL)), full((1, D_MODEL))],
        out_specs=[full((n, D_MODEL)), full((n, D_MODEL))],
        out_shape=[jax.ShapeDtypeStruct((n, D_MODEL), BF16)] * 2,
        compiler_params=_cparams(("arbitrary",)),
        name="sample_post",
    )(y_ssd, xs, proj2, y_ret, proj2, dfull, nw)


def _merge_kernel(s5_ref, ssd_ref, ret_ref, gates_ref, h_ref, ws5_ref, wssd_ref, wret_ref, wout_ref, o_ref):
    glu = _dot(s5_ref[...], ws5_ref[...])
    y_a = glu[:, :D_MODEL] * _sigmoid(glu[:, D_MODEL:])
    y_b = _dot(ssd_ref[...], wssd_ref[...])
    y_c = _dot(ret_ref[...], wret_ref[...])
    gate = _sigmoid(gates_ref[...])
    merged = (gate[:, :D_MODEL] * y_a + gate[:, D_MODEL:2 * D_MODEL] * y_b + gate[:, 2 * D_MODEL:] * y_c)
    o_ref[...] = h_ref[...] + _dot(merged.astype(BF16), wout_ref[...])


def _merge(s5_tm, ssd_n, ret_n, proj, h3, ws5, wssd, wret, wout, tm):
    nb, L, _ = h3.shape
    row3 = lambda w, off: pl.BlockSpec((None, tm, w), lambda b, i: (b, i, off // w))
    wfull = lambda s: pl.BlockSpec(s, lambda b, i: (0, 0))
    return pl.pallas_call(
        _merge_kernel,
        grid=(nb, L // tm),
        in_specs=[pl.BlockSpec((tm, D_MODEL), lambda b, i: (i, b)),
                  row3(D_MODEL, 0), row3(D_MODEL, 0), row3(3 * D_MODEL, OFF_GATES), row3(D_MODEL, 0),
                  wfull((D_MODEL, 2 * D_MODEL)), wfull((D_MODEL, D_MODEL)), wfull((D_MODEL, D_MODEL)),
                  wfull((D_MODEL, D_MODEL))],
        out_specs=row3(D_MODEL, 0),
        out_shape=jax.ShapeDtypeStruct((nb, L, D_MODEL), F32),
        compiler_params=_cparams(("parallel", "parallel")),
        name="merge",
    )(s5_tm, ssd_n, ret_n, proj, h3, ws5, wssd, wret, wout)


def _mlp_kernel(x_ref, nw_ref, wup_ref, wdn_ref, fw_ref, o_ref, xn_ref, acc_ref, *, final_norm):
    k = pl.program_id(1)
    nk = pl.num_programs(1)

    @pl.when(k == 0)
    def _():
        x = x_ref[...]
        xn_ref[...] = _rmsnorm(x, nw_ref[...]).astype(BF16)
        acc_ref[...] = x

    hid = jnp.maximum(_dot(xn_ref[...], wup_ref[...]), 0.0)
    acc_ref[...] += _dot((hid * hid).astype(BF16), wdn_ref[...])

    @pl.when(k == nk - 1)
    def _():
        h = acc_ref[...]
        o_ref[...] = _rmsnorm(h, fw_ref[...]) if final_norm else h


def _mlp(h2, nw, wup, wdn, fw, tm, tf, final_norm):
    rows = h2.shape[0]
    return pl.pallas_call(
        functools.partial(_mlp_kernel, final_norm=final_norm),
        grid=(rows // tm, D_FF // tf),
        in_specs=[pl.BlockSpec((tm, D_MODEL), lambda i, k: (i, 0)),
                  pl.BlockSpec((1, D_MODEL), lambda i, k: (0, 0)),
                  pl.BlockSpec((D_MODEL, tf), lambda i, k: (0, k)),
                  pl.BlockSpec((tf, D_MODEL), lambda i, k: (k, 0)),
                  pl.BlockSpec((1, D_MODEL), lambda i, k: (0, 0))],
        out_specs=pl.BlockSpec((tm, D_MODEL), lambda i, k: (i, 0)),
        out_shape=jax.ShapeDtypeStruct((rows, D_MODEL), F32),
        scratch_shapes=[pltpu.VMEM((tm, D_MODEL), BF16), pltpu.VMEM((tm, D_MODEL), F32)],
        compiler_params=_cparams(("parallel", "arbitrary")),
        name="mlp",
    )(h2, nw, wup, wdn, fw)


def _prep_layer(p):
    w_in = p['w_in']
    sizes = (D_MODEL, D_MODEL, SSD_CONV_DIM, SSD_HEADS, RET_HEADS * RET_QK_DIM, RET_HEADS * RET_QK_DIM,
             D_MODEL, D_MODEL, 3 * D_MODEL)
    offs = np.concatenate([[0], np.cumsum(sizes)])
    u, z, xbc, dt, q, k, v, g, gates = [w_in[:, offs[n]:offs[n + 1]] for n in range(len(sizes))]
    out = {}
    out['w_u'] = u.astype(BF16)
    out['w_main'] = jnp.concatenate([xbc, z, q, k, v, g, gates], axis=1).astype(BF16)
    out['w_dt'] = jnp.pad(dt, ((0, 0), (0, LANES - SSD_HEADS))).astype(BF16)
    row = lambda a: a.reshape(1, -1).astype(F32)
    padrow = lambda a: jnp.pad(a.astype(F32), (0, LANES - a.shape[0])).reshape(1, LANES)
    rep = lambda a: jnp.repeat(a.astype(F32), SSD_HEAD_DIM).reshape(1, D_MODEL)
    out['norm1_w'] = row(p['norm1_w'])
    out['norm2_w'] = row(p['norm2_w'])
    out['conv_w'] = p['ssd_conv_w'].astype(F32)
    out['conv_b'] = row(p['ssd_conv_b'])
    out['b16'] = padrow(p['ssd_dt_bias'])
    out['al16'] = padrow(p['ssd_a_log'])
    out['bfull'] = rep(p['ssd_dt_bias'])
    out['afull'] = rep(-jnp.exp(p['ssd_a_log'].astype(F32)))
    out['dfull'] = rep(p['ssd_d'])
    out['ssd_nw'] = row(p['ssd_norm_w'])
    out['s5_d'] = row(p['s5_d'])
    abr, abi, bbr, bbi = _s5_discretise(p['s5_a_re'].astype(F32), p['s5_a_im'].astype(F32),
                                        p['s5_log_dt'].astype(F32), p['s5_b_re'].astype(F32),
                                        p['s5_b_im'].astype(F32))
    out['s5'] = _s5_weights(abr, abi, bbr, bbi, p['s5_c_re'].astype(F32), p['s5_c_im'].astype(F32))
    for name in ('w_s5_out', 'w_ssd_out', 'w_ret_out', 'w_out', 'w_mlp_up', 'w_mlp_down'):
        out[name] = p[name].astype(BF16)
    return out


def _head_expand():
    e = (jnp.arange(LANES)[:, None] == (jnp.arange(D_MODEL)[None, :] // SSD_HEAD_DIM))
    return e.astype(BF16)


def _tile(n, pref):
    return pref if n % pref == 0 else n


def _prompt_trunk(x, layers, final_w):
    nb, L, _ = x.shape
    eh = _head_expand()
    _, lm, gp, we, gq = _ret_consts(CHUNK)
    cos2, sin2 = _rope_tables(jnp.arange(L, dtype=jnp.int32))
    zeros_state = jnp.zeros((nb, S5_NSTATE), F32)
    steps = 32
    h = x
    outs = {k: [] for k in ('re', 'im', 'ssd', 'conv', 'ret')}
    for li, lp in enumerate(layers):
        tm = _tile(L, 1024)
        proj, dt = _proj_main(h, lp['norm1_w'], lp['w_main'], lp['w_dt'], tm, 1024)
        u_tm = _proj_u(h, lp['norm1_w'], lp['w_u'], tm)
        wbr, wbi, abr, abi, wcr, wci = lp['s5']
        y_s5, hr, hi = _s5_scan(u_tm.reshape(L * nb, D_MODEL), wbr, wbi, abr, abi, wcr, wci, lp['s5_d'],
                                zeros_state, zeros_state, nb, steps)
        ssd_n, ssd_s = _ssd_prompt(proj, dt, lp['conv_w'], lp['conv_b'], lp['b16'], lp['al16'],
                                   lp['bfull'], lp['afull'], lp['dfull'], lp['ssd_nw'], eh)
        ret_n, ret_s = _ret_prompt(proj, cos2, sin2, lm, gp, we, gq)
        h = _merge(y_s5.reshape(L, nb * D_MODEL), ssd_n, ret_n, proj, h,
                   lp['w_s5_out'], lp['w_ssd_out'], lp['w_ret_out'], lp['w_out'], _tile(L, 256))
        last = li == len(layers) - 1
        h = _mlp(h.reshape(nb * L, D_MODEL), lp['norm2_w'], lp['w_mlp_up'], lp['w_mlp_down'], final_w,
                 _tile(nb * L, 512), 1024, last).reshape(nb, L, D_MODEL)
        outs['re'].append(hr.reshape(nb, S5_GROUPS, S5_STATE))
        outs['im'].append(hi.reshape(nb, S5_GROUPS, S5_STATE))
        outs['ssd'].append(ssd_s.reshape(nb, SSD_HEADS, SSD_HEAD_DIM, SSD_STATE))
        outs['conv'].append(proj[:, L - (SSD_CONV - 1):, OFF_XBC:OFF_XBC + SSD_CONV_DIM])
        outs['ret'].append(ret_s.reshape(nb, RET_HEADS, RET_V_DIM, RET_QK_DIM))
    return (h,) + tuple(jnp.stack(outs[k]) for k in ('re', 'im', 'ssd', 'conv', 'ret'))


def _sample_trunk(x, states, layers, final_w):
    n = x.shape[0]
    st_re, st_im, st_ssd, st_conv, st_ret = states
    eh = _head_expand()
    log_gamma, _, _, _, _ = _ret_consts(1)
    cos2, sin2 = _rope_tables(PAST_LEN + jnp.arange(1, dtype=jnp.int32))
    e_ret = jnp.broadcast_to(jnp.repeat(jnp.exp(log_gamma), RET_V_DIM // STATE_BLK)[None, :],
                             (n, STATE_ROWS // STATE_BLK)).reshape(-1)
    h = x.reshape(1, n, D_MODEL)
    outs = {k: [] for k in ('re', 'im', 'ssd', 'conv', 'ret')}
    for li, lp in enumerate(layers):
        proj, dt = _proj_main(h, lp['norm1_w'], lp['w_main'], lp['w_dt'], n, 1024)
        u = _proj_u(h, lp['norm1_w'], lp['w_u'], n)
        wbr, wbi, abr, abi, wcr, wci = lp['s5']
        y_s5, hr, hi = _s5_scan(u, wbr, wbi, abr, abi, wcr, wci, lp['s5_d'],
                                st_re[li].reshape(n, S5_NSTATE).astype(F32),
                                st_im[li].reshape(n, S5_NSTATE).astype(F32), n, 1)
        proj2, dt2 = proj.reshape(n, N_MAIN), dt.reshape(n, LANES)
        conv2 = st_conv[li].reshape(n, (SSD_CONV - 1) * SSD_CONV_DIM).astype(F32)
        vv, e16, bs, cs, xs, conv_new, qr, kr = _sample_pre(
            proj2, dt2, conv2, lp['conv_w'], lp['conv_b'], lp['b16'], lp['al16'], lp['bfull'], lp['afull'],
            eh, cos2, sin2)
        ssd_new, y_ssd = _state_step(
            e16[:, :SSD_HEADS].reshape(-1), st_ssd[li].reshape(n, STATE_ROWS, SSD_STATE).astype(F32),
            vv.reshape(n, SUBLANES, LANES), bs.reshape(n, SSD_GROUPS, SSD_STATE),
            cs.reshape(n, SSD_GROUPS, SSD_STATE))
        vret = proj2[:, OFF_V:OFF_V + D_MODEL]
        ret_new, y_ret = _state_step(
            e_ret, st_ret[li].reshape(n, STATE_ROWS, RET_QK_DIM).astype(F32),
            vret.reshape(n, SUBLANES, LANES), kr.reshape(n, RET_HEADS, RET_QK_DIM),
            qr.reshape(n, RET_HEADS, RET_QK_DIM))
        ssd_n, ret_n = _sample_post(y_ssd.reshape(n, D_MODEL), xs, proj2, y_ret.reshape(n, D_MODEL),
                                    lp['dfull'], lp['ssd_nw'])
        h = _merge(y_s5, ssd_n.reshape(1, n, D_MODEL), ret_n.reshape(1, n, D_MODEL), proj, h,
                   lp['w_s5_out'], lp['w_ssd_out'], lp['w_ret_out'], lp['w_out'], n)
        last = li == len(layers) - 1
        h = _mlp(h.reshape(n, D_MODEL), lp['norm2_w'], lp['w_mlp_up'], lp['w_mlp_down'], final_w,
                 n, 1024, last).reshape(1, n, D_MODEL)
        outs['re'].append(hr.reshape(n, S5_GROUPS, S5_STATE))
        outs['im'].append(hi.reshape(n, S5_GROUPS, S5_STATE))
        outs['ssd'].append(ssd_new.reshape(n, SSD_HEADS, SSD_HEAD_DIM, SSD_STATE))
        outs['conv'].append(conv_new.reshape(n, SSD_CONV - 1, SSD_CONV_DIM))
        outs['ret'].append(ret_new.reshape(n, RET_HEADS, RET_V_DIM, RET_QK_DIM))
    return (h.reshape(n, 1, D_MODEL),) + tuple(jnp.stack(outs[k]) for k in ('re', 'im', 'ssd', 'conv', 'ret'))


def kernel(x_prompt, x_sample, state_s5_re, state_s5_im, state_ssd, state_conv, state_ret, norm1_w, w_in, s5_a_re, s5_a_im, s5_log_dt, s5_b_re, s5_b_im, s5_c_re, s5_c_im, s5_d, w_s5_out, ssd_conv_w, ssd_conv_b, ssd_dt_bias, ssd_a_log, ssd_d, ssd_norm_w, w_ssd_out, w_ret_out, w_out, norm2_w, w_mlp_up, w_mlp_down, final_norm_w):
    params = dict(norm1_w=norm1_w, w_in=w_in, s5_a_re=s5_a_re, s5_a_im=s5_a_im, s5_log_dt=s5_log_dt,
                  s5_b_re=s5_b_re, s5_b_im=s5_b_im, s5_c_re=s5_c_re, s5_c_im=s5_c_im, s5_d=s5_d,
                  w_s5_out=w_s5_out, ssd_conv_w=ssd_conv_w, ssd_conv_b=ssd_conv_b, ssd_dt_bias=ssd_dt_bias,
                  ssd_a_log=ssd_a_log, ssd_d=ssd_d, ssd_norm_w=ssd_norm_w, w_ssd_out=w_ssd_out,
                  w_ret_out=w_ret_out, w_out=w_out, norm2_w=norm2_w, w_mlp_up=w_mlp_up, w_mlp_down=w_mlp_down)
    depth = w_in.shape[0]
    layers = [_prep_layer({name: arr[l] for name, arr in params.items()}) for l in range(depth)]
    final_w = final_norm_w.reshape(1, D_MODEL).astype(F32)
    y_p, re_p, im_p, ssd_p, conv_p, ret_p = _prompt_trunk(x_prompt.astype(F32), layers, final_w)
    y_s, re_s, im_s, ssd_s, conv_s, ret_s = _sample_trunk(
        x_sample.astype(F32), (state_s5_re, state_s5_im, state_ssd, state_conv, state_ret), layers, final_w)
    return (y_p, y_s, re_p, im_p, ssd_p, conv_p, ret_p, re_s, im_s, ssd_s, conv_s, ret_s)
```

```python
import functools
import math

import jax
import jax.numpy as jnp
from jax import lax
from jax.experimental import pallas as pl
from jax.experimental.pallas import tpu as pltpu

F32 = jnp.float32
BF16 = jnp.bfloat16

D_MODEL = 1024
S5_GROUP = 16
S5_STATE = 64
S5_GROUPS = D_MODEL // S5_GROUP
S5_NSTATE = S5_GROUPS * S5_STATE
S5_SUPER = 8
SSD_HEADS = 16
SSD_HEAD_DIM = 64
SSD_GROUPS = 4
SSD_STATE = 128
SSD_CONV = 4
SSD_CONV_DIM = D_MODEL + 2 * SSD_GROUPS * SSD_STATE
RET_HEADS = 4
RET_QK_DIM = 128
RET_V_DIM = 256
ROPE_BASE = 10000.0
D_FF = 4 * D_MODEL
CHUNK = 128
EPS = 1e-6
PAST_LEN = 16384
LANES = 128
SUBLANES = 8
VMEM_LIMIT = 52 * 1024 * 1024

OFF_Z, OFF_XS, OFF_BC, OFF_Q, OFF_K, OFF_V, OFF_G, OFF_GATES = 0, 1024, 2048, 3072, 3584, 4096, 5120, 6144
N_MAIN = 9216
SLAB_A_COLS = OFF_Q
SLAB_B_START = 2 * D_MODEL + SSD_CONV_DIM + SSD_HEADS
PROJ_TN = 1024


def _cparams(sem):
    return pltpu.CompilerParams(dimension_semantics=sem, vmem_limit_bytes=VMEM_LIMIT)


def _sigmoid(x):
    return 1.0 / (1.0 + jnp.exp(-x))


def _silu(x):
    return x * _sigmoid(x)


def _softplus(x):
    return jnp.maximum(x, 0.0) + jnp.log1p(jnp.exp(-jnp.abs(x)))


def _gelu_tanh(x):
    return 0.5 * x * (1.0 + jnp.tanh(math.sqrt(2.0 / math.pi) * (x + 0.044715 * (x * x * x))))


def _rmsnorm(x, w):
    return x * lax.rsqrt(jnp.mean(x * x, axis=-1, keepdims=True) + EPS) * w


def _cumsum_rows(x):
    n = x.shape[0]
    row = lax.broadcasted_iota(jnp.int32, x.shape, 0)
    k = 1
    while k < n:
        x = x + jnp.where(row >= k, pltpu.roll(x, k, 0), 0.0)
        k *= 2
    return x


def _group_rms(y, width):
    outs = []
    for s in range(0, y.shape[-1], width):
        yg = y[:, s:s + width]
        outs.append(yg * lax.rsqrt(jnp.mean(yg * yg, axis=-1, keepdims=True) + EPS))
    return jnp.concatenate(outs, axis=-1)


def _dot(a, b):
    return jnp.dot(a, b, preferred_element_type=F32)


def _split_dot(x, e):
    hi = x.astype(BF16)
    lo = (x - hi.astype(F32)).astype(BF16)
    return _dot(hi, e) + _dot(lo, e)


def _proj_kernel(x_ref, nw_ref, wu_ref, wa_ref, wb_ref, wdt_ref, o_ref, odt_ref, ou_ref, xn_ref, *, na):
    j = pl.program_id(2)

    @pl.when(j == 0)
    def _():
        xn = _rmsnorm(x_ref[...], nw_ref[...]).astype(BF16)
        xn_ref[...] = xn
        odt_ref[...] = _dot(xn, wdt_ref[...])
        ou_ref[...] = _dot(xn, wu_ref[...])

    @pl.when(j < na)
    def _():
        o_ref[...] = _dot(xn_ref[...], wa_ref[...])

    @pl.when(j >= na)
    def _():
        o_ref[...] = _dot(xn_ref[...], wb_ref[...])


def _proj_main(x3, nw, w_all, w_b, wdt, tm):
    nb, L, _ = x3.shape
    tn = PROJ_TN
    na = SLAB_A_COLS // tn
    return pl.pallas_call(
        functools.partial(_proj_kernel, na=na),
        grid=(nb, L // tm, N_MAIN // tn),
        in_specs=[pl.BlockSpec((None, tm, D_MODEL), lambda b, i, j: (b, i, 0)),
                  pl.BlockSpec((1, D_MODEL), lambda b, i, j: (0, 0)),
                  pl.BlockSpec((D_MODEL, D_MODEL), lambda b, i, j: (0, 0)),
                  pl.BlockSpec((D_MODEL, tn), lambda b, i, j: (0, D_MODEL // tn + jnp.minimum(j, na - 1))),
                  pl.BlockSpec((D_MODEL, tn), lambda b, i, j: (0, jnp.maximum(j - na, 0))),
                  pl.BlockSpec((D_MODEL, LANES), lambda b, i, j: (0, 0))],
        out_specs=[pl.BlockSpec((None, tm, tn), lambda b, i, j: (b, i, j)),
                   pl.BlockSpec((None, tm, LANES), lambda b, i, j: (b, i, 0)),
                   pl.BlockSpec((tm, D_MODEL), lambda b, i, j: (i, b))],
        out_shape=[jax.ShapeDtypeStruct((nb, L, N_MAIN), F32), jax.ShapeDtypeStruct((nb, L, LANES), F32),
                   jax.ShapeDtypeStruct((L, nb * D_MODEL), F32)],
        scratch_shapes=[pltpu.VMEM((tm, D_MODEL), BF16)],
        compiler_params=_cparams(("parallel", "parallel", "arbitrary")),
        name="proj_main",
    )(x3, nw, w_all, w_all, w_b, wdt)


def _s5_disc_kernel(are_ref, aim_ref, ldt_ref, bre_ref, bim_ref, abr_ref, abi_ref, bbr_ref, bbi_ref):
    ar, ai = are_ref[...], aim_ref[...]
    dt = jnp.exp(ldt_ref[...])
    mag = jnp.exp(ar * dt)
    abr = mag * jnp.cos(ai * dt)
    abi = mag * jnp.sin(ai * dt)
    abr_ref[...] = abr
    abi_ref[...] = abi
    nr, ni = abr - 1.0, abi
    den = 1.0 / (ar * ar + ai * ai)
    cr = (nr * ar + ni * ai) * den
    ci = (ni * ar - nr * ai) * den
    br, bi = bre_ref[...], bim_ref[...]
    bbr_ref[...] = cr * br - ci * bi
    bbi_ref[...] = cr * bi + ci * br


def _s5_discretise(a_re, a_im, log_dt, b_re, b_im):
    g, p = a_re.shape
    gs = b_re.shape[-1]
    full3 = lambda s: pl.BlockSpec(s, lambda: (0, 0, 0))
    return pl.pallas_call(
        _s5_disc_kernel,
        in_specs=[full3((g, 1, p)), full3((g, 1, p)), full3((g, 1, 1)), full3((g, gs, p)), full3((g, gs, p))],
        out_specs=[full3((g, 1, p)), full3((g, 1, p)), full3((g, gs, p)), full3((g, gs, p))],
        out_shape=[jax.ShapeDtypeStruct((g, 1, p), F32)] * 2 + [jax.ShapeDtypeStruct((g, gs, p), F32)] * 2,
        name="s5_discretise",
    )(a_re.reshape(g, 1, p), a_im.reshape(g, 1, p), log_dt.reshape(g, 1, 1),
      jnp.swapaxes(b_re, 1, 2), jnp.swapaxes(b_im, 1, 2))


def _s5_kernel(u_ref, wbr_ref, wbi_ref, ar_ref, ai_ref, wcr_ref, wci_ref, d_ref, h0r_ref, h0i_ref,
               y_ref, hr_ref, hi_ref, bur, bui, *, nbatch, steps, lane_w):
    @pl.when(pl.program_id(0) == 0)
    def _():
        hr_ref[...] = h0r_ref[...]
        hi_ref[...] = h0i_ref[...]

    cw = S5_SUPER * S5_GROUP
    sw = S5_SUPER * S5_STATE
    u = u_ref[...]
    ub = u.astype(BF16)
    for j in range(S5_GROUPS // S5_SUPER):
        uj = ub[:, j * cw:(j + 1) * cw]
        bur[:, j * sw:(j + 1) * sw] = _dot(uj, wbr_ref[j])
        bui[:, j * sw:(j + 1) * sw] = _dot(uj, wbi_ref[j])

    for lc in range(S5_NSTATE // lane_w):
        sl = slice(lc * lane_w, (lc + 1) * lane_w)
        are = jnp.broadcast_to(ar_ref[:, sl], (nbatch, lane_w))
        aim = jnp.broadcast_to(ai_ref[:, sl], (nbatch, lane_w))

        def body(t, carry, sl=sl, are=are, aim=aim):
            hr, hi = carry
            rows = pl.ds(pl.multiple_of(t * nbatch, SUBLANES), nbatch)
            nr = are * hr - aim * hi + bur[rows, sl]
            ni = are * hi + aim * hr + bui[rows, sl]
            bur[rows, sl] = nr
            bui[rows, sl] = ni
            return nr, ni

        hr, hi = lax.fori_loop(0, steps, body, (hr_ref[:, sl], hi_ref[:, sl]), unroll=min(steps, 4))
        hr_ref[:, sl] = hr
        hi_ref[:, sl] = hi

    for j in range(S5_GROUPS // S5_SUPER):
        yj = (_dot(bur[:, j * sw:(j + 1) * sw].astype(BF16), wcr_ref[j])
              + _dot(bui[:, j * sw:(j + 1) * sw].astype(BF16), wci_ref[j]))
        yj = yj + d_ref[:, j * cw:(j + 1) * cw] * u[:, j * cw:(j + 1) * cw]
        y_ref[:, j * cw:(j + 1) * cw] = _gelu_tanh(yj).astype(y_ref.dtype)


def _s5_scan(u2, wbr, wbi, abr, abi, wcr, wci, d, h0r, h0i, nbatch, steps):
    rows = u2.shape[0]
    blk = nbatch * steps
    lane_w = max(LANES, S5_NSTATE // nbatch)
    nsup = S5_GROUPS // S5_SUPER
    cw, sw = S5_SUPER * S5_GROUP, S5_SUPER * S5_STATE
    c2 = lambda s: pl.BlockSpec(s, lambda c: (0, 0))
    c3 = lambda s: pl.BlockSpec(s, lambda c: (0, 0, 0))
    kern = functools.partial(_s5_kernel, nbatch=nbatch, steps=steps, lane_w=lane_w)
    return pl.pallas_call(
        kern,
        grid=(rows // blk,),
        in_specs=[pl.BlockSpec((blk, D_MODEL), lambda c: (c, 0)),
                  c3((nsup, cw, sw)), c3((nsup, cw, sw)),
                  c2((1, S5_NSTATE)), c2((1, S5_NSTATE)),
                  c3((nsup, sw, cw)), c3((nsup, sw, cw)),
                  c2((1, D_MODEL)),
                  c2((nbatch, S5_NSTATE)), c2((nbatch, S5_NSTATE))],
        out_specs=[pl.BlockSpec((blk, D_MODEL), lambda c: (c, 0)),
                   c2((nbatch, S5_NSTATE)), c2((nbatch, S5_NSTATE))],
        out_shape=[jax.ShapeDtypeStruct((rows, D_MODEL), BF16),
                   jax.ShapeDtypeStruct((nbatch, S5_NSTATE), F32),
                   jax.ShapeDtypeStruct((nbatch, S5_NSTATE), F32)],
        scratch_shapes=[pltpu.VMEM((blk, S5_NSTATE), F32), pltpu.VMEM((blk, S5_NSTATE), F32)],
        compiler_params=_cparams(("arbitrary",)),
        name="s5_scan",
    )(u2, wbr, wbi, abr, abi, wcr, wci, d, h0r, h0i)


def _s5_weights(abr, abi, bbr, bbi, c_re, c_im):
    nsup = S5_GROUPS // S5_SUPER
    eye = jnp.eye(S5_SUPER, dtype=F32)

    def bexp(b):
        b = b.reshape(nsup, S5_SUPER, S5_GROUP, S5_STATE)
        return (b[:, :, :, None, :] * eye[None, :, None, :, None]).reshape(
            nsup, S5_SUPER * S5_GROUP, S5_SUPER * S5_STATE).astype(BF16)

    def cexp(c):
        c = jnp.swapaxes(c, 1, 2).reshape(nsup, S5_SUPER, S5_STATE, S5_GROUP)
        return (c[:, :, :, None, :] * eye[None, :, None, :, None]).reshape(
            nsup, S5_SUPER * S5_STATE, S5_SUPER * S5_GROUP).astype(BF16)

    return (bexp(bbr), bexp(bbi), abr.reshape(1, S5_NSTATE), abi.reshape(1, S5_NSTATE),
            cexp(c_re), cexp(-c_im))


def _ssd_gate_norm(y, xs, z, dfull, nw):
    y = (y + dfull * xs) * _silu(z)
    return _group_rms(y, D_MODEL // SSD_GROUPS) * nw


def _ssd_chunk_kernel(xs_ref, bc_ref, z_ref, dt_ref, cw_ref, cb_ref, b16_ref, al16_ref,
                      dfull_ref, nw_ref, eh_ref, o_ref, s_ref, xpad, st):
    c = pl.program_id(1)
    nc = pl.num_programs(1)
    q = CHUNK
    pad = SUBLANES

    @pl.when(c == 0)
    def _():
        xpad[0:pad, :] = jnp.zeros((pad, SSD_CONV_DIM), F32)
        st[...] = jnp.zeros(st.shape, F32)

    @pl.when(c > 0)
    def _():
        xpad[0:pad, :] = xpad[q:q + pad, :]

    xpad[pad:pad + q, :D_MODEL] = xs_ref[...]
    xpad[pad:pad + q, D_MODEL:] = bc_ref[...]
    acc = cb_ref[...] + xpad[pad - 3:pad - 3 + q, :] * cw_ref[0:1, :]
    for tap in range(1, SSD_CONV):
        acc = acc + xpad[pad - 3 + tap:pad - 3 + tap + q, :] * cw_ref[tap:tap + 1, :]
    xc = _silu(acc)
    xs = xc[:, :D_MODEL]
    nbc = SSD_GROUPS * SSD_STATE
    bs = xc[:, D_MODEL:D_MODEL + nbc]
    cs = xc[:, D_MODEL + nbc:]

    delta16 = _softplus(dt_ref[...] + b16_ref[...])
    acum16 = _cumsum_rows(delta16 * (-jnp.exp(al16_ref[...])))
    acum_t = acum16.T
    delta_t = delta16.T
    last16 = acum16[q - 1:q, :]
    ea_f = _split_dot(jnp.exp(acum16), eh_ref[...])
    dw_f = _split_dot(delta16 * jnp.exp(last16 - acum16), eh_ref[...])
    xw = (xs * dw_f).astype(BF16)

    ri = lax.broadcasted_iota(jnp.int32, (q, q), 0)
    ci = lax.broadcasted_iota(jnp.int32, (q, q), 1)
    causal = ri >= ci
    hw = D_MODEL // SSD_GROUPS
    hpg = SSD_HEADS // SSD_GROUPS
    xsb = xs.astype(BF16)
    ys = []
    for g in range(SSD_GROUPS):
        bg = bs[:, g * SSD_STATE:(g + 1) * SSD_STATE]
        cg = cs[:, g * SSD_STATE:(g + 1) * SSD_STATE].astype(BF16)
        bgt = bg.T.astype(BF16)
        scores = _dot(cg, bgt)
        sg = st[:, g * hw:(g + 1) * hw]
        y_inter = _dot(cg, sg.astype(BF16)) * ea_f[:, g * hw:(g + 1) * hw]
        for r in range(hpg):
            hd = g * hpg + r
            seg = acum16[:, hd:hd + 1] - acum_t[hd:hd + 1, :]
            lm = jnp.exp(jnp.where(causal, seg, -jnp.inf)) * delta_t[hd:hd + 1, :]
            p = (scores * lm).astype(BF16)
            ys.append(_dot(p, xsb[:, hd * SSD_HEAD_DIM:(hd + 1) * SSD_HEAD_DIM])
                      + y_inter[:, r * SSD_HEAD_DIM:(r + 1) * SSD_HEAD_DIM])
        st[:, g * hw:(g + 1) * hw] = (ea_f[q - 1:q, g * hw:(g + 1) * hw] * sg
                                      + _dot(bgt, xw[:, g * hw:(g + 1) * hw]))
    y = jnp.concatenate(ys, axis=-1)
    o_ref[...] = _ssd_gate_norm(y, xs, z_ref[...], dfull_ref[...], nw_ref[...]).astype(o_ref.dtype)

    @pl.when(c == nc - 1)
    def _():
        for k in range(D_MODEL // LANES):
            s_ref[k * LANES:(k + 1) * LANES, :] = st[:, k * LANES:(k + 1) * LANES].T


def _ssd_prompt(proj, dt, cw, cb, b16, al16, dfull, nw, eh):
    nb, L, _ = proj.shape
    q = CHUNK
    c2 = lambda s: pl.BlockSpec(s, lambda b, c: (0, 0))
    return pl.pallas_call(
        _ssd_chunk_kernel,
        grid=(nb, L // q),
        in_specs=[pl.BlockSpec((None, q, D_MODEL), lambda b, c: (b, c, OFF_XS // D_MODEL)),
                  pl.BlockSpec((None, q, D_MODEL), lambda b, c: (b, c, OFF_BC // D_MODEL)),
                  pl.BlockSpec((None, q, D_MODEL), lambda b, c: (b, c, OFF_Z // D_MODEL)),
                  pl.BlockSpec((None, q, LANES), lambda b, c: (b, c, 0)),
                  c2((SSD_CONV, SSD_CONV_DIM)), c2((1, SSD_CONV_DIM)),
                  c2((1, LANES)), c2((1, LANES)),
                  c2((1, D_MODEL)), c2((1, D_MODEL)),
                  c2((LANES, D_MODEL))],
        out_specs=[pl.BlockSpec((None, q, D_MODEL), lambda b, c: (b, c, 0)),
                   pl.BlockSpec((None, D_MODEL, SSD_STATE), lambda b, c: (b, 0, 0))],
        out_shape=[jax.ShapeDtypeStruct((nb, L, D_MODEL), BF16),
                   jax.ShapeDtypeStruct((nb, D_MODEL, SSD_STATE), F32)],
        scratch_shapes=[pltpu.VMEM((q + 2 * SUBLANES, SSD_CONV_DIM), F32),
                        pltpu.VMEM((SSD_STATE, D_MODEL), F32)],
        compiler_params=_cparams(("parallel", "arbitrary")),
        name="ssd_chunk",
    )(proj, proj, proj, dt, cw, cb, b16, al16, dfull, nw, eh)


def _rotary(x, cos2, sin2):
    outs = []
    for h in range(RET_HEADS):
        xh = x[:, h * RET_QK_DIM:(h + 1) * RET_QK_DIM]
        outs.append(xh * cos2 + pltpu.roll(xh, RET_QK_DIM // 2, 1) * sin2)
    return outs


def _ret_gate_norm(y, g):
    return _group_rms(y, RET_V_DIM) * _silu(g)


def _ret_chunk_kernel(q_ref, k_ref, v_ref, g_ref, cos_ref, sin_ref, lm_ref, gp_ref, we_ref, gq_ref,
                      o_ref, s_ref, st):
    c = pl.program_id(1)
    nc = pl.num_programs(1)

    @pl.when(c == 0)
    def _():
        st[...] = jnp.zeros(st.shape, F32)

    cos2, sin2 = cos_ref[...], sin_ref[...]
    qr = _rotary(q_ref[...], cos2, sin2)
    kr = _rotary(k_ref[...], cos2, sin2)
    v = v_ref[...]
    vw = (v * we_ref[...]).astype(BF16)
    vb = v.astype(BF16)
    ys = []
    for h in range(RET_HEADS):
        sl = slice(h * RET_V_DIM, (h + 1) * RET_V_DIM)
        qh = qr[h].astype(BF16)
        kht = (kr[h] * (RET_QK_DIM ** -0.5)).T.astype(BF16)
        p = (_dot(qh, kht) * lm_ref[h]).astype(BF16)
        sh = st[:, sl]
        ys.append(_dot(p, vb[:, sl]) + _dot(qh, sh.astype(BF16)) * gp_ref[:, sl])
        st[:, sl] = gq_ref[:, sl] * sh + _dot(kht, vw[:, sl])
    y = jnp.concatenate(ys, axis=-1)
    o_ref[...] = _ret_gate_norm(y, g_ref[...]).astype(o_ref.dtype)

    @pl.when(c == nc - 1)
    def _():
        for k in range(D_MODEL // LANES):
            s_ref[k * LANES:(k + 1) * LANES, :] = st[:, k * LANES:(k + 1) * LANES].T


def _ret_consts(q):
    hidx = jnp.arange(RET_HEADS, dtype=F32)
    log_gamma = jnp.log1p(-jnp.exp2(-5.0 - hidx))
    i = jnp.arange(q, dtype=F32)
    seg = (i[:, None] - i[None, :])[None] * log_gamma[:, None, None]
    lm = jnp.exp(jnp.where((i[:, None] >= i[None, :])[None], seg, -jnp.inf))
    rep = lambda t: jnp.repeat(t, RET_V_DIM, axis=-1)
    gp = rep(jnp.exp((i[:, None] + 1.0) * log_gamma[None, :]))
    we = rep(jnp.exp((q - 1.0 - i[:, None]) * log_gamma[None, :]))
    gq = rep(jnp.exp(q * log_gamma)[None, :])
    return log_gamma, lm, gp, we, gq


def _rope_tables(pos):
    half = RET_QK_DIM // 2
    inv = ROPE_BASE ** (-jnp.arange(half, dtype=F32) / half)
    ang = pos.astype(F32)[:, None] * inv[None]
    cos, sin = jnp.cos(ang), jnp.sin(ang)
    return jnp.concatenate([cos, cos], axis=-1), jnp.concatenate([-sin, sin], axis=-1)


def _ret_prompt(proj, cos2, sin2, lm, gp, we, gq):
    nb, L, _ = proj.shape
    q = CHUNK
    qkw = RET_HEADS * RET_QK_DIM
    c2 = lambda s: pl.BlockSpec(s, lambda b, c: (0, 0))
    return pl.pallas_call(
        _ret_chunk_kernel,
        grid=(nb, L // q),
        in_specs=[pl.BlockSpec((None, q, qkw), lambda b, c: (b, c, OFF_Q // qkw)),
                  pl.BlockSpec((None, q, qkw), lambda b, c: (b, c, OFF_K // qkw)),
                  pl.BlockSpec((None, q, D_MODEL), lambda b, c: (b, c, OFF_V // D_MODEL)),
                  pl.BlockSpec((None, q, D_MODEL), lambda b, c: (b, c, OFF_G // D_MODEL)),
                  pl.BlockSpec((q, RET_QK_DIM), lambda b, c: (c, 0)),
                  pl.BlockSpec((q, RET_QK_DIM), lambda b, c: (c, 0)),
                  pl.BlockSpec((RET_HEADS, q, q), lambda b, c: (0, 0, 0)),
                  c2((q, D_MODEL)), c2((q, D_MODEL)), c2((1, D_MODEL))],
        out_specs=[pl.BlockSpec((None, q, D_MODEL), lambda b, c: (b, c, 0)),
                   pl.BlockSpec((None, D_MODEL, RET_QK_DIM), lambda b, c: (b, 0, 0))],
        out_shape=[jax.ShapeDtypeStruct((nb, L, D_MODEL), BF16),
                   jax.ShapeDtypeStruct((nb, D_MODEL, RET_QK_DIM), F32)],
        scratch_shapes=[pltpu.VMEM((RET_QK_DIM, D_MODEL), F32)],
        compiler_params=_cparams(("parallel", "arbitrary")),
        name="ret_chunk",
    )(proj, proj, proj, proj, cos2, sin2, lm, gp, we, gq)


def _sample_pre_kernel(xsin_ref, bcin_ref, dt_ref, q_ref, k_ref, conv_ref, cw_ref, cb_ref, b16_ref, al16_ref,
                       eh_ref, cos_ref, sin_ref,
                       vv_ref, e16_ref, bs_ref, cs_ref, xs_ref, convn_ref, qr_ref, kr_ref):
    cd = SSD_CONV_DIM
    x = jnp.concatenate([xsin_ref[...], bcin_ref[...]], axis=-1)
    acc = cb_ref[...] + x * cw_ref[SSD_CONV - 1:SSD_CONV, :]
    for tap in range(SSD_CONV - 1):
        acc = acc + conv_ref[:, tap * cd:(tap + 1) * cd] * cw_ref[tap:tap + 1, :]
    convn_ref[:, 0:(SSD_CONV - 2) * cd] = conv_ref[:, cd:(SSD_CONV - 1) * cd]
    convn_ref[:, (SSD_CONV - 2) * cd:] = x
    xc = _silu(acc)
    xs = xc[:, :D_MODEL]
    nbc = SSD_GROUPS * SSD_STATE
    xs_ref[...] = xs
    bs_ref[...] = xc[:, D_MODEL:D_MODEL + nbc]
    cs_ref[...] = xc[:, D_MODEL + nbc:]
    delta16 = _softplus(dt_ref[...] + b16_ref[...])
    e16_ref[...] = jnp.exp(delta16 * (-jnp.exp(al16_ref[...])))
    vv_ref[...] = xs * _split_dot(delta16, eh_ref[...])
    cos2, sin2 = cos_ref[...], sin_ref[...]
    qr_ref[...] = jnp.concatenate(_rotary(q_ref[...], cos2, sin2), axis=-1)
    kr_ref[...] = jnp.concatenate(_rotary(k_ref[...], cos2, sin2), axis=-1) * (RET_QK_DIM ** -0.5)


def _sample_pre(proj2, dt2, conv2, cw, cb, b16, al16, eh, cos2, sin2):
    n = proj2.shape[0]
    qkw = RET_HEADS * RET_QK_DIM
    nbc = SSD_GROUPS * SSD_STATE
    full = lambda s: pl.BlockSpec(s, lambda i: (0, 0))
    col = lambda w, off: pl.BlockSpec((n, w), lambda i: (0, off // w))
    shp = lambda w: jax.ShapeDtypeStruct((n, w), F32)
    return pl.pallas_call(
        _sample_pre_kernel,
        grid=(1,),
        in_specs=[col(D_MODEL, OFF_XS), col(D_MODEL, OFF_BC), full((n, LANES)), col(qkw, OFF_Q), col(qkw, OFF_K),
                  full((n, (SSD_CONV - 1) * SSD_CONV_DIM)),
                  full((SSD_CONV, SSD_CONV_DIM)), full((1, SSD_CONV_DIM)),
                  full((1, LANES)), full((1, LANES)),
                  full((LANES, D_MODEL)), full((1, RET_QK_DIM)), full((1, RET_QK_DIM))],
        out_specs=[full((n, D_MODEL)), full((n, LANES)), full((n, nbc)), full((n, nbc)), full((n, D_MODEL)),
                   full((n, (SSD_CONV - 1) * SSD_CONV_DIM)), full((n, qkw)), full((n, qkw))],
        out_shape=[shp(D_MODEL), shp(LANES), shp(nbc), shp(nbc), shp(D_MODEL),
                   shp((SSD_CONV - 1) * SSD_CONV_DIM), shp(qkw), shp(qkw)],
        compiler_params=_cparams(("arbitrary",)),
        name="sample_pre",
    )(proj2, proj2, dt2, proj2, proj2, conv2, cw, cb, b16, al16, eh, cos2, sin2)


STATE_ROWS = D_MODEL
STATE_BLK = 64


def _state_step_kernel(e_ref, s_ref, v_ref, k_ref, q_ref, *rest, bb):
    sn_ref, y_ref = rest[-2:]
    i = pl.program_id(0)
    nblk = STATE_ROWS // STATE_BLK
    gw = STATE_ROWS // 4
    ones = jnp.ones((SUBLANES, LANES), BF16)
    zpad = jnp.zeros((LANES - SUBLANES, LANES), F32)
    nt = (((1,), (1,)), ((), ()))
    for t in range(bb):
        b = i * bb + t
        vt = jnp.concatenate([v_ref[t], zpad], axis=0).T
        kk = k_ref[t]
        qq = q_ref[t]
        for cch in range(STATE_ROWS // LANES):
            prods = []
            for half in range(LANES // STATE_BLK):
                blk = cch * (LANES // STATE_BLK) + half
                r0 = blk * STATE_BLK
                g = r0 // gw
                vcol = vt[half * STATE_BLK:(half + 1) * STATE_BLK, cch:cch + 1]
                sn = e_ref[b * nblk + blk] * s_ref[t, r0:r0 + STATE_BLK, :] + vcol * kk[g:g + 1, :]
                sn_ref[t, r0:r0 + STATE_BLK, :] = sn
                prods.append(sn * qq[g:g + 1, :])
            p = jnp.concatenate(prods, axis=0)
            hi = p.astype(BF16)
            lo = (p - hi.astype(F32)).astype(BF16)
            ysum = (lax.dot_general(ones, hi, nt, preferred_element_type=F32)
                    + lax.dot_general(ones, lo, nt, preferred_element_type=F32))
            y_ref[t, cch:cch + 1, :] = ysum[0:1, :]


def _state_step(e_flat, s_all, layer, v3, k3, q3, prev=None, bb=8):
    depth, n = s_all.shape[:2]
    blk3 = lambda a, b_: pl.BlockSpec((bb, a, b_), lambda i: (i, 0, 0))
    sblk = pl.BlockSpec((None, bb, STATE_ROWS, LANES), lambda i: (layer, i, 0, 0))
    in_specs = [pl.BlockSpec(memory_space=pltpu.SMEM), sblk, blk3(SUBLANES, LANES), blk3(4, LANES), blk3(4, LANES)]
    args = [e_flat, s_all, v3, k3, q3]
    aliases = {}
    if prev is not None:
        in_specs.append(pl.BlockSpec(memory_space=pl.ANY))
        args.append(prev)
        aliases = {len(args) - 1: 0}
    return pl.pallas_call(
        functools.partial(_state_step_kernel, bb=bb),
        grid=(n // bb,),
        in_specs=in_specs,
        out_specs=[sblk, blk3(SUBLANES, LANES)],
        out_shape=[jax.ShapeDtypeStruct((depth, n, STATE_ROWS, LANES), F32),
                   jax.ShapeDtypeStruct((n, SUBLANES, LANES), F32)],
        input_output_aliases=aliases,
        compiler_params=_cparams(("parallel",)),
        name="state_step",
    )(*args)


def _sample_post_kernel(ys_ref, xs_ref, z_ref, yr_ref, g_ref, dfull_ref, nw_ref, os_ref, or_ref):
    os_ref[...] = _ssd_gate_norm(ys_ref[...], xs_ref[...], z_ref[...], dfull_ref[...], nw_ref[...]).astype(os_ref.dtype)
    or_ref[...] = _ret_gate_norm(yr_ref[...], g_ref[...]).astype(or_ref.dtype)


def _sample_post(y_ssd, xs, proj2, y_ret, dfull, nw):
    n = y_ssd.shape[0]
    full = lambda s: pl.BlockSpec(s, lambda i: (0, 0))
    col = lambda off: pl.BlockSpec((n, D_MODEL), lambda i: (0, off // D_MODEL))
    return pl.pallas_call(
        _sample_post_kernel,
        grid=(1,),
        in_specs=[full((n, D_MODEL)), full((n, D_MODEL)), col(OFF_Z), full((n, D_MODEL)), col(OFF_G),
                  full((1, D_MODEL)), full((1, D_MODEL))],
        out_specs=[full((n, D_MODEL)), full((n, D_MODEL))],
        out_shape=[jax.ShapeDtypeStruct((n, D_MODEL), BF16)] * 2,
        compiler_params=_cparams(("arbitrary",)),
        name="sample_post",
    )(y_ssd, xs, proj2, y_ret, proj2, dfull, nw)


def _merge_kernel(s5_ref, ssd_ref, ret_ref, gates_ref, h_ref, ws5_ref, wssd_ref, wret_ref, wout_ref, o_ref):
    glu = _dot(s5_ref[...], ws5_ref[...])
    y_a = glu[:, :D_MODEL] * _sigmoid(glu[:, D_MODEL:])
    y_b = _dot(ssd_ref[...], wssd_ref[...])
    y_c = _dot(ret_ref[...], wret_ref[...])
    gate = _sigmoid(gates_ref[...])
    merged = (gate[:, :D_MODEL] * y_a + gate[:, D_MODEL:2 * D_MODEL] * y_b + gate[:, 2 * D_MODEL:] * y_c)
    o_ref[...] = h_ref[...] + _dot(merged.astype(BF16), wout_ref[...])


def _merge(s5_tm, ssd_n, ret_n, proj, h3, ws5, wssd, wret, wout, tm):
    nb, L, _ = h3.shape
    row3 = lambda w, off: pl.BlockSpec((None, tm, w), lambda b, i: (b, i, off // w))
    wfull = lambda s: pl.BlockSpec(s, lambda b, i: (0, 0))
    return pl.pallas_call(
        _merge_kernel,
        grid=(nb, L // tm),
        in_specs=[pl.BlockSpec((tm, D_MODEL), lambda b, i: (i, b)),
                  row3(D_MODEL, 0), row3(D_MODEL, 0), row3(3 * D_MODEL, OFF_GATES), row3(D_MODEL, 0),
                  wfull((D_MODEL, 2 * D_MODEL)), wfull((D_MODEL, D_MODEL)), wfull((D_MODEL, D_MODEL)),
                  wfull((D_MODEL, D_MODEL))],
        out_specs=row3(D_MODEL, 0),
        out_shape=jax.ShapeDtypeStruct((nb, L, D_MODEL), F32),
        compiler_params=_cparams(("parallel", "parallel")),
        name="merge",
    )(s5_tm, ssd_n, ret_n, proj, h3, ws5, wssd, wret, wout)


def _mlp_kernel(x_ref, nw_ref, wup_ref, wdn_ref, fw_ref, o_ref, xn_ref, acc_ref, *, final_norm):
    k = pl.program_id(1)
    nk = pl.num_programs(1)

    @pl.when(k == 0)
    def _():
        x = x_ref[...]
        xn_ref[...] = _rmsnorm(x, nw_ref[...]).astype(BF16)
        acc_ref[...] = x

    hid = jnp.maximum(_dot(xn_ref[...], wup_ref[...]), 0.0)
    acc_ref[...] += _dot((hid * hid).astype(BF16), wdn_ref[...])

    @pl.when(k == nk - 1)
    def _():
        h = acc_ref[...]
        o_ref[...] = _rmsnorm(h, fw_ref[...]) if final_norm else h


def _mlp(h2, nw, wup, wdn, fw, tm, tf, final_norm):
    rows = h2.shape[0]
    return pl.pallas_call(
        functools.partial(_mlp_kernel, final_norm=final_norm),
        grid=(rows // tm, D_FF // tf),
        in_specs=[pl.BlockSpec((tm, D_MODEL), lambda i, k: (i, 0)),
                  pl.BlockSpec((1, D_MODEL), lambda i, k: (0, 0)),
                  pl.BlockSpec((D_MODEL, tf), lambda i, k: (0, k)),
                  pl.BlockSpec((tf, D_MODEL), lambda i, k: (k, 0)),
                  pl.BlockSpec((1, D_MODEL), lambda i, k: (0, 0))],
        out_specs=pl.BlockSpec((tm, D_MODEL), lambda i, k: (i, 0)),
        out_shape=jax.ShapeDtypeStruct((rows, D_MODEL), F32),
        scratch_shapes=[pltpu.VMEM((tm, D_MODEL), BF16), pltpu.VMEM((tm, D_MODEL), F32)],
        compiler_params=_cparams(("parallel", "arbitrary")),
        name="mlp",
    )(h2, nw, wup, wdn, fw)


def _prep_layer(p):
    w_all = p['w_in'].astype(BF16)
    dt_start = SLAB_B_START - SSD_HEADS
    out = {}
    out['w_all'] = w_all
    out['w_b'] = w_all[:, SLAB_B_START:]
    out['w_dt'] = jnp.pad(w_all[:, dt_start:SLAB_B_START], ((0, 0), (0, LANES - SSD_HEADS)))
    row = lambda a: a.reshape(1, -1).astype(F32)
    padrow = lambda a: jnp.pad(a.astype(F32), (0, LANES - a.shape[0])).reshape(1, LANES)
    rep = lambda a: jnp.repeat(a.astype(F32), SSD_HEAD_DIM).reshape(1, D_MODEL)
    out['norm1_w'] = row(p['norm1_w'])
    out['norm2_w'] = row(p['norm2_w'])
    out['conv_w'] = p['ssd_conv_w'].astype(F32)
    out['conv_b'] = row(p['ssd_conv_b'])
    out['b16'] = padrow(p['ssd_dt_bias'])
    out['al16'] = padrow(p['ssd_a_log'])
    out['dfull'] = rep(p['ssd_d'])
    out['ssd_nw'] = row(p['ssd_norm_w'])
    out['s5_d'] = row(p['s5_d'])
    abr, abi, bbr, bbi = _s5_discretise(p['s5_a_re'].astype(F32), p['s5_a_im'].astype(F32),
                                        p['s5_log_dt'].astype(F32), p['s5_b_re'].astype(F32),
                                        p['s5_b_im'].astype(F32))
    out['s5'] = _s5_weights(abr, abi, bbr, bbi, p['s5_c_re'].astype(F32), p['s5_c_im'].astype(F32))
    for name in ('w_s5_out', 'w_ssd_out', 'w_ret_out', 'w_out', 'w_mlp_up', 'w_mlp_down'):
        out[name] = p[name].astype(BF16)
    return out


def _head_expand():
    e = (jnp.arange(LANES)[:, None] == (jnp.arange(D_MODEL)[None, :] // SSD_HEAD_DIM))
    return e.astype(BF16)


def _tile(n, pref):
    return pref if n % pref == 0 else n


def _prompt_trunk(x, layers, final_w):
    nb, L, _ = x.shape
    eh = _head_expand()
    _, lm, gp, we, gq = _ret_consts(CHUNK)
    cos2, sin2 = _rope_tables(jnp.arange(L, dtype=jnp.int32))
    zeros_state = jnp.zeros((nb, S5_NSTATE), F32)
    steps = 64
    h = x
    outs = {k: [] for k in ('re', 'im', 'ssd', 'conv', 'ret')}
    for li, lp in enumerate(layers):
        tm = _tile(L, 1024)
        proj, dt, u_tm = _proj_main(h, lp['norm1_w'], lp['w_all'], lp['w_b'], lp['w_dt'], tm)
        wbr, wbi, abr, abi, wcr, wci = lp['s5']
        y_s5, hr, hi = _s5_scan(u_tm.reshape(L * nb, D_MODEL), wbr, wbi, abr, abi, wcr, wci, lp['s5_d'],
                                zeros_state, zeros_state, nb, steps)
        ssd_n, ssd_s = _ssd_prompt(proj, dt, lp['conv_w'], lp['conv_b'], lp['b16'], lp['al16'],
                                   lp['dfull'], lp['ssd_nw'], eh)
        ret_n, ret_s = _ret_prompt(proj, cos2, sin2, lm, gp, we, gq)
        h = _merge(y_s5.reshape(L, nb * D_MODEL), ssd_n, ret_n, proj, h,
                   lp['w_s5_out'], lp['w_ssd_out'], lp['w_ret_out'], lp['w_out'], _tile(L, 256))
        last = li == len(layers) - 1
        h = _mlp(h.reshape(nb * L, D_MODEL), lp['norm2_w'], lp['w_mlp_up'], lp['w_mlp_down'], final_w,
                 _tile(nb * L, 512), 1024, last).reshape(nb, L, D_MODEL)
        outs['re'].append(hr.reshape(nb, S5_GROUPS, S5_STATE))
        outs['im'].append(hi.reshape(nb, S5_GROUPS, S5_STATE))
        outs['ssd'].append(ssd_s.reshape(nb, SSD_HEADS, SSD_HEAD_DIM, SSD_STATE))
        outs['conv'].append(proj[:, L - (SSD_CONV - 1):, OFF_XS:OFF_XS + SSD_CONV_DIM])
        outs['ret'].append(ret_s.reshape(nb, RET_HEADS, RET_V_DIM, RET_QK_DIM))
    return (h,) + tuple(jnp.stack(outs[k]) for k in ('re', 'im', 'ssd', 'conv', 'ret'))


def _sample_trunk(x, states, layers, final_w):
    n = x.shape[0]
    st_re, st_im, st_ssd, st_conv, st_ret = states
    eh = _head_expand()
    log_gamma, _, _, _, _ = _ret_consts(1)
    cos2, sin2 = _rope_tables(PAST_LEN + jnp.arange(1, dtype=jnp.int32))
    e_ret = jnp.broadcast_to(jnp.repeat(jnp.exp(log_gamma), RET_V_DIM // STATE_BLK)[None, :],
                             (n, STATE_ROWS // STATE_BLK)).reshape(-1)
    h = x.reshape(1, n, D_MODEL)
    depth = len(layers)
    ssd_all = st_ssd.reshape(depth, n, STATE_ROWS, SSD_STATE).astype(F32)
    ret_all = st_ret.reshape(depth, n, STATE_ROWS, RET_QK_DIM).astype(F32)
    ssd_new = ret_new = None
    outs = {k: [] for k in ('re', 'im', 'conv')}
    for li, lp in enumerate(layers):
        proj, dt, u = _proj_main(h, lp['norm1_w'], lp['w_all'], lp['w_b'], lp['w_dt'], n)
        wbr, wbi, abr, abi, wcr, wci = lp['s5']
        y_s5, hr, hi = _s5_scan(u, wbr, wbi, abr, abi, wcr, wci, lp['s5_d'],
                                st_re[li].reshape(n, S5_NSTATE).astype(F32),
                                st_im[li].reshape(n, S5_NSTATE).astype(F32), n, 1)
        proj2, dt2 = proj.reshape(n, N_MAIN), dt.reshape(n, LANES)
        conv2 = st_conv[li].reshape(n, (SSD_CONV - 1) * SSD_CONV_DIM).astype(F32)
        vv, e16, bs, cs, xs, conv_new, qr, kr = _sample_pre(
            proj2, dt2, conv2, lp['conv_w'], lp['conv_b'], lp['b16'], lp['al16'], eh, cos2, sin2)
        ssd_new, y_ssd = _state_step(
            e16[:, :SSD_HEADS].reshape(-1), ssd_all, li,
            vv.reshape(n, SUBLANES, LANES), bs.reshape(n, SSD_GROUPS, SSD_STATE),
            cs.reshape(n, SSD_GROUPS, SSD_STATE), prev=ssd_new)
        vret = proj2[:, OFF_V:OFF_V + D_MODEL]
        ret_new, y_ret = _state_step(
            e_ret, ret_all, li,
            vret.reshape(n, SUBLANES, LANES), kr.reshape(n, RET_HEADS, RET_QK_DIM),
            qr.reshape(n, RET_HEADS, RET_QK_DIM), prev=ret_new)
        ssd_n, ret_n = _sample_post(y_ssd.reshape(n, D_MODEL), xs, proj2, y_ret.reshape(n, D_MODEL),
                                    lp['dfull'], lp['ssd_nw'])
        h = _merge(y_s5, ssd_n.reshape(1, n, D_MODEL), ret_n.reshape(1, n, D_MODEL), proj, h,
                   lp['w_s5_out'], lp['w_ssd_out'], lp['w_ret_out'], lp['w_out'], n)
        last = li == len(layers) - 1
        h = _mlp(h.reshape(n, D_MODEL), lp['norm2_w'], lp['w_mlp_up'], lp['w_mlp_down'], final_w,
                 n, 1024, last).reshape(1, n, D_MODEL)
        outs['re'].append(hr.reshape(n, S5_GROUPS, S5_STATE))
        outs['im'].append(hi.reshape(n, S5_GROUPS, S5_STATE))
        outs['conv'].append(conv_new.reshape(n, SSD_CONV - 1, SSD_CONV_DIM))
    return (h.reshape(n, 1, D_MODEL), jnp.stack(outs['re']), jnp.stack(outs['im']),
            ssd_new.reshape(depth, n, SSD_HEADS, SSD_HEAD_DIM, SSD_STATE), jnp.stack(outs['conv']),
            ret_new.reshape(depth, n, RET_HEADS, RET_V_DIM, RET_QK_DIM))


def kernel(x_prompt, x_sample, state_s5_re, state_s5_im, state_ssd, state_conv, state_ret, norm1_w, w_in, s5_a_re, s5_a_im, s5_log_dt, s5_b_re, s5_b_im, s5_c_re, s5_c_im, s5_d, w_s5_out, ssd_conv_w, ssd_conv_b, ssd_dt_bias, ssd_a_log, ssd_d, ssd_norm_w, w_ssd_out, w_ret_out, w_out, norm2_w, w_mlp_up, w_mlp_down, final_norm_w):
    params = dict(norm1_w=norm1_w, w_in=w_in, s5_a_re=s5_a_re, s5_a_im=s5_a_im, s5_log_dt=s5_log_dt,
                  s5_b_re=s5_b_re, s5_b_im=s5_b_im, s5_c_re=s5_c_re, s5_c_im=s5_c_im, s5_d=s5_d,
                  w_s5_out=w_s5_out, ssd_conv_w=ssd_conv_w, ssd_conv_b=ssd_conv_b, ssd_dt_bias=ssd_dt_bias,
                  ssd_a_log=ssd_a_log, ssd_d=ssd_d, ssd_norm_w=ssd_norm_w, w_ssd_out=w_ssd_out,
                  w_ret_out=w_ret_out, w_out=w_out, norm2_w=norm2_w, w_mlp_up=w_mlp_up, w_mlp_down=w_mlp_down)
    depth = w_in.shape[0]
    layers = [_prep_layer({name: arr[l] for name, arr in params.items()}) for l in range(depth)]
    final_w = final_norm_w.reshape(1, D_MODEL).astype(F32)
    y_p, re_p, im_p, ssd_p, conv_p, ret_p = _prompt_trunk(x_prompt.astype(F32), layers, final_w)
    y_s, re_s, im_s, ssd_s, conv_s, ret_s = _sample_trunk(
        x_sample.astype(F32), (state_s5_re, state_s5_im, state_ssd, state_conv, state_ret), layers, final_w)
    return (y_p, y_s, re_p, im_p, ssd_p, conv_p, ret_p, re_s, im_s, ssd_s, conv_s, ret_s)
```

```python
import functools
import math

import jax
import jax.numpy as jnp
from jax import lax
from jax.experimental import pallas as pl
from jax.experimental.pallas import tpu as pltpu

F32 = jnp.float32
BF16 = jnp.bfloat16

D_MODEL = 1024
S5_GROUP = 16
S5_STATE = 64
S5_GROUPS = D_MODEL // S5_GROUP
S5_NSTATE = S5_GROUPS * S5_STATE
S5_SUPER = 8
SSD_HEADS = 16
SSD_HEAD_DIM = 64
SSD_GROUPS = 4
SSD_STATE = 128
SSD_CONV = 4
SSD_CONV_DIM = D_MODEL + 2 * SSD_GROUPS * SSD_STATE
RET_HEADS = 4
RET_QK_DIM = 128
RET_V_DIM = 256
ROPE_BASE = 10000.0
D_FF = 4 * D_MODEL
CHUNK = 128
EPS = 1e-6
PAST_LEN = 16384
LANES = 128
SUBLANES = 8
VMEM_LIMIT = 52 * 1024 * 1024

OFF_U, OFF_Z, OFF_XS, OFF_BC, OFF_Q, OFF_K, OFF_V, OFF_G, OFF_GATES = (
    0, 1024, 2048, 3072, 4096, 4608, 5120, 6144, 7168)
N_MAIN = 10240
SLAB_A_COLS = OFF_Q
SLAB_B_START = 2 * D_MODEL + SSD_CONV_DIM + SSD_HEADS
PROJ_TN = 1024


def _cparams(sem):
    return pltpu.CompilerParams(dimension_semantics=sem, vmem_limit_bytes=VMEM_LIMIT)


def _sigmoid(x):
    return 1.0 / (1.0 + jnp.exp(-x))


def _silu(x):
    return x * _sigmoid(x)


def _softplus(x):
    return jnp.maximum(x, 0.0) + jnp.log1p(jnp.exp(-jnp.abs(x)))


def _gelu_tanh(x):
    return 0.5 * x * (1.0 + jnp.tanh(math.sqrt(2.0 / math.pi) * (x + 0.044715 * (x * x * x))))


def _rmsnorm(x, w):
    return x * lax.rsqrt(jnp.mean(x * x, axis=-1, keepdims=True) + EPS) * w


def _cumsum_rows(x):
    n = x.shape[0]
    row = lax.broadcasted_iota(jnp.int32, x.shape, 0)
    k = 1
    while k < n:
        x = x + jnp.where(row >= k, pltpu.roll(x, k, 0), 0.0)
        k *= 2
    return x


def _group_rms(y, width):
    outs = []
    for s in range(0, y.shape[-1], width):
        yg = y[:, s:s + width]
        outs.append(yg * lax.rsqrt(jnp.mean(yg * yg, axis=-1, keepdims=True) + EPS))
    return jnp.concatenate(outs, axis=-1)


def _dot(a, b):
    return jnp.dot(a, b, preferred_element_type=F32)


def _split_dot(x, e):
    hi = x.astype(BF16)
    lo = (x - hi.astype(F32)).astype(BF16)
    return _dot(hi, e) + _dot(lo, e)


def _proj_kernel(x_ref, nw_ref, wa_ref, wb_ref, wdt_ref, o_ref, odt_ref, xn_ref, *, na):
    j = pl.program_id(2)

    @pl.when(j == 0)
    def _():
        xn = _rmsnorm(x_ref[...], nw_ref[...]).astype(BF16)
        xn_ref[...] = xn
        odt_ref[...] = _dot(xn, wdt_ref[...])

    @pl.when(j < na)
    def _():
        o_ref[...] = _dot(xn_ref[...], wa_ref[...])

    @pl.when(j >= na)
    def _():
        o_ref[...] = _dot(xn_ref[...], wb_ref[...])


def _proj_main(x3, nw, w_all, w_b, wdt, tm):
    nb, L, _ = x3.shape
    tn = PROJ_TN
    na = SLAB_A_COLS // tn
    return pl.pallas_call(
        functools.partial(_proj_kernel, na=na),
        grid=(nb, L // tm, N_MAIN // tn),
        in_specs=[pl.BlockSpec((None, tm, D_MODEL), lambda b, i, j: (b, i, 0)),
                  pl.BlockSpec((1, D_MODEL), lambda b, i, j: (0, 0)),
                  pl.BlockSpec((D_MODEL, tn), lambda b, i, j: (0, jnp.minimum(j, na - 1))),
                  pl.BlockSpec((D_MODEL, tn), lambda b, i, j: (0, jnp.maximum(j - na, 0))),
                  pl.BlockSpec((D_MODEL, LANES), lambda b, i, j: (0, 0))],
        out_specs=[pl.BlockSpec((None, tm, tn), lambda b, i, j: (b, i, j)),
                   pl.BlockSpec((None, tm, LANES), lambda b, i, j: (b, i, 0))],
        out_shape=[jax.ShapeDtypeStruct((nb, L, N_MAIN), F32), jax.ShapeDtypeStruct((nb, L, LANES), F32)],
        scratch_shapes=[pltpu.VMEM((tm, D_MODEL), BF16)],
        compiler_params=_cparams(("parallel", "parallel", "arbitrary")),
        name="proj_main",
    )(x3, nw, w_all, w_b, wdt)


def _s5_disc_kernel(are_ref, aim_ref, ldt_ref, bre_ref, bim_ref, abr_ref, abi_ref, bbr_ref, bbi_ref):
    ar, ai = are_ref[...], aim_ref[...]
    dt = jnp.exp(ldt_ref[...])
    mag = jnp.exp(ar * dt)
    abr = mag * jnp.cos(ai * dt)
    abi = mag * jnp.sin(ai * dt)
    abr_ref[...] = abr
    abi_ref[...] = abi
    nr, ni = abr - 1.0, abi
    den = 1.0 / (ar * ar + ai * ai)
    cr = (nr * ar + ni * ai) * den
    ci = (ni * ar - nr * ai) * den
    br, bi = bre_ref[...], bim_ref[...]
    bbr_ref[...] = cr * br - ci * bi
    bbi_ref[...] = cr * bi + ci * br


def _s5_discretise(a_re, a_im, log_dt, b_re, b_im):
    g, p = a_re.shape
    gs = b_re.shape[-1]
    full3 = lambda s: pl.BlockSpec(s, lambda: (0, 0, 0))
    return pl.pallas_call(
        _s5_disc_kernel,
        in_specs=[full3((g, 1, p)), full3((g, 1, p)), full3((g, 1, 1)), full3((g, gs, p)), full3((g, gs, p))],
        out_specs=[full3((g, 1, p)), full3((g, 1, p)), full3((g, gs, p)), full3((g, gs, p))],
        out_shape=[jax.ShapeDtypeStruct((g, 1, p), F32)] * 2 + [jax.ShapeDtypeStruct((g, gs, p), F32)] * 2,
        name="s5_discretise",
    )(a_re.reshape(g, 1, p), a_im.reshape(g, 1, p), log_dt.reshape(g, 1, 1),
      jnp.swapaxes(b_re, 1, 2), jnp.swapaxes(b_im, 1, 2))


def _s5_kernel(u_ref, wbr_ref, wbi_ref, ar_ref, ai_ref, wcr_ref, wci_ref, d_ref, h0r_ref, h0i_ref,
               y_ref, hr_ref, hi_ref, bur, bui, *, nbatch, steps, lane_w):
    @pl.when(pl.program_id(0) == 0)
    def _():
        hr_ref[...] = h0r_ref[...]
        hi_ref[...] = h0i_ref[...]

    cw = S5_SUPER * S5_GROUP
    sw = S5_SUPER * S5_STATE
    batch_major = steps > 1
    u = pltpu.einshape("btd->(tb)d", u_ref[...]) if batch_major else u_ref[...]
    ub = u.astype(BF16)
    for j in range(S5_GROUPS // S5_SUPER):
        uj = ub[:, j * cw:(j + 1) * cw]
        bur[:, j * sw:(j + 1) * sw] = _dot(uj, wbr_ref[j])
        bui[:, j * sw:(j + 1) * sw] = _dot(uj, wbi_ref[j])

    for lc in range(S5_NSTATE // lane_w):
        sl = slice(lc * lane_w, (lc + 1) * lane_w)
        are = jnp.broadcast_to(ar_ref[:, sl], (nbatch, lane_w))
        aim = jnp.broadcast_to(ai_ref[:, sl], (nbatch, lane_w))

        def body(t, carry, sl=sl, are=are, aim=aim):
            hr, hi = carry
            rows = pl.ds(pl.multiple_of(t * nbatch, SUBLANES), nbatch)
            nr = are * hr - aim * hi + bur[rows, sl]
            ni = are * hi + aim * hr + bui[rows, sl]
            bur[rows, sl] = nr
            bui[rows, sl] = ni
            return nr, ni

        hr, hi = lax.fori_loop(0, steps, body, (hr_ref[:, sl], hi_ref[:, sl]), unroll=min(steps, 4))
        hr_ref[:, sl] = hr
        hi_ref[:, sl] = hi

    for j in range(S5_GROUPS // S5_SUPER):
        yj = (_dot(bur[:, j * sw:(j + 1) * sw].astype(BF16), wcr_ref[j])
              + _dot(bui[:, j * sw:(j + 1) * sw].astype(BF16), wci_ref[j]))
        yj = _gelu_tanh(yj + d_ref[:, j * cw:(j + 1) * cw] * u[:, j * cw:(j + 1) * cw])
        if batch_major:
            y_ref[:, :, j * cw:(j + 1) * cw] = pltpu.einshape("(tb)d->btd", yj, b=nbatch).astype(y_ref.dtype)
        else:
            y_ref[:, j * cw:(j + 1) * cw] = yj.astype(y_ref.dtype)


def _s5_scan(proj, wbr, wbi, abr, abi, wcr, wci, d, h0r, h0i, steps):
    nb, L, _ = proj.shape
    nsup = S5_GROUPS // S5_SUPER
    cw, sw = S5_SUPER * S5_GROUP, S5_SUPER * S5_STATE
    c2 = lambda s: pl.BlockSpec(s, lambda c: (0, 0))
    c3 = lambda s: pl.BlockSpec(s, lambda c: (0, 0, 0))
    if steps > 1:
        nbatch, grid = nb, L // steps
        u_spec = pl.BlockSpec((nb, steps, D_MODEL), lambda c: (0, c, OFF_U // D_MODEL))
        y_spec = pl.BlockSpec((nb, steps, D_MODEL), lambda c: (0, c, 0))
    else:
        nbatch, grid = L, 1
        u_spec = pl.BlockSpec((None, L, D_MODEL), lambda c: (0, 0, OFF_U // D_MODEL))
        y_spec = pl.BlockSpec((None, L, D_MODEL), lambda c: (0, 0, 0))
    blk = nbatch * steps
    lane_w = max(LANES, S5_NSTATE // nbatch)
    kern = functools.partial(_s5_kernel, nbatch=nbatch, steps=steps, lane_w=lane_w)
    return pl.pallas_call(
        kern,
        grid=(grid,),
        in_specs=[u_spec,
                  c3((nsup, cw, sw)), c3((nsup, cw, sw)),
                  c2((1, S5_NSTATE)), c2((1, S5_NSTATE)),
                  c3((nsup, sw, cw)), c3((nsup, sw, cw)),
                  c2((1, D_MODEL)),
                  c2((nbatch, S5_NSTATE)), c2((nbatch, S5_NSTATE))],
        out_specs=[y_spec, c2((nbatch, S5_NSTATE)), c2((nbatch, S5_NSTATE))],
        out_shape=[jax.ShapeDtypeStruct((nb, L, D_MODEL), BF16),
                   jax.ShapeDtypeStruct((nbatch, S5_NSTATE), F32),
                   jax.ShapeDtypeStruct((nbatch, S5_NSTATE), F32)],
        scratch_shapes=[pltpu.VMEM((blk, S5_NSTATE), F32), pltpu.VMEM((blk, S5_NSTATE), F32)],
        compiler_params=_cparams(("arbitrary",)),
        name="s5_scan",
    )(proj, wbr, wbi, abr, abi, wcr, wci, d, h0r, h0i)


def _s5_weights(abr, abi, bbr, bbi, c_re, c_im):
    nsup = S5_GROUPS // S5_SUPER
    eye = jnp.eye(S5_SUPER, dtype=F32)

    def bexp(b):
        b = b.reshape(nsup, S5_SUPER, S5_GROUP, S5_STATE)
        return (b[:, :, :, None, :] * eye[None, :, None, :, None]).reshape(
            nsup, S5_SUPER * S5_GROUP, S5_SUPER * S5_STATE).astype(BF16)

    def cexp(c):
        c = jnp.swapaxes(c, 1, 2).reshape(nsup, S5_SUPER, S5_STATE, S5_GROUP)
        return (c[:, :, :, None, :] * eye[None, :, None, :, None]).reshape(
            nsup, S5_SUPER * S5_STATE, S5_SUPER * S5_GROUP).astype(BF16)

    return (bexp(bbr), bexp(bbi), abr.reshape(1, S5_NSTATE), abi.reshape(1, S5_NSTATE),
            cexp(c_re), cexp(-c_im))


def _ssd_gate_norm(y, xs, z, dfull, nw):
    y = (y + dfull * xs) * _silu(z)
    return _group_rms(y, D_MODEL // SSD_GROUPS) * nw


def _ssd_chunk_kernel(xs_ref, bc_ref, z_ref, dt_ref, cw_ref, cb_ref, b16_ref, al16_ref,
                      dfull_ref, nw_ref, eh_ref, o_ref, s_ref, xpad, st):
    c = pl.program_id(1)
    nc = pl.num_programs(1)
    q = CHUNK
    pad = SUBLANES

    @pl.when(c == 0)
    def _():
        xpad[0:pad, :] = jnp.zeros((pad, SSD_CONV_DIM), F32)
        st[...] = jnp.zeros(st.shape, F32)

    @pl.when(c > 0)
    def _():
        xpad[0:pad, :] = xpad[q:q + pad, :]

    xpad[pad:pad + q, :D_MODEL] = xs_ref[...]
    xpad[pad:pad + q, D_MODEL:] = bc_ref[...]
    acc = cb_ref[...] + xpad[pad - 3:pad - 3 + q, :] * cw_ref[0:1, :]
    for tap in range(1, SSD_CONV):
        acc = acc + xpad[pad - 3 + tap:pad - 3 + tap + q, :] * cw_ref[tap:tap + 1, :]
    xc = _silu(acc)
    xs = xc[:, :D_MODEL]
    nbc = SSD_GROUPS * SSD_STATE
    bs = xc[:, D_MODEL:D_MODEL + nbc]
    cs = xc[:, D_MODEL + nbc:]

    delta16 = _softplus(dt_ref[...] + b16_ref[...])
    acum16 = _cumsum_rows(delta16 * (-jnp.exp(al16_ref[...])))
    acum_t = acum16.T
    delta_t = delta16.T
    last16 = acum16[q - 1:q, :]
    ea_f = _split_dot(jnp.exp(acum16), eh_ref[...])
    dw_f = _split_dot(delta16 * jnp.exp(last16 - acum16), eh_ref[...])
    xw = (xs * dw_f).astype(BF16)

    ri = lax.broadcasted_iota(jnp.int32, (q, q), 0)
    ci = lax.broadcasted_iota(jnp.int32, (q, q), 1)
    causal = ri >= ci
    hw = D_MODEL // SSD_GROUPS
    hpg = SSD_HEADS // SSD_GROUPS
    xsb = xs.astype(BF16)
    ys = []
    for g in range(SSD_GROUPS):
        bg = bs[:, g * SSD_STATE:(g + 1) * SSD_STATE]
        cg = cs[:, g * SSD_STATE:(g + 1) * SSD_STATE].astype(BF16)
        bgt = bg.T.astype(BF16)
        scores = _dot(cg, bgt)
        sg = st[:, g * hw:(g + 1) * hw]
        y_inter = _dot(cg, sg.astype(BF16)) * ea_f[:, g * hw:(g + 1) * hw]
        for r in range(hpg):
            hd = g * hpg + r
            seg = acum16[:, hd:hd + 1] - acum_t[hd:hd + 1, :]
            lm = jnp.exp(jnp.where(causal, seg, -jnp.inf)) * delta_t[hd:hd + 1, :]
            p = (scores * lm).astype(BF16)
            ys.append(_dot(p, xsb[:, hd * SSD_HEAD_DIM:(hd + 1) * SSD_HEAD_DIM])
                      + y_inter[:, r * SSD_HEAD_DIM:(r + 1) * SSD_HEAD_DIM])
        st[:, g * hw:(g + 1) * hw] = (ea_f[q - 1:q, g * hw:(g + 1) * hw] * sg
                                      + _dot(bgt, xw[:, g * hw:(g + 1) * hw]))
    y = jnp.concatenate(ys, axis=-1)
    o_ref[...] = _ssd_gate_norm(y, xs, z_ref[...], dfull_ref[...], nw_ref[...]).astype(o_ref.dtype)

    @pl.when(c == nc - 1)
    def _():
        for k in range(D_MODEL // LANES):
            s_ref[k * LANES:(k + 1) * LANES, :] = st[:, k * LANES:(k + 1) * LANES].T


def _ssd_prompt(proj, dt, cw, cb, b16, al16, dfull, nw, eh):
    nb, L, _ = proj.shape
    q = CHUNK
    c2 = lambda s: pl.BlockSpec(s, lambda b, c: (0, 0))
    return pl.pallas_call(
        _ssd_chunk_kernel,
        grid=(nb, L // q),
        in_specs=[pl.BlockSpec((None, q, D_MODEL), lambda b, c: (b, c, OFF_XS // D_MODEL)),
                  pl.BlockSpec((None, q, D_MODEL), lambda b, c: (b, c, OFF_BC // D_MODEL)),
                  pl.BlockSpec((None, q, D_MODEL), lambda b, c: (b, c, OFF_Z // D_MODEL)),
                  pl.BlockSpec((None, q, LANES), lambda b, c: (b, c, 0)),
                  c2((SSD_CONV, SSD_CONV_DIM)), c2((1, SSD_CONV_DIM)),
                  c2((1, LANES)), c2((1, LANES)),
                  c2((1, D_MODEL)), c2((1, D_MODEL)),
                  c2((LANES, D_MODEL))],
        out_specs=[pl.BlockSpec((None, q, D_MODEL), lambda b, c: (b, c, 0)),
                   pl.BlockSpec((None, D_MODEL, SSD_STATE), lambda b, c: (b, 0, 0))],
        out_shape=[jax.ShapeDtypeStruct((nb, L, D_MODEL), BF16),
                   jax.ShapeDtypeStruct((nb, D_MODEL, SSD_STATE), F32)],
        scratch_shapes=[pltpu.VMEM((q + 2 * SUBLANES, SSD_CONV_DIM), F32),
                        pltpu.VMEM((SSD_STATE, D_MODEL), F32)],
        compiler_params=_cparams(("parallel", "arbitrary")),
        name="ssd_chunk",
    )(proj, proj, proj, dt, cw, cb, b16, al16, dfull, nw, eh)


def _rotary(x, cos2, sin2):
    outs = []
    for h in range(RET_HEADS):
        xh = x[:, h * RET_QK_DIM:(h + 1) * RET_QK_DIM]
        outs.append(xh * cos2 + pltpu.roll(xh, RET_QK_DIM // 2, 1) * sin2)
    return outs


def _ret_gate_norm(y, g):
    return _group_rms(y, RET_V_DIM) * _silu(g)


def _ret_chunk_kernel(q_ref, k_ref, v_ref, g_ref, cos_ref, sin_ref, lm_ref, gp_ref, we_ref, gq_ref,
                      o_ref, s_ref, st):
    c = pl.program_id(1)
    nc = pl.num_programs(1)

    @pl.when(c == 0)
    def _():
        st[...] = jnp.zeros(st.shape, F32)

    cos2, sin2 = cos_ref[...], sin_ref[...]
    qr = _rotary(q_ref[...], cos2, sin2)
    kr = _rotary(k_ref[...], cos2, sin2)
    v = v_ref[...]
    vw = (v * we_ref[...]).astype(BF16)
    vb = v.astype(BF16)
    ys = []
    for h in range(RET_HEADS):
        sl = slice(h * RET_V_DIM, (h + 1) * RET_V_DIM)
        qh = qr[h].astype(BF16)
        kht = (kr[h] * (RET_QK_DIM ** -0.5)).T.astype(BF16)
        p = (_dot(qh, kht) * lm_ref[h]).astype(BF16)
        sh = st[:, sl]
        ys.append(_dot(p, vb[:, sl]) + _dot(qh, sh.astype(BF16)) * gp_ref[:, sl])
        st[:, sl] = gq_ref[:, sl] * sh + _dot(kht, vw[:, sl])
    y = jnp.concatenate(ys, axis=-1)
    o_ref[...] = _ret_gate_norm(y, g_ref[...]).astype(o_ref.dtype)

    @pl.when(c == nc - 1)
    def _():
        for k in range(D_MODEL // LANES):
            s_ref[k * LANES:(k + 1) * LANES, :] = st[:, k * LANES:(k + 1) * LANES].T


def _ret_consts(q):
    hidx = jnp.arange(RET_HEADS, dtype=F32)
    log_gamma = jnp.log1p(-jnp.exp2(-5.0 - hidx))
    i = jnp.arange(q, dtype=F32)
    seg = (i[:, None] - i[None, :])[None] * log_gamma[:, None, None]
    lm = jnp.exp(jnp.where((i[:, None] >= i[None, :])[None], seg, -jnp.inf))
    rep = lambda t: jnp.repeat(t, RET_V_DIM, axis=-1)
    gp = rep(jnp.exp((i[:, None] + 1.0) * log_gamma[None, :]))
    we = rep(jnp.exp((q - 1.0 - i[:, None]) * log_gamma[None, :]))
    gq = rep(jnp.exp(q * log_gamma)[None, :])
    return log_gamma, lm, gp, we, gq


def _rope_tables(pos):
    half = RET_QK_DIM // 2
    inv = ROPE_BASE ** (-jnp.arange(half, dtype=F32) / half)
    ang = pos.astype(F32)[:, None] * inv[None]
    cos, sin = jnp.cos(ang), jnp.sin(ang)
    return jnp.concatenate([cos, cos], axis=-1), jnp.concatenate([-sin, sin], axis=-1)


def _ret_prompt(proj, cos2, sin2, lm, gp, we, gq):
    nb, L, _ = proj.shape
    q = CHUNK
    qkw = RET_HEADS * RET_QK_DIM
    c2 = lambda s: pl.BlockSpec(s, lambda b, c: (0, 0))
    return pl.pallas_call(
        _ret_chunk_kernel,
        grid=(nb, L // q),
        in_specs=[pl.BlockSpec((None, q, qkw), lambda b, c: (b, c, OFF_Q // qkw)),
                  pl.BlockSpec((None, q, qkw), lambda b, c: (b, c, OFF_K // qkw)),
                  pl.BlockSpec((None, q, D_MODEL), lambda b, c: (b, c, OFF_V // D_MODEL)),
                  pl.BlockSpec((None, q, D_MODEL), lambda b, c: (b, c, OFF_G // D_MODEL)),
                  pl.BlockSpec((q, RET_QK_DIM), lambda b, c: (c, 0)),
                  pl.BlockSpec((q, RET_QK_DIM), lambda b, c: (c, 0)),
                  pl.BlockSpec((RET_HEADS, q, q), lambda b, c: (0, 0, 0)),
                  c2((q, D_MODEL)), c2((q, D_MODEL)), c2((1, D_MODEL))],
        out_specs=[pl.BlockSpec((None, q, D_MODEL), lambda b, c: (b, c, 0)),
                   pl.BlockSpec((None, D_MODEL, RET_QK_DIM), lambda b, c: (b, 0, 0))],
        out_shape=[jax.ShapeDtypeStruct((nb, L, D_MODEL), BF16),
                   jax.ShapeDtypeStruct((nb, D_MODEL, RET_QK_DIM), F32)],
        scratch_shapes=[pltpu.VMEM((RET_QK_DIM, D_MODEL), F32)],
        compiler_params=_cparams(("parallel", "arbitrary")),
        name="ret_chunk",
    )(proj, proj, proj, proj, cos2, sin2, lm, gp, we, gq)


def _sample_pre_kernel(xsin_ref, bcin_ref, dt_ref, q_ref, k_ref, conv_ref, cw_ref, cb_ref, b16_ref, al16_ref,
                       eh_ref, cos_ref, sin_ref,
                       vv_ref, e16_ref, bs_ref, cs_ref, xs_ref, convn_ref, qr_ref, kr_ref):
    cd = SSD_CONV_DIM
    x = jnp.concatenate([xsin_ref[...], bcin_ref[...]], axis=-1)
    acc = cb_ref[...] + x * cw_ref[SSD_CONV - 1:SSD_CONV, :]
    for tap in range(SSD_CONV - 1):
        acc = acc + conv_ref[:, tap * cd:(tap + 1) * cd] * cw_ref[tap:tap + 1, :]
    convn_ref[:, 0:(SSD_CONV - 2) * cd] = conv_ref[:, cd:(SSD_CONV - 1) * cd]
    convn_ref[:, (SSD_CONV - 2) * cd:] = x
    xc = _silu(acc)
    xs = xc[:, :D_MODEL]
    nbc = SSD_GROUPS * SSD_STATE
    xs_ref[...] = xs
    bs_ref[...] = xc[:, D_MODEL:D_MODEL + nbc]
    cs_ref[...] = xc[:, D_MODEL + nbc:]
    delta16 = _softplus(dt_ref[...] + b16_ref[...])
    e16_ref[...] = jnp.exp(delta16 * (-jnp.exp(al16_ref[...])))
    vv_ref[...] = xs * _split_dot(delta16, eh_ref[...])
    cos2, sin2 = cos_ref[...], sin_ref[...]
    qr_ref[...] = jnp.concatenate(_rotary(q_ref[...], cos2, sin2), axis=-1)
    kr_ref[...] = jnp.concatenate(_rotary(k_ref[...], cos2, sin2), axis=-1) * (RET_QK_DIM ** -0.5)


def _sample_pre(proj2, dt2, conv2, cw, cb, b16, al16, eh, cos2, sin2):
    n = proj2.shape[0]
    qkw = RET_HEADS * RET_QK_DIM
    nbc = SSD_GROUPS * SSD_STATE
    full = lambda s: pl.BlockSpec(s, lambda i: (0, 0))
    col = lambda w, off: pl.BlockSpec((n, w), lambda i: (0, off // w))
    shp = lambda w: jax.ShapeDtypeStruct((n, w), F32)
    return pl.pallas_call(
        _sample_pre_kernel,
        grid=(1,),
        in_specs=[col(D_MODEL, OFF_XS), col(D_MODEL, OFF_BC), full((n, LANES)), col(qkw, OFF_Q), col(qkw, OFF_K),
                  full((n, (SSD_CONV - 1) * SSD_CONV_DIM)),
                  full((SSD_CONV, SSD_CONV_DIM)), full((1, SSD_CONV_DIM)),
                  full((1, LANES)), full((1, LANES)),
                  full((LANES, D_MODEL)), full((1, RET_QK_DIM)), full((1, RET_QK_DIM))],
        out_specs=[full((n, D_MODEL)), full((n, LANES)), full((n, nbc)), full((n, nbc)), full((n, D_MODEL)),
                   full((n, (SSD_CONV - 1) * SSD_CONV_DIM)), full((n, qkw)), full((n, qkw))],
        out_shape=[shp(D_MODEL), shp(LANES), shp(nbc), shp(nbc), shp(D_MODEL),
                   shp((SSD_CONV - 1) * SSD_CONV_DIM), shp(qkw), shp(qkw)],
        compiler_params=_cparams(("arbitrary",)),
        name="sample_pre",
    )(proj2, proj2, dt2, proj2, proj2, conv2, cw, cb, b16, al16, eh, cos2, sin2)


STATE_ROWS = D_MODEL
STATE_BLK = 64


def _state_step_kernel(e_ref, s_ref, v_ref, k_ref, q_ref, *rest, bb, layer, fill):
    sn_all, y_ref = rest[-2:]
    if fill:
        for l in range(sn_all.shape[0]):
            if l != layer:
                sn_all[l] = jnp.zeros(sn_all.shape[1:], F32)
        sn_ref = sn_all.at[layer]
    else:
        sn_ref = sn_all
    i = pl.program_id(0)
    nblk = STATE_ROWS // STATE_BLK
    gw = STATE_ROWS // 4
    ones = jnp.ones((SUBLANES, LANES), BF16)
    zpad = jnp.zeros((LANES - SUBLANES, LANES), F32)
    nt = (((1,), (1,)), ((), ()))
    for t in range(bb):
        b = i * bb + t
        vt = jnp.concatenate([v_ref[t], zpad], axis=0).T
        kk = k_ref[t]
        qq = q_ref[t]
        for cch in range(STATE_ROWS // LANES):
            prods = []
            for half in range(LANES // STATE_BLK):
                blk = cch * (LANES // STATE_BLK) + half
                r0 = blk * STATE_BLK
                g = r0 // gw
                vcol = vt[half * STATE_BLK:(half + 1) * STATE_BLK, cch:cch + 1]
                sn = e_ref[b * nblk + blk] * s_ref[t, r0:r0 + STATE_BLK, :] + vcol * kk[g:g + 1, :]
                sn_ref[t, r0:r0 + STATE_BLK, :] = sn
                prods.append(sn * qq[g:g + 1, :])
            p = jnp.concatenate(prods, axis=0)
            hi = p.astype(BF16)
            lo = (p - hi.astype(F32)).astype(BF16)
            ysum = (lax.dot_general(ones, hi, nt, preferred_element_type=F32)
                    + lax.dot_general(ones, lo, nt, preferred_element_type=F32))
            y_ref[t, cch:cch + 1, :] = ysum[0:1, :]


def _state_step(e_flat, s_all, layer, v3, k3, q3, prev=None, bb=8):
    depth, n = s_all.shape[:2]
    blk3 = lambda a, b_: pl.BlockSpec((bb, a, b_), lambda i: (i, 0, 0))
    sblk = pl.BlockSpec((None, bb, STATE_ROWS, LANES), lambda i: (layer, i, 0, 0))
    in_specs = [pl.BlockSpec(memory_space=pltpu.SMEM), sblk, blk3(SUBLANES, LANES), blk3(4, LANES), blk3(4, LANES)]
    args = [e_flat, s_all, v3, k3, q3]
    aliases = {}
    if prev is not None:
        in_specs.append(pl.BlockSpec(memory_space=pl.ANY))
        args.append(prev)
        aliases = {len(args) - 1: 0}
        oblk = sblk
    else:
        oblk = pl.BlockSpec((depth, bb, STATE_ROWS, LANES), lambda i: (0, i, 0, 0))
    return pl.pallas_call(
        functools.partial(_state_step_kernel, bb=bb, layer=layer, fill=prev is None),
        grid=(n // bb,),
        in_specs=in_specs,
        out_specs=[oblk, blk3(SUBLANES, LANES)],
        out_shape=[jax.ShapeDtypeStruct((depth, n, STATE_ROWS, LANES), F32),
                   jax.ShapeDtypeStruct((n, SUBLANES, LANES), F32)],
        input_output_aliases=aliases,
        compiler_params=_cparams(("parallel",)),
        name="state_step",
    )(*args)


def _sample_post_kernel(ys_ref, xs_ref, z_ref, yr_ref, g_ref, dfull_ref, nw_ref, os_ref, or_ref):
    os_ref[...] = _ssd_gate_norm(ys_ref[...], xs_ref[...], z_ref[...], dfull_ref[...], nw_ref[...]).astype(os_ref.dtype)
    or_ref[...] = _ret_gate_norm(yr_ref[...], g_ref[...]).astype(or_ref.dtype)


def _sample_post(y_ssd, xs, proj2, y_ret, dfull, nw):
    n = y_ssd.shape[0]
    full = lambda s: pl.BlockSpec(s, lambda i: (0, 0))
    col = lambda off: pl.BlockSpec((n, D_MODEL), lambda i: (0, off // D_MODEL))
    return pl.pallas_call(
        _sample_post_kernel,
        grid=(1,),
        in_specs=[full((n, D_MODEL)), full((n, D_MODEL)), col(OFF_Z), full((n, D_MODEL)), col(OFF_G),
                  full((1, D_MODEL)), full((1, D_MODEL))],
        out_specs=[full((n, D_MODEL)), full((n, D_MODEL))],
        out_shape=[jax.ShapeDtypeStruct((n, D_MODEL), BF16)] * 2,
        compiler_params=_cparams(("arbitrary",)),
        name="sample_post",
    )(y_ssd, xs, proj2, y_ret, proj2, dfull, nw)


def _merge_kernel(s5_ref, ssd_ref, ret_ref, ga_ref, gb_ref, gc_ref, h_ref, ws5_ref, wssd_ref, wret_ref, wout_ref,
                  o_ref):
    glu = _dot(s5_ref[...], ws5_ref[...])
    y_a = glu[:, :D_MODEL] * _sigmoid(glu[:, D_MODEL:])
    y_b = _dot(ssd_ref[...], wssd_ref[...])
    y_c = _dot(ret_ref[...], wret_ref[...])
    merged = _sigmoid(ga_ref[...]) * y_a + _sigmoid(gb_ref[...]) * y_b + _sigmoid(gc_ref[...]) * y_c
    o_ref[...] = h_ref[...] + _dot(merged.astype(BF16), wout_ref[...])


def _merge(s5_g, ssd_n, ret_n, proj, h3, ws5, wssd, wret, wout, tm):
    nb, L, _ = h3.shape
    row3 = lambda off: pl.BlockSpec((None, tm, D_MODEL), lambda b, i: (b, i, off // D_MODEL))
    wfull = lambda s: pl.BlockSpec(s, lambda b, i: (0, 0))
    return pl.pallas_call(
        _merge_kernel,
        grid=(nb, L // tm),
        in_specs=[row3(0), row3(0), row3(0),
                  row3(OFF_GATES), row3(OFF_GATES + D_MODEL), row3(OFF_GATES + 2 * D_MODEL), row3(0),
                  wfull((D_MODEL, 2 * D_MODEL)), wfull((D_MODEL, D_MODEL)), wfull((D_MODEL, D_MODEL)),
                  wfull((D_MODEL, D_MODEL))],
        out_specs=row3(0),
        out_shape=jax.ShapeDtypeStruct((nb, L, D_MODEL), F32),
        compiler_params=_cparams(("parallel", "parallel")),
        name="merge",
    )(s5_g, ssd_n, ret_n, proj, proj, proj, h3, ws5, wssd, wret, wout)


def _mlp_kernel(x_ref, nw_ref, wup_ref, wdn_ref, fw_ref, o_ref, xn_ref, acc_ref, *, final_norm):
    k = pl.program_id(1)
    nk = pl.num_programs(1)

    @pl.when(k == 0)
    def _():
        x = x_ref[...]
        xn_ref[...] = _rmsnorm(x, nw_ref[...]).astype(BF16)
        acc_ref[...] = x

    hid = jnp.maximum(_dot(xn_ref[...], wup_ref[...]), 0.0)
    acc_ref[...] += _dot((hid * hid).astype(BF16), wdn_ref[...])

    @pl.when(k == nk - 1)
    def _():
        h = acc_ref[...]
        o_ref[...] = _rmsnorm(h, fw_ref[...]) if final_norm else h


def _mlp(h2, nw, wup, wdn, fw, tm, tf, final_norm):
    rows = h2.shape[0]
    return pl.pallas_call(
        functools.partial(_mlp_kernel, final_norm=final_norm),
        grid=(rows // tm, D_FF // tf),
        in_specs=[pl.BlockSpec((tm, D_MODEL), lambda i, k: (i, 0)),
                  pl.BlockSpec((1, D_MODEL), lambda i, k: (0, 0)),
                  pl.BlockSpec((D_MODEL, tf), lambda i, k: (0, k)),
                  pl.BlockSpec((tf, D_MODEL), lambda i, k: (k, 0)),
                  pl.BlockSpec((1, D_MODEL), lambda i, k: (0, 0))],
        out_specs=pl.BlockSpec((tm, D_MODEL), lambda i, k: (i, 0)),
        out_shape=jax.ShapeDtypeStruct((rows, D_MODEL), F32),
        scratch_shapes=[pltpu.VMEM((tm, D_MODEL), BF16), pltpu.VMEM((tm, D_MODEL), F32)],
        compiler_params=_cparams(("parallel", "arbitrary")),
        name="mlp",
    )(h2, nw, wup, wdn, fw)


def _prep_layer(p):
    w_all = p['w_in'].astype(BF16)
    dt_start = SLAB_B_START - SSD_HEADS
    out = {}
    out['w_all'] = w_all
    out['w_b'] = w_all[:, SLAB_B_START:]
    out['w_dt'] = jnp.pad(w_all[:, dt_start:SLAB_B_START], ((0, 0), (0, LANES - SSD_HEADS)))
    row = lambda a: a.reshape(1, -1).astype(F32)
    padrow = lambda a: jnp.pad(a.astype(F32), (0, LANES - a.shape[0])).reshape(1, LANES)
    rep = lambda a: jnp.repeat(a.astype(F32), SSD_HEAD_DIM).reshape(1, D_MODEL)
    out['norm1_w'] = row(p['norm1_w'])
    out['norm2_w'] = row(p['norm2_w'])
    out['conv_w'] = p['ssd_conv_w'].astype(F32)
    out['conv_b'] = row(p['ssd_conv_b'])
    out['b16'] = padrow(p['ssd_dt_bias'])
    out['al16'] = padrow(p['ssd_a_log'])
    out['dfull'] = rep(p['ssd_d'])
    out['ssd_nw'] = row(p['ssd_norm_w'])
    out['s5_d'] = row(p['s5_d'])
    abr, abi, bbr, bbi = _s5_discretise(p['s5_a_re'].astype(F32), p['s5_a_im'].astype(F32),
                                        p['s5_log_dt'].astype(F32), p['s5_b_re'].astype(F32),
                                        p['s5_b_im'].astype(F32))
    out['s5'] = _s5_weights(abr, abi, bbr, bbi, p['s5_c_re'].astype(F32), p['s5_c_im'].astype(F32))
    for name in ('w_s5_out', 'w_ssd_out', 'w_ret_out', 'w_out', 'w_mlp_up', 'w_mlp_down'):
        out[name] = p[name].astype(BF16)
    return out


def _head_expand():
    e = (jnp.arange(LANES)[:, None] == (jnp.arange(D_MODEL)[None, :] // SSD_HEAD_DIM))
    return e.astype(BF16)


def _tile(n, pref):
    return pref if n % pref == 0 else n


def _prompt_trunk(x, layers, final_w):
    nb, L, _ = x.shape
    eh = _head_expand()
    _, lm, gp, we, gq = _ret_consts(CHUNK)
    cos2, sin2 = _rope_tables(jnp.arange(L, dtype=jnp.int32))
    zeros_state = jnp.zeros((nb, S5_NSTATE), F32)
    steps = 64
    h = x
    outs = {k: [] for k in ('re', 'im', 'ssd', 'conv', 'ret')}
    for li, lp in enumerate(layers):
        tm = _tile(L, 1024)
        proj, dt = _proj_main(h, lp['norm1_w'], lp['w_all'], lp['w_b'], lp['w_dt'], tm)
        wbr, wbi, abr, abi, wcr, wci = lp['s5']
        y_s5, hr, hi = _s5_scan(proj, wbr, wbi, abr, abi, wcr, wci, lp['s5_d'],
                                zeros_state, zeros_state, steps)
        ssd_n, ssd_s = _ssd_prompt(proj, dt, lp['conv_w'], lp['conv_b'], lp['b16'], lp['al16'],
                                   lp['dfull'], lp['ssd_nw'], eh)
        ret_n, ret_s = _ret_prompt(proj, cos2, sin2, lm, gp, we, gq)
        h = _merge(y_s5, ssd_n, ret_n, proj, h,
                   lp['w_s5_out'], lp['w_ssd_out'], lp['w_ret_out'], lp['w_out'], _tile(L, 256))
        last = li == len(layers) - 1
        h = _mlp(h.reshape(nb * L, D_MODEL), lp['norm2_w'], lp['w_mlp_up'], lp['w_mlp_down'], final_w,
                 _tile(nb * L, 512), 1024, last).reshape(nb, L, D_MODEL)
        outs['re'].append(hr.reshape(nb, S5_GROUPS, S5_STATE))
        outs['im'].append(hi.reshape(nb, S5_GROUPS, S5_STATE))
        outs['ssd'].append(ssd_s.reshape(nb, SSD_HEADS, SSD_HEAD_DIM, SSD_STATE))
        outs['conv'].append(proj[:, L - (SSD_CONV - 1):, OFF_XS:OFF_XS + SSD_CONV_DIM])
        outs['ret'].append(ret_s.reshape(nb, RET_HEADS, RET_V_DIM, RET_QK_DIM))
    return (h,) + tuple(jnp.stack(outs[k]) for k in ('re', 'im', 'ssd', 'conv', 'ret'))


def _sample_trunk(x, states, layers, final_w):
    n = x.shape[0]
    st_re, st_im, st_ssd, st_conv, st_ret = states
    eh = _head_expand()
    log_gamma, _, _, _, _ = _ret_consts(1)
    cos2, sin2 = _rope_tables(PAST_LEN + jnp.arange(1, dtype=jnp.int32))
    e_ret = jnp.broadcast_to(jnp.repeat(jnp.exp(log_gamma), RET_V_DIM // STATE_BLK)[None, :],
                             (n, STATE_ROWS // STATE_BLK)).reshape(-1)
    h = x.reshape(1, n, D_MODEL)
    depth = len(layers)
    ssd_all = st_ssd.reshape(depth, n, STATE_ROWS, SSD_STATE).astype(F32)
    ret_all = st_ret.reshape(depth, n, STATE_ROWS, RET_QK_DIM).astype(F32)
    ssd_new = ret_new = None
    outs = {k: [] for k in ('re', 'im', 'conv')}
    for li, lp in enumerate(layers):
        proj, dt = _proj_main(h, lp['norm1_w'], lp['w_all'], lp['w_b'], lp['w_dt'], n)
        wbr, wbi, abr, abi, wcr, wci = lp['s5']
        y_s5, hr, hi = _s5_scan(proj, wbr, wbi, abr, abi, wcr, wci, lp['s5_d'],
                                st_re[li].reshape(n, S5_NSTATE).astype(F32),
                                st_im[li].reshape(n, S5_NSTATE).astype(F32), 1)
        proj2, dt2 = proj.reshape(n, N_MAIN), dt.reshape(n, LANES)
        conv2 = st_conv[li].reshape(n, (SSD_CONV - 1) * SSD_CONV_DIM).astype(F32)
        vv, e16, bs, cs, xs, conv_new, qr, kr = _sample_pre(
            proj2, dt2, conv2, lp['conv_w'], lp['conv_b'], lp['b16'], lp['al16'], eh, cos2, sin2)
        ssd_new, y_ssd = _state_step(
            e16[:, :SSD_HEADS].reshape(-1), ssd_all, li,
            vv.reshape(n, SUBLANES, LANES), bs.reshape(n, SSD_GROUPS, SSD_STATE),
            cs.reshape(n, SSD_GROUPS, SSD_STATE), prev=ssd_new)
        vret = proj2[:, OFF_V:OFF_V + D_MODEL]
        ret_new, y_ret = _state_step(
            e_ret, ret_all, li,
            vret.reshape(n, SUBLANES, LANES), kr.reshape(n, RET_HEADS, RET_QK_DIM),
            qr.reshape(n, RET_HEADS, RET_QK_DIM), prev=ret_new)
        ssd_n, ret_n = _sample_post(y_ssd.reshape(n, D_MODEL), xs, proj2, y_ret.reshape(n, D_MODEL),
                                    lp['dfull'], lp['ssd_nw'])
        h = _merge(y_s5, ssd_n.reshape(1, n, D_MODEL), ret_n.reshape(1, n, D_MODEL), proj, h,
                   lp['w_s5_out'], lp['w_ssd_out'], lp['w_ret_out'], lp['w_out'], n)
        last = li == len(layers) - 1
        h = _mlp(h.reshape(n, D_MODEL), lp['norm2_w'], lp['w_mlp_up'], lp['w_mlp_down'], final_w,
                 n, 1024, last).reshape(1, n, D_MODEL)
        outs['re'].append(hr.reshape(n, S5_GROUPS, S5_STATE))
        outs['im'].append(hi.reshape(n, S5_GROUPS, S5_STATE))
        outs['conv'].append(conv_new.reshape(n, SSD_CONV - 1, SSD_CONV_DIM))
    return (h.reshape(n, 1, D_MODEL), jnp.stack(outs['re']), jnp.stack(outs['im']),
            ssd_new.reshape(depth, n, SSD_HEADS, SSD_HEAD_DIM, SSD_STATE), jnp.stack(outs['conv']),
            ret_new.reshape(depth, n, RET_HEADS, RET_V_DIM, RET_QK_DIM))


def kernel(x_prompt, x_sample, state_s5_re, state_s5_im, state_ssd, state_conv, state_ret, norm1_w, w_in, s5_a_re, s5_a_im, s5_log_dt, s5_b_re, s5_b_im, s5_c_re, s5_c_im, s5_d, w_s5_out, ssd_conv_w, ssd_conv_b, ssd_dt_bias, ssd_a_log, ssd_d, ssd_norm_w, w_ssd_out, w_ret_out, w_out, norm2_w, w_mlp_up, w_mlp_down, final_norm_w):
    params = dict(norm1_w=norm1_w, w_in=w_in, s5_a_re=s5_a_re, s5_a_im=s5_a_im, s5_log_dt=s5_log_dt,
                  s5_b_re=s5_b_re, s5_b_im=s5_b_im, s5_c_re=s5_c_re, s5_c_im=s5_c_im, s5_d=s5_d,
                  w_s5_out=w_s5_out, ssd_conv_w=ssd_conv_w, ssd_conv_b=ssd_conv_b, ssd_dt_bias=ssd_dt_bias,
                  ssd_a_log=ssd_a_log, ssd_d=ssd_d, ssd_norm_w=ssd_norm_w, w_ssd_out=w_ssd_out,
                  w_ret_out=w_ret_out, w_out=w_out, norm2_w=norm2_w, w_mlp_up=w_mlp_up, w_mlp_down=w_mlp_down)
    depth = w_in.shape[0]
    layers = [_prep_layer({name: arr[l] for name, arr in params.items()}) for l in range(depth)]
    final_w = final_norm_w.reshape(1, D_MODEL).astype(F32)
    y_p, re_p, im_p, ssd_p, conv_p, ret_p = _prompt_trunk(x_prompt.astype(F32), layers, final_w)
    y_s, re_s, im_s, ssd_s, conv_s, ret_s = _sample_trunk(
        x_sample.astype(F32), (state_s5_re, state_s5_im, state_ssd, state_conv, state_ret), layers, final_w)
    return (y_p, y_s, re_p, im_p, ssd_p, conv_p, ret_p, re_s, im_s, ssd_s, conv_s, ret_s)
```

```python
import functools
import math

import jax
import jax.numpy as jnp
from jax import lax
from jax.experimental import pallas as pl
from jax.experimental.pallas import tpu as pltpu

F32 = jnp.float32
BF16 = jnp.bfloat16

D_MODEL = 1024
S5_GROUP = 16
S5_STATE = 64
S5_GROUPS = D_MODEL // S5_GROUP
S5_NSTATE = S5_GROUPS * S5_STATE
S5_SUPER = 8
SSD_HEADS = 16
SSD_HEAD_DIM = 64
SSD_GROUPS = 4
SSD_STATE = 128
SSD_CONV = 4
SSD_CONV_DIM = D_MODEL + 2 * SSD_GROUPS * SSD_STATE
RET_HEADS = 4
RET_QK_DIM = 128
RET_V_DIM = 256
ROPE_BASE = 10000.0
D_FF = 4 * D_MODEL
CHUNK = 128
EPS = 1e-6
PAST_LEN = 16384
LANES = 128
SUBLANES = 8
VMEM_LIMIT = 52 * 1024 * 1024

OFF_U, OFF_Z, OFF_XS, OFF_BC, OFF_Q, OFF_K, OFF_V, OFF_G, OFF_GATES = (
    0, 1024, 2048, 3072, 4096, 4608, 5120, 6144, 7168)
N_MAIN = 10240
SLAB_A_COLS = OFF_Q
SLAB_B_START = 2 * D_MODEL + SSD_CONV_DIM + SSD_HEADS
PROJ_TN = 1024


def _cparams(sem):
    return pltpu.CompilerParams(dimension_semantics=sem, vmem_limit_bytes=VMEM_LIMIT)


def _sigmoid(x):
    return 1.0 / (1.0 + jnp.exp(-x))


def _silu(x):
    return x * _sigmoid(x)


def _softplus(x):
    return jnp.maximum(x, 0.0) + jnp.log1p(jnp.exp(-jnp.abs(x)))


def _gelu_tanh(x):
    return 0.5 * x * (1.0 + jnp.tanh(math.sqrt(2.0 / math.pi) * (x + 0.044715 * (x * x * x))))


def _rmsnorm(x, w):
    return x * lax.rsqrt(jnp.mean(x * x, axis=-1, keepdims=True) + EPS) * w


def _cumsum_rows(x):
    n = x.shape[0]
    row = lax.broadcasted_iota(jnp.int32, x.shape, 0)
    k = 1
    while k < n:
        x = x + jnp.where(row >= k, pltpu.roll(x, k, 0), 0.0)
        k *= 2
    return x


def _group_rms(y, width):
    outs = []
    for s in range(0, y.shape[-1], width):
        yg = y[:, s:s + width]
        outs.append(yg * lax.rsqrt(jnp.mean(yg * yg, axis=-1, keepdims=True) + EPS))
    return jnp.concatenate(outs, axis=-1)


def _dot(a, b):
    return jnp.dot(a, b, preferred_element_type=F32)


def _split_dot(x, e):
    hi = x.astype(BF16)
    lo = (x - hi.astype(F32)).astype(BF16)
    return _dot(hi, e) + _dot(lo, e)


def _proj_kernel(x_ref, nw_ref, wa_ref, wb_ref, wdt_ref, o_ref, odt_ref, xn_ref, *, na):
    j = pl.program_id(2)

    @pl.when(j == 0)
    def _():
        xn = _rmsnorm(x_ref[...], nw_ref[...]).astype(BF16)
        xn_ref[...] = xn
        odt_ref[...] = _dot(xn, wdt_ref[...])

    @pl.when(j < na)
    def _():
        o_ref[...] = _dot(xn_ref[...], wa_ref[...]).astype(o_ref.dtype)

    @pl.when(j >= na)
    def _():
        o_ref[...] = _dot(xn_ref[...], wb_ref[...]).astype(o_ref.dtype)


def _proj_main(x3, nw, w_all, w_b, wdt, tm):
    nb, L, _ = x3.shape
    tn = PROJ_TN
    na = SLAB_A_COLS // tn
    return pl.pallas_call(
        functools.partial(_proj_kernel, na=na),
        grid=(nb, L // tm, N_MAIN // tn),
        in_specs=[pl.BlockSpec((None, tm, D_MODEL), lambda b, i, j: (b, i, 0)),
                  pl.BlockSpec((1, D_MODEL), lambda b, i, j: (0, 0)),
                  pl.BlockSpec((D_MODEL, tn), lambda b, i, j: (0, jnp.minimum(j, na - 1))),
                  pl.BlockSpec((D_MODEL, tn), lambda b, i, j: (0, jnp.maximum(j - na, 0))),
                  pl.BlockSpec((D_MODEL, LANES), lambda b, i, j: (0, 0))],
        out_specs=[pl.BlockSpec((None, tm, tn), lambda b, i, j: (b, i, j)),
                   pl.BlockSpec((None, tm, LANES), lambda b, i, j: (b, i, 0))],
        out_shape=[jax.ShapeDtypeStruct((nb, L, N_MAIN), BF16), jax.ShapeDtypeStruct((nb, L, LANES), F32)],
        scratch_shapes=[pltpu.VMEM((tm, D_MODEL), BF16)],
        compiler_params=_cparams(("parallel", "parallel", "arbitrary")),
        name="proj_main",
    )(x3, nw, w_all, w_b, wdt)


def _s5_disc_kernel(are_ref, aim_ref, ldt_ref, bre_ref, bim_ref, abr_ref, abi_ref, bbr_ref, bbi_ref):
    ar, ai = are_ref[...], aim_ref[...]
    dt = jnp.exp(ldt_ref[...])
    mag = jnp.exp(ar * dt)
    abr = mag * jnp.cos(ai * dt)
    abi = mag * jnp.sin(ai * dt)
    abr_ref[...] = abr
    abi_ref[...] = abi
    nr, ni = abr - 1.0, abi
    den = 1.0 / (ar * ar + ai * ai)
    cr = (nr * ar + ni * ai) * den
    ci = (ni * ar - nr * ai) * den
    br, bi = bre_ref[...], bim_ref[...]
    bbr_ref[...] = cr * br - ci * bi
    bbi_ref[...] = cr * bi + ci * br


def _s5_discretise(a_re, a_im, log_dt, b_re, b_im):
    g, p = a_re.shape
    gs = b_re.shape[-1]
    full3 = lambda s: pl.BlockSpec(s, lambda: (0, 0, 0))
    return pl.pallas_call(
        _s5_disc_kernel,
        in_specs=[full3((g, 1, p)), full3((g, 1, p)), full3((g, 1, 1)), full3((g, gs, p)), full3((g, gs, p))],
        out_specs=[full3((g, 1, p)), full3((g, 1, p)), full3((g, gs, p)), full3((g, gs, p))],
        out_shape=[jax.ShapeDtypeStruct((g, 1, p), F32)] * 2 + [jax.ShapeDtypeStruct((g, gs, p), F32)] * 2,
        name="s5_discretise",
    )(a_re.reshape(g, 1, p), a_im.reshape(g, 1, p), log_dt.reshape(g, 1, 1),
      jnp.swapaxes(b_re, 1, 2), jnp.swapaxes(b_im, 1, 2))


def _s5_kernel(u_ref, wbr_ref, wbi_ref, ar_ref, ai_ref, wcr_ref, wci_ref, d_ref, h0r_ref, h0i_ref,
               y_ref, hr_ref, hi_ref, bur, bui, *, nbatch, steps, lane_w):
    @pl.when(pl.program_id(0) == 0)
    def _():
        hr_ref[...] = h0r_ref[...]
        hi_ref[...] = h0i_ref[...]

    cw = S5_SUPER * S5_GROUP
    sw = S5_SUPER * S5_STATE
    batch_major = steps > 1
    u = u_ref[...].astype(F32)
    if batch_major:
        u = jnp.swapaxes(u, 0, 1).reshape(steps * nbatch, D_MODEL)
    ub = u.astype(BF16)
    for j in range(S5_GROUPS // S5_SUPER):
        uj = ub[:, j * cw:(j + 1) * cw]
        bur[:, j * sw:(j + 1) * sw] = _dot(uj, wbr_ref[j])
        bui[:, j * sw:(j + 1) * sw] = _dot(uj, wbi_ref[j])

    for lc in range(S5_NSTATE // lane_w):
        sl = slice(lc * lane_w, (lc + 1) * lane_w)
        are = jnp.broadcast_to(ar_ref[:, sl], (nbatch, lane_w))
        aim = jnp.broadcast_to(ai_ref[:, sl], (nbatch, lane_w))

        def body(t, carry, sl=sl, are=are, aim=aim):
            hr, hi = carry
            rows = pl.ds(pl.multiple_of(t * nbatch, SUBLANES), nbatch)
            nr = are * hr - aim * hi + bur[rows, sl]
            ni = are * hi + aim * hr + bui[rows, sl]
            bur[rows, sl] = nr
            bui[rows, sl] = ni
            return nr, ni

        hr, hi = lax.fori_loop(0, steps, body, (hr_ref[:, sl], hi_ref[:, sl]), unroll=min(steps, 4))
        hr_ref[:, sl] = hr
        hi_ref[:, sl] = hi

    for j in range(S5_GROUPS // S5_SUPER):
        yj = (_dot(bur[:, j * sw:(j + 1) * sw].astype(BF16), wcr_ref[j])
              + _dot(bui[:, j * sw:(j + 1) * sw].astype(BF16), wci_ref[j]))
        yj = _gelu_tanh(yj + d_ref[:, j * cw:(j + 1) * cw] * u[:, j * cw:(j + 1) * cw])
        if batch_major:
            y_ref[:, :, j * cw:(j + 1) * cw] = jnp.swapaxes(yj.reshape(steps, nbatch, cw), 0, 1).astype(y_ref.dtype)
        else:
            y_ref[:, j * cw:(j + 1) * cw] = yj.astype(y_ref.dtype)


def _s5_scan(proj, wbr, wbi, abr, abi, wcr, wci, d, h0r, h0i, steps):
    nb, L, _ = proj.shape
    nsup = S5_GROUPS // S5_SUPER
    cw, sw = S5_SUPER * S5_GROUP, S5_SUPER * S5_STATE
    c2 = lambda s: pl.BlockSpec(s, lambda c: (0, 0))
    c3 = lambda s: pl.BlockSpec(s, lambda c: (0, 0, 0))
    if steps > 1:
        nbatch, grid = nb, L // steps
        u_spec = pl.BlockSpec((nb, steps, D_MODEL), lambda c: (0, c, OFF_U // D_MODEL))
        y_spec = pl.BlockSpec((nb, steps, D_MODEL), lambda c: (0, c, 0))
    else:
        nbatch, grid = L, 1
        u_spec = pl.BlockSpec((None, L, D_MODEL), lambda c: (0, 0, OFF_U // D_MODEL))
        y_spec = pl.BlockSpec((None, L, D_MODEL), lambda c: (0, 0, 0))
    blk = nbatch * steps
    lane_w = max(LANES, S5_NSTATE // nbatch)
    kern = functools.partial(_s5_kernel, nbatch=nbatch, steps=steps, lane_w=lane_w)
    return pl.pallas_call(
        kern,
        grid=(grid,),
        in_specs=[u_spec,
                  c3((nsup, cw, sw)), c3((nsup, cw, sw)),
                  c2((1, S5_NSTATE)), c2((1, S5_NSTATE)),
                  c3((nsup, sw, cw)), c3((nsup, sw, cw)),
                  c2((1, D_MODEL)),
                  c2((nbatch, S5_NSTATE)), c2((nbatch, S5_NSTATE))],
        out_specs=[y_spec, c2((nbatch, S5_NSTATE)), c2((nbatch, S5_NSTATE))],
        out_shape=[jax.ShapeDtypeStruct((nb, L, D_MODEL), BF16),
                   jax.ShapeDtypeStruct((nbatch, S5_NSTATE), F32),
                   jax.ShapeDtypeStruct((nbatch, S5_NSTATE), F32)],
        scratch_shapes=[pltpu.VMEM((blk, S5_NSTATE), F32), pltpu.VMEM((blk, S5_NSTATE), F32)],
        compiler_params=_cparams(("arbitrary",)),
        name="s5_scan",
    )(proj, wbr, wbi, abr, abi, wcr, wci, d, h0r, h0i)


def _s5_weights(abr, abi, bbr, bbi, c_re, c_im):
    nsup = S5_GROUPS // S5_SUPER
    eye = jnp.eye(S5_SUPER, dtype=F32)

    def bexp(b):
        b = b.reshape(nsup, S5_SUPER, S5_GROUP, S5_STATE)
        return (b[:, :, :, None, :] * eye[None, :, None, :, None]).reshape(
            nsup, S5_SUPER * S5_GROUP, S5_SUPER * S5_STATE).astype(BF16)

    def cexp(c):
        c = jnp.swapaxes(c, 1, 2).reshape(nsup, S5_SUPER, S5_STATE, S5_GROUP)
        return (c[:, :, :, None, :] * eye[None, :, None, :, None]).reshape(
            nsup, S5_SUPER * S5_STATE, S5_SUPER * S5_GROUP).astype(BF16)

    return (bexp(bbr), bexp(bbi), abr.reshape(1, S5_NSTATE), abi.reshape(1, S5_NSTATE),
            cexp(c_re), cexp(-c_im))


def _ssd_gate_norm(y, xs, z, dfull, nw):
    y = (y + dfull * xs) * _silu(z)
    return _group_rms(y, D_MODEL // SSD_GROUPS) * nw


def _ssd_chunk_kernel(xs_ref, bc_ref, z_ref, dt_ref, cw_ref, cb_ref, b16_ref, al16_ref,
                      dfull_ref, nw_ref, eh_ref, o_ref, s_ref, xpad, st):
    c = pl.program_id(1)
    nc = pl.num_programs(1)
    q = CHUNK
    pad = SUBLANES

    @pl.when(c == 0)
    def _():
        xpad[0:pad, :] = jnp.zeros((pad, SSD_CONV_DIM), F32)
        st[...] = jnp.zeros(st.shape, F32)

    @pl.when(c > 0)
    def _():
        xpad[0:pad, :] = xpad[q:q + pad, :]

    xpad[pad:pad + q, :D_MODEL] = xs_ref[...].astype(F32)
    xpad[pad:pad + q, D_MODEL:] = bc_ref[...].astype(F32)
    acc = cb_ref[...] + xpad[pad - 3:pad - 3 + q, :] * cw_ref[0:1, :]
    for tap in range(1, SSD_CONV):
        acc = acc + xpad[pad - 3 + tap:pad - 3 + tap + q, :] * cw_ref[tap:tap + 1, :]
    xc = _silu(acc)
    xs = xc[:, :D_MODEL]
    nbc = SSD_GROUPS * SSD_STATE
    bs = xc[:, D_MODEL:D_MODEL + nbc]
    cs = xc[:, D_MODEL + nbc:]

    delta16 = _softplus(dt_ref[...] + b16_ref[...])
    acum16 = _cumsum_rows(delta16 * (-jnp.exp(al16_ref[...])))
    acum_t = acum16.T
    delta_t = delta16.T
    last16 = acum16[q - 1:q, :]
    ea_f = _split_dot(jnp.exp(acum16), eh_ref[...])
    dw_f = _split_dot(delta16 * jnp.exp(last16 - acum16), eh_ref[...])
    xw = (xs * dw_f).astype(BF16)

    ri = lax.broadcasted_iota(jnp.int32, (q, q), 0)
    ci = lax.broadcasted_iota(jnp.int32, (q, q), 1)
    causal = ri >= ci
    hw = D_MODEL // SSD_GROUPS
    hpg = SSD_HEADS // SSD_GROUPS
    xsb = xs.astype(BF16)
    ys = []
    for g in range(SSD_GROUPS):
        bg = bs[:, g * SSD_STATE:(g + 1) * SSD_STATE]
        cg = cs[:, g * SSD_STATE:(g + 1) * SSD_STATE].astype(BF16)
        bgt = bg.T.astype(BF16)
        scores = _dot(cg, bgt)
        sg = st[:, g * hw:(g + 1) * hw]
        y_inter = _dot(cg, sg.astype(BF16)) * ea_f[:, g * hw:(g + 1) * hw]
        for r in range(hpg):
            hd = g * hpg + r
            seg = acum16[:, hd:hd + 1] - acum_t[hd:hd + 1, :]
            lm = jnp.exp(jnp.where(causal, seg, -jnp.inf)) * delta_t[hd:hd + 1, :]
            p = (scores * lm).astype(BF16)
            ys.append(_dot(p, xsb[:, hd * SSD_HEAD_DIM:(hd + 1) * SSD_HEAD_DIM])
                      + y_inter[:, r * SSD_HEAD_DIM:(r + 1) * SSD_HEAD_DIM])
        st[:, g * hw:(g + 1) * hw] = (ea_f[q - 1:q, g * hw:(g + 1) * hw] * sg
                                      + _dot(bgt, xw[:, g * hw:(g + 1) * hw]))
    y = jnp.concatenate(ys, axis=-1)
    o_ref[...] = _ssd_gate_norm(y, xs, z_ref[...].astype(F32), dfull_ref[...], nw_ref[...]).astype(o_ref.dtype)

    @pl.when(c == nc - 1)
    def _():
        for k in range(D_MODEL // LANES):
            s_ref[k * LANES:(k + 1) * LANES, :] = st[:, k * LANES:(k + 1) * LANES].T


def _ssd_prompt(proj, dt, cw, cb, b16, al16, dfull, nw, eh):
    nb, L, _ = proj.shape
    q = CHUNK
    c2 = lambda s: pl.BlockSpec(s, lambda b, c: (0, 0))
    return pl.pallas_call(
        _ssd_chunk_kernel,
        grid=(nb, L // q),
        in_specs=[pl.BlockSpec((None, q, D_MODEL), lambda b, c: (b, c, OFF_XS // D_MODEL)),
                  pl.BlockSpec((None, q, D_MODEL), lambda b, c: (b, c, OFF_BC // D_MODEL)),
                  pl.BlockSpec((None, q, D_MODEL), lambda b, c: (b, c, OFF_Z // D_MODEL)),
                  pl.BlockSpec((None, q, LANES), lambda b, c: (b, c, 0)),
                  c2((SSD_CONV, SSD_CONV_DIM)), c2((1, SSD_CONV_DIM)),
                  c2((1, LANES)), c2((1, LANES)),
                  c2((1, D_MODEL)), c2((1, D_MODEL)),
                  c2((LANES, D_MODEL))],
        out_specs=[pl.BlockSpec((None, q, D_MODEL), lambda b, c: (b, c, 0)),
                   pl.BlockSpec((None, D_MODEL, SSD_STATE), lambda b, c: (b, 0, 0))],
        out_shape=[jax.ShapeDtypeStruct((nb, L, D_MODEL), BF16),
                   jax.ShapeDtypeStruct((nb, D_MODEL, SSD_STATE), F32)],
        scratch_shapes=[pltpu.VMEM((q + 2 * SUBLANES, SSD_CONV_DIM), F32),
                        pltpu.VMEM((SSD_STATE, D_MODEL), F32)],
        compiler_params=_cparams(("parallel", "arbitrary")),
        name="ssd_chunk",
    )(proj, proj, proj, dt, cw, cb, b16, al16, dfull, nw, eh)


def _rotary(x, cos2, sin2):
    outs = []
    for h in range(RET_HEADS):
        xh = x[:, h * RET_QK_DIM:(h + 1) * RET_QK_DIM]
        outs.append(xh * cos2 + pltpu.roll(xh, RET_QK_DIM // 2, 1) * sin2)
    return outs


def _ret_gate_norm(y, g):
    return _group_rms(y, RET_V_DIM) * _silu(g)


def _ret_chunk_kernel(q_ref, k_ref, v_ref, g_ref, cos_ref, sin_ref, lm_ref, gp_ref, we_ref, gq_ref,
                      o_ref, s_ref, st):
    c = pl.program_id(1)
    nc = pl.num_programs(1)

    @pl.when(c == 0)
    def _():
        st[...] = jnp.zeros(st.shape, F32)

    cos2, sin2 = cos_ref[...], sin_ref[...]
    for bi in range(q_ref.shape[0]):
        qr = _rotary(q_ref[bi].astype(F32), cos2, sin2)
        kr = _rotary(k_ref[bi].astype(F32), cos2, sin2)
        vb = v_ref[bi]
        vw = (vb.astype(F32) * we_ref[...]).astype(BF16)
        ys = []
        for h in range(RET_HEADS):
            sl = slice(h * RET_V_DIM, (h + 1) * RET_V_DIM)
            qh = qr[h].astype(BF16)
            kht = (kr[h] * (RET_QK_DIM ** -0.5)).T.astype(BF16)
            p = (_dot(qh, kht) * lm_ref[h]).astype(BF16)
            sh = st[bi, :, sl]
            ys.append(_dot(p, vb[:, sl]) + _dot(qh, sh.astype(BF16)) * gp_ref[:, sl])
            st[bi, :, sl] = gq_ref[:, sl] * sh + _dot(kht, vw[:, sl])
        y = jnp.concatenate(ys, axis=-1)
        o_ref[bi] = _ret_gate_norm(y, g_ref[bi].astype(F32)).astype(o_ref.dtype)

    @pl.when(c == nc - 1)
    def _():
        for bi in range(q_ref.shape[0]):
            for k in range(D_MODEL // LANES):
                s_ref[bi, k * LANES:(k + 1) * LANES, :] = st[bi, :, k * LANES:(k + 1) * LANES].T


def _ret_consts(q):
    hidx = jnp.arange(RET_HEADS, dtype=F32)
    log_gamma = jnp.log1p(-jnp.exp2(-5.0 - hidx))
    i = jnp.arange(q, dtype=F32)
    seg = (i[:, None] - i[None, :])[None] * log_gamma[:, None, None]
    lm = jnp.exp(jnp.where((i[:, None] >= i[None, :])[None], seg, -jnp.inf))
    rep = lambda t: jnp.repeat(t, RET_V_DIM, axis=-1)
    gp = rep(jnp.exp((i[:, None] + 1.0) * log_gamma[None, :]))
    we = rep(jnp.exp((q - 1.0 - i[:, None]) * log_gamma[None, :]))
    gq = rep(jnp.exp(q * log_gamma)[None, :])
    return log_gamma, lm, gp, we, gq


def _rope_tables(pos):
    half = RET_QK_DIM // 2
    inv = ROPE_BASE ** (-jnp.arange(half, dtype=F32) / half)
    ang = pos.astype(F32)[:, None] * inv[None]
    cos, sin = jnp.cos(ang), jnp.sin(ang)
    return jnp.concatenate([cos, cos], axis=-1), jnp.concatenate([-sin, sin], axis=-1)


def _ret_prompt(proj, cos2, sin2, lm, gp, we, gq, nseq=2):
    nb, L, _ = proj.shape
    q = CHUNK
    qkw = RET_HEADS * RET_QK_DIM
    c2 = lambda s: pl.BlockSpec(s, lambda b, c: (0, 0))
    return pl.pallas_call(
        _ret_chunk_kernel,
        grid=(nb // nseq, L // q),
        in_specs=[pl.BlockSpec((nseq, q, qkw), lambda b, c: (b, c, OFF_Q // qkw)),
                  pl.BlockSpec((nseq, q, qkw), lambda b, c: (b, c, OFF_K // qkw)),
                  pl.BlockSpec((nseq, q, D_MODEL), lambda b, c: (b, c, OFF_V // D_MODEL)),
                  pl.BlockSpec((nseq, q, D_MODEL), lambda b, c: (b, c, OFF_G // D_MODEL)),
                  pl.BlockSpec((q, RET_QK_DIM), lambda b, c: (c, 0)),
                  pl.BlockSpec((q, RET_QK_DIM), lambda b, c: (c, 0)),
                  pl.BlockSpec((RET_HEADS, q, q), lambda b, c: (0, 0, 0)),
                  c2((q, D_MODEL)), c2((q, D_MODEL)), c2((1, D_MODEL))],
        out_specs=[pl.BlockSpec((nseq, q, D_MODEL), lambda b, c: (b, c, 0)),
                   pl.BlockSpec((nseq, D_MODEL, RET_QK_DIM), lambda b, c: (b, 0, 0))],
        out_shape=[jax.ShapeDtypeStruct((nb, L, D_MODEL), BF16),
                   jax.ShapeDtypeStruct((nb, D_MODEL, RET_QK_DIM), F32)],
        scratch_shapes=[pltpu.VMEM((nseq, RET_QK_DIM, D_MODEL), F32)],
        compiler_params=_cparams(("parallel", "arbitrary")),
        name="ret_chunk",
    )(proj, proj, proj, proj, cos2, sin2, lm, gp, we, gq)


def _sample_pre_kernel(xsin_ref, bcin_ref, dt_ref, q_ref, k_ref, conv_ref, cw_ref, cb_ref, b16_ref, al16_ref,
                       eh_ref, cos_ref, sin_ref,
                       vv_ref, e16_ref, bs_ref, cs_ref, xs_ref, convn_ref, qr_ref, kr_ref):
    cd = SSD_CONV_DIM
    x = jnp.concatenate([xsin_ref[...], bcin_ref[...]], axis=-1).astype(F32)
    acc = cb_ref[...] + x * cw_ref[SSD_CONV - 1:SSD_CONV, :]
    for tap in range(SSD_CONV - 1):
        acc = acc + conv_ref[:, tap * cd:(tap + 1) * cd] * cw_ref[tap:tap + 1, :]
    convn_ref[:, 0:(SSD_CONV - 2) * cd] = conv_ref[:, cd:(SSD_CONV - 1) * cd]
    convn_ref[:, (SSD_CONV - 2) * cd:] = x
    xc = _silu(acc)
    xs = xc[:, :D_MODEL]
    nbc = SSD_GROUPS * SSD_STATE
    xs_ref[...] = xs
    bs_ref[...] = xc[:, D_MODEL:D_MODEL + nbc]
    cs_ref[...] = xc[:, D_MODEL + nbc:]
    delta16 = _softplus(dt_ref[...] + b16_ref[...])
    e16_ref[...] = jnp.exp(delta16 * (-jnp.exp(al16_ref[...])))
    vv_ref[...] = xs * _split_dot(delta16, eh_ref[...])
    cos2, sin2 = cos_ref[...], sin_ref[...]
    qr_ref[...] = jnp.concatenate(_rotary(q_ref[...].astype(F32), cos2, sin2), axis=-1)
    kr_ref[...] = jnp.concatenate(_rotary(k_ref[...].astype(F32), cos2, sin2), axis=-1) * (RET_QK_DIM ** -0.5)


def _sample_pre(proj2, dt2, conv2, cw, cb, b16, al16, eh, cos2, sin2):
    n = proj2.shape[0]
    qkw = RET_HEADS * RET_QK_DIM
    nbc = SSD_GROUPS * SSD_STATE
    full = lambda s: pl.BlockSpec(s, lambda i: (0, 0))
    col = lambda w, off: pl.BlockSpec((n, w), lambda i: (0, off // w))
    shp = lambda w: jax.ShapeDtypeStruct((n, w), F32)
    return pl.pallas_call(
        _sample_pre_kernel,
        grid=(1,),
        in_specs=[col(D_MODEL, OFF_XS), col(D_MODEL, OFF_BC), full((n, LANES)), col(qkw, OFF_Q), col(qkw, OFF_K),
                  full((n, (SSD_CONV - 1) * SSD_CONV_DIM)),
                  full((SSD_CONV, SSD_CONV_DIM)), full((1, SSD_CONV_DIM)),
                  full((1, LANES)), full((1, LANES)),
                  full((LANES, D_MODEL)), full((1, RET_QK_DIM)), full((1, RET_QK_DIM))],
        out_specs=[full((n, D_MODEL)), full((n, LANES)), full((n, nbc)), full((n, nbc)), full((n, D_MODEL)),
                   full((n, (SSD_CONV - 1) * SSD_CONV_DIM)), full((n, qkw)), full((n, qkw))],
        out_shape=[shp(D_MODEL), shp(LANES), shp(nbc), shp(nbc), shp(D_MODEL),
                   shp((SSD_CONV - 1) * SSD_CONV_DIM), shp(qkw), shp(qkw)],
        compiler_params=_cparams(("arbitrary",)),
        name="sample_pre",
    )(proj2, proj2, dt2, proj2, proj2, conv2, cw, cb, b16, al16, eh, cos2, sin2)


STATE_ROWS = D_MODEL
STATE_BLK = 64


def _state_step_kernel(e_ref, s_ref, v_ref, k_ref, q_ref, *rest, bb, layer, fill):
    sn_all, y_ref = rest[-2:]
    if fill:
        for l in range(sn_all.shape[0]):
            if l != layer:
                sn_all[l] = jnp.zeros(sn_all.shape[1:], F32)
        sn_ref = sn_all.at[layer]
    else:
        sn_ref = sn_all
    i = pl.program_id(0)
    nblk = STATE_ROWS // STATE_BLK
    gw = STATE_ROWS // 4
    ones = jnp.ones((SUBLANES, LANES), BF16)
    zpad = jnp.zeros((LANES - SUBLANES, LANES), F32)
    nt = (((1,), (1,)), ((), ()))
    for t in range(bb):
        b = i * bb + t
        vt = jnp.concatenate([v_ref[t], zpad], axis=0).T
        kk = k_ref[t]
        qq = q_ref[t]
        for cch in range(STATE_ROWS // LANES):
            prods = []
            for half in range(LANES // STATE_BLK):
                blk = cch * (LANES // STATE_BLK) + half
                r0 = blk * STATE_BLK
                g = r0 // gw
                vcol = vt[half * STATE_BLK:(half + 1) * STATE_BLK, cch:cch + 1]
                sn = e_ref[b * nblk + blk] * s_ref[t, r0:r0 + STATE_BLK, :] + vcol * kk[g:g + 1, :]
                sn_ref[t, r0:r0 + STATE_BLK, :] = sn
                prods.append(sn * qq[g:g + 1, :])
            p = jnp.concatenate(prods, axis=0)
            hi = p.astype(BF16)
            lo = (p - hi.astype(F32)).astype(BF16)
            ysum = (lax.dot_general(ones, hi, nt, preferred_element_type=F32)
                    + lax.dot_general(ones, lo, nt, preferred_element_type=F32))
            y_ref[t, cch:cch + 1, :] = ysum[0:1, :]


def _state_step(e_flat, s_all, layer, v3, k3, q3, prev=None, bb=8):
    depth, n = s_all.shape[:2]
    blk3 = lambda a, b_: pl.BlockSpec((bb, a, b_), lambda i: (i, 0, 0))
    sblk = pl.BlockSpec((None, bb, STATE_ROWS, LANES), lambda i: (layer, i, 0, 0))
    in_specs = [pl.BlockSpec(memory_space=pltpu.SMEM), sblk, blk3(SUBLANES, LANES), blk3(4, LANES), blk3(4, LANES)]
    args = [e_flat, s_all, v3, k3, q3]
    aliases = {}
    if prev is not None:
        in_specs.append(pl.BlockSpec(memory_space=pl.ANY))
        args.append(prev)
        aliases = {len(args) - 1: 0}
        oblk = sblk
    else:
        oblk = pl.BlockSpec((depth, bb, STATE_ROWS, LANES), lambda i: (0, i, 0, 0))
    return pl.pallas_call(
        functools.partial(_state_step_kernel, bb=bb, layer=layer, fill=prev is None),
        grid=(n // bb,),
        in_specs=in_specs,
        out_specs=[oblk, blk3(SUBLANES, LANES)],
        out_shape=[jax.ShapeDtypeStruct((depth, n, STATE_ROWS, LANES), F32),
                   jax.ShapeDtypeStruct((n, SUBLANES, LANES), F32)],
        input_output_aliases=aliases,
        compiler_params=_cparams(("parallel",)),
        name="state_step",
    )(*args)


def _sample_post_kernel(ys_ref, xs_ref, z_ref, yr_ref, g_ref, dfull_ref, nw_ref, os_ref, or_ref):
    os_ref[...] = _ssd_gate_norm(ys_ref[...], xs_ref[...], z_ref[...].astype(F32), dfull_ref[...],
                                 nw_ref[...]).astype(os_ref.dtype)
    or_ref[...] = _ret_gate_norm(yr_ref[...], g_ref[...].astype(F32)).astype(or_ref.dtype)


def _sample_post(y_ssd, xs, proj2, y_ret, dfull, nw):
    n = y_ssd.shape[0]
    full = lambda s: pl.BlockSpec(s, lambda i: (0, 0))
    col = lambda off: pl.BlockSpec((n, D_MODEL), lambda i: (0, off // D_MODEL))
    return pl.pallas_call(
        _sample_post_kernel,
        grid=(1,),
        in_specs=[full((n, D_MODEL)), full((n, D_MODEL)), col(OFF_Z), full((n, D_MODEL)), col(OFF_G),
                  full((1, D_MODEL)), full((1, D_MODEL))],
        out_specs=[full((n, D_MODEL)), full((n, D_MODEL))],
        out_shape=[jax.ShapeDtypeStruct((n, D_MODEL), BF16)] * 2,
        compiler_params=_cparams(("arbitrary",)),
        name="sample_post",
    )(y_ssd, xs, proj2, y_ret, proj2, dfull, nw)


def _merge_kernel(s5_ref, ssd_ref, ret_ref, ga_ref, gb_ref, gc_ref, h_ref, ws5_ref, wssd_ref, wret_ref, wout_ref,
                  o_ref):
    glu = _dot(s5_ref[...], ws5_ref[...])
    y_a = glu[:, :D_MODEL] * _sigmoid(glu[:, D_MODEL:])
    y_b = _dot(ssd_ref[...], wssd_ref[...])
    y_c = _dot(ret_ref[...], wret_ref[...])
    merged = (_sigmoid(ga_ref[...].astype(F32)) * y_a + _sigmoid(gb_ref[...].astype(F32)) * y_b
              + _sigmoid(gc_ref[...].astype(F32)) * y_c)
    o_ref[...] = h_ref[...] + _dot(merged.astype(BF16), wout_ref[...])


def _merge(s5_g, ssd_n, ret_n, proj, h3, ws5, wssd, wret, wout, tm):
    nb, L, _ = h3.shape
    row3 = lambda off: pl.BlockSpec((None, tm, D_MODEL), lambda b, i: (b, i, off // D_MODEL))
    wfull = lambda s: pl.BlockSpec(s, lambda b, i: (0, 0))
    return pl.pallas_call(
        _merge_kernel,
        grid=(nb, L // tm),
        in_specs=[row3(0), row3(0), row3(0),
                  row3(OFF_GATES), row3(OFF_GATES + D_MODEL), row3(OFF_GATES + 2 * D_MODEL), row3(0),
                  wfull((D_MODEL, 2 * D_MODEL)), wfull((D_MODEL, D_MODEL)), wfull((D_MODEL, D_MODEL)),
                  wfull((D_MODEL, D_MODEL))],
        out_specs=row3(0),
        out_shape=jax.ShapeDtypeStruct((nb, L, D_MODEL), F32),
        compiler_params=_cparams(("parallel", "parallel")),
        name="merge",
    )(s5_g, ssd_n, ret_n, proj, proj, proj, h3, ws5, wssd, wret, wout)


def _mlp_kernel(x_ref, nw_ref, wup_ref, wdn_ref, fw_ref, o_ref, xn_ref, acc_ref, *, final_norm):
    k = pl.program_id(1)
    nk = pl.num_programs(1)

    @pl.when(k == 0)
    def _():
        x = x_ref[...]
        xn_ref[...] = _rmsnorm(x, nw_ref[...]).astype(BF16)
        acc_ref[...] = x

    hid = jnp.maximum(_dot(xn_ref[...], wup_ref[...]), 0.0)
    acc_ref[...] += _dot((hid * hid).astype(BF16), wdn_ref[...])

    @pl.when(k == nk - 1)
    def _():
        h = acc_ref[...]
        o_ref[...] = _rmsnorm(h, fw_ref[...]) if final_norm else h


def _mlp(h2, nw, wup, wdn, fw, tm, tf, final_norm):
    rows = h2.shape[0]
    return pl.pallas_call(
        functools.partial(_mlp_kernel, final_norm=final_norm),
        grid=(rows // tm, D_FF // tf),
        in_specs=[pl.BlockSpec((tm, D_MODEL), lambda i, k: (i, 0)),
                  pl.BlockSpec((1, D_MODEL), lambda i, k: (0, 0)),
                  pl.BlockSpec((D_MODEL, tf), lambda i, k: (0, k)),
                  pl.BlockSpec((tf, D_MODEL), lambda i, k: (k, 0)),
                  pl.BlockSpec((1, D_MODEL), lambda i, k: (0, 0))],
        out_specs=pl.BlockSpec((tm, D_MODEL), lambda i, k: (i, 0)),
        out_shape=jax.ShapeDtypeStruct((rows, D_MODEL), F32),
        scratch_shapes=[pltpu.VMEM((tm, D_MODEL), BF16), pltpu.VMEM((tm, D_MODEL), F32)],
        compiler_params=_cparams(("parallel", "arbitrary")),
        name="mlp",
    )(h2, nw, wup, wdn, fw)


def _prep_layer(p):
    w_all = p['w_in'].astype(BF16)
    dt_start = SLAB_B_START - SSD_HEADS
    out = {}
    out['w_all'] = w_all
    out['w_b'] = w_all[:, SLAB_B_START:]
    out['w_dt'] = jnp.pad(w_all[:, dt_start:SLAB_B_START], ((0, 0), (0, LANES - SSD_HEADS)))
    row = lambda a: a.reshape(1, -1).astype(F32)
    padrow = lambda a: jnp.pad(a.astype(F32), (0, LANES - a.shape[0])).reshape(1, LANES)
    rep = lambda a: jnp.repeat(a.astype(F32), SSD_HEAD_DIM).reshape(1, D_MODEL)
    out['norm1_w'] = row(p['norm1_w'])
    out['norm2_w'] = row(p['norm2_w'])
    out['conv_w'] = p['ssd_conv_w'].astype(F32)
    out['conv_b'] = row(p['ssd_conv_b'])
    out['b16'] = padrow(p['ssd_dt_bias'])
    out['al16'] = padrow(p['ssd_a_log'])
    out['dfull'] = rep(p['ssd_d'])
    out['ssd_nw'] = row(p['ssd_norm_w'])
    out['s5_d'] = row(p['s5_d'])
    abr, abi, bbr, bbi = _s5_discretise(p['s5_a_re'].astype(F32), p['s5_a_im'].astype(F32),
                                        p['s5_log_dt'].astype(F32), p['s5_b_re'].astype(F32),
                                        p['s5_b_im'].astype(F32))
    out['s5'] = _s5_weights(abr, abi, bbr, bbi, p['s5_c_re'].astype(F32), p['s5_c_im'].astype(F32))
    for name in ('w_s5_out', 'w_ssd_out', 'w_ret_out', 'w_out', 'w_mlp_up', 'w_mlp_down'):
        out[name] = p[name].astype(BF16)
    return out


def _head_expand():
    e = (jnp.arange(LANES)[:, None] == (jnp.arange(D_MODEL)[None, :] // SSD_HEAD_DIM))
    return e.astype(BF16)


def _tile(n, pref):
    return pref if n % pref == 0 else n


def _prompt_trunk(x, layers, final_w):
    nb, L, _ = x.shape
    eh = _head_expand()
    _, lm, gp, we, gq = _ret_consts(CHUNK)
    cos2, sin2 = _rope_tables(jnp.arange(L, dtype=jnp.int32))
    zeros_state = jnp.zeros((nb, S5_NSTATE), F32)
    steps = 64
    h = x
    outs = {k: [] for k in ('re', 'im', 'ssd', 'conv', 'ret')}
    for li, lp in enumerate(layers):
        tm = _tile(L, 1024)
        proj, dt = _proj_main(h, lp['norm1_w'], lp['w_all'], lp['w_b'], lp['w_dt'], tm)
        wbr, wbi, abr, abi, wcr, wci = lp['s5']
        y_s5, hr, hi = _s5_scan(proj, wbr, wbi, abr, abi, wcr, wci, lp['s5_d'],
                                zeros_state, zeros_state, steps)
        ssd_n, ssd_s = _ssd_prompt(proj, dt, lp['conv_w'], lp['conv_b'], lp['b16'], lp['al16'],
                                   lp['dfull'], lp['ssd_nw'], eh)
        ret_n, ret_s = _ret_prompt(proj, cos2, sin2, lm, gp, we, gq)
        h = _merge(y_s5, ssd_n, ret_n, proj, h,
                   lp['w_s5_out'], lp['w_ssd_out'], lp['w_ret_out'], lp['w_out'], _tile(L, 256))
        last = li == len(layers) - 1
        h = _mlp(h.reshape(nb * L, D_MODEL), lp['norm2_w'], lp['w_mlp_up'], lp['w_mlp_down'], final_w,
                 _tile(nb * L, 512), 1024, last).reshape(nb, L, D_MODEL)
        outs['re'].append(hr.reshape(nb, S5_GROUPS, S5_STATE))
        outs['im'].append(hi.reshape(nb, S5_GROUPS, S5_STATE))
        outs['ssd'].append(ssd_s.reshape(nb, SSD_HEADS, SSD_HEAD_DIM, SSD_STATE))
        outs['conv'].append(proj[:, L - (SSD_CONV - 1):, OFF_XS:OFF_XS + SSD_CONV_DIM].astype(F32))
        outs['ret'].append(ret_s.reshape(nb, RET_HEADS, RET_V_DIM, RET_QK_DIM))
    return (h,) + tuple(jnp.stack(outs[k]) for k in ('re', 'im', 'ssd', 'conv', 'ret'))


def _sample_trunk(x, states, layers, final_w):
    n = x.shape[0]
    st_re, st_im, st_ssd, st_conv, st_ret = states
    eh = _head_expand()
    log_gamma, _, _, _, _ = _ret_consts(1)
    cos2, sin2 = _rope_tables(PAST_LEN + jnp.arange(1, dtype=jnp.int32))
    e_ret = jnp.broadcast_to(jnp.repeat(jnp.exp(log_gamma), RET_V_DIM // STATE_BLK)[None, :],
                             (n, STATE_ROWS // STATE_BLK)).reshape(-1)
    h = x.reshape(1, n, D_MODEL)
    depth = len(layers)
    ssd_all = st_ssd.reshape(depth, n, STATE_ROWS, SSD_STATE).astype(F32)
    ret_all = st_ret.reshape(depth, n, STATE_ROWS, RET_QK_DIM).astype(F32)
    ssd_new = ret_new = None
    outs = {k: [] for k in ('re', 'im', 'conv')}
    for li, lp in enumerate(layers):
        proj, dt = _proj_main(h, lp['norm1_w'], lp['w_all'], lp['w_b'], lp['w_dt'], n)
        wbr, wbi, abr, abi, wcr, wci = lp['s5']
        y_s5, hr, hi = _s5_scan(proj, wbr, wbi, abr, abi, wcr, wci, lp['s5_d'],
                                st_re[li].reshape(n, S5_NSTATE).astype(F32),
                                st_im[li].reshape(n, S5_NSTATE).astype(F32), 1)
        proj2, dt2 = proj.reshape(n, N_MAIN), dt.reshape(n, LANES)
        conv2 = st_conv[li].reshape(n, (SSD_CONV - 1) * SSD_CONV_DIM).astype(F32)
        vv, e16, bs, cs, xs, conv_new, qr, kr = _sample_pre(
            proj2, dt2, conv2, lp['conv_w'], lp['conv_b'], lp['b16'], lp['al16'], eh, cos2, sin2)
        ssd_new, y_ssd = _state_step(
            e16[:, :SSD_HEADS].reshape(-1), ssd_all, li,
            vv.reshape(n, SUBLANES, LANES), bs.reshape(n, SSD_GROUPS, SSD_STATE),
            cs.reshape(n, SSD_GROUPS, SSD_STATE), prev=ssd_new)
        vret = proj2[:, OFF_V:OFF_V + D_MODEL].astype(F32)
        ret_new, y_ret = _state_step(
            e_ret, ret_all, li,
            vret.reshape(n, SUBLANES, LANES), kr.reshape(n, RET_HEADS, RET_QK_DIM),
            qr.reshape(n, RET_HEADS, RET_QK_DIM), prev=ret_new)
        ssd_n, ret_n = _sample_post(y_ssd.reshape(n, D_MODEL), xs, proj2, y_ret.reshape(n, D_MODEL),
                                    lp['dfull'], lp['ssd_nw'])
        h = _merge(y_s5, ssd_n.reshape(1, n, D_MODEL), ret_n.reshape(1, n, D_MODEL), proj, h,
                   lp['w_s5_out'], lp['w_ssd_out'], lp['w_ret_out'], lp['w_out'], n)
        last = li == len(layers) - 1
        h = _mlp(h.reshape(n, D_MODEL), lp['norm2_w'], lp['w_mlp_up'], lp['w_mlp_down'], final_w,
                 n, 1024, last).reshape(1, n, D_MODEL)
        outs['re'].append(hr.reshape(n, S5_GROUPS, S5_STATE))
        outs['im'].append(hi.reshape(n, S5_GROUPS, S5_STATE))
        outs['conv'].append(conv_new.reshape(n, SSD_CONV - 1, SSD_CONV_DIM))
    return (h.reshape(n, 1, D_MODEL), jnp.stack(outs['re']), jnp.stack(outs['im']),
            ssd_new.reshape(depth, n, SSD_HEADS, SSD_HEAD_DIM, SSD_STATE), jnp.stack(outs['conv']),
            ret_new.reshape(depth, n, RET_HEADS, RET_V_DIM, RET_QK_DIM))


def kernel(x_prompt, x_sample, state_s5_re, state_s5_im, state_ssd, state_conv, state_ret, norm1_w, w_in, s5_a_re, s5_a_im, s5_log_dt, s5_b_re, s5_b_im, s5_c_re, s5_c_im, s5_d, w_s5_out, ssd_conv_w, ssd_conv_b, ssd_dt_bias, ssd_a_log, ssd_d, ssd_norm_w, w_ssd_out, w_ret_out, w_out, norm2_w, w_mlp_up, w_mlp_down, final_norm_w):
    params = dict(norm1_w=norm1_w, w_in=w_in, s5_a_re=s5_a_re, s5_a_im=s5_a_im, s5_log_dt=s5_log_dt,
                  s5_b_re=s5_b_re, s5_b_im=s5_b_im, s5_c_re=s5_c_re, s5_c_im=s5_c_im, s5_d=s5_d,
                  w_s5_out=w_s5_out, ssd_conv_w=ssd_conv_w, ssd_conv_b=ssd_conv_b, ssd_dt_bias=ssd_dt_bias,
                  ssd_a_log=ssd_a_log, ssd_d=ssd_d, ssd_norm_w=ssd_norm_w, w_ssd_out=w_ssd_out,
                  w_ret_out=w_ret_out, w_out=w_out, norm2_w=norm2_w, w_mlp_up=w_mlp_up, w_mlp_down=w_mlp_down)
    depth = w_in.shape[0]
    layers = [_prep_layer({name: arr[l] for name, arr in params.items()}) for l in range(depth)]
    final_w = final_norm_w.reshape(1, D_MODEL).astype(F32)
    y_p, re_p, im_p, ssd_p, conv_p, ret_p = _prompt_trunk(x_prompt.astype(F32), layers, final_w)
    y_s, re_s, im_s, ssd_s, conv_s, ret_s = _sample_trunk(
        x_sample.astype(F32), (state_s5_re, state_s5_im, state_ssd, state_conv, state_ret), layers, final_w)
    return (y_p, y_s, re_p, im_p, ssd_p, conv_p, ret_p, re_s, im_s, ssd_s, conv_s, ret_s)
```

```python
import functools
import math

import jax
import jax.numpy as jnp
from jax import lax
from jax.experimental import pallas as pl
from jax.experimental.pallas import tpu as pltpu

F32 = jnp.float32
BF16 = jnp.bfloat16

D_MODEL = 1024
S5_GROUP = 16
S5_STATE = 64
S5_GROUPS = D_MODEL // S5_GROUP
S5_NSTATE = S5_GROUPS * S5_STATE
S5_SUPER = 8
SSD_HEADS = 16
SSD_HEAD_DIM = 64
SSD_GROUPS = 4
SSD_STATE = 128
SSD_CONV = 4
SSD_CONV_DIM = D_MODEL + 2 * SSD_GROUPS * SSD_STATE
RET_HEADS = 4
RET_QK_DIM = 128
RET_V_DIM = 256
ROPE_BASE = 10000.0
D_FF = 4 * D_MODEL
CHUNK = 128
EPS = 1e-6
PAST_LEN = 16384
LANES = 128
SUBLANES = 8
VMEM_LIMIT = 52 * 1024 * 1024

OFF_U, OFF_Z, OFF_XS, OFF_BC, OFF_Q, OFF_K, OFF_V, OFF_G, OFF_GATES = (
    0, 1024, 2048, 3072, 4096, 4608, 5120, 6144, 7168)
N_MAIN = 10240
SLAB_A_COLS = OFF_Q
SLAB_B_START = 2 * D_MODEL + SSD_CONV_DIM + SSD_HEADS
PROJ_TN = 2048


def _cparams(sem):
    return pltpu.CompilerParams(dimension_semantics=sem, vmem_limit_bytes=VMEM_LIMIT)


def _sigmoid(x):
    return 1.0 / (1.0 + jnp.exp(-x))


def _silu(x):
    return x * _sigmoid(x)


def _softplus(x):
    return jnp.maximum(x, 0.0) + jnp.log1p(jnp.exp(-jnp.abs(x)))


def _gelu_tanh(x):
    return 0.5 * x * (1.0 + jnp.tanh(math.sqrt(2.0 / math.pi) * (x + 0.044715 * (x * x * x))))


def _rmsnorm(x, w):
    return x * lax.rsqrt(jnp.mean(x * x, axis=-1, keepdims=True) + EPS) * w


def _cumsum_rows(x):
    n = x.shape[0]
    row = lax.broadcasted_iota(jnp.int32, x.shape, 0)
    k = 1
    while k < n:
        x = x + jnp.where(row >= k, pltpu.roll(x, k, 0), 0.0)
        k *= 2
    return x


def _group_rms(y, width):
    outs = []
    for s in range(0, y.shape[-1], width):
        yg = y[:, s:s + width]
        outs.append(yg * lax.rsqrt(jnp.mean(yg * yg, axis=-1, keepdims=True) + EPS))
    return jnp.concatenate(outs, axis=-1)


def _dot(a, b):
    return jnp.dot(a, b, preferred_element_type=F32)


def _split_dot(x, e):
    hi = x.astype(BF16)
    lo = (x - hi.astype(F32)).astype(BF16)
    return _dot(hi, e) + _dot(lo, e)


def _proj_kernel(x_ref, nw_ref, wa_ref, wb_ref, wdt_ref, o_ref, odt_ref, xn_ref, *, na):
    j = pl.program_id(2)

    @pl.when(j == 0)
    def _():
        xn = _rmsnorm(x_ref[...], nw_ref[...]).astype(BF16)
        xn_ref[...] = xn
        odt_ref[...] = _dot(xn, wdt_ref[...])

    @pl.when(j < na)
    def _():
        o_ref[...] = _dot(xn_ref[...], wa_ref[...]).astype(o_ref.dtype)

    @pl.when(j >= na)
    def _():
        o_ref[...] = _dot(xn_ref[...], wb_ref[...]).astype(o_ref.dtype)


def _proj_main(x3, nw, w_a, w_b, wdt, layer, tm):
    nb, L, _ = x3.shape
    tn = PROJ_TN
    na = SLAB_A_COLS // tn
    return pl.pallas_call(
        functools.partial(_proj_kernel, na=na),
        grid=(nb, L // tm, N_MAIN // tn),
        in_specs=[pl.BlockSpec((None, tm, D_MODEL), lambda b, i, j: (b, i, 0)),
                  pl.BlockSpec((1, D_MODEL), lambda b, i, j: (0, 0)),
                  pl.BlockSpec((None, D_MODEL, tn), lambda b, i, j: (layer, 0, jnp.minimum(j, na - 1))),
                  pl.BlockSpec((None, D_MODEL, tn), lambda b, i, j: (layer, 0, jnp.maximum(j - na, 0))),
                  pl.BlockSpec((None, D_MODEL, LANES), lambda b, i, j: (layer, 0, 0))],
        out_specs=[pl.BlockSpec((None, tm, tn), lambda b, i, j: (b, i, j)),
                   pl.BlockSpec((None, tm, LANES), lambda b, i, j: (b, i, 0))],
        out_shape=[jax.ShapeDtypeStruct((nb, L, N_MAIN), BF16), jax.ShapeDtypeStruct((nb, L, LANES), F32)],
        scratch_shapes=[pltpu.VMEM((tm, D_MODEL), BF16)],
        compiler_params=_cparams(("parallel", "parallel", "arbitrary")),
        name="proj_main",
    )(x3, nw, w_a, w_b, wdt)


def _s5_disc_kernel(are_ref, aim_ref, ldt_ref, bre_ref, bim_ref, abr_ref, abi_ref, bbr_ref, bbi_ref):
    ar, ai = are_ref[...], aim_ref[...]
    dt = jnp.exp(ldt_ref[...])
    mag = jnp.exp(ar * dt)
    abr = mag * jnp.cos(ai * dt)
    abi = mag * jnp.sin(ai * dt)
    abr_ref[...] = abr
    abi_ref[...] = abi
    nr, ni = abr - 1.0, abi
    den = 1.0 / (ar * ar + ai * ai)
    cr = (nr * ar + ni * ai) * den
    ci = (ni * ar - nr * ai) * den
    br, bi = bre_ref[...], bim_ref[...]
    bbr_ref[...] = cr * br - ci * bi
    bbi_ref[...] = cr * bi + ci * br


def _s5_discretise(a_re, a_im, log_dt, b_re, b_im):
    g, p = a_re.shape
    gs = b_re.shape[-1]
    full3 = lambda s: pl.BlockSpec(s, lambda: (0, 0, 0))
    return pl.pallas_call(
        _s5_disc_kernel,
        in_specs=[full3((g, 1, p)), full3((g, 1, p)), full3((g, 1, 1)), full3((g, gs, p)), full3((g, gs, p))],
        out_specs=[full3((g, 1, p)), full3((g, 1, p)), full3((g, gs, p)), full3((g, gs, p))],
        out_shape=[jax.ShapeDtypeStruct((g, 1, p), F32)] * 2 + [jax.ShapeDtypeStruct((g, gs, p), F32)] * 2,
        name="s5_discretise",
    )(a_re.reshape(g, 1, p), a_im.reshape(g, 1, p), log_dt.reshape(g, 1, 1),
      jnp.swapaxes(b_re, 1, 2), jnp.swapaxes(b_im, 1, 2))


def _s5_kernel(u_ref, wbr_ref, wbi_ref, ar_ref, ai_ref, wcr_ref, wci_ref, d_ref, h0r_ref, h0i_ref,
               y_ref, hr_ref, hi_ref, bur, bui, *, nbatch, steps, lane_w):
    @pl.when(pl.program_id(0) == 0)
    def _():
        hr_ref[...] = h0r_ref[...]
        hi_ref[...] = h0i_ref[...]

    cw = S5_SUPER * S5_GROUP
    sw = S5_SUPER * S5_STATE
    batch_major = steps > 1
    u = u_ref[...].astype(F32)
    if batch_major:
        u = jnp.swapaxes(u, 0, 1).reshape(steps * nbatch, D_MODEL)
    ub = u.astype(BF16)
    for j in range(S5_GROUPS // S5_SUPER):
        uj = ub[:, j * cw:(j + 1) * cw]
        bur[:, j * sw:(j + 1) * sw] = _dot(uj, wbr_ref[j])
        bui[:, j * sw:(j + 1) * sw] = _dot(uj, wbi_ref[j])

    for lc in range(S5_NSTATE // lane_w):
        sl = slice(lc * lane_w, (lc + 1) * lane_w)
        are = jnp.broadcast_to(ar_ref[:, sl], (nbatch, lane_w))
        aim = jnp.broadcast_to(ai_ref[:, sl], (nbatch, lane_w))

        def body(t, carry, sl=sl, are=are, aim=aim):
            hr, hi = carry
            rows = pl.ds(pl.multiple_of(t * nbatch, SUBLANES), nbatch)
            nr = are * hr - aim * hi + bur[rows, sl]
            ni = are * hi + aim * hr + bui[rows, sl]
            bur[rows, sl] = nr
            bui[rows, sl] = ni
            return nr, ni

        hr, hi = lax.fori_loop(0, steps, body, (hr_ref[:, sl], hi_ref[:, sl]), unroll=min(steps, 4))
        hr_ref[:, sl] = hr
        hi_ref[:, sl] = hi

    for j in range(S5_GROUPS // S5_SUPER):
        yj = (_dot(bur[:, j * sw:(j + 1) * sw].astype(BF16), wcr_ref[j])
              + _dot(bui[:, j * sw:(j + 1) * sw].astype(BF16), wci_ref[j]))
        yj = _gelu_tanh(yj + d_ref[:, j * cw:(j + 1) * cw] * u[:, j * cw:(j + 1) * cw])
        if batch_major:
            y_ref[:, :, j * cw:(j + 1) * cw] = jnp.swapaxes(yj.reshape(steps, nbatch, cw), 0, 1).astype(y_ref.dtype)
        else:
            y_ref[:, j * cw:(j + 1) * cw] = yj.astype(y_ref.dtype)


def _s5_scan(proj, wbr, wbi, abr, abi, wcr, wci, d, h0r, h0i, steps):
    nb, L, _ = proj.shape
    nsup = S5_GROUPS // S5_SUPER
    cw, sw = S5_SUPER * S5_GROUP, S5_SUPER * S5_STATE
    c2 = lambda s: pl.BlockSpec(s, lambda c: (0, 0))
    c3 = lambda s: pl.BlockSpec(s, lambda c: (0, 0, 0))
    if steps > 1:
        nbatch, grid = nb, L // steps
        u_spec = pl.BlockSpec((nb, steps, D_MODEL), lambda c: (0, c, OFF_U // D_MODEL))
        y_spec = pl.BlockSpec((nb, steps, D_MODEL), lambda c: (0, c, 0))
    else:
        nbatch, grid = L, 1
        u_spec = pl.BlockSpec((None, L, D_MODEL), lambda c: (0, 0, OFF_U // D_MODEL))
        y_spec = pl.BlockSpec((None, L, D_MODEL), lambda c: (0, 0, 0))
    blk = nbatch * steps
    lane_w = max(LANES, S5_NSTATE // nbatch)
    kern = functools.partial(_s5_kernel, nbatch=nbatch, steps=steps, lane_w=lane_w)
    return pl.pallas_call(
        kern,
        grid=(grid,),
        in_specs=[u_spec,
                  c3((nsup, cw, sw)), c3((nsup, cw, sw)),
                  c2((1, S5_NSTATE)), c2((1, S5_NSTATE)),
                  c3((nsup, sw, cw)), c3((nsup, sw, cw)),
                  c2((1, D_MODEL)),
                  c2((nbatch, S5_NSTATE)), c2((nbatch, S5_NSTATE))],
        out_specs=[y_spec, c2((nbatch, S5_NSTATE)), c2((nbatch, S5_NSTATE))],
        out_shape=[jax.ShapeDtypeStruct((nb, L, D_MODEL), BF16),
                   jax.ShapeDtypeStruct((nbatch, S5_NSTATE), F32),
                   jax.ShapeDtypeStruct((nbatch, S5_NSTATE), F32)],
        scratch_shapes=[pltpu.VMEM((blk, S5_NSTATE), F32), pltpu.VMEM((blk, S5_NSTATE), F32)],
        compiler_params=_cparams(("arbitrary",)),
        name="s5_scan",
    )(proj, wbr, wbi, abr, abi, wcr, wci, d, h0r, h0i)


def _s5_weights(abr, abi, bbr, bbi, c_re, c_im):
    nsup = S5_GROUPS // S5_SUPER
    eye = jnp.eye(S5_SUPER, dtype=F32)

    def bexp(b):
        b = b.reshape(nsup, S5_SUPER, S5_GROUP, S5_STATE)
        return (b[:, :, :, None, :] * eye[None, :, None, :, None]).reshape(
            nsup, S5_SUPER * S5_GROUP, S5_SUPER * S5_STATE).astype(BF16)

    def cexp(c):
        c = jnp.swapaxes(c, 1, 2).reshape(nsup, S5_SUPER, S5_STATE, S5_GROUP)
        return (c[:, :, :, None, :] * eye[None, :, None, :, None]).reshape(
            nsup, S5_SUPER * S5_STATE, S5_SUPER * S5_GROUP).astype(BF16)

    return (bexp(bbr), bexp(bbi), abr.reshape(1, S5_NSTATE), abi.reshape(1, S5_NSTATE),
            cexp(c_re), cexp(-c_im))


def _ssd_gate_norm(y, xs, z, dfull, nw):
    y = (y + dfull * xs) * _silu(z)
    return _group_rms(y, D_MODEL // SSD_GROUPS) * nw


def _ssd_chunk_kernel(xs_ref, bc_ref, z_ref, dt_ref, cw_ref, cb_ref, b16_ref, al16_ref,
                      dfull_ref, nw_ref, eh_ref, o_ref, s_ref, xpad, st):
    c = pl.program_id(1)
    nc = pl.num_programs(1)
    q = CHUNK
    pad = SUBLANES

    @pl.when(c == 0)
    def _():
        xpad[0:pad, :] = jnp.zeros((pad, SSD_CONV_DIM), F32)
        st[...] = jnp.zeros(st.shape, F32)

    @pl.when(c > 0)
    def _():
        xpad[0:pad, :] = xpad[q:q + pad, :]

    xpad[pad:pad + q, :D_MODEL] = xs_ref[...].astype(F32)
    xpad[pad:pad + q, D_MODEL:] = bc_ref[...].astype(F32)
    acc = cb_ref[...] + xpad[pad - 3:pad - 3 + q, :] * cw_ref[0:1, :]
    for tap in range(1, SSD_CONV):
        acc = acc + xpad[pad - 3 + tap:pad - 3 + tap + q, :] * cw_ref[tap:tap + 1, :]
    xc = _silu(acc)
    xs = xc[:, :D_MODEL]
    nbc = SSD_GROUPS * SSD_STATE
    bs = xc[:, D_MODEL:D_MODEL + nbc]
    cs = xc[:, D_MODEL + nbc:]

    delta16 = _softplus(dt_ref[...] + b16_ref[...])
    acum16 = _cumsum_rows(delta16 * (-jnp.exp(al16_ref[...])))
    acum_t = acum16.T
    delta_t = delta16.T
    last16 = acum16[q - 1:q, :]
    ea_f = _split_dot(jnp.exp(acum16), eh_ref[...])
    dw_f = _split_dot(delta16 * jnp.exp(last16 - acum16), eh_ref[...])
    xw = (xs * dw_f).astype(BF16)

    ri = lax.broadcasted_iota(jnp.int32, (q, q), 0)
    ci = lax.broadcasted_iota(jnp.int32, (q, q), 1)
    causal = ri >= ci
    hw = D_MODEL // SSD_GROUPS
    hpg = SSD_HEADS // SSD_GROUPS
    xsb = xs.astype(BF16)
    ys = []
    for g in range(SSD_GROUPS):
        bg = bs[:, g * SSD_STATE:(g + 1) * SSD_STATE]
        cg = cs[:, g * SSD_STATE:(g + 1) * SSD_STATE].astype(BF16)
        bgt = bg.T.astype(BF16)
        scores = _dot(cg, bgt)
        sg = st[:, g * hw:(g + 1) * hw]
        y_inter = _dot(cg, sg.astype(BF16)) * ea_f[:, g * hw:(g + 1) * hw]
        for r in range(hpg):
            hd = g * hpg + r
            seg = acum16[:, hd:hd + 1] - acum_t[hd:hd + 1, :]
            lm = jnp.exp(jnp.where(causal, seg, -jnp.inf)) * delta_t[hd:hd + 1, :]
            p = (scores * lm).astype(BF16)
            ys.append(_dot(p, xsb[:, hd * SSD_HEAD_DIM:(hd + 1) * SSD_HEAD_DIM])
                      + y_inter[:, r * SSD_HEAD_DIM:(r + 1) * SSD_HEAD_DIM])
        st[:, g * hw:(g + 1) * hw] = (ea_f[q - 1:q, g * hw:(g + 1) * hw] * sg
                                      + _dot(bgt, xw[:, g * hw:(g + 1) * hw]))
    y = jnp.concatenate(ys, axis=-1)
    o_ref[...] = _ssd_gate_norm(y, xs, z_ref[...].astype(F32), dfull_ref[...], nw_ref[...]).astype(o_ref.dtype)

    @pl.when(c == nc - 1)
    def _():
        for k in range(D_MODEL // LANES):
            s_ref[k * LANES:(k + 1) * LANES, :] = st[:, k * LANES:(k + 1) * LANES].T


def _ssd_prompt(proj, dt, cw, cb, b16, al16, dfull, nw, eh):
    nb, L, _ = proj.shape
    q = CHUNK
    c2 = lambda s: pl.BlockSpec(s, lambda b, c: (0, 0))
    return pl.pallas_call(
        _ssd_chunk_kernel,
        grid=(nb, L // q),
        in_specs=[pl.BlockSpec((None, q, D_MODEL), lambda b, c: (b, c, OFF_XS // D_MODEL)),
                  pl.BlockSpec((None, q, D_MODEL), lambda b, c: (b, c, OFF_BC // D_MODEL)),
                  pl.BlockSpec((None, q, D_MODEL), lambda b, c: (b, c, OFF_Z // D_MODEL)),
                  pl.BlockSpec((None, q, LANES), lambda b, c: (b, c, 0)),
                  c2((SSD_CONV, SSD_CONV_DIM)), c2((1, SSD_CONV_DIM)),
                  c2((1, LANES)), c2((1, LANES)),
                  c2((1, D_MODEL)), c2((1, D_MODEL)),
                  c2((LANES, D_MODEL))],
        out_specs=[pl.BlockSpec((None, q, D_MODEL), lambda b, c: (b, c, 0)),
                   pl.BlockSpec((None, D_MODEL, SSD_STATE), lambda b, c: (b, 0, 0))],
        out_shape=[jax.ShapeDtypeStruct((nb, L, D_MODEL), BF16),
                   jax.ShapeDtypeStruct((nb, D_MODEL, SSD_STATE), F32)],
        scratch_shapes=[pltpu.VMEM((q + 2 * SUBLANES, SSD_CONV_DIM), F32),
                        pltpu.VMEM((SSD_STATE, D_MODEL), F32)],
        compiler_params=_cparams(("parallel", "arbitrary")),
        name="ssd_chunk",
    )(proj, proj, proj, dt, cw, cb, b16, al16, dfull, nw, eh)


def _rotary(x, cos2, sin2):
    outs = []
    for h in range(RET_HEADS):
        xh = x[:, h * RET_QK_DIM:(h + 1) * RET_QK_DIM]
        outs.append(xh * cos2 + pltpu.roll(xh, RET_QK_DIM // 2, 1) * sin2)
    return outs


def _ret_gate_norm(y, g):
    return _group_rms(y, RET_V_DIM) * _silu(g)


def _ret_chunk_kernel(q_ref, k_ref, v_ref, g_ref, cos_ref, sin_ref, lm_ref, gp_ref, we_ref, gq_ref,
                      o_ref, s_ref, st):
    c = pl.program_id(1)
    nc = pl.num_programs(1)

    @pl.when(c == 0)
    def _():
        st[...] = jnp.zeros(st.shape, F32)

    cos2, sin2 = cos_ref[...], sin_ref[...]
    for bi in range(q_ref.shape[0]):
        qr = _rotary(q_ref[bi].astype(F32), cos2, sin2)
        kr = _rotary(k_ref[bi].astype(F32), cos2, sin2)
        vb = v_ref[bi]
        vw = (vb.astype(F32) * we_ref[...]).astype(BF16)
        ys = []
        for h in range(RET_HEADS):
            sl = slice(h * RET_V_DIM, (h + 1) * RET_V_DIM)
            qh = qr[h].astype(BF16)
            kht = (kr[h] * (RET_QK_DIM ** -0.5)).T.astype(BF16)
            p = (_dot(qh, kht) * lm_ref[h]).astype(BF16)
            sh = st[bi, :, sl]
            ys.append(_dot(p, vb[:, sl]) + _dot(qh, sh.astype(BF16)) * gp_ref[:, sl])
            st[bi, :, sl] = gq_ref[:, sl] * sh + _dot(kht, vw[:, sl])
        y = jnp.concatenate(ys, axis=-1)
        o_ref[bi] = _ret_gate_norm(y, g_ref[bi].astype(F32)).astype(o_ref.dtype)

    @pl.when(c == nc - 1)
    def _():
        for bi in range(q_ref.shape[0]):
            for k in range(D_MODEL // LANES):
                s_ref[bi, k * LANES:(k + 1) * LANES, :] = st[bi, :, k * LANES:(k + 1) * LANES].T


def _ret_consts(q):
    hidx = jnp.arange(RET_HEADS, dtype=F32)
    log_gamma = jnp.log1p(-jnp.exp2(-5.0 - hidx))
    i = jnp.arange(q, dtype=F32)
    seg = (i[:, None] - i[None, :])[None] * log_gamma[:, None, None]
    lm = jnp.exp(jnp.where((i[:, None] >= i[None, :])[None], seg, -jnp.inf))
    rep = lambda t: jnp.repeat(t, RET_V_DIM, axis=-1)
    gp = rep(jnp.exp((i[:, None] + 1.0) * log_gamma[None, :]))
    we = rep(jnp.exp((q - 1.0 - i[:, None]) * log_gamma[None, :]))
    gq = rep(jnp.exp(q * log_gamma)[None, :])
    return log_gamma, lm, gp, we, gq


def _rope_tables(pos):
    half = RET_QK_DIM // 2
    inv = ROPE_BASE ** (-jnp.arange(half, dtype=F32) / half)
    ang = pos.astype(F32)[:, None] * inv[None]
    cos, sin = jnp.cos(ang), jnp.sin(ang)
    return jnp.concatenate([cos, cos], axis=-1), jnp.concatenate([-sin, sin], axis=-1)


def _ret_prompt(proj, cos2, sin2, lm, gp, we, gq, nseq=2):
    nb, L, _ = proj.shape
    q = CHUNK
    qkw = RET_HEADS * RET_QK_DIM
    c2 = lambda s: pl.BlockSpec(s, lambda b, c: (0, 0))
    return pl.pallas_call(
        _ret_chunk_kernel,
        grid=(nb // nseq, L // q),
        in_specs=[pl.BlockSpec((nseq, q, qkw), lambda b, c: (b, c, OFF_Q // qkw)),
                  pl.BlockSpec((nseq, q, qkw), lambda b, c: (b, c, OFF_K // qkw)),
                  pl.BlockSpec((nseq, q, D_MODEL), lambda b, c: (b, c, OFF_V // D_MODEL)),
                  pl.BlockSpec((nseq, q, D_MODEL), lambda b, c: (b, c, OFF_G // D_MODEL)),
                  pl.BlockSpec((q, RET_QK_DIM), lambda b, c: (c, 0)),
                  pl.BlockSpec((q, RET_QK_DIM), lambda b, c: (c, 0)),
                  pl.BlockSpec((RET_HEADS, q, q), lambda b, c: (0, 0, 0)),
                  c2((q, D_MODEL)), c2((q, D_MODEL)), c2((1, D_MODEL))],
        out_specs=[pl.BlockSpec((nseq, q, D_MODEL), lambda b, c: (b, c, 0)),
                   pl.BlockSpec((nseq, D_MODEL, RET_QK_DIM), lambda b, c: (b, 0, 0))],
        out_shape=[jax.ShapeDtypeStruct((nb, L, D_MODEL), BF16),
                   jax.ShapeDtypeStruct((nb, D_MODEL, RET_QK_DIM), F32)],
        scratch_shapes=[pltpu.VMEM((nseq, RET_QK_DIM, D_MODEL), F32)],
        compiler_params=_cparams(("parallel", "arbitrary")),
        name="ret_chunk",
    )(proj, proj, proj, proj, cos2, sin2, lm, gp, we, gq)


def _sample_pre_kernel(xsin_ref, bcin_ref, dt_ref, q_ref, k_ref, conv_ref, cw_ref, cb_ref, b16_ref, al16_ref,
                       eh_ref, cos_ref, sin_ref,
                       vv_ref, e16_ref, bs_ref, cs_ref, xs_ref, convn_ref, qr_ref, kr_ref):
    cd = SSD_CONV_DIM
    x = jnp.concatenate([xsin_ref[...], bcin_ref[...]], axis=-1).astype(F32)
    acc = cb_ref[...] + x * cw_ref[SSD_CONV - 1:SSD_CONV, :]
    for tap in range(SSD_CONV - 1):
        acc = acc + conv_ref[:, tap * cd:(tap + 1) * cd] * cw_ref[tap:tap + 1, :]
    convn_ref[:, 0:(SSD_CONV - 2) * cd] = conv_ref[:, cd:(SSD_CONV - 1) * cd]
    convn_ref[:, (SSD_CONV - 2) * cd:] = x
    xc = _silu(acc)
    xs = xc[:, :D_MODEL]
    nbc = SSD_GROUPS * SSD_STATE
    xs_ref[...] = xs
    bs_ref[...] = xc[:, D_MODEL:D_MODEL + nbc]
    cs_ref[...] = xc[:, D_MODEL + nbc:]
    delta16 = _softplus(dt_ref[...] + b16_ref[...])
    e16_ref[...] = jnp.exp(delta16 * (-jnp.exp(al16_ref[...])))
    vv_ref[...] = xs * _split_dot(delta16, eh_ref[...])
    cos2, sin2 = cos_ref[...], sin_ref[...]
    qr_ref[...] = jnp.concatenate(_rotary(q_ref[...].astype(F32), cos2, sin2), axis=-1)
    kr_ref[...] = jnp.concatenate(_rotary(k_ref[...].astype(F32), cos2, sin2), axis=-1) * (RET_QK_DIM ** -0.5)


def _sample_pre(proj2, dt2, conv2, cw, cb, b16, al16, eh, cos2, sin2):
    n = proj2.shape[0]
    qkw = RET_HEADS * RET_QK_DIM
    nbc = SSD_GROUPS * SSD_STATE
    full = lambda s: pl.BlockSpec(s, lambda i: (0, 0))
    col = lambda w, off: pl.BlockSpec((n, w), lambda i: (0, off // w))
    shp = lambda w: jax.ShapeDtypeStruct((n, w), F32)
    return pl.pallas_call(
        _sample_pre_kernel,
        grid=(1,),
        in_specs=[col(D_MODEL, OFF_XS), col(D_MODEL, OFF_BC), full((n, LANES)), col(qkw, OFF_Q), col(qkw, OFF_K),
                  full((n, (SSD_CONV - 1) * SSD_CONV_DIM)),
                  full((SSD_CONV, SSD_CONV_DIM)), full((1, SSD_CONV_DIM)),
                  full((1, LANES)), full((1, LANES)),
                  full((LANES, D_MODEL)), full((1, RET_QK_DIM)), full((1, RET_QK_DIM))],
        out_specs=[full((n, D_MODEL)), full((n, LANES)), full((n, nbc)), full((n, nbc)), full((n, D_MODEL)),
                   full((n, (SSD_CONV - 1) * SSD_CONV_DIM)), full((n, qkw)), full((n, qkw))],
        out_shape=[shp(D_MODEL), shp(LANES), shp(nbc), shp(nbc), shp(D_MODEL),
                   shp((SSD_CONV - 1) * SSD_CONV_DIM), shp(qkw), shp(qkw)],
        compiler_params=_cparams(("arbitrary",)),
        name="sample_pre",
    )(proj2, proj2, dt2, proj2, proj2, conv2, cw, cb, b16, al16, eh, cos2, sin2)


STATE_ROWS = D_MODEL
STATE_BLK = 64


def _state_step_kernel(e_ref, s_ref, v_ref, k_ref, q_ref, *rest, bb, layer, fill):
    sn_all, y_ref = rest[-2:]
    if fill:
        for l in range(sn_all.shape[0]):
            if l != layer:
                sn_all[l] = jnp.zeros(sn_all.shape[1:], F32)
        sn_ref = sn_all.at[layer]
    else:
        sn_ref = sn_all
    i = pl.program_id(0)
    nblk = STATE_ROWS // STATE_BLK
    gw = STATE_ROWS // 4
    ones = jnp.ones((SUBLANES, LANES), BF16)
    zpad = jnp.zeros((LANES - SUBLANES, LANES), F32)
    nt = (((1,), (1,)), ((), ()))
    for t in range(bb):
        b = i * bb + t
        vt = jnp.concatenate([v_ref[t], zpad], axis=0).T
        kk = k_ref[t]
        qq = q_ref[t]
        for cch in range(STATE_ROWS // LANES):
            prods = []
            for half in range(LANES // STATE_BLK):
                blk = cch * (LANES // STATE_BLK) + half
                r0 = blk * STATE_BLK
                g = r0 // gw
                vcol = vt[half * STATE_BLK:(half + 1) * STATE_BLK, cch:cch + 1]
                sn = e_ref[b * nblk + blk] * s_ref[t, r0:r0 + STATE_BLK, :] + vcol * kk[g:g + 1, :]
                sn_ref[t, r0:r0 + STATE_BLK, :] = sn
                prods.append(sn * qq[g:g + 1, :])
            p = jnp.concatenate(prods, axis=0)
            hi = p.astype(BF16)
            lo = (p - hi.astype(F32)).astype(BF16)
            ysum = (lax.dot_general(ones, hi, nt, preferred_element_type=F32)
                    + lax.dot_general(ones, lo, nt, preferred_element_type=F32))
            y_ref[t, cch:cch + 1, :] = ysum[0:1, :]


def _state_step(e_flat, s_all, layer, v3, k3, q3, prev=None, bb=8):
    depth, n = s_all.shape[:2]
    blk3 = lambda a, b_: pl.BlockSpec((bb, a, b_), lambda i: (i, 0, 0))
    sblk = pl.BlockSpec((None, bb, STATE_ROWS, LANES), lambda i: (layer, i, 0, 0))
    in_specs = [pl.BlockSpec(memory_space=pltpu.SMEM), sblk, blk3(SUBLANES, LANES), blk3(4, LANES), blk3(4, LANES)]
    args = [e_flat, s_all, v3, k3, q3]
    aliases = {}
    if prev is not None:
        in_specs.append(pl.BlockSpec(memory_space=pl.ANY))
        args.append(prev)
        aliases = {len(args) - 1: 0}
        oblk = sblk
    else:
        oblk = pl.BlockSpec((depth, bb, STATE_ROWS, LANES), lambda i: (0, i, 0, 0))
    return pl.pallas_call(
        functools.partial(_state_step_kernel, bb=bb, layer=layer, fill=prev is None),
        grid=(n // bb,),
        in_specs=in_specs,
        out_specs=[oblk, blk3(SUBLANES, LANES)],
        out_shape=[jax.ShapeDtypeStruct((depth, n, STATE_ROWS, LANES), F32),
                   jax.ShapeDtypeStruct((n, SUBLANES, LANES), F32)],
        input_output_aliases=aliases,
        compiler_params=_cparams(("parallel",)),
        name="state_step",
    )(*args)


def _sample_post_kernel(ys_ref, xs_ref, z_ref, yr_ref, g_ref, dfull_ref, nw_ref, os_ref, or_ref):
    os_ref[...] = _ssd_gate_norm(ys_ref[...], xs_ref[...], z_ref[...].astype(F32), dfull_ref[...],
                                 nw_ref[...]).astype(os_ref.dtype)
    or_ref[...] = _ret_gate_norm(yr_ref[...], g_ref[...].astype(F32)).astype(or_ref.dtype)


def _sample_post(y_ssd, xs, proj2, y_ret, dfull, nw):
    n = y_ssd.shape[0]
    full = lambda s: pl.BlockSpec(s, lambda i: (0, 0))
    col = lambda off: pl.BlockSpec((n, D_MODEL), lambda i: (0, off // D_MODEL))
    return pl.pallas_call(
        _sample_post_kernel,
        grid=(1,),
        in_specs=[full((n, D_MODEL)), full((n, D_MODEL)), col(OFF_Z), full((n, D_MODEL)), col(OFF_G),
                  full((1, D_MODEL)), full((1, D_MODEL))],
        out_specs=[full((n, D_MODEL)), full((n, D_MODEL))],
        out_shape=[jax.ShapeDtypeStruct((n, D_MODEL), BF16)] * 2,
        compiler_params=_cparams(("arbitrary",)),
        name="sample_post",
    )(y_ssd, xs, proj2, y_ret, proj2, dfull, nw)


def _merge_kernel(s5_ref, ssd_ref, ret_ref, ga_ref, gb_ref, gc_ref, h_ref, ws5_ref, wssd_ref, wret_ref, wout_ref,
                  o_ref):
    glu = _dot(s5_ref[...], ws5_ref[...])
    y_a = glu[:, :D_MODEL] * _sigmoid(glu[:, D_MODEL:])
    y_b = _dot(ssd_ref[...], wssd_ref[...])
    y_c = _dot(ret_ref[...], wret_ref[...])
    merged = (_sigmoid(ga_ref[...].astype(F32)) * y_a + _sigmoid(gb_ref[...].astype(F32)) * y_b
              + _sigmoid(gc_ref[...].astype(F32)) * y_c)
    o_ref[...] = h_ref[...] + _dot(merged.astype(BF16), wout_ref[...])


def _merge(s5_g, ssd_n, ret_n, proj, h3, ws5, wssd, wret, wout, tm):
    nb, L, _ = h3.shape
    row3 = lambda off: pl.BlockSpec((None, tm, D_MODEL), lambda b, i: (b, i, off // D_MODEL))
    wfull = lambda s: pl.BlockSpec(s, lambda b, i: (0, 0), pipeline_mode=pl.Buffered(1))
    return pl.pallas_call(
        _merge_kernel,
        grid=(nb, L // tm),
        in_specs=[row3(0), row3(0), row3(0),
                  row3(OFF_GATES), row3(OFF_GATES + D_MODEL), row3(OFF_GATES + 2 * D_MODEL), row3(0),
                  wfull((D_MODEL, 2 * D_MODEL)), wfull((D_MODEL, D_MODEL)), wfull((D_MODEL, D_MODEL)),
                  wfull((D_MODEL, D_MODEL))],
        out_specs=row3(0),
        out_shape=jax.ShapeDtypeStruct((nb, L, D_MODEL), F32),
        compiler_params=_cparams(("parallel", "parallel")),
        name="merge",
    )(s5_g, ssd_n, ret_n, proj, proj, proj, h3, ws5, wssd, wret, wout)


def _mlp_kernel(x_ref, nw_ref, wup_ref, wdn_ref, fw_ref, o_ref, xn_ref, acc_ref, *, final_norm):
    k = pl.program_id(1)
    nk = pl.num_programs(1)

    @pl.when(k == 0)
    def _():
        x = x_ref[...]
        xn_ref[...] = _rmsnorm(x, nw_ref[...]).astype(BF16)
        acc_ref[...] = x

    hid = jnp.maximum(_dot(xn_ref[...], wup_ref[...]), 0.0)
    acc_ref[...] += _dot((hid * hid).astype(BF16), wdn_ref[...])

    @pl.when(k == nk - 1)
    def _():
        h = acc_ref[...]
        o_ref[...] = _rmsnorm(h, fw_ref[...]) if final_norm else h


def _mlp(h2, nw, wup, wdn, fw, tm, tf, final_norm):
    rows = h2.shape[0]
    return pl.pallas_call(
        functools.partial(_mlp_kernel, final_norm=final_norm),
        grid=(rows // tm, D_FF // tf),
        in_specs=[pl.BlockSpec((tm, D_MODEL), lambda i, k: (i, 0)),
                  pl.BlockSpec((1, D_MODEL), lambda i, k: (0, 0)),
                  pl.BlockSpec((D_MODEL, tf), lambda i, k: (0, k)),
                  pl.BlockSpec((tf, D_MODEL), lambda i, k: (k, 0)),
                  pl.BlockSpec((1, D_MODEL), lambda i, k: (0, 0))],
        out_specs=pl.BlockSpec((tm, D_MODEL), lambda i, k: (i, 0)),
        out_shape=jax.ShapeDtypeStruct((rows, D_MODEL), F32),
        scratch_shapes=[pltpu.VMEM((tm, D_MODEL), BF16), pltpu.VMEM((tm, D_MODEL), F32)],
        compiler_params=_cparams(("parallel", "arbitrary")),
        name="mlp",
    )(h2, nw, wup, wdn, fw)


def _prep_layer(p):
    out = {}
    row = lambda a: a.reshape(1, -1).astype(F32)
    padrow = lambda a: jnp.pad(a.astype(F32), (0, LANES - a.shape[0])).reshape(1, LANES)
    rep = lambda a: jnp.repeat(a.astype(F32), SSD_HEAD_DIM).reshape(1, D_MODEL)
    out['norm1_w'] = row(p['norm1_w'])
    out['norm2_w'] = row(p['norm2_w'])
    out['conv_w'] = p['ssd_conv_w'].astype(F32)
    out['conv_b'] = row(p['ssd_conv_b'])
    out['b16'] = padrow(p['ssd_dt_bias'])
    out['al16'] = padrow(p['ssd_a_log'])
    out['dfull'] = rep(p['ssd_d'])
    out['ssd_nw'] = row(p['ssd_norm_w'])
    out['s5_d'] = row(p['s5_d'])
    abr, abi, bbr, bbi = _s5_discretise(p['s5_a_re'].astype(F32), p['s5_a_im'].astype(F32),
                                        p['s5_log_dt'].astype(F32), p['s5_b_re'].astype(F32),
                                        p['s5_b_im'].astype(F32))
    out['s5'] = _s5_weights(abr, abi, bbr, bbi, p['s5_c_re'].astype(F32), p['s5_c_im'].astype(F32))
    for name in ('w_s5_out', 'w_ssd_out', 'w_ret_out', 'w_out', 'w_mlp_up', 'w_mlp_down'):
        out[name] = p[name].astype(BF16)
    return out


def _head_expand():
    e = (jnp.arange(LANES)[:, None] == (jnp.arange(D_MODEL)[None, :] // SSD_HEAD_DIM))
    return e.astype(BF16)


def _tile(n, pref):
    return pref if n % pref == 0 else n


def _prompt_trunk(x, layers, final_w):
    nb, L, _ = x.shape
    eh = _head_expand()
    _, lm, gp, we, gq = _ret_consts(CHUNK)
    cos2, sin2 = _rope_tables(jnp.arange(L, dtype=jnp.int32))
    zeros_state = jnp.zeros((nb, S5_NSTATE), F32)
    steps = 64
    h = x
    outs = {k: [] for k in ('re', 'im', 'ssd', 'conv', 'ret')}
    for li, lp in enumerate(layers):
        tm = _tile(L, 1024)
        proj, dt = _proj_main(h, lp['norm1_w'], *lp['proj_w'], tm)
        wbr, wbi, abr, abi, wcr, wci = lp['s5']
        y_s5, hr, hi = _s5_scan(proj, wbr, wbi, abr, abi, wcr, wci, lp['s5_d'],
                                zeros_state, zeros_state, steps)
        ssd_n, ssd_s = _ssd_prompt(proj, dt, lp['conv_w'], lp['conv_b'], lp['b16'], lp['al16'],
                                   lp['dfull'], lp['ssd_nw'], eh)
        ret_n, ret_s = _ret_prompt(proj, cos2, sin2, lm, gp, we, gq)
        h = _merge(y_s5, ssd_n, ret_n, proj, h,
                   lp['w_s5_out'], lp['w_ssd_out'], lp['w_ret_out'], lp['w_out'], _tile(L, 512))
        last = li == len(layers) - 1
        h = _mlp(h.reshape(nb * L, D_MODEL), lp['norm2_w'], lp['w_mlp_up'], lp['w_mlp_down'], final_w,
                 _tile(nb * L, 512), 1024, last).reshape(nb, L, D_MODEL)
        outs['re'].append(hr.reshape(nb, S5_GROUPS, S5_STATE))
        outs['im'].append(hi.reshape(nb, S5_GROUPS, S5_STATE))
        outs['ssd'].append(ssd_s.reshape(nb, SSD_HEADS, SSD_HEAD_DIM, SSD_STATE))
        outs['conv'].append(proj[:, L - (SSD_CONV - 1):, OFF_XS:OFF_XS + SSD_CONV_DIM].astype(F32))
        outs['ret'].append(ret_s.reshape(nb, RET_HEADS, RET_V_DIM, RET_QK_DIM))
    return (h,) + tuple(jnp.stack(outs[k]) for k in ('re', 'im', 'ssd', 'conv', 'ret'))


def _sample_trunk(x, states, layers, final_w):
    n = x.shape[0]
    st_re, st_im, st_ssd, st_conv, st_ret = states
    eh = _head_expand()
    log_gamma, _, _, _, _ = _ret_consts(1)
    cos2, sin2 = _rope_tables(PAST_LEN + jnp.arange(1, dtype=jnp.int32))
    e_ret = jnp.broadcast_to(jnp.repeat(jnp.exp(log_gamma), RET_V_DIM // STATE_BLK)[None, :],
                             (n, STATE_ROWS // STATE_BLK)).reshape(-1)
    h = x.reshape(1, n, D_MODEL)
    depth = len(layers)
    ssd_all = st_ssd.reshape(depth, n, STATE_ROWS, SSD_STATE).astype(F32)
    ret_all = st_ret.reshape(depth, n, STATE_ROWS, RET_QK_DIM).astype(F32)
    ssd_new = ret_new = None
    outs = {k: [] for k in ('re', 'im', 'conv')}
    for li, lp in enumerate(layers):
        proj, dt = _proj_main(h, lp['norm1_w'], *lp['proj_w'], n)
        wbr, wbi, abr, abi, wcr, wci = lp['s5']
        y_s5, hr, hi = _s5_scan(proj, wbr, wbi, abr, abi, wcr, wci, lp['s5_d'],
                                st_re[li].reshape(n, S5_NSTATE).astype(F32),
                                st_im[li].reshape(n, S5_NSTATE).astype(F32), 1)
        proj2, dt2 = proj.reshape(n, N_MAIN), dt.reshape(n, LANES)
        conv2 = st_conv[li].reshape(n, (SSD_CONV - 1) * SSD_CONV_DIM).astype(F32)
        vv, e16, bs, cs, xs, conv_new, qr, kr = _sample_pre(
            proj2, dt2, conv2, lp['conv_w'], lp['conv_b'], lp['b16'], lp['al16'], eh, cos2, sin2)
        ssd_new, y_ssd = _state_step(
            e16[:, :SSD_HEADS].reshape(-1), ssd_all, li,
            vv.reshape(n, SUBLANES, LANES), bs.reshape(n, SSD_GROUPS, SSD_STATE),
            cs.reshape(n, SSD_GROUPS, SSD_STATE), prev=ssd_new)
        vret = proj2[:, OFF_V:OFF_V + D_MODEL].astype(F32)
        ret_new, y_ret = _state_step(
            e_ret, ret_all, li,
            vret.reshape(n, SUBLANES, LANES), kr.reshape(n, RET_HEADS, RET_QK_DIM),
            qr.reshape(n, RET_HEADS, RET_QK_DIM), prev=ret_new)
        ssd_n, ret_n = _sample_post(y_ssd.reshape(n, D_MODEL), xs, proj2, y_ret.reshape(n, D_MODEL),
                                    lp['dfull'], lp['ssd_nw'])
        h = _merge(y_s5, ssd_n.reshape(1, n, D_MODEL), ret_n.reshape(1, n, D_MODEL), proj, h,
                   lp['w_s5_out'], lp['w_ssd_out'], lp['w_ret_out'], lp['w_out'], n)
        last = li == len(layers) - 1
        h = _mlp(h.reshape(n, D_MODEL), lp['norm2_w'], lp['w_mlp_up'], lp['w_mlp_down'], final_w,
                 n, 1024, last).reshape(1, n, D_MODEL)
        outs['re'].append(hr.reshape(n, S5_GROUPS, S5_STATE))
        outs['im'].append(hi.reshape(n, S5_GROUPS, S5_STATE))
        outs['conv'].append(conv_new.reshape(n, SSD_CONV - 1, SSD_CONV_DIM))
    return (h.reshape(n, 1, D_MODEL), jnp.stack(outs['re']), jnp.stack(outs['im']),
            ssd_new.reshape(depth, n, SSD_HEADS, SSD_HEAD_DIM, SSD_STATE), jnp.stack(outs['conv']),
            ret_new.reshape(depth, n, RET_HEADS, RET_V_DIM, RET_QK_DIM))


def kernel(x_prompt, x_sample, state_s5_re, state_s5_im, state_ssd, state_conv, state_ret, norm1_w, w_in, s5_a_re, s5_a_im, s5_log_dt, s5_b_re, s5_b_im, s5_c_re, s5_c_im, s5_d, w_s5_out, ssd_conv_w, ssd_conv_b, ssd_dt_bias, ssd_a_log, ssd_d, ssd_norm_w, w_ssd_out, w_ret_out, w_out, norm2_w, w_mlp_up, w_mlp_down, final_norm_w):
    params = dict(norm1_w=norm1_w, w_in=w_in, s5_a_re=s5_a_re, s5_a_im=s5_a_im, s5_log_dt=s5_log_dt,
                  s5_b_re=s5_b_re, s5_b_im=s5_b_im, s5_c_re=s5_c_re, s5_c_im=s5_c_im, s5_d=s5_d,
                  w_s5_out=w_s5_out, ssd_conv_w=ssd_conv_w, ssd_conv_b=ssd_conv_b, ssd_dt_bias=ssd_dt_bias,
                  ssd_a_log=ssd_a_log, ssd_d=ssd_d, ssd_norm_w=ssd_norm_w, w_ssd_out=w_ssd_out,
                  w_ret_out=w_ret_out, w_out=w_out, norm2_w=norm2_w, w_mlp_up=w_mlp_up, w_mlp_down=w_mlp_down)
    depth = w_in.shape[0]
    layers = [_prep_layer({name: arr[l] for name, arr in params.items() if name != 'w_in'}) for l in range(depth)]
    dt_start = SLAB_B_START - SSD_HEADS
    w_a = w_in[:, :, :SLAB_A_COLS].astype(BF16)
    w_b = w_in[:, :, SLAB_B_START:].astype(BF16)
    w_dt = jnp.pad(w_in[:, :, dt_start:SLAB_B_START], ((0, 0), (0, 0), (0, LANES - SSD_HEADS))).astype(BF16)
    for l, lp in enumerate(layers):
        lp['proj_w'] = (w_a, w_b, w_dt, l)
    final_w = final_norm_w.reshape(1, D_MODEL).astype(F32)
    y_p, re_p, im_p, ssd_p, conv_p, ret_p = _prompt_trunk(x_prompt.astype(F32), layers, final_w)
    y_s, re_s, im_s, ssd_s, conv_s, ret_s = _sample_trunk(
        x_sample.astype(F32), (state_s5_re, state_s5_im, state_ssd, state_conv, state_ret), layers, final_w)
    return (y_p, y_s, re_p, im_p, ssd_p, conv_p, ret_p, re_s, im_s, ssd_s, conv_s, ret_s)
```

```python
import functools
import math

import jax
import jax.numpy as jnp
from jax import lax
from jax.experimental import pallas as pl
from jax.experimental.pallas import tpu as pltpu

F32 = jnp.float32
BF16 = jnp.bfloat16

D_MODEL = 1024
S5_GROUP = 16
S5_STATE = 64
S5_GROUPS = D_MODEL // S5_GROUP
S5_NSTATE = S5_GROUPS * S5_STATE
S5_SUPER = 8
SSD_HEADS = 16
SSD_HEAD_DIM = 64
SSD_GROUPS = 4
SSD_STATE = 128
SSD_CONV = 4
SSD_CONV_DIM = D_MODEL + 2 * SSD_GROUPS * SSD_STATE
RET_HEADS = 4
RET_QK_DIM = 128
RET_V_DIM = 256
ROPE_BASE = 10000.0
D_FF = 4 * D_MODEL
CHUNK = 128
EPS = 1e-6
PAST_LEN = 16384
LANES = 128
SUBLANES = 8
VMEM_LIMIT = 52 * 1024 * 1024

OFF_U, OFF_Z, OFF_XS, OFF_BC, OFF_Q, OFF_K, OFF_V, OFF_G, OFF_GATES = (
    0, 1024, 2048, 3072, 4096, 4608, 5120, 6144, 7168)
N_MAIN = 10240
SLAB_A_COLS = OFF_Q
SLAB_B_START = 2 * D_MODEL + SSD_CONV_DIM + SSD_HEADS
PROJ_TN = 2048


def _cparams(sem):
    return pltpu.CompilerParams(dimension_semantics=sem, vmem_limit_bytes=VMEM_LIMIT)


def _sigmoid(x):
    return 1.0 / (1.0 + jnp.exp(-x))


def _silu(x):
    return x * _sigmoid(x)


def _softplus(x):
    return jnp.maximum(x, 0.0) + jnp.log1p(jnp.exp(-jnp.abs(x)))


def _gelu_tanh(x):
    return 0.5 * x * (1.0 + jnp.tanh(math.sqrt(2.0 / math.pi) * (x + 0.044715 * (x * x * x))))


def _rmsnorm(x, w):
    return x * lax.rsqrt(jnp.mean(x * x, axis=-1, keepdims=True) + EPS) * w


def _cumsum_rows(x):
    n = x.shape[0]
    row = lax.broadcasted_iota(jnp.int32, x.shape, 0)
    k = 1
    while k < n:
        x = x + jnp.where(row >= k, pltpu.roll(x, k, 0), 0.0)
        k *= 2
    return x


def _group_rms(y, width):
    outs = []
    for s in range(0, y.shape[-1], width):
        yg = y[:, s:s + width]
        outs.append(yg * lax.rsqrt(jnp.mean(yg * yg, axis=-1, keepdims=True) + EPS))
    return jnp.concatenate(outs, axis=-1)


def _dot(a, b):
    return jnp.dot(a, b, preferred_element_type=F32)


def _split_dot(x, e):
    hi = x.astype(BF16)
    lo = (x - hi.astype(F32)).astype(BF16)
    return _dot(hi, e) + _dot(lo, e)


def _proj_kernel(x_ref, nw_ref, wa_ref, wb_ref, wdt_ref, o_ref, odt_ref, xn_ref, *, na):
    j = pl.program_id(2)

    @pl.when(j == 0)
    def _():
        xn = _rmsnorm(x_ref[...], nw_ref[...]).astype(BF16)
        xn_ref[...] = xn
        odt_ref[...] = _dot(xn, wdt_ref[...])

    @pl.when(j < na)
    def _():
        o_ref[...] = _dot(xn_ref[...], wa_ref[...]).astype(o_ref.dtype)

    @pl.when(j >= na)
    def _():
        o_ref[...] = _dot(xn_ref[...], wb_ref[...]).astype(o_ref.dtype)


PREP_TN = 1024
DT_START = SLAB_B_START - SSD_HEADS
SLAB_SHIFT = SLAB_B_START % PREP_TN


def _slab_a_kernel(w_ref, wdt_ref, oa_ref, odt_ref):
    oa_ref[...] = w_ref[...].astype(BF16)
    lane = lax.broadcasted_iota(jnp.int32, (D_MODEL, LANES), 1)
    odt_ref[...] = jnp.where(lane < SSD_HEADS, wdt_ref[:, :LANES], 0.0).astype(BF16)


def _slab_b_kernel(lo_ref, hi_ref, o_ref):
    keep = PREP_TN - SLAB_SHIFT
    lane = lax.broadcasted_iota(jnp.int32, (D_MODEL, PREP_TN), 1)
    lo = pltpu.roll(lo_ref[...], keep, 1)
    hi = pltpu.roll(hi_ref[...], keep, 1)
    o_ref[...] = jnp.where(lane < keep, lo, hi).astype(BF16)


def _proj_weight_slabs(w_in):
    depth = w_in.shape[0]
    tn = PREP_TN
    blk = lambda f: pl.BlockSpec((None, D_MODEL, tn), f)
    w_a, w_dt = pl.pallas_call(
        _slab_a_kernel,
        grid=(depth, SLAB_A_COLS // tn),
        in_specs=[blk(lambda l, j: (l, 0, j)), blk(lambda l, j: (l, 0, DT_START // tn))],
        out_specs=[blk(lambda l, j: (l, 0, j)), pl.BlockSpec((None, D_MODEL, LANES), lambda l, j: (l, 0, 0))],
        out_shape=[jax.ShapeDtypeStruct((depth, D_MODEL, SLAB_A_COLS), BF16),
                   jax.ShapeDtypeStruct((depth, D_MODEL, LANES), BF16)],
        compiler_params=_cparams(("parallel", "arbitrary")),
        name="weight_slab_a",
    )(w_in, w_in)
    nbcols = N_MAIN - SLAB_A_COLS
    first = SLAB_B_START // tn
    w_b = pl.pallas_call(
        _slab_b_kernel,
        grid=(depth, nbcols // tn),
        in_specs=[blk(lambda l, j: (l, 0, first + j)), blk(lambda l, j: (l, 0, first + j + 1))],
        out_specs=blk(lambda l, j: (l, 0, j)),
        out_shape=jax.ShapeDtypeStruct((depth, D_MODEL, nbcols), BF16),
        compiler_params=_cparams(("parallel", "parallel")),
        name="weight_slab_b",
    )(w_in, w_in)
    return w_a, w_b, w_dt


def _proj_main(x3, nw, w_a, w_b, wdt, layer, tm):
    nb, L, _ = x3.shape
    tn = PROJ_TN
    na = SLAB_A_COLS // tn
    return pl.pallas_call(
        functools.partial(_proj_kernel, na=na),
        grid=(nb, L // tm, N_MAIN // tn),
        in_specs=[pl.BlockSpec((None, tm, D_MODEL), lambda b, i, j: (b, i, 0)),
                  pl.BlockSpec((1, D_MODEL), lambda b, i, j: (0, 0)),
                  pl.BlockSpec((None, D_MODEL, tn), lambda b, i, j: (layer, 0, jnp.minimum(j, na - 1))),
                  pl.BlockSpec((None, D_MODEL, tn), lambda b, i, j: (layer, 0, jnp.maximum(j - na, 0))),
                  pl.BlockSpec((None, D_MODEL, LANES), lambda b, i, j: (layer, 0, 0))],
        out_specs=[pl.BlockSpec((None, tm, tn), lambda b, i, j: (b, i, j)),
                   pl.BlockSpec((None, tm, LANES), lambda b, i, j: (b, i, 0))],
        out_shape=[jax.ShapeDtypeStruct((nb, L, N_MAIN), BF16), jax.ShapeDtypeStruct((nb, L, LANES), F32)],
        scratch_shapes=[pltpu.VMEM((tm, D_MODEL), BF16)],
        compiler_params=_cparams(("parallel", "parallel", "arbitrary")),
        name="proj_main",
    )(x3, nw, w_a, w_b, wdt)


def _s5_disc_kernel(are_ref, aim_ref, ldt_ref, bre_ref, bim_ref, abr_ref, abi_ref, bbr_ref, bbi_ref):
    ar, ai = are_ref[...], aim_ref[...]
    dt = jnp.exp(ldt_ref[...])
    mag = jnp.exp(ar * dt)
    abr = mag * jnp.cos(ai * dt)
    abi = mag * jnp.sin(ai * dt)
    abr_ref[...] = abr
    abi_ref[...] = abi
    nr, ni = abr - 1.0, abi
    den = 1.0 / (ar * ar + ai * ai)
    cr = (nr * ar + ni * ai) * den
    ci = (ni * ar - nr * ai) * den
    br, bi = bre_ref[...], bim_ref[...]
    bbr_ref[...] = cr * br - ci * bi
    bbi_ref[...] = cr * bi + ci * br


def _s5_discretise(a_re, a_im, log_dt, b_re, b_im):
    g, p = a_re.shape
    gs = b_re.shape[-1]
    full3 = lambda s: pl.BlockSpec(s, lambda: (0, 0, 0))
    return pl.pallas_call(
        _s5_disc_kernel,
        in_specs=[full3((g, 1, p)), full3((g, 1, p)), full3((g, 1, 1)), full3((g, gs, p)), full3((g, gs, p))],
        out_specs=[full3((g, 1, p)), full3((g, 1, p)), full3((g, gs, p)), full3((g, gs, p))],
        out_shape=[jax.ShapeDtypeStruct((g, 1, p), F32)] * 2 + [jax.ShapeDtypeStruct((g, gs, p), F32)] * 2,
        name="s5_discretise",
    )(a_re.reshape(g, 1, p), a_im.reshape(g, 1, p), log_dt.reshape(g, 1, 1),
      jnp.swapaxes(b_re, 1, 2), jnp.swapaxes(b_im, 1, 2))


def _s5_kernel(u_ref, wbr_ref, wbi_ref, ar_ref, ai_ref, wcr_ref, wci_ref, d_ref, h0r_ref, h0i_ref,
               y_ref, hr_ref, hi_ref, bur, bui, *, nbatch, steps, lane_w):
    @pl.when(pl.program_id(0) == 0)
    def _():
        hr_ref[...] = h0r_ref[...]
        hi_ref[...] = h0i_ref[...]

    cw = S5_SUPER * S5_GROUP
    sw = S5_SUPER * S5_STATE
    batch_major = steps > 1
    u = u_ref[...].astype(F32)
    if batch_major:
        u = jnp.swapaxes(u, 0, 1).reshape(steps * nbatch, D_MODEL)
    ub = u.astype(BF16)
    for j in range(S5_GROUPS // S5_SUPER):
        uj = ub[:, j * cw:(j + 1) * cw]
        bur[:, j * sw:(j + 1) * sw] = _dot(uj, wbr_ref[j])
        bui[:, j * sw:(j + 1) * sw] = _dot(uj, wbi_ref[j])

        for lc in range(sw // lane_w):
            sl = slice(j * sw + lc * lane_w, j * sw + (lc + 1) * lane_w)
            are = jnp.broadcast_to(ar_ref[:, sl], (nbatch, lane_w))
            aim = jnp.broadcast_to(ai_ref[:, sl], (nbatch, lane_w))
            hr, hi = hr_ref[:, sl], hi_ref[:, sl]
            for t in range(steps):
                rows = slice(t * nbatch, (t + 1) * nbatch)
                hr, hi = are * hr - aim * hi + bur[rows, sl], are * hi + aim * hr + bui[rows, sl]
                bur[rows, sl] = hr
                bui[rows, sl] = hi
            hr_ref[:, sl] = hr
            hi_ref[:, sl] = hi

        yj = (_dot(bur[:, j * sw:(j + 1) * sw].astype(BF16), wcr_ref[j])
              + _dot(bui[:, j * sw:(j + 1) * sw].astype(BF16), wci_ref[j]))
        yj = _gelu_tanh(yj + d_ref[:, j * cw:(j + 1) * cw] * u[:, j * cw:(j + 1) * cw])
        if batch_major:
            y_ref[:, :, j * cw:(j + 1) * cw] = jnp.swapaxes(yj.reshape(steps, nbatch, cw), 0, 1).astype(y_ref.dtype)
        else:
            y_ref[:, j * cw:(j + 1) * cw] = yj.astype(y_ref.dtype)


def _s5_scan(proj, wbr, wbi, abr, abi, wcr, wci, d, h0r, h0i, steps):
    nb, L, _ = proj.shape
    nsup = S5_GROUPS // S5_SUPER
    cw, sw = S5_SUPER * S5_GROUP, S5_SUPER * S5_STATE
    c2 = lambda s: pl.BlockSpec(s, lambda c: (0, 0))
    c3 = lambda s: pl.BlockSpec(s, lambda c: (0, 0, 0))
    if steps > 1:
        nbatch, grid = nb, L // steps
        u_spec = pl.BlockSpec((nb, steps, D_MODEL), lambda c: (0, c, OFF_U // D_MODEL))
        y_spec = pl.BlockSpec((nb, steps, D_MODEL), lambda c: (0, c, 0))
    else:
        nbatch, grid = L, 1
        u_spec = pl.BlockSpec((None, L, D_MODEL), lambda c: (0, 0, OFF_U // D_MODEL))
        y_spec = pl.BlockSpec((None, L, D_MODEL), lambda c: (0, 0, 0))
    blk = nbatch * steps
    lane_w = max(LANES, S5_NSTATE // nbatch)
    kern = functools.partial(_s5_kernel, nbatch=nbatch, steps=steps, lane_w=lane_w)
    return pl.pallas_call(
        kern,
        grid=(grid,),
        in_specs=[u_spec,
                  c3((nsup, cw, sw)), c3((nsup, cw, sw)),
                  c2((1, S5_NSTATE)), c2((1, S5_NSTATE)),
                  c3((nsup, sw, cw)), c3((nsup, sw, cw)),
                  c2((1, D_MODEL)),
                  c2((nbatch, S5_NSTATE)), c2((nbatch, S5_NSTATE))],
        out_specs=[y_spec, c2((nbatch, S5_NSTATE)), c2((nbatch, S5_NSTATE))],
        out_shape=[jax.ShapeDtypeStruct((nb, L, D_MODEL), BF16),
                   jax.ShapeDtypeStruct((nbatch, S5_NSTATE), F32),
                   jax.ShapeDtypeStruct((nbatch, S5_NSTATE), F32)],
        scratch_shapes=[pltpu.VMEM((blk, S5_NSTATE), F32), pltpu.VMEM((blk, S5_NSTATE), F32)],
        compiler_params=_cparams(("arbitrary",)),
        name="s5_scan",
    )(proj, wbr, wbi, abr, abi, wcr, wci, d, h0r, h0i)


def _s5_weights(abr, abi, bbr, bbi, c_re, c_im):
    nsup = S5_GROUPS // S5_SUPER
    eye = jnp.eye(S5_SUPER, dtype=F32)

    def bexp(b):
        b = b.reshape(nsup, S5_SUPER, S5_GROUP, S5_STATE)
        return (b[:, :, :, None, :] * eye[None, :, None, :, None]).reshape(
            nsup, S5_SUPER * S5_GROUP, S5_SUPER * S5_STATE).astype(BF16)

    def cexp(c):
        c = jnp.swapaxes(c, 1, 2).reshape(nsup, S5_SUPER, S5_STATE, S5_GROUP)
        return (c[:, :, :, None, :] * eye[None, :, None, :, None]).reshape(
            nsup, S5_SUPER * S5_STATE, S5_SUPER * S5_GROUP).astype(BF16)

    return (bexp(bbr), bexp(bbi), abr.reshape(1, S5_NSTATE), abi.reshape(1, S5_NSTATE),
            cexp(c_re), cexp(-c_im))


def _ssd_gate_norm(y, xs, z, dfull, nw):
    y = (y + dfull * xs) * _silu(z)
    return _group_rms(y, D_MODEL // SSD_GROUPS) * nw


def _ssd_chunk_kernel(xs_ref, bc_ref, z_ref, dt_ref, cw_ref, cb_ref, b16_ref, al16_ref,
                      dfull_ref, nw_ref, eh_ref, o_ref, s_ref, xpad, st):
    c = pl.program_id(1)
    nc = pl.num_programs(1)
    q = CHUNK
    pad = SUBLANES

    @pl.when(c == 0)
    def _():
        xpad[0:pad, :] = jnp.zeros((pad, SSD_CONV_DIM), F32)
        st[...] = jnp.zeros(st.shape, F32)

    @pl.when(c > 0)
    def _():
        xpad[0:pad, :] = xpad[q:q + pad, :]

    xpad[pad:pad + q, :D_MODEL] = xs_ref[...].astype(F32)
    xpad[pad:pad + q, D_MODEL:] = bc_ref[...].astype(F32)
    acc = cb_ref[...] + xpad[pad - 3:pad - 3 + q, :] * cw_ref[0:1, :]
    for tap in range(1, SSD_CONV):
        acc = acc + xpad[pad - 3 + tap:pad - 3 + tap + q, :] * cw_ref[tap:tap + 1, :]
    xc = _silu(acc)
    xs = xc[:, :D_MODEL]
    nbc = SSD_GROUPS * SSD_STATE
    bs = xc[:, D_MODEL:D_MODEL + nbc]
    cs = xc[:, D_MODEL + nbc:]

    delta16 = _softplus(dt_ref[...] + b16_ref[...])
    acum16 = _cumsum_rows(delta16 * (-jnp.exp(al16_ref[...])))
    acum_t = acum16.T
    delta_t = delta16.T
    last16 = acum16[q - 1:q, :]
    ea_f = _split_dot(jnp.exp(acum16), eh_ref[...])
    dw_f = _split_dot(delta16 * jnp.exp(last16 - acum16), eh_ref[...])
    xw = (xs * dw_f).astype(BF16)

    ri = lax.broadcasted_iota(jnp.int32, (q, q), 0)
    ci = lax.broadcasted_iota(jnp.int32, (q, q), 1)
    causal = ri >= ci
    hw = D_MODEL // SSD_GROUPS
    hpg = SSD_HEADS // SSD_GROUPS
    xsb = xs.astype(BF16)
    ys = []
    for g in range(SSD_GROUPS):
        bg = bs[:, g * SSD_STATE:(g + 1) * SSD_STATE]
        cg = cs[:, g * SSD_STATE:(g + 1) * SSD_STATE].astype(BF16)
        bgt = bg.T.astype(BF16)
        scores = _dot(cg, bgt)
        sg = st[:, g * hw:(g + 1) * hw]
        y_inter = _dot(cg, sg.astype(BF16)) * ea_f[:, g * hw:(g + 1) * hw]
        for r in range(hpg):
            hd = g * hpg + r
            seg = acum16[:, hd:hd + 1] - acum_t[hd:hd + 1, :]
            lm = jnp.exp(jnp.where(causal, seg, -jnp.inf)) * delta_t[hd:hd + 1, :]
            p = (scores * lm).astype(BF16)
            ys.append(_dot(p, xsb[:, hd * SSD_HEAD_DIM:(hd + 1) * SSD_HEAD_DIM])
                      + y_inter[:, r * SSD_HEAD_DIM:(r + 1) * SSD_HEAD_DIM])
        st[:, g * hw:(g + 1) * hw] = (ea_f[q - 1:q, g * hw:(g + 1) * hw] * sg
                                      + _dot(bgt, xw[:, g * hw:(g + 1) * hw]))
    y = jnp.concatenate(ys, axis=-1)
    o_ref[...] = _ssd_gate_norm(y, xs, z_ref[...].astype(F32), dfull_ref[...], nw_ref[...]).astype(o_ref.dtype)

    @pl.when(c == nc - 1)
    def _():
        for k in range(D_MODEL // LANES):
            s_ref[k * LANES:(k + 1) * LANES, :] = st[:, k * LANES:(k + 1) * LANES].T


def _ssd_prompt(proj, dt, cw, cb, b16, al16, dfull, nw, eh):
    nb, L, _ = proj.shape
    q = CHUNK
    c2 = lambda s: pl.BlockSpec(s, lambda b, c: (0, 0))
    return pl.pallas_call(
        _ssd_chunk_kernel,
        grid=(nb, L // q),
        in_specs=[pl.BlockSpec((None, q, D_MODEL), lambda b, c: (b, c, OFF_XS // D_MODEL)),
                  pl.BlockSpec((None, q, D_MODEL), lambda b, c: (b, c, OFF_BC // D_MODEL)),
                  pl.BlockSpec((None, q, D_MODEL), lambda b, c: (b, c, OFF_Z // D_MODEL)),
                  pl.BlockSpec((None, q, LANES), lambda b, c: (b, c, 0)),
                  c2((SSD_CONV, SSD_CONV_DIM)), c2((1, SSD_CONV_DIM)),
                  c2((1, LANES)), c2((1, LANES)),
                  c2((1, D_MODEL)), c2((1, D_MODEL)),
                  c2((LANES, D_MODEL))],
        out_specs=[pl.BlockSpec((None, q, D_MODEL), lambda b, c: (b, c, 0)),
                   pl.BlockSpec((None, D_MODEL, SSD_STATE), lambda b, c: (b, 0, 0))],
        out_shape=[jax.ShapeDtypeStruct((nb, L, D_MODEL), BF16),
                   jax.ShapeDtypeStruct((nb, D_MODEL, SSD_STATE), F32)],
        scratch_shapes=[pltpu.VMEM((q + 2 * SUBLANES, SSD_CONV_DIM), F32),
                        pltpu.VMEM((SSD_STATE, D_MODEL), F32)],
        compiler_params=_cparams(("parallel", "arbitrary")),
        name="ssd_chunk",
    )(proj, proj, proj, dt, cw, cb, b16, al16, dfull, nw, eh)


def _rotary(x, cos2, sin2):
    outs = []
    for h in range(RET_HEADS):
        xh = x[:, h * RET_QK_DIM:(h + 1) * RET_QK_DIM]
        outs.append(xh * cos2 + pltpu.roll(xh, RET_QK_DIM // 2, 1) * sin2)
    return outs


def _ret_gate_norm(y, g):
    return _group_rms(y, RET_V_DIM) * _silu(g)


def _ret_chunk_kernel(q_ref, k_ref, v_ref, g_ref, cos_ref, sin_ref, lm_ref, gp_ref, we_ref, gq_ref,
                      o_ref, s_ref, st):
    c = pl.program_id(1)
    nc = pl.num_programs(1)

    @pl.when(c == 0)
    def _():
        st[...] = jnp.zeros(st.shape, F32)

    cos2, sin2 = cos_ref[...], sin_ref[...]
    for bi in range(q_ref.shape[0]):
        qr = _rotary(q_ref[bi].astype(F32), cos2, sin2)
        kr = _rotary(k_ref[bi].astype(F32), cos2, sin2)
        vb = v_ref[bi]
        vw = (vb.astype(F32) * we_ref[...]).astype(BF16)
        ys = []
        for h in range(RET_HEADS):
            sl = slice(h * RET_V_DIM, (h + 1) * RET_V_DIM)
            qh = qr[h].astype(BF16)
            kht = (kr[h] * (RET_QK_DIM ** -0.5)).T.astype(BF16)
            p = (_dot(qh, kht) * lm_ref[h]).astype(BF16)
            sh = st[bi, :, sl]
            ys.append(_dot(p, vb[:, sl]) + _dot(qh, sh.astype(BF16)) * gp_ref[:, sl])
            st[bi, :, sl] = gq_ref[:, sl] * sh + _dot(kht, vw[:, sl])
        y = jnp.concatenate(ys, axis=-1)
        o_ref[bi] = _ret_gate_norm(y, g_ref[bi].astype(F32)).astype(o_ref.dtype)

    @pl.when(c == nc - 1)
    def _():
        for bi in range(q_ref.shape[0]):
            for k in range(D_MODEL // LANES):
                s_ref[bi, k * LANES:(k + 1) * LANES, :] = st[bi, :, k * LANES:(k + 1) * LANES].T


def _ret_consts(q):
    hidx = jnp.arange(RET_HEADS, dtype=F32)
    log_gamma = jnp.log1p(-jnp.exp2(-5.0 - hidx))
    i = jnp.arange(q, dtype=F32)
    seg = (i[:, None] - i[None, :])[None] * log_gamma[:, None, None]
    lm = jnp.exp(jnp.where((i[:, None] >= i[None, :])[None], seg, -jnp.inf))
    rep = lambda t: jnp.repeat(t, RET_V_DIM, axis=-1)
    gp = rep(jnp.exp((i[:, None] + 1.0) * log_gamma[None, :]))
    we = rep(jnp.exp((q - 1.0 - i[:, None]) * log_gamma[None, :]))
    gq = rep(jnp.exp(q * log_gamma)[None, :])
    return log_gamma, lm, gp, we, gq


def _rope_tables(pos):
    half = RET_QK_DIM // 2
    inv = ROPE_BASE ** (-jnp.arange(half, dtype=F32) / half)
    ang = pos.astype(F32)[:, None] * inv[None]
    cos, sin = jnp.cos(ang), jnp.sin(ang)
    return jnp.concatenate([cos, cos], axis=-1), jnp.concatenate([-sin, sin], axis=-1)


def _ret_prompt(proj, cos2, sin2, lm, gp, we, gq, nseq=2):
    nb, L, _ = proj.shape
    q = CHUNK
    qkw = RET_HEADS * RET_QK_DIM
    c2 = lambda s: pl.BlockSpec(s, lambda b, c: (0, 0))
    return pl.pallas_call(
        _ret_chunk_kernel,
        grid=(nb // nseq, L // q),
        in_specs=[pl.BlockSpec((nseq, q, qkw), lambda b, c: (b, c, OFF_Q // qkw)),
                  pl.BlockSpec((nseq, q, qkw), lambda b, c: (b, c, OFF_K // qkw)),
                  pl.BlockSpec((nseq, q, D_MODEL), lambda b, c: (b, c, OFF_V // D_MODEL)),
                  pl.BlockSpec((nseq, q, D_MODEL), lambda b, c: (b, c, OFF_G // D_MODEL)),
                  pl.BlockSpec((q, RET_QK_DIM), lambda b, c: (c, 0)),
                  pl.BlockSpec((q, RET_QK_DIM), lambda b, c: (c, 0)),
                  pl.BlockSpec((RET_HEADS, q, q), lambda b, c: (0, 0, 0)),
                  c2((q, D_MODEL)), c2((q, D_MODEL)), c2((1, D_MODEL))],
        out_specs=[pl.BlockSpec((nseq, q, D_MODEL), lambda b, c: (b, c, 0)),
                   pl.BlockSpec((nseq, D_MODEL, RET_QK_DIM), lambda b, c: (b, 0, 0))],
        out_shape=[jax.ShapeDtypeStruct((nb, L, D_MODEL), BF16),
                   jax.ShapeDtypeStruct((nb, D_MODEL, RET_QK_DIM), F32)],
        scratch_shapes=[pltpu.VMEM((nseq, RET_QK_DIM, D_MODEL), F32)],
        compiler_params=_cparams(("parallel", "arbitrary")),
        name="ret_chunk",
    )(proj, proj, proj, proj, cos2, sin2, lm, gp, we, gq)


def _sample_pre_kernel(xsin_ref, bcin_ref, dt_ref, q_ref, k_ref, conv_ref, cw_ref, cb_ref, b16_ref, al16_ref,
                       eh_ref, cos_ref, sin_ref,
                       vv_ref, e16_ref, bs_ref, cs_ref, xs_ref, convn_ref, qr_ref, kr_ref):
    cd = SSD_CONV_DIM
    x = jnp.concatenate([xsin_ref[...], bcin_ref[...]], axis=-1).astype(F32)
    acc = cb_ref[...] + x * cw_ref[SSD_CONV - 1:SSD_CONV, :]
    for tap in range(SSD_CONV - 1):
        acc = acc + conv_ref[:, tap * cd:(tap + 1) * cd] * cw_ref[tap:tap + 1, :]
    convn_ref[:, 0:(SSD_CONV - 2) * cd] = conv_ref[:, cd:(SSD_CONV - 1) * cd]
    convn_ref[:, (SSD_CONV - 2) * cd:] = x
    xc = _silu(acc)
    xs = xc[:, :D_MODEL]
    nbc = SSD_GROUPS * SSD_STATE
    xs_ref[...] = xs
    bs_ref[...] = xc[:, D_MODEL:D_MODEL + nbc]
    cs_ref[...] = xc[:, D_MODEL + nbc:]
    delta16 = _softplus(dt_ref[...] + b16_ref[...])
    e16_ref[...] = jnp.exp(delta16 * (-jnp.exp(al16_ref[...])))
    vv_ref[...] = xs * _split_dot(delta16, eh_ref[...])
    cos2, sin2 = cos_ref[...], sin_ref[...]
    qr_ref[...] = jnp.concatenate(_rotary(q_ref[...].astype(F32), cos2, sin2), axis=-1)
    kr_ref[...] = jnp.concatenate(_rotary(k_ref[...].astype(F32), cos2, sin2), axis=-1) * (RET_QK_DIM ** -0.5)


def _sample_pre(proj2, dt2, conv2, cw, cb, b16, al16, eh, cos2, sin2):
    n = proj2.shape[0]
    qkw = RET_HEADS * RET_QK_DIM
    nbc = SSD_GROUPS * SSD_STATE
    full = lambda s: pl.BlockSpec(s, lambda i: (0, 0))
    col = lambda w, off: pl.BlockSpec((n, w), lambda i: (0, off // w))
    shp = lambda w: jax.ShapeDtypeStruct((n, w), F32)
    return pl.pallas_call(
        _sample_pre_kernel,
        grid=(1,),
        in_specs=[col(D_MODEL, OFF_XS), col(D_MODEL, OFF_BC), full((n, LANES)), col(qkw, OFF_Q), col(qkw, OFF_K),
                  full((n, (SSD_CONV - 1) * SSD_CONV_DIM)),
                  full((SSD_CONV, SSD_CONV_DIM)), full((1, SSD_CONV_DIM)),
                  full((1, LANES)), full((1, LANES)),
                  full((LANES, D_MODEL)), full((1, RET_QK_DIM)), full((1, RET_QK_DIM))],
        out_specs=[full((n, D_MODEL)), full((n, LANES)), full((n, nbc)), full((n, nbc)), full((n, D_MODEL)),
                   full((n, (SSD_CONV - 1) * SSD_CONV_DIM)), full((n, qkw)), full((n, qkw))],
        out_shape=[shp(D_MODEL), shp(LANES), shp(nbc), shp(nbc), shp(D_MODEL),
                   shp((SSD_CONV - 1) * SSD_CONV_DIM), shp(qkw), shp(qkw)],
        compiler_params=_cparams(("arbitrary",)),
        name="sample_pre",
    )(proj2, proj2, dt2, proj2, proj2, conv2, cw, cb, b16, al16, eh, cos2, sin2)


STATE_ROWS = D_MODEL
STATE_BLK = 64


def _state_step_kernel(e_ref, s_ref, v_ref, k_ref, q_ref, *rest, bb, layer, fill):
    sn_all, y_ref = rest[-2:]
    if fill:
        for l in range(sn_all.shape[0]):
            if l != layer:
                sn_all[l] = jnp.zeros(sn_all.shape[1:], F32)
        sn_ref = sn_all.at[layer]
    else:
        sn_ref = sn_all
    i = pl.program_id(0)
    nblk = STATE_ROWS // STATE_BLK
    gw = STATE_ROWS // 4
    ones = jnp.ones((SUBLANES, LANES), BF16)
    zpad = jnp.zeros((LANES - SUBLANES, LANES), F32)
    nt = (((1,), (1,)), ((), ()))
    for t in range(bb):
        b = i * bb + t
        vt = jnp.concatenate([v_ref[t], zpad], axis=0).T
        kk = k_ref[t]
        qq = q_ref[t]
        for cch in range(STATE_ROWS // LANES):
            prods = []
            for half in range(LANES // STATE_BLK):
                blk = cch * (LANES // STATE_BLK) + half
                r0 = blk * STATE_BLK
                g = r0 // gw
                vcol = vt[half * STATE_BLK:(half + 1) * STATE_BLK, cch:cch + 1]
                sn = e_ref[b * nblk + blk] * s_ref[t, r0:r0 + STATE_BLK, :] + vcol * kk[g:g + 1, :]
                sn_ref[t, r0:r0 + STATE_BLK, :] = sn
                prods.append(sn * qq[g:g + 1, :])
            p = jnp.concatenate(prods, axis=0)
            hi = p.astype(BF16)
            lo = (p - hi.astype(F32)).astype(BF16)
            ysum = (lax.dot_general(ones, hi, nt, preferred_element_type=F32)
                    + lax.dot_general(ones, lo, nt, preferred_element_type=F32))
            y_ref[t, cch:cch + 1, :] = ysum[0:1, :]


def _state_step(e_flat, s_all, layer, v3, k3, q3, prev=None, bb=8):
    depth, n = s_all.shape[:2]
    blk3 = lambda a, b_: pl.BlockSpec((bb, a, b_), lambda i: (i, 0, 0))
    sblk = pl.BlockSpec((None, bb, STATE_ROWS, LANES), lambda i: (layer, i, 0, 0))
    in_specs = [pl.BlockSpec(memory_space=pltpu.SMEM), sblk, blk3(SUBLANES, LANES), blk3(4, LANES), blk3(4, LANES)]
    args = [e_flat, s_all, v3, k3, q3]
    aliases = {}
    if prev is not None:
        in_specs.append(pl.BlockSpec(memory_space=pl.ANY))
        args.append(prev)
        aliases = {len(args) - 1: 0}
        oblk = sblk
    else:
        oblk = pl.BlockSpec((depth, bb, STATE_ROWS, LANES), lambda i: (0, i, 0, 0))
    return pl.pallas_call(
        functools.partial(_state_step_kernel, bb=bb, layer=layer, fill=prev is None),
        grid=(n // bb,),
        in_specs=in_specs,
        out_specs=[oblk, blk3(SUBLANES, LANES)],
        out_shape=[jax.ShapeDtypeStruct((depth, n, STATE_ROWS, LANES), F32),
                   jax.ShapeDtypeStruct((n, SUBLANES, LANES), F32)],
        input_output_aliases=aliases,
        compiler_params=_cparams(("parallel",)),
        name="state_step",
    )(*args)


def _sample_post_kernel(ys_ref, xs_ref, z_ref, yr_ref, g_ref, dfull_ref, nw_ref, os_ref, or_ref):
    os_ref[...] = _ssd_gate_norm(ys_ref[...], xs_ref[...], z_ref[...].astype(F32), dfull_ref[...],
                                 nw_ref[...]).astype(os_ref.dtype)
    or_ref[...] = _ret_gate_norm(yr_ref[...], g_ref[...].astype(F32)).astype(or_ref.dtype)


def _sample_post(y_ssd, xs, proj2, y_ret, dfull, nw):
    n = y_ssd.shape[0]
    full = lambda s: pl.BlockSpec(s, lambda i: (0, 0))
    col = lambda off: pl.BlockSpec((n, D_MODEL), lambda i: (0, off // D_MODEL))
    return pl.pallas_call(
        _sample_post_kernel,
        grid=(1,),
        in_specs=[full((n, D_MODEL)), full((n, D_MODEL)), col(OFF_Z), full((n, D_MODEL)), col(OFF_G),
                  full((1, D_MODEL)), full((1, D_MODEL))],
        out_specs=[full((n, D_MODEL)), full((n, D_MODEL))],
        out_shape=[jax.ShapeDtypeStruct((n, D_MODEL), BF16)] * 2,
        compiler_params=_cparams(("arbitrary",)),
        name="sample_post",
    )(y_ssd, xs, proj2, y_ret, proj2, dfull, nw)


def _merge_kernel(s5_ref, ssd_ref, ret_ref, ga_ref, gb_ref, gc_ref, h_ref, ws5_ref, wssd_ref, wret_ref, wout_ref,
                  o_ref):
    glu = _dot(s5_ref[...], ws5_ref[...])
    y_a = glu[:, :D_MODEL] * _sigmoid(glu[:, D_MODEL:])
    y_b = _dot(ssd_ref[...], wssd_ref[...])
    y_c = _dot(ret_ref[...], wret_ref[...])
    merged = (_sigmoid(ga_ref[...].astype(F32)) * y_a + _sigmoid(gb_ref[...].astype(F32)) * y_b
              + _sigmoid(gc_ref[...].astype(F32)) * y_c)
    o_ref[...] = h_ref[...] + _dot(merged.astype(BF16), wout_ref[...])


def _merge(s5_g, ssd_n, ret_n, proj, h3, ws5, wssd, wret, wout, layer, tm):
    nb, L, _ = h3.shape
    row3 = lambda off: pl.BlockSpec((None, tm, D_MODEL), lambda b, i: (b, i, off // D_MODEL))
    wfull = lambda s: pl.BlockSpec((None,) + s, lambda b, i: (layer, 0, 0), pipeline_mode=pl.Buffered(1))
    return pl.pallas_call(
        _merge_kernel,
        grid=(nb, L // tm),
        in_specs=[row3(0), row3(0), row3(0),
                  row3(OFF_GATES), row3(OFF_GATES + D_MODEL), row3(OFF_GATES + 2 * D_MODEL), row3(0),
                  wfull((D_MODEL, 2 * D_MODEL)), wfull((D_MODEL, D_MODEL)), wfull((D_MODEL, D_MODEL)),
                  wfull((D_MODEL, D_MODEL))],
        out_specs=row3(0),
        out_shape=jax.ShapeDtypeStruct((nb, L, D_MODEL), F32),
        compiler_params=_cparams(("parallel", "parallel")),
        name="merge",
    )(s5_g, ssd_n, ret_n, proj, proj, proj, h3, ws5, wssd, wret, wout)


def _mlp_kernel(x_ref, nw_ref, wup_ref, wdn_ref, fw_ref, o_ref, xn_ref, acc_ref, *, final_norm):
    k = pl.program_id(1)
    nk = pl.num_programs(1)

    @pl.when(k == 0)
    def _():
        x = x_ref[...]
        xn_ref[...] = _rmsnorm(x, nw_ref[...]).astype(BF16)
        acc_ref[...] = x

    hid = jnp.maximum(_dot(xn_ref[...], wup_ref[...]), 0.0)
    acc_ref[...] += _dot((hid * hid).astype(BF16), wdn_ref[...])

    @pl.when(k == nk - 1)
    def _():
        h = acc_ref[...]
        o_ref[...] = _rmsnorm(h, fw_ref[...]) if final_norm else h


def _mlp(h2, nw, wup, wdn, fw, layer, tm, tf, final_norm):
    rows = h2.shape[0]
    return pl.pallas_call(
        functools.partial(_mlp_kernel, final_norm=final_norm),
        grid=(rows // tm, D_FF // tf),
        in_specs=[pl.BlockSpec((tm, D_MODEL), lambda i, k: (i, 0)),
                  pl.BlockSpec((1, D_MODEL), lambda i, k: (0, 0)),
                  pl.BlockSpec((None, D_MODEL, tf), lambda i, k: (layer, 0, k)),
                  pl.BlockSpec((None, tf, D_MODEL), lambda i, k: (layer, k, 0)),
                  pl.BlockSpec((1, D_MODEL), lambda i, k: (0, 0))],
        out_specs=pl.BlockSpec((tm, D_MODEL), lambda i, k: (i, 0)),
        out_shape=jax.ShapeDtypeStruct((rows, D_MODEL), F32),
        scratch_shapes=[pltpu.VMEM((tm, D_MODEL), BF16), pltpu.VMEM((tm, D_MODEL), F32)],
        compiler_params=_cparams(("parallel", "arbitrary")),
        name="mlp",
    )(h2, nw, wup, wdn, fw)


def _prep_layer(p):
    out = {}
    row = lambda a: a.reshape(1, -1).astype(F32)
    padrow = lambda a: jnp.pad(a.astype(F32), (0, LANES - a.shape[0])).reshape(1, LANES)
    rep = lambda a: jnp.repeat(a.astype(F32), SSD_HEAD_DIM).reshape(1, D_MODEL)
    out['norm1_w'] = row(p['norm1_w'])
    out['norm2_w'] = row(p['norm2_w'])
    out['conv_w'] = p['ssd_conv_w'].astype(F32)
    out['conv_b'] = row(p['ssd_conv_b'])
    out['b16'] = padrow(p['ssd_dt_bias'])
    out['al16'] = padrow(p['ssd_a_log'])
    out['dfull'] = rep(p['ssd_d'])
    out['ssd_nw'] = row(p['ssd_norm_w'])
    out['s5_d'] = row(p['s5_d'])
    abr, abi, bbr, bbi = _s5_discretise(p['s5_a_re'].astype(F32), p['s5_a_im'].astype(F32),
                                        p['s5_log_dt'].astype(F32), p['s5_b_re'].astype(F32),
                                        p['s5_b_im'].astype(F32))
    out['s5'] = _s5_weights(abr, abi, bbr, bbi, p['s5_c_re'].astype(F32), p['s5_c_im'].astype(F32))
    return out


def _head_expand():
    e = (jnp.arange(LANES)[:, None] == (jnp.arange(D_MODEL)[None, :] // SSD_HEAD_DIM))
    return e.astype(BF16)


def _tile(n, pref):
    return pref if n % pref == 0 else n


def _prompt_trunk(x, layers, final_w):
    nb, L, _ = x.shape
    eh = _head_expand()
    _, lm, gp, we, gq = _ret_consts(CHUNK)
    cos2, sin2 = _rope_tables(jnp.arange(L, dtype=jnp.int32))
    zeros_state = jnp.zeros((nb, S5_NSTATE), F32)
    steps = 64
    h = x
    outs = {k: [] for k in ('re', 'im', 'ssd', 'conv', 'ret')}
    for li, lp in enumerate(layers):
        tm = _tile(L, 1024)
        proj, dt = _proj_main(h, lp['norm1_w'], *lp['proj_w'], tm)
        wbr, wbi, abr, abi, wcr, wci = lp['s5']
        y_s5, hr, hi = _s5_scan(proj, wbr, wbi, abr, abi, wcr, wci, lp['s5_d'],
                                zeros_state, zeros_state, steps)
        ssd_n, ssd_s = _ssd_prompt(proj, dt, lp['conv_w'], lp['conv_b'], lp['b16'], lp['al16'],
                                   lp['dfull'], lp['ssd_nw'], eh)
        ret_n, ret_s = _ret_prompt(proj, cos2, sin2, lm, gp, we, gq)
        h = _merge(y_s5, ssd_n, ret_n, proj, h,
                   *lp['merge_w'], li, _tile(L, 512))
        last = li == len(layers) - 1
        h = _mlp(h.reshape(nb * L, D_MODEL), lp['norm2_w'], *lp['mlp_w'], final_w, li,
                 _tile(nb * L, 512), 1024, last).reshape(nb, L, D_MODEL)
        outs['re'].append(hr.reshape(nb, S5_GROUPS, S5_STATE))
        outs['im'].append(hi.reshape(nb, S5_GROUPS, S5_STATE))
        outs['ssd'].append(ssd_s.reshape(nb, SSD_HEADS, SSD_HEAD_DIM, SSD_STATE))
        outs['conv'].append(proj[:, L - (SSD_CONV - 1):, OFF_XS:OFF_XS + SSD_CONV_DIM].astype(F32))
        outs['ret'].append(ret_s.reshape(nb, RET_HEADS, RET_V_DIM, RET_QK_DIM))
    return (h,) + tuple(jnp.stack(outs[k]) for k in ('re', 'im', 'ssd', 'conv', 'ret'))


def _sample_trunk(x, states, layers, final_w):
    n = x.shape[0]
    st_re, st_im, st_ssd, st_conv, st_ret = states
    eh = _head_expand()
    log_gamma, _, _, _, _ = _ret_consts(1)
    cos2, sin2 = _rope_tables(PAST_LEN + jnp.arange(1, dtype=jnp.int32))
    e_ret = jnp.broadcast_to(jnp.repeat(jnp.exp(log_gamma), RET_V_DIM // STATE_BLK)[None, :],
                             (n, STATE_ROWS // STATE_BLK)).reshape(-1)
    h = x.reshape(1, n, D_MODEL)
    depth = len(layers)
    ssd_all = st_ssd.reshape(depth, n, STATE_ROWS, SSD_STATE).astype(F32)
    ret_all = st_ret.reshape(depth, n, STATE_ROWS, RET_QK_DIM).astype(F32)
    ssd_new = ret_new = None
    outs = {k: [] for k in ('re', 'im', 'conv')}
    for li, lp in enumerate(layers):
        proj, dt = _proj_main(h, lp['norm1_w'], *lp['proj_w'], n)
        wbr, wbi, abr, abi, wcr, wci = lp['s5']
        y_s5, hr, hi = _s5_scan(proj, wbr, wbi, abr, abi, wcr, wci, lp['s5_d'],
                                st_re[li].reshape(n, S5_NSTATE).astype(F32),
                                st_im[li].reshape(n, S5_NSTATE).astype(F32), 1)
        proj2, dt2 = proj.reshape(n, N_MAIN), dt.reshape(n, LANES)
        conv2 = st_conv[li].reshape(n, (SSD_CONV - 1) * SSD_CONV_DIM).astype(F32)
        vv, e16, bs, cs, xs, conv_new, qr, kr = _sample_pre(
            proj2, dt2, conv2, lp['conv_w'], lp['conv_b'], lp['b16'], lp['al16'], eh, cos2, sin2)
        ssd_new, y_ssd = _state_step(
            e16[:, :SSD_HEADS].reshape(-1), ssd_all, li,
            vv.reshape(n, SUBLANES, LANES), bs.reshape(n, SSD_GROUPS, SSD_STATE),
            cs.reshape(n, SSD_GROUPS, SSD_STATE), prev=ssd_new)
        vret = proj2[:, OFF_V:OFF_V + D_MODEL].astype(F32)
        ret_new, y_ret = _state_step(
            e_ret, ret_all, li,
            vret.reshape(n, SUBLANES, LANES), kr.reshape(n, RET_HEADS, RET_QK_DIM),
            qr.reshape(n, RET_HEADS, RET_QK_DIM), prev=ret_new)
        ssd_n, ret_n = _sample_post(y_ssd.reshape(n, D_MODEL), xs, proj2, y_ret.reshape(n, D_MODEL),
                                    lp['dfull'], lp['ssd_nw'])
        h = _merge(y_s5, ssd_n.reshape(1, n, D_MODEL), ret_n.reshape(1, n, D_MODEL), proj, h,
                   *lp['merge_w'], li, n)
        last = li == len(layers) - 1
        h = _mlp(h.reshape(n, D_MODEL), lp['norm2_w'], *lp['mlp_w'], final_w, li,
                 n, 1024, last).reshape(1, n, D_MODEL)
        outs['re'].append(hr.reshape(n, S5_GROUPS, S5_STATE))
        outs['im'].append(hi.reshape(n, S5_GROUPS, S5_STATE))
        outs['conv'].append(conv_new.reshape(n, SSD_CONV - 1, SSD_CONV_DIM))
    return (h.reshape(n, 1, D_MODEL), jnp.stack(outs['re']), jnp.stack(outs['im']),
            ssd_new.reshape(depth, n, SSD_HEADS, SSD_HEAD_DIM, SSD_STATE), jnp.stack(outs['conv']),
            ret_new.reshape(depth, n, RET_HEADS, RET_V_DIM, RET_QK_DIM))


def kernel(x_prompt, x_sample, state_s5_re, state_s5_im, state_ssd, state_conv, state_ret, norm1_w, w_in, s5_a_re, s5_a_im, s5_log_dt, s5_b_re, s5_b_im, s5_c_re, s5_c_im, s5_d, w_s5_out, ssd_conv_w, ssd_conv_b, ssd_dt_bias, ssd_a_log, ssd_d, ssd_norm_w, w_ssd_out, w_ret_out, w_out, norm2_w, w_mlp_up, w_mlp_down, final_norm_w):
    params = dict(norm1_w=norm1_w, w_in=w_in, s5_a_re=s5_a_re, s5_a_im=s5_a_im, s5_log_dt=s5_log_dt,
                  s5_b_re=s5_b_re, s5_b_im=s5_b_im, s5_c_re=s5_c_re, s5_c_im=s5_c_im, s5_d=s5_d,
                  w_s5_out=w_s5_out, ssd_conv_w=ssd_conv_w, ssd_conv_b=ssd_conv_b, ssd_dt_bias=ssd_dt_bias,
                  ssd_a_log=ssd_a_log, ssd_d=ssd_d, ssd_norm_w=ssd_norm_w, w_ssd_out=w_ssd_out,
                  w_ret_out=w_ret_out, w_out=w_out, norm2_w=norm2_w, w_mlp_up=w_mlp_up, w_mlp_down=w_mlp_down)
    depth = w_in.shape[0]
    big = ('w_in', 'w_s5_out', 'w_ssd_out', 'w_ret_out', 'w_out', 'w_mlp_up', 'w_mlp_down')
    layers = [_prep_layer({name: arr[l] for name, arr in params.items() if name not in big}) for l in range(depth)]
    merge_w = tuple(params[name].astype(BF16) for name in ('w_s5_out', 'w_ssd_out', 'w_ret_out', 'w_out'))
    mlp_w = tuple(params[name].astype(BF16) for name in ('w_mlp_up', 'w_mlp_down'))
    w_a, w_b, w_dt = _proj_weight_slabs(w_in.astype(F32))
    for l, lp in enumerate(layers):
        lp['proj_w'] = (w_a, w_b, w_dt, l)
        lp['merge_w'] = merge_w
        lp['mlp_w'] = mlp_w
    final_w = final_norm_w.reshape(1, D_MODEL).astype(F32)
    y_p, re_p, im_p, ssd_p, conv_p, ret_p = _prompt_trunk(x_prompt.astype(F32), layers, final_w)
    y_s, re_s, im_s, ssd_s, conv_s, ret_s = _sample_trunk(
        x_sample.astype(F32), (state_s5_re, state_s5_im, state_ssd, state_conv, state_ret), layers, final_w)
    return (y_p, y_s, re_p, im_p, ssd_p, conv_p, ret_p, re_s, im_s, ssd_s, conv_s, ret_s)
```

```python
import functools
import math

import jax
import jax.numpy as jnp
from jax import lax
from jax.experimental import pallas as pl
from jax.experimental.pallas import tpu as pltpu

F32 = jnp.float32
BF16 = jnp.bfloat16

D_MODEL = 1024
S5_GROUP = 16
S5_STATE = 64
S5_GROUPS = D_MODEL // S5_GROUP
S5_NSTATE = S5_GROUPS * S5_STATE
S5_SUPER = 8
SSD_HEADS = 16
SSD_HEAD_DIM = 64
SSD_GROUPS = 4
SSD_STATE = 128
SSD_CONV = 4
SSD_CONV_DIM = D_MODEL + 2 * SSD_GROUPS * SSD_STATE
RET_HEADS = 4
RET_QK_DIM = 128
RET_V_DIM = 256
ROPE_BASE = 10000.0
D_FF = 4 * D_MODEL
CHUNK = 128
EPS = 1e-6
PAST_LEN = 16384
LANES = 128
SUBLANES = 8
VMEM_LIMIT = 52 * 1024 * 1024

OFF_U, OFF_Z, OFF_XS, OFF_BC, OFF_Q, OFF_K, OFF_V, OFF_G, OFF_GATES = (
    0, 1024, 2048, 3072, 4096, 4608, 5120, 6144, 7168)
N_MAIN = 10240
SLAB_A_COLS = OFF_Q
SLAB_B_START = 2 * D_MODEL + SSD_CONV_DIM + SSD_HEADS
PROJ_TN = 2048


def _cparams(sem):
    return pltpu.CompilerParams(dimension_semantics=sem, vmem_limit_bytes=VMEM_LIMIT)


def _sigmoid(x):
    return 1.0 / (1.0 + jnp.exp(-x))


def _silu(x):
    return x * _sigmoid(x)


def _softplus(x):
    return jnp.maximum(x, 0.0) + jnp.log1p(jnp.exp(-jnp.abs(x)))


def _gelu_tanh(x):
    return 0.5 * x * (1.0 + jnp.tanh(math.sqrt(2.0 / math.pi) * (x + 0.044715 * (x * x * x))))


def _rmsnorm(x, w):
    return x * lax.rsqrt(jnp.mean(x * x, axis=-1, keepdims=True) + EPS) * w


def _cumsum_rows(x):
    n = x.shape[0]
    row = lax.broadcasted_iota(jnp.int32, x.shape, 0)
    k = 1
    while k < n:
        x = x + jnp.where(row >= k, pltpu.roll(x, k, 0), 0.0)
        k *= 2
    return x


def _group_rms(y, width):
    outs = []
    for s in range(0, y.shape[-1], width):
        yg = y[:, s:s + width]
        outs.append(yg * lax.rsqrt(jnp.mean(yg * yg, axis=-1, keepdims=True) + EPS))
    return jnp.concatenate(outs, axis=-1)


def _dot(a, b):
    return jnp.dot(a, b, preferred_element_type=F32)


def _split_dot(x, e):
    hi = x.astype(BF16)
    lo = (x - hi.astype(F32)).astype(BF16)
    return _dot(hi, e) + _dot(lo, e)


def _proj_kernel(x_ref, nw_ref, wa_ref, wb_ref, wdt_ref, o_ref, odt_ref, xn_ref, *, na):
    j = pl.program_id(2)

    @pl.when(j == 0)
    def _():
        xn = _rmsnorm(x_ref[...], nw_ref[...]).astype(BF16)
        xn_ref[...] = xn
        odt_ref[...] = _dot(xn, wdt_ref[...])

    @pl.when(j < na)
    def _():
        o_ref[...] = _dot(xn_ref[...], wa_ref[...]).astype(o_ref.dtype)

    @pl.when(j >= na)
    def _():
        o_ref[...] = _dot(xn_ref[...], wb_ref[...]).astype(o_ref.dtype)


PREP_TN = 1024
DT_START = SLAB_B_START - SSD_HEADS
SLAB_SHIFT = SLAB_B_START % PREP_TN


def _slab_b_kernel(lo_ref, hi_ref, o_ref):
    keep = PREP_TN - SLAB_SHIFT
    lane = lax.broadcasted_iota(jnp.int32, (D_MODEL, PREP_TN), 1)
    lo = pltpu.roll(lo_ref[...].astype(F32), keep, 1)
    hi = pltpu.roll(hi_ref[...].astype(F32), keep, 1)
    o_ref[...] = jnp.where(lane < keep, lo, hi).astype(BF16)


def _proj_weight_slabs(w_in):
    depth = w_in.shape[0]
    tn = PREP_TN
    w_all = w_in.astype(BF16)
    w_dt = jnp.pad(w_all[:, :, DT_START:SLAB_B_START], ((0, 0), (0, 0), (0, LANES - SSD_HEADS)))
    blk = lambda f: pl.BlockSpec((None, D_MODEL, tn), f)
    nbcols = N_MAIN - SLAB_A_COLS
    first = SLAB_B_START // tn
    w_b = pl.pallas_call(
        _slab_b_kernel,
        grid=(depth, nbcols // tn),
        in_specs=[blk(lambda l, j: (l, 0, first + j)), blk(lambda l, j: (l, 0, first + j + 1))],
        out_specs=blk(lambda l, j: (l, 0, j)),
        out_shape=jax.ShapeDtypeStruct((depth, D_MODEL, nbcols), BF16),
        compiler_params=_cparams(("parallel", "parallel")),
        name="weight_slab_b",
    )(w_all, w_all)
    return w_all, w_b, w_dt


def _proj_main(x3, nw, w_a, w_b, wdt, layer, tm):
    nb, L, _ = x3.shape
    tn = PROJ_TN
    na = SLAB_A_COLS // tn
    return pl.pallas_call(
        functools.partial(_proj_kernel, na=na),
        grid=(nb, L // tm, N_MAIN // tn),
        in_specs=[pl.BlockSpec((None, tm, D_MODEL), lambda b, i, j: (b, i, 0)),
                  pl.BlockSpec((1, D_MODEL), lambda b, i, j: (0, 0)),
                  pl.BlockSpec((None, D_MODEL, tn), lambda b, i, j: (layer, 0, jnp.minimum(j, na - 1))),
                  pl.BlockSpec((None, D_MODEL, tn), lambda b, i, j: (layer, 0, jnp.maximum(j - na, 0))),
                  pl.BlockSpec((None, D_MODEL, LANES), lambda b, i, j: (layer, 0, 0))],
        out_specs=[pl.BlockSpec((None, tm, tn), lambda b, i, j: (b, i, j)),
                   pl.BlockSpec((None, tm, LANES), lambda b, i, j: (b, i, 0))],
        out_shape=[jax.ShapeDtypeStruct((nb, L, N_MAIN), BF16), jax.ShapeDtypeStruct((nb, L, LANES), F32)],
        scratch_shapes=[pltpu.VMEM((tm, D_MODEL), BF16)],
        compiler_params=_cparams(("parallel", "parallel", "arbitrary")),
        name="proj_main",
    )(x3, nw, w_a, w_b, wdt)


def _s5_disc_kernel(are_ref, aim_ref, ldt_ref, bre_ref, bim_ref, abr_ref, abi_ref, bbr_ref, bbi_ref):
    ar, ai = are_ref[...], aim_ref[...]
    dt = jnp.exp(ldt_ref[...])
    mag = jnp.exp(ar * dt)
    abr = mag * jnp.cos(ai * dt)
    abi = mag * jnp.sin(ai * dt)
    abr_ref[...] = abr
    abi_ref[...] = abi
    nr, ni = abr - 1.0, abi
    den = 1.0 / (ar * ar + ai * ai)
    cr = (nr * ar + ni * ai) * den
    ci = (ni * ar - nr * ai) * den
    br, bi = bre_ref[...], bim_ref[...]
    bbr_ref[...] = cr * br - ci * bi
    bbi_ref[...] = cr * bi + ci * br


def _s5_discretise(a_re, a_im, log_dt, b_re, b_im):
    g, p = a_re.shape
    gs = b_re.shape[-1]
    full3 = lambda s: pl.BlockSpec(s, lambda: (0, 0, 0))
    return pl.pallas_call(
        _s5_disc_kernel,
        in_specs=[full3((g, 1, p)), full3((g, 1, p)), full3((g, 1, 1)), full3((g, gs, p)), full3((g, gs, p))],
        out_specs=[full3((g, 1, p)), full3((g, 1, p)), full3((g, gs, p)), full3((g, gs, p))],
        out_shape=[jax.ShapeDtypeStruct((g, 1, p), F32)] * 2 + [jax.ShapeDtypeStruct((g, gs, p), F32)] * 2,
        name="s5_discretise",
    )(a_re.reshape(g, 1, p), a_im.reshape(g, 1, p), log_dt.reshape(g, 1, 1),
      jnp.swapaxes(b_re, 1, 2), jnp.swapaxes(b_im, 1, 2))


def _s5_kernel(u_ref, wbr_ref, wbi_ref, ar_ref, ai_ref, wcr_ref, wci_ref, d_ref, h0r_ref, h0i_ref,
               y_ref, hr_ref, hi_ref, bur, bui, *, nbatch, steps, lane_w):
    @pl.when(pl.program_id(0) == 0)
    def _():
        hr_ref[...] = h0r_ref[...]
        hi_ref[...] = h0i_ref[...]

    cw = S5_SUPER * S5_GROUP
    sw = S5_SUPER * S5_STATE
    batch_major = steps > 1
    u = u_ref[...].astype(F32)
    if batch_major:
        u = jnp.swapaxes(u, 0, 1).reshape(steps * nbatch, D_MODEL)
    ub = u.astype(BF16)
    for j in range(S5_GROUPS // S5_SUPER):
        uj = ub[:, j * cw:(j + 1) * cw]
        bur[:, j * sw:(j + 1) * sw] = _dot(uj, wbr_ref[j])
        bui[:, j * sw:(j + 1) * sw] = _dot(uj, wbi_ref[j])

        for lc in range(sw // lane_w):
            sl = slice(j * sw + lc * lane_w, j * sw + (lc + 1) * lane_w)
            are = jnp.broadcast_to(ar_ref[:, sl], (nbatch, lane_w))
            aim = jnp.broadcast_to(ai_ref[:, sl], (nbatch, lane_w))
            hr, hi = hr_ref[:, sl], hi_ref[:, sl]
            for t in range(steps):
                rows = slice(t * nbatch, (t + 1) * nbatch)
                hr, hi = are * hr - aim * hi + bur[rows, sl], are * hi + aim * hr + bui[rows, sl]
                bur[rows, sl] = hr
                bui[rows, sl] = hi
            hr_ref[:, sl] = hr
            hi_ref[:, sl] = hi

        yj = (_dot(bur[:, j * sw:(j + 1) * sw].astype(BF16), wcr_ref[j])
              + _dot(bui[:, j * sw:(j + 1) * sw].astype(BF16), wci_ref[j]))
        yj = _gelu_tanh(yj + d_ref[:, j * cw:(j + 1) * cw] * u[:, j * cw:(j + 1) * cw])
        if batch_major:
            y_ref[:, :, j * cw:(j + 1) * cw] = jnp.swapaxes(yj.reshape(steps, nbatch, cw), 0, 1).astype(y_ref.dtype)
        else:
            y_ref[:, j * cw:(j + 1) * cw] = yj.astype(y_ref.dtype)


def _s5_scan(proj, wbr, wbi, abr, abi, wcr, wci, d, h0r, h0i, steps):
    nb, L, _ = proj.shape
    nsup = S5_GROUPS // S5_SUPER
    cw, sw = S5_SUPER * S5_GROUP, S5_SUPER * S5_STATE
    c2 = lambda s: pl.BlockSpec(s, lambda c: (0, 0))
    c3 = lambda s: pl.BlockSpec(s, lambda c: (0, 0, 0))
    if steps > 1:
        nbatch, grid = nb, L // steps
        u_spec = pl.BlockSpec((nb, steps, D_MODEL), lambda c: (0, c, OFF_U // D_MODEL))
        y_spec = pl.BlockSpec((nb, steps, D_MODEL), lambda c: (0, c, 0))
    else:
        nbatch, grid = L, 1
        u_spec = pl.BlockSpec((None, L, D_MODEL), lambda c: (0, 0, OFF_U // D_MODEL))
        y_spec = pl.BlockSpec((None, L, D_MODEL), lambda c: (0, 0, 0))
    blk = nbatch * steps
    lane_w = max(LANES, S5_NSTATE // nbatch)
    kern = functools.partial(_s5_kernel, nbatch=nbatch, steps=steps, lane_w=lane_w)
    return pl.pallas_call(
        kern,
        grid=(grid,),
        in_specs=[u_spec,
                  c3((nsup, cw, sw)), c3((nsup, cw, sw)),
                  c2((1, S5_NSTATE)), c2((1, S5_NSTATE)),
                  c3((nsup, sw, cw)), c3((nsup, sw, cw)),
                  c2((1, D_MODEL)),
                  c2((nbatch, S5_NSTATE)), c2((nbatch, S5_NSTATE))],
        out_specs=[y_spec, c2((nbatch, S5_NSTATE)), c2((nbatch, S5_NSTATE))],
        out_shape=[jax.ShapeDtypeStruct((nb, L, D_MODEL), BF16),
                   jax.ShapeDtypeStruct((nbatch, S5_NSTATE), F32),
                   jax.ShapeDtypeStruct((nbatch, S5_NSTATE), F32)],
        scratch_shapes=[pltpu.VMEM((blk, S5_NSTATE), F32), pltpu.VMEM((blk, S5_NSTATE), F32)],
        compiler_params=_cparams(("arbitrary",)),
        name="s5_scan",
    )(proj, wbr, wbi, abr, abi, wcr, wci, d, h0r, h0i)


def _s5_weights(abr, abi, bbr, bbi, c_re, c_im):
    nsup = S5_GROUPS // S5_SUPER
    eye = jnp.eye(S5_SUPER, dtype=F32)

    def bexp(b):
        b = b.reshape(nsup, S5_SUPER, S5_GROUP, S5_STATE)
        return (b[:, :, :, None, :] * eye[None, :, None, :, None]).reshape(
            nsup, S5_SUPER * S5_GROUP, S5_SUPER * S5_STATE).astype(BF16)

    def cexp(c):
        c = jnp.swapaxes(c, 1, 2).reshape(nsup, S5_SUPER, S5_STATE, S5_GROUP)
        return (c[:, :, :, None, :] * eye[None, :, None, :, None]).reshape(
            nsup, S5_SUPER * S5_STATE, S5_SUPER * S5_GROUP).astype(BF16)

    return (bexp(bbr), bexp(bbi), abr.reshape(1, S5_NSTATE), abi.reshape(1, S5_NSTATE),
            cexp(c_re), cexp(-c_im))


def _ssd_gate_norm(y, xs, z, dfull, nw):
    y = (y + dfull * xs) * _silu(z)
    return _group_rms(y, D_MODEL // SSD_GROUPS) * nw


CONV_PAD = 16
CONV_ROWS = 2 * CHUNK


def _conv_shift_mats():
    i = jnp.arange(CHUNK)[:, None]
    j = jnp.arange(CONV_ROWS)[None, :]
    return jnp.stack([(j == CONV_PAD + i - s) for s in range(1, SSD_CONV)]).astype(BF16)


def _ssd_chunk_kernel(xs_ref, bc_ref, z_ref, dt_ref, cw_ref, cb_ref, b16_ref, al16_ref,
                      dfull_ref, nw_ref, eh_ref, sh_ref, o_ref, s_ref, xpad, st):
    c = pl.program_id(1)
    nc = pl.num_programs(1)
    nseq = xs_ref.shape[0]

    @pl.when(c == 0)
    def _():
        xpad[...] = jnp.zeros(xpad.shape, BF16)
        st[...] = jnp.zeros(st.shape, F32)

    for bi in range(nseq):
        _ssd_chunk_seq(xs_ref.at[bi], bc_ref.at[bi], z_ref.at[bi], dt_ref.at[bi], cw_ref, cb_ref, b16_ref,
                       al16_ref, dfull_ref, nw_ref, eh_ref, sh_ref, o_ref.at[bi], xpad.at[bi], st.at[bi])

    @pl.when(c == nc - 1)
    def _():
        for bi in range(nseq):
            for k in range(D_MODEL // LANES):
                s_ref[bi, k * LANES:(k + 1) * LANES, :] = st[bi, :, k * LANES:(k + 1) * LANES].T


def _ssd_chunk_seq(xs_ref, bc_ref, z_ref, dt_ref, cw_ref, cb_ref, b16_ref, al16_ref,
                   dfull_ref, nw_ref, eh_ref, sh_ref, o_ref, xpad, st):
    q = CHUNK
    pad = CONV_PAD
    xpad[0:pad, :] = xpad[q:q + pad, :]

    xpad[pad:pad + q, :D_MODEL] = xs_ref[...]
    xpad[pad:pad + q, D_MODEL:] = bc_ref[...]
    staged = xpad[...]
    acc = cb_ref[...] + xpad[pad:pad + q, :].astype(F32) * cw_ref[SSD_CONV - 1:SSD_CONV, :]
    for s in range(1, SSD_CONV):
        acc = acc + _dot(sh_ref[s - 1], staged) * cw_ref[SSD_CONV - 1 - s:SSD_CONV - s, :]
    xc = _silu(acc)
    xs = xc[:, :D_MODEL]
    nbc = SSD_GROUPS * SSD_STATE
    bs = xc[:, D_MODEL:D_MODEL + nbc]
    cs = xc[:, D_MODEL + nbc:]

    delta16 = _softplus(dt_ref[...] + b16_ref[...])
    acum16 = _cumsum_rows(delta16 * (-jnp.exp(al16_ref[...])))
    acum_t = acum16.T
    delta_t = delta16.T
    last16 = acum16[q - 1:q, :]
    ea_f = _split_dot(jnp.exp(acum16), eh_ref[...])
    dw_f = _split_dot(delta16 * jnp.exp(last16 - acum16), eh_ref[...])
    xw = (xs * dw_f).astype(BF16)

    ri = lax.broadcasted_iota(jnp.int32, (q, q), 0)
    ci = lax.broadcasted_iota(jnp.int32, (q, q), 1)
    causal = ri >= ci
    hw = D_MODEL // SSD_GROUPS
    hpg = SSD_HEADS // SSD_GROUPS
    xsb = xs.astype(BF16)
    ys = []
    for g in range(SSD_GROUPS):
        bg = bs[:, g * SSD_STATE:(g + 1) * SSD_STATE]
        cg = cs[:, g * SSD_STATE:(g + 1) * SSD_STATE].astype(BF16)
        bgt = bg.T.astype(BF16)
        scores = _dot(cg, bgt)
        sg = st[:, g * hw:(g + 1) * hw]
        y_inter = _dot(cg, sg.astype(BF16)) * ea_f[:, g * hw:(g + 1) * hw]
        for r in range(hpg):
            hd = g * hpg + r
            seg = acum16[:, hd:hd + 1] - acum_t[hd:hd + 1, :]
            lm = jnp.exp(jnp.where(causal, seg, -jnp.inf)) * delta_t[hd:hd + 1, :]
            p = (scores * lm).astype(BF16)
            ys.append(_dot(p, xsb[:, hd * SSD_HEAD_DIM:(hd + 1) * SSD_HEAD_DIM])
                      + y_inter[:, r * SSD_HEAD_DIM:(r + 1) * SSD_HEAD_DIM])
        st[:, g * hw:(g + 1) * hw] = (ea_f[q - 1:q, g * hw:(g + 1) * hw] * sg
                                      + _dot(bgt, xw[:, g * hw:(g + 1) * hw]))
    y = jnp.concatenate(ys, axis=-1)
    o_ref[...] = _ssd_gate_norm(y, xs, z_ref[...].astype(F32), dfull_ref[...], nw_ref[...]).astype(o_ref.dtype)


def _ssd_prompt(proj, dt, cw, cb, b16, al16, dfull, nw, eh, nseq=2):
    nb, L, _ = proj.shape
    q = CHUNK
    c2 = lambda s: pl.BlockSpec(s, lambda b, c: (0, 0))
    return pl.pallas_call(
        _ssd_chunk_kernel,
        grid=(nb // nseq, L // q),
        in_specs=[pl.BlockSpec((nseq, q, D_MODEL), lambda b, c: (b, c, OFF_XS // D_MODEL)),
                  pl.BlockSpec((nseq, q, D_MODEL), lambda b, c: (b, c, OFF_BC // D_MODEL)),
                  pl.BlockSpec((nseq, q, D_MODEL), lambda b, c: (b, c, OFF_Z // D_MODEL)),
                  pl.BlockSpec((nseq, q, LANES), lambda b, c: (b, c, 0)),
                  c2((SSD_CONV, SSD_CONV_DIM)), c2((1, SSD_CONV_DIM)),
                  c2((1, LANES)), c2((1, LANES)),
                  c2((1, D_MODEL)), c2((1, D_MODEL)),
                  c2((LANES, D_MODEL)),
                  pl.BlockSpec((SSD_CONV - 1, q, CONV_ROWS), lambda b, c: (0, 0, 0))],
        out_specs=[pl.BlockSpec((nseq, q, D_MODEL), lambda b, c: (b, c, 0)),
                   pl.BlockSpec((nseq, D_MODEL, SSD_STATE), lambda b, c: (b, 0, 0))],
        out_shape=[jax.ShapeDtypeStruct((nb, L, D_MODEL), BF16),
                   jax.ShapeDtypeStruct((nb, D_MODEL, SSD_STATE), F32)],
        scratch_shapes=[pltpu.VMEM((nseq, CONV_ROWS, SSD_CONV_DIM), BF16),
                        pltpu.VMEM((nseq, SSD_STATE, D_MODEL), F32)],
        compiler_params=_cparams(("parallel", "arbitrary")),
        name="ssd_chunk",
    )(proj, proj, proj, dt, cw, cb, b16, al16, dfull, nw, eh, _conv_shift_mats())


def _rotary(x, cos2, sin2):
    outs = []
    for h in range(RET_HEADS):
        xh = x[:, h * RET_QK_DIM:(h + 1) * RET_QK_DIM]
        outs.append(xh * cos2 + pltpu.roll(xh, RET_QK_DIM // 2, 1) * sin2)
    return outs


def _ret_gate_norm(y, g):
    return _group_rms(y, RET_V_DIM) * _silu(g)


def _ret_chunk_kernel(q_ref, k_ref, v_ref, g_ref, cos_ref, sin_ref, lm_ref, gp_ref, we_ref, gq_ref,
                      o_ref, s_ref, st):
    c = pl.program_id(1)
    nc = pl.num_programs(1)

    @pl.when(c == 0)
    def _():
        st[...] = jnp.zeros(st.shape, F32)

    cos2, sin2 = cos_ref[...], sin_ref[...]
    for bi in range(q_ref.shape[0]):
        qr = _rotary(q_ref[bi].astype(F32), cos2, sin2)
        kr = _rotary(k_ref[bi].astype(F32), cos2, sin2)
        vb = v_ref[bi]
        vw = (vb.astype(F32) * we_ref[...]).astype(BF16)
        ys = []
        for h in range(RET_HEADS):
            sl = slice(h * RET_V_DIM, (h + 1) * RET_V_DIM)
            qh = qr[h].astype(BF16)
            kht = (kr[h] * (RET_QK_DIM ** -0.5)).T.astype(BF16)
            p = (_dot(qh, kht) * lm_ref[h]).astype(BF16)
            sh = st[bi, :, sl]
            ys.append(_dot(p, vb[:, sl]) + _dot(qh, sh.astype(BF16)) * gp_ref[:, sl])
            st[bi, :, sl] = gq_ref[:, sl] * sh + _dot(kht, vw[:, sl])
        y = jnp.concatenate(ys, axis=-1)
        o_ref[bi] = _ret_gate_norm(y, g_ref[bi].astype(F32)).astype(o_ref.dtype)

    @pl.when(c == nc - 1)
    def _():
        for bi in range(q_ref.shape[0]):
            for k in range(D_MODEL // LANES):
                s_ref[bi, k * LANES:(k + 1) * LANES, :] = st[bi, :, k * LANES:(k + 1) * LANES].T


def _ret_consts(q):
    hidx = jnp.arange(RET_HEADS, dtype=F32)
    log_gamma = jnp.log1p(-jnp.exp2(-5.0 - hidx))
    i = jnp.arange(q, dtype=F32)
    seg = (i[:, None] - i[None, :])[None] * log_gamma[:, None, None]
    lm = jnp.exp(jnp.where((i[:, None] >= i[None, :])[None], seg, -jnp.inf))
    rep = lambda t: jnp.repeat(t, RET_V_DIM, axis=-1)
    gp = rep(jnp.exp((i[:, None] + 1.0) * log_gamma[None, :]))
    we = rep(jnp.exp((q - 1.0 - i[:, None]) * log_gamma[None, :]))
    gq = rep(jnp.exp(q * log_gamma)[None, :])
    return log_gamma, lm, gp, we, gq


def _rope_tables(pos):
    half = RET_QK_DIM // 2
    inv = ROPE_BASE ** (-jnp.arange(half, dtype=F32) / half)
    ang = pos.astype(F32)[:, None] * inv[None]
    cos, sin = jnp.cos(ang), jnp.sin(ang)
    return jnp.concatenate([cos, cos], axis=-1), jnp.concatenate([-sin, sin], axis=-1)


def _ret_prompt(proj, cos2, sin2, lm, gp, we, gq, nseq=2):
    nb, L, _ = proj.shape
    q = CHUNK
    qkw = RET_HEADS * RET_QK_DIM
    c2 = lambda s: pl.BlockSpec(s, lambda b, c: (0, 0))
    return pl.pallas_call(
        _ret_chunk_kernel,
        grid=(nb // nseq, L // q),
        in_specs=[pl.BlockSpec((nseq, q, qkw), lambda b, c: (b, c, OFF_Q // qkw)),
                  pl.BlockSpec((nseq, q, qkw), lambda b, c: (b, c, OFF_K // qkw)),
                  pl.BlockSpec((nseq, q, D_MODEL), lambda b, c: (b, c, OFF_V // D_MODEL)),
                  pl.BlockSpec((nseq, q, D_MODEL), lambda b, c: (b, c, OFF_G // D_MODEL)),
                  pl.BlockSpec((q, RET_QK_DIM), lambda b, c: (c, 0)),
                  pl.BlockSpec((q, RET_QK_DIM), lambda b, c: (c, 0)),
                  pl.BlockSpec((RET_HEADS, q, q), lambda b, c: (0, 0, 0)),
                  c2((q, D_MODEL)), c2((q, D_MODEL)), c2((1, D_MODEL))],
        out_specs=[pl.BlockSpec((nseq, q, D_MODEL), lambda b, c: (b, c, 0)),
                   pl.BlockSpec((nseq, D_MODEL, RET_QK_DIM), lambda b, c: (b, 0, 0))],
        out_shape=[jax.ShapeDtypeStruct((nb, L, D_MODEL), BF16),
                   jax.ShapeDtypeStruct((nb, D_MODEL, RET_QK_DIM), F32)],
        scratch_shapes=[pltpu.VMEM((nseq, RET_QK_DIM, D_MODEL), F32)],
        compiler_params=_cparams(("parallel", "arbitrary")),
        name="ret_chunk",
    )(proj, proj, proj, proj, cos2, sin2, lm, gp, we, gq)


def _sample_pre_kernel(xsin_ref, bcin_ref, dt_ref, q_ref, k_ref, conv_ref, cw_ref, cb_ref, b16_ref, al16_ref,
                       eh_ref, cos_ref, sin_ref,
                       vv_ref, e16_ref, bs_ref, cs_ref, xs_ref, convn_ref, qr_ref, kr_ref):
    cd = SSD_CONV_DIM
    x = jnp.concatenate([xsin_ref[...], bcin_ref[...]], axis=-1).astype(F32)
    acc = cb_ref[...] + x * cw_ref[SSD_CONV - 1:SSD_CONV, :]
    for tap in range(SSD_CONV - 1):
        acc = acc + conv_ref[:, tap * cd:(tap + 1) * cd] * cw_ref[tap:tap + 1, :]
    convn_ref[:, 0:(SSD_CONV - 2) * cd] = conv_ref[:, cd:(SSD_CONV - 1) * cd]
    convn_ref[:, (SSD_CONV - 2) * cd:] = x
    xc = _silu(acc)
    xs = xc[:, :D_MODEL]
    nbc = SSD_GROUPS * SSD_STATE
    xs_ref[...] = xs
    bs_ref[...] = xc[:, D_MODEL:D_MODEL + nbc]
    cs_ref[...] = xc[:, D_MODEL + nbc:]
    delta16 = _softplus(dt_ref[...] + b16_ref[...])
    e16_ref[...] = jnp.exp(delta16 * (-jnp.exp(al16_ref[...])))
    vv_ref[...] = xs * _split_dot(delta16, eh_ref[...])
    cos2, sin2 = cos_ref[...], sin_ref[...]
    qr_ref[...] = jnp.concatenate(_rotary(q_ref[...].astype(F32), cos2, sin2), axis=-1)
    kr_ref[...] = jnp.concatenate(_rotary(k_ref[...].astype(F32), cos2, sin2), axis=-1) * (RET_QK_DIM ** -0.5)


def _sample_pre(proj2, dt2, conv2, cw, cb, b16, al16, eh, cos2, sin2):
    n = proj2.shape[0]
    qkw = RET_HEADS * RET_QK_DIM
    nbc = SSD_GROUPS * SSD_STATE
    full = lambda s: pl.BlockSpec(s, lambda i: (0, 0))
    col = lambda w, off: pl.BlockSpec((n, w), lambda i: (0, off // w))
    shp = lambda w: jax.ShapeDtypeStruct((n, w), F32)
    return pl.pallas_call(
        _sample_pre_kernel,
        grid=(1,),
        in_specs=[col(D_MODEL, OFF_XS), col(D_MODEL, OFF_BC), full((n, LANES)), col(qkw, OFF_Q), col(qkw, OFF_K),
                  full((n, (SSD_CONV - 1) * SSD_CONV_DIM)),
                  full((SSD_CONV, SSD_CONV_DIM)), full((1, SSD_CONV_DIM)),
                  full((1, LANES)), full((1, LANES)),
                  full((LANES, D_MODEL)), full((1, RET_QK_DIM)), full((1, RET_QK_DIM))],
        out_specs=[full((n, D_MODEL)), full((n, LANES)), full((n, nbc)), full((n, nbc)), full((n, D_MODEL)),
                   full((n, (SSD_CONV - 1) * SSD_CONV_DIM)), full((n, qkw)), full((n, qkw))],
        out_shape=[shp(D_MODEL), shp(LANES), shp(nbc), shp(nbc), shp(D_MODEL),
                   shp((SSD_CONV - 1) * SSD_CONV_DIM), shp(qkw), shp(qkw)],
        compiler_params=_cparams(("arbitrary",)),
        name="sample_pre",
    )(proj2, proj2, dt2, proj2, proj2, conv2, cw, cb, b16, al16, eh, cos2, sin2)


STATE_ROWS = D_MODEL
STATE_BLK = 64


def _state_step_kernel(e_ref, s_ref, v_ref, k_ref, q_ref, *rest, bb, layer, fill):
    sn_all, y_ref = rest[-2:]
    if fill:
        for l in range(sn_all.shape[0]):
            if l != layer:
                sn_all[l] = jnp.zeros(sn_all.shape[1:], F32)
        sn_ref = sn_all.at[layer]
    else:
        sn_ref = sn_all
    i = pl.program_id(0)
    nblk = STATE_ROWS // STATE_BLK
    gw = STATE_ROWS // 4
    ones = jnp.ones((SUBLANES, LANES), BF16)
    zpad = jnp.zeros((LANES - SUBLANES, LANES), F32)
    nt = (((1,), (1,)), ((), ()))
    for t in range(bb):
        b = i * bb + t
        vt = jnp.concatenate([v_ref[t], zpad], axis=0).T
        kk = k_ref[t]
        qq = q_ref[t]
        for cch in range(STATE_ROWS // LANES):
            prods = []
            for half in range(LANES // STATE_BLK):
                blk = cch * (LANES // STATE_BLK) + half
                r0 = blk * STATE_BLK
                g = r0 // gw
                vcol = vt[half * STATE_BLK:(half + 1) * STATE_BLK, cch:cch + 1]
                sn = e_ref[b * nblk + blk] * s_ref[t, r0:r0 + STATE_BLK, :] + vcol * kk[g:g + 1, :]
                sn_ref[t, r0:r0 + STATE_BLK, :] = sn
                prods.append(sn * qq[g:g + 1, :])
            p = jnp.concatenate(prods, axis=0)
            hi = p.astype(BF16)
            lo = (p - hi.astype(F32)).astype(BF16)
            ysum = (lax.dot_general(ones, hi, nt, preferred_element_type=F32)
                    + lax.dot_general(ones, lo, nt, preferred_element_type=F32))
            y_ref[t, cch:cch + 1, :] = ysum[0:1, :]


def _state_step(e_flat, s_all, layer, v3, k3, q3, prev=None, bb=8):
    depth, n = s_all.shape[:2]
    blk3 = lambda a, b_: pl.BlockSpec((bb, a, b_), lambda i: (i, 0, 0))
    sblk = pl.BlockSpec((None, bb, STATE_ROWS, LANES), lambda i: (layer, i, 0, 0))
    in_specs = [pl.BlockSpec(memory_space=pltpu.SMEM), sblk, blk3(SUBLANES, LANES), blk3(4, LANES), blk3(4, LANES)]
    args = [e_flat, s_all, v3, k3, q3]
    aliases = {}
    if prev is not None:
        in_specs.append(pl.BlockSpec(memory_space=pl.ANY))
        args.append(prev)
        aliases = {len(args) - 1: 0}
        oblk = sblk
    else:
        oblk = pl.BlockSpec((depth, bb, STATE_ROWS, LANES), lambda i: (0, i, 0, 0))
    return pl.pallas_call(
        functools.partial(_state_step_kernel, bb=bb, layer=layer, fill=prev is None),
        grid=(n // bb,),
        in_specs=in_specs,
        out_specs=[oblk, blk3(SUBLANES, LANES)],
        out_shape=[jax.ShapeDtypeStruct((depth, n, STATE_ROWS, LANES), F32),
                   jax.ShapeDtypeStruct((n, SUBLANES, LANES), F32)],
        input_output_aliases=aliases,
        compiler_params=_cparams(("parallel",)),
        name="state_step",
    )(*args)


def _sample_post_kernel(ys_ref, xs_ref, z_ref, yr_ref, g_ref, dfull_ref, nw_ref, os_ref, or_ref):
    os_ref[...] = _ssd_gate_norm(ys_ref[...], xs_ref[...], z_ref[...].astype(F32), dfull_ref[...],
                                 nw_ref[...]).astype(os_ref.dtype)
    or_ref[...] = _ret_gate_norm(yr_ref[...], g_ref[...].astype(F32)).astype(or_ref.dtype)


def _sample_post(y_ssd, xs, proj2, y_ret, dfull, nw):
    n = y_ssd.shape[0]
    full = lambda s: pl.BlockSpec(s, lambda i: (0, 0))
    col = lambda off: pl.BlockSpec((n, D_MODEL), lambda i: (0, off // D_MODEL))
    return pl.pallas_call(
        _sample_post_kernel,
        grid=(1,),
        in_specs=[full((n, D_MODEL)), full((n, D_MODEL)), col(OFF_Z), full((n, D_MODEL)), col(OFF_G),
                  full((1, D_MODEL)), full((1, D_MODEL))],
        out_specs=[full((n, D_MODEL)), full((n, D_MODEL))],
        out_shape=[jax.ShapeDtypeStruct((n, D_MODEL), BF16)] * 2,
        compiler_params=_cparams(("arbitrary",)),
        name="sample_post",
    )(y_ssd, xs, proj2, y_ret, proj2, dfull, nw)


def _merge_kernel(s5_ref, ssd_ref, ret_ref, ga_ref, gb_ref, gc_ref, h_ref, ws5_ref, wssd_ref, wret_ref, wout_ref,
                  o_ref):
    glu = _dot(s5_ref[...], ws5_ref[...])
    y_a = glu[:, :D_MODEL] * _sigmoid(glu[:, D_MODEL:])
    y_b = _dot(ssd_ref[...], wssd_ref[...])
    y_c = _dot(ret_ref[...], wret_ref[...])
    merged = (_sigmoid(ga_ref[...].astype(F32)) * y_a + _sigmoid(gb_ref[...].astype(F32)) * y_b
              + _sigmoid(gc_ref[...].astype(F32)) * y_c)
    o_ref[...] = h_ref[...] + _dot(merged.astype(BF16), wout_ref[...])


def _merge(s5_g, ssd_n, ret_n, proj, h3, ws5, wssd, wret, wout, layer, tm):
    nb, L, _ = h3.shape
    row3 = lambda off: pl.BlockSpec((None, tm, D_MODEL), lambda b, i: (b, i, off // D_MODEL))
    wfull = lambda s: pl.BlockSpec((None,) + s, lambda b, i: (layer, 0, 0), pipeline_mode=pl.Buffered(1))
    return pl.pallas_call(
        _merge_kernel,
        grid=(nb, L // tm),
        in_specs=[row3(0), row3(0), row3(0),
                  row3(OFF_GATES), row3(OFF_GATES + D_MODEL), row3(OFF_GATES + 2 * D_MODEL), row3(0),
                  wfull((D_MODEL, 2 * D_MODEL)), wfull((D_MODEL, D_MODEL)), wfull((D_MODEL, D_MODEL)),
                  wfull((D_MODEL, D_MODEL))],
        out_specs=row3(0),
        out_shape=jax.ShapeDtypeStruct((nb, L, D_MODEL), F32),
        compiler_params=_cparams(("parallel", "parallel")),
        name="merge",
    )(s5_g, ssd_n, ret_n, proj, proj, proj, h3, ws5, wssd, wret, wout)


def _mlp_kernel(x_ref, nw_ref, wup_ref, wdn_ref, fw_ref, o_ref, xn_ref, acc_ref, *, final_norm):
    k = pl.program_id(1)
    nk = pl.num_programs(1)

    @pl.when(k == 0)
    def _():
        x = x_ref[...]
        xn_ref[...] = _rmsnorm(x, nw_ref[...]).astype(BF16)
        acc_ref[...] = x

    hid = jnp.maximum(_dot(xn_ref[...], wup_ref[...]), 0.0)
    acc_ref[...] += _dot((hid * hid).astype(BF16), wdn_ref[...])

    @pl.when(k == nk - 1)
    def _():
        h = acc_ref[...]
        o_ref[...] = _rmsnorm(h, fw_ref[...]) if final_norm else h


def _mlp(h2, nw, wup, wdn, fw, layer, tm, tf, final_norm):
    rows = h2.shape[0]
    return pl.pallas_call(
        functools.partial(_mlp_kernel, final_norm=final_norm),
        grid=(rows // tm, D_FF // tf),
        in_specs=[pl.BlockSpec((tm, D_MODEL), lambda i, k: (i, 0)),
                  pl.BlockSpec((1, D_MODEL), lambda i, k: (0, 0)),
                  pl.BlockSpec((None, D_MODEL, tf), lambda i, k: (layer, 0, k)),
                  pl.BlockSpec((None, tf, D_MODEL), lambda i, k: (layer, k, 0)),
                  pl.BlockSpec((1, D_MODEL), lambda i, k: (0, 0))],
        out_specs=pl.BlockSpec((tm, D_MODEL), lambda i, k: (i, 0)),
        out_shape=jax.ShapeDtypeStruct((rows, D_MODEL), F32),
        scratch_shapes=[pltpu.VMEM((tm, D_MODEL), BF16), pltpu.VMEM((tm, D_MODEL), F32)],
        compiler_params=_cparams(("parallel", "arbitrary")),
        name="mlp",
    )(h2, nw, wup, wdn, fw)


def _prep_layer(p):
    out = {}
    row = lambda a: a.reshape(1, -1).astype(F32)
    padrow = lambda a: jnp.pad(a.astype(F32), (0, LANES - a.shape[0])).reshape(1, LANES)
    rep = lambda a: jnp.repeat(a.astype(F32), SSD_HEAD_DIM).reshape(1, D_MODEL)
    out['norm1_w'] = row(p['norm1_w'])
    out['norm2_w'] = row(p['norm2_w'])
    out['conv_w'] = p['ssd_conv_w'].astype(F32)
    out['conv_b'] = row(p['ssd_conv_b'])
    out['b16'] = padrow(p['ssd_dt_bias'])
    out['al16'] = padrow(p['ssd_a_log'])
    out['dfull'] = rep(p['ssd_d'])
    out['ssd_nw'] = row(p['ssd_norm_w'])
    out['s5_d'] = row(p['s5_d'])
    abr, abi, bbr, bbi = _s5_discretise(p['s5_a_re'].astype(F32), p['s5_a_im'].astype(F32),
                                        p['s5_log_dt'].astype(F32), p['s5_b_re'].astype(F32),
                                        p['s5_b_im'].astype(F32))
    out['s5'] = _s5_weights(abr, abi, bbr, bbi, p['s5_c_re'].astype(F32), p['s5_c_im'].astype(F32))
    return out


def _head_expand():
    e = (jnp.arange(LANES)[:, None] == (jnp.arange(D_MODEL)[None, :] // SSD_HEAD_DIM))
    return e.astype(BF16)


def _tile(n, pref):
    return pref if n % pref == 0 else n


def _prompt_trunk(x, layers, final_w):
    nb, L, _ = x.shape
    eh = _head_expand()
    _, lm, gp, we, gq = _ret_consts(CHUNK)
    cos2, sin2 = _rope_tables(jnp.arange(L, dtype=jnp.int32))
    zeros_state = jnp.zeros((nb, S5_NSTATE), F32)
    steps = 64
    h = x
    outs = {k: [] for k in ('re', 'im', 'ssd', 'conv', 'ret')}
    for li, lp in enumerate(layers):
        tm = _tile(L, 1024)
        proj, dt = _proj_main(h, lp['norm1_w'], *lp['proj_w'], tm)
        wbr, wbi, abr, abi, wcr, wci = lp['s5']
        y_s5, hr, hi = _s5_scan(proj, wbr, wbi, abr, abi, wcr, wci, lp['s5_d'],
                                zeros_state, zeros_state, steps)
        ssd_n, ssd_s = _ssd_prompt(proj, dt, lp['conv_w'], lp['conv_b'], lp['b16'], lp['al16'],
                                   lp['dfull'], lp['ssd_nw'], eh)
        ret_n, ret_s = _ret_prompt(proj, cos2, sin2, lm, gp, we, gq)
        h = _merge(y_s5, ssd_n, ret_n, proj, h,
                   *lp['merge_w'], li, _tile(L, 512))
        last = li == len(layers) - 1
        h = _mlp(h.reshape(nb * L, D_MODEL), lp['norm2_w'], *lp['mlp_w'], final_w, li,
                 _tile(nb * L, 512), 1024, last).reshape(nb, L, D_MODEL)
        outs['re'].append(hr.reshape(nb, S5_GROUPS, S5_STATE))
        outs['im'].append(hi.reshape(nb, S5_GROUPS, S5_STATE))
        outs['ssd'].append(ssd_s.reshape(nb, SSD_HEADS, SSD_HEAD_DIM, SSD_STATE))
        outs['conv'].append(proj[:, L - (SSD_CONV - 1):, OFF_XS:OFF_XS + SSD_CONV_DIM].astype(F32))
        outs['ret'].append(ret_s.reshape(nb, RET_HEADS, RET_V_DIM, RET_QK_DIM))
    return (h,) + tuple(jnp.stack(outs[k]) for k in ('re', 'im', 'ssd', 'conv', 'ret'))


def _sample_trunk(x, states, layers, final_w):
    n = x.shape[0]
    st_re, st_im, st_ssd, st_conv, st_ret = states
    eh = _head_expand()
    log_gamma, _, _, _, _ = _ret_consts(1)
    cos2, sin2 = _rope_tables(PAST_LEN + jnp.arange(1, dtype=jnp.int32))
    e_ret = jnp.broadcast_to(jnp.repeat(jnp.exp(log_gamma), RET_V_DIM // STATE_BLK)[None, :],
                             (n, STATE_ROWS // STATE_BLK)).reshape(-1)
    h = x.reshape(1, n, D_MODEL)
    depth = len(layers)
    ssd_all = st_ssd.reshape(depth, n, STATE_ROWS, SSD_STATE).astype(F32)
    ret_all = st_ret.reshape(depth, n, STATE_ROWS, RET_QK_DIM).astype(F32)
    ssd_new = ret_new = None
    outs = {k: [] for k in ('re', 'im', 'conv')}
    for li, lp in enumerate(layers):
        proj, dt = _proj_main(h, lp['norm1_w'], *lp['proj_w'], n)
        wbr, wbi, abr, abi, wcr, wci = lp['s5']
        y_s5, hr, hi = _s5_scan(proj, wbr, wbi, abr, abi, wcr, wci, lp['s5_d'],
                                st_re[li].reshape(n, S5_NSTATE).astype(F32),
                                st_im[li].reshape(n, S5_NSTATE).astype(F32), 1)
        proj2, dt2 = proj.reshape(n, N_MAIN), dt.reshape(n, LANES)
        conv2 = st_conv[li].reshape(n, (SSD_CONV - 1) * SSD_CONV_DIM).astype(F32)
        vv, e16, bs, cs, xs, conv_new, qr, kr = _sample_pre(
            proj2, dt2, conv2, lp['conv_w'], lp['conv_b'], lp['b16'], lp['al16'], eh, cos2, sin2)
        ssd_new, y_ssd = _state_step(
            e16[:, :SSD_HEADS].reshape(-1), ssd_all, li,
            vv.reshape(n, SUBLANES, LANES), bs.reshape(n, SSD_GROUPS, SSD_STATE),
            cs.reshape(n, SSD_GROUPS, SSD_STATE), prev=ssd_new)
        vret = proj2[:, OFF_V:OFF_V + D_MODEL].astype(F32)
        ret_new, y_ret = _state_step(
            e_ret, ret_all, li,
            vret.reshape(n, SUBLANES, LANES), kr.reshape(n, RET_HEADS, RET_QK_DIM),
            qr.reshape(n, RET_HEADS, RET_QK_DIM), prev=ret_new)
        ssd_n, ret_n = _sample_post(y_ssd.reshape(n, D_MODEL), xs, proj2, y_ret.reshape(n, D_MODEL),
                                    lp['dfull'], lp['ssd_nw'])
        h = _merge(y_s5, ssd_n.reshape(1, n, D_MODEL), ret_n.reshape(1, n, D_MODEL), proj, h,
                   *lp['merge_w'], li, n)
        last = li == len(layers) - 1
        h = _mlp(h.reshape(n, D_MODEL), lp['norm2_w'], *lp['mlp_w'], final_w, li,
                 n, 1024, last).reshape(1, n, D_MODEL)
        outs['re'].append(hr.reshape(n, S5_GROUPS, S5_STATE))
        outs['im'].append(hi.reshape(n, S5_GROUPS, S5_STATE))
        outs['conv'].append(conv_new.reshape(n, SSD_CONV - 1, SSD_CONV_DIM))
    return (h.reshape(n, 1, D_MODEL), jnp.stack(outs['re']), jnp.stack(outs['im']),
            ssd_new.reshape(depth, n, SSD_HEADS, SSD_HEAD_DIM, SSD_STATE), jnp.stack(outs['conv']),
            ret_new.reshape(depth, n, RET_HEADS, RET_V_DIM, RET_QK_DIM))


def kernel(x_prompt, x_sample, state_s5_re, state_s5_im, state_ssd, state_conv, state_ret, norm1_w, w_in, s5_a_re, s5_a_im, s5_log_dt, s5_b_re, s5_b_im, s5_c_re, s5_c_im, s5_d, w_s5_out, ssd_conv_w, ssd_conv_b, ssd_dt_bias, ssd_a_log, ssd_d, ssd_norm_w, w_ssd_out, w_ret_out, w_out, norm2_w, w_mlp_up, w_mlp_down, final_norm_w):
    params = dict(norm1_w=norm1_w, w_in=w_in, s5_a_re=s5_a_re, s5_a_im=s5_a_im, s5_log_dt=s5_log_dt,
                  s5_b_re=s5_b_re, s5_b_im=s5_b_im, s5_c_re=s5_c_re, s5_c_im=s5_c_im, s5_d=s5_d,
                  w_s5_out=w_s5_out, ssd_conv_w=ssd_conv_w, ssd_conv_b=ssd_conv_b, ssd_dt_bias=ssd_dt_bias,
                  ssd_a_log=ssd_a_log, ssd_d=ssd_d, ssd_norm_w=ssd_norm_w, w_ssd_out=w_ssd_out,
                  w_ret_out=w_ret_out, w_out=w_out, norm2_w=norm2_w, w_mlp_up=w_mlp_up, w_mlp_down=w_mlp_down)
    depth = w_in.shape[0]
    big = ('w_in', 'w_s5_out', 'w_ssd_out', 'w_ret_out', 'w_out', 'w_mlp_up', 'w_mlp_down')
    layers = [_prep_layer({name: arr[l] for name, arr in params.items() if name not in big}) for l in range(depth)]
    merge_w = tuple(params[name].astype(BF16) for name in ('w_s5_out', 'w_ssd_out', 'w_ret_out', 'w_out'))
    mlp_w = tuple(params[name].astype(BF16) for name in ('w_mlp_up', 'w_mlp_down'))
    w_a, w_b, w_dt = _proj_weight_slabs(w_in.astype(F32))
    for l, lp in enumerate(layers):
        lp['proj_w'] = (w_a, w_b, w_dt, l)
        lp['merge_w'] = merge_w
        lp['mlp_w'] = mlp_w
    final_w = final_norm_w.reshape(1, D_MODEL).astype(F32)
    y_p, re_p, im_p, ssd_p, conv_p, ret_p = _prompt_trunk(x_prompt.astype(F32), layers, final_w)
    y_s, re_s, im_s, ssd_s, conv_s, ret_s = _sample_trunk(
        x_sample.astype(F32), (state_s5_re, state_s5_im, state_ssd, state_conv, state_ret), layers, final_w)
    return (y_p, y_s, re_p, im_p, ssd_p, conv_p, ret_p, re_s, im_s, ssd_s, conv_s, ret_s)
```

```python
import functools
import math

import jax
import jax.numpy as jnp
from jax import lax
from jax.experimental import pallas as pl
from jax.experimental.pallas import tpu as pltpu

F32 = jnp.float32
BF16 = jnp.bfloat16

D_MODEL = 1024
S5_GROUP = 16
S5_STATE = 64
S5_GROUPS = D_MODEL // S5_GROUP
S5_NSTATE = S5_GROUPS * S5_STATE
S5_SUPER = 8
SSD_HEADS = 16
SSD_HEAD_DIM = 64
SSD_GROUPS = 4
SSD_STATE = 128
SSD_CONV = 4
SSD_CONV_DIM = D_MODEL + 2 * SSD_GROUPS * SSD_STATE
RET_HEADS = 4
RET_QK_DIM = 128
RET_V_DIM = 256
ROPE_BASE = 10000.0
D_FF = 4 * D_MODEL
CHUNK = 128
EPS = 1e-6
PAST_LEN = 16384
LANES = 128
SUBLANES = 8
VMEM_LIMIT = 52 * 1024 * 1024

OFF_U, OFF_Z, OFF_XS, OFF_BC, OFF_Q, OFF_K, OFF_V, OFF_G, OFF_GATES = (
    0, 1024, 2048, 3072, 4096, 4608, 5120, 6144, 7168)
N_MAIN = 10240
SLAB_A_COLS = OFF_Q
SLAB_B_START = 2 * D_MODEL + SSD_CONV_DIM + SSD_HEADS
PROJ_TN = 2048


def _cparams(sem):
    return pltpu.CompilerParams(dimension_semantics=sem, vmem_limit_bytes=VMEM_LIMIT)


def _sigmoid(x):
    return 1.0 / (1.0 + jnp.exp(-x))


def _silu(x):
    return x * _sigmoid(x)


def _softplus(x):
    return jnp.maximum(x, 0.0) + jnp.log1p(jnp.exp(-jnp.abs(x)))


def _gelu_tanh(x):
    return 0.5 * x * (1.0 + jnp.tanh(math.sqrt(2.0 / math.pi) * (x + 0.044715 * (x * x * x))))


def _rmsnorm(x, w):
    return x * lax.rsqrt(jnp.mean(x * x, axis=-1, keepdims=True) + EPS) * w


def _cumsum_rows(x):
    n = x.shape[0]
    row = lax.broadcasted_iota(jnp.int32, x.shape, 0)
    k = 1
    while k < n:
        x = x + jnp.where(row >= k, pltpu.roll(x, k, 0), 0.0)
        k *= 2
    return x


def _group_rms(y, width):
    outs = []
    for s in range(0, y.shape[-1], width):
        yg = y[:, s:s + width]
        outs.append(yg * lax.rsqrt(jnp.mean(yg * yg, axis=-1, keepdims=True) + EPS))
    return jnp.concatenate(outs, axis=-1)


def _dot(a, b):
    return jnp.dot(a, b, preferred_element_type=F32)


def _split_dot(x, e):
    hi = x.astype(BF16)
    lo = (x - hi.astype(F32)).astype(BF16)
    return _dot(hi, e) + _dot(lo, e)


def _proj_kernel(x_ref, nw_ref, wa_ref, wb_ref, wdt_ref, o_ref, odt_ref, xn_ref, *, na):
    j = pl.program_id(2)

    @pl.when(j == 0)
    def _():
        xn = _rmsnorm(x_ref[...], nw_ref[...]).astype(BF16)
        xn_ref[...] = xn
        odt_ref[...] = _dot(xn, wdt_ref[...])

    @pl.when(j < na)
    def _():
        o_ref[...] = _dot(xn_ref[...], wa_ref[...]).astype(o_ref.dtype)

    @pl.when(j >= na)
    def _():
        o_ref[...] = _dot(xn_ref[...], wb_ref[...]).astype(o_ref.dtype)


PREP_TN = 1024
DT_START = SLAB_B_START - SSD_HEADS
SLAB_SHIFT = SLAB_B_START % PREP_TN


def _slab_b_kernel(lo_ref, hi_ref, o_ref):
    keep = PREP_TN - SLAB_SHIFT
    lane = lax.broadcasted_iota(jnp.int32, (D_MODEL, PREP_TN), 1)
    lo = pltpu.roll(lo_ref[...].astype(F32), keep, 1)
    hi = pltpu.roll(hi_ref[...].astype(F32), keep, 1)
    o_ref[...] = jnp.where(lane < keep, lo, hi).astype(BF16)


def _proj_weight_slabs(w_in):
    depth = w_in.shape[0]
    tn = PREP_TN
    w_all = w_in.astype(BF16)
    w_dt = jnp.pad(w_in[:, :, DT_START:SLAB_B_START], ((0, 0), (0, 0), (0, LANES - SSD_HEADS))).astype(BF16)
    blk = lambda f: pl.BlockSpec((None, D_MODEL, tn), f)
    nbcols = N_MAIN - SLAB_A_COLS
    first = SLAB_B_START // tn
    w_b = pl.pallas_call(
        _slab_b_kernel,
        grid=(depth, nbcols // tn),
        in_specs=[blk(lambda l, j: (l, 0, first + j)), blk(lambda l, j: (l, 0, first + j + 1))],
        out_specs=blk(lambda l, j: (l, 0, j)),
        out_shape=jax.ShapeDtypeStruct((depth, D_MODEL, nbcols), BF16),
        compiler_params=_cparams(("parallel", "parallel")),
        name="weight_slab_b",
    )(w_all, w_all)
    return w_all, w_b, w_dt


def _proj_main(x3, nw, w_a, w_b, wdt, layer, tm):
    nb, L, _ = x3.shape
    tn = PROJ_TN
    na = SLAB_A_COLS // tn
    return pl.pallas_call(
        functools.partial(_proj_kernel, na=na),
        grid=(nb, L // tm, N_MAIN // tn),
        in_specs=[pl.BlockSpec((None, tm, D_MODEL), lambda b, i, j: (b, i, 0)),
                  pl.BlockSpec((1, D_MODEL), lambda b, i, j: (0, 0)),
                  pl.BlockSpec((None, D_MODEL, tn), lambda b, i, j: (layer, 0, jnp.minimum(j, na - 1))),
                  pl.BlockSpec((None, D_MODEL, tn), lambda b, i, j: (layer, 0, jnp.maximum(j - na, 0))),
                  pl.BlockSpec((None, D_MODEL, LANES), lambda b, i, j: (layer, 0, 0))],
        out_specs=[pl.BlockSpec((None, tm, tn), lambda b, i, j: (b, i, j)),
                   pl.BlockSpec((None, tm, LANES), lambda b, i, j: (b, i, 0))],
        out_shape=[jax.ShapeDtypeStruct((nb, L, N_MAIN), BF16), jax.ShapeDtypeStruct((nb, L, LANES), F32)],
        scratch_shapes=[pltpu.VMEM((tm, D_MODEL), BF16)],
        compiler_params=_cparams(("parallel", "parallel", "arbitrary")),
        name="proj_main",
    )(x3, nw, w_a, w_b, wdt)


def _s5_disc_kernel(are_ref, aim_ref, ldt_ref, bre_ref, bim_ref, abr_ref, abi_ref, bbr_ref, bbi_ref):
    ar, ai = are_ref[...], aim_ref[...]
    dt = jnp.exp(ldt_ref[...])
    mag = jnp.exp(ar * dt)
    abr = mag * jnp.cos(ai * dt)
    abi = mag * jnp.sin(ai * dt)
    abr_ref[...] = abr
    abi_ref[...] = abi
    nr, ni = abr - 1.0, abi
    den = 1.0 / (ar * ar + ai * ai)
    cr = (nr * ar + ni * ai) * den
    ci = (ni * ar - nr * ai) * den
    br, bi = bre_ref[...], bim_ref[...]
    bbr_ref[...] = cr * br - ci * bi
    bbi_ref[...] = cr * bi + ci * br


def _s5_discretise(a_re, a_im, log_dt, b_re, b_im):
    g, p = a_re.shape
    gs = b_re.shape[-1]
    full3 = lambda s: pl.BlockSpec(s, lambda: (0, 0, 0))
    return pl.pallas_call(
        _s5_disc_kernel,
        in_specs=[full3((g, 1, p)), full3((g, 1, p)), full3((g, 1, 1)), full3((g, gs, p)), full3((g, gs, p))],
        out_specs=[full3((g, 1, p)), full3((g, 1, p)), full3((g, gs, p)), full3((g, gs, p))],
        out_shape=[jax.ShapeDtypeStruct((g, 1, p), F32)] * 2 + [jax.ShapeDtypeStruct((g, gs, p), F32)] * 2,
        name="s5_discretise",
    )(a_re.reshape(g, 1, p), a_im.reshape(g, 1, p), log_dt.reshape(g, 1, 1),
      jnp.swapaxes(b_re, 1, 2), jnp.swapaxes(b_im, 1, 2))


def _s5_kernel(u_ref, wbr_ref, wbi_ref, ar_ref, ai_ref, wcr_ref, wci_ref, d_ref, h0r_ref, h0i_ref,
               y_ref, hr_ref, hi_ref, bur, bui, *, nbatch, steps, lane_w):
    @pl.when(pl.program_id(0) == 0)
    def _():
        hr_ref[...] = h0r_ref[...]
        hi_ref[...] = h0i_ref[...]

    cw = S5_SUPER * S5_GROUP
    sw = S5_SUPER * S5_STATE
    batch_major = steps > 1
    u = u_ref[...].astype(F32)
    if batch_major:
        u = jnp.swapaxes(u, 0, 1).reshape(steps * nbatch, D_MODEL)
    ub = u.astype(BF16)
    for j in range(S5_GROUPS // S5_SUPER):
        uj = ub[:, j * cw:(j + 1) * cw]
        bur[:, j * sw:(j + 1) * sw] = _dot(uj, wbr_ref[j])
        bui[:, j * sw:(j + 1) * sw] = _dot(uj, wbi_ref[j])

        for lc in range(sw // lane_w):
            sl = slice(j * sw + lc * lane_w, j * sw + (lc + 1) * lane_w)
            are = jnp.broadcast_to(ar_ref[:, sl], (nbatch, lane_w))
            aim = jnp.broadcast_to(ai_ref[:, sl], (nbatch, lane_w))
            hr, hi = hr_ref[:, sl], hi_ref[:, sl]
            for t in range(steps):
                rows = slice(t * nbatch, (t + 1) * nbatch)
                hr, hi = are * hr - aim * hi + bur[rows, sl], are * hi + aim * hr + bui[rows, sl]
                bur[rows, sl] = hr
                bui[rows, sl] = hi
            hr_ref[:, sl] = hr
            hi_ref[:, sl] = hi

        yj = (_dot(bur[:, j * sw:(j + 1) * sw].astype(BF16), wcr_ref[j])
              + _dot(bui[:, j * sw:(j + 1) * sw].astype(BF16), wci_ref[j]))
        yj = _gelu_tanh(yj + d_ref[:, j * cw:(j + 1) * cw] * u[:, j * cw:(j + 1) * cw])
        if batch_major:
            y_ref[:, :, j * cw:(j + 1) * cw] = jnp.swapaxes(yj.reshape(steps, nbatch, cw), 0, 1).astype(y_ref.dtype)
        else:
            y_ref[:, j * cw:(j + 1) * cw] = yj.astype(y_ref.dtype)


def _s5_scan(proj, wbr, wbi, abr, abi, wcr, wci, d, h0r, h0i, steps):
    nb, L, _ = proj.shape
    nsup = S5_GROUPS // S5_SUPER
    cw, sw = S5_SUPER * S5_GROUP, S5_SUPER * S5_STATE
    c2 = lambda s: pl.BlockSpec(s, lambda c: (0, 0))
    c3 = lambda s: pl.BlockSpec(s, lambda c: (0, 0, 0))
    if steps > 1:
        nbatch, grid = nb, L // steps
        u_spec = pl.BlockSpec((nb, steps, D_MODEL), lambda c: (0, c, OFF_U // D_MODEL))
        y_spec = pl.BlockSpec((nb, steps, D_MODEL), lambda c: (0, c, 0))
    else:
        nbatch, grid = L, 1
        u_spec = pl.BlockSpec((None, L, D_MODEL), lambda c: (0, 0, OFF_U // D_MODEL))
        y_spec = pl.BlockSpec((None, L, D_MODEL), lambda c: (0, 0, 0))
    blk = nbatch * steps
    lane_w = max(LANES, S5_NSTATE // nbatch)
    kern = functools.partial(_s5_kernel, nbatch=nbatch, steps=steps, lane_w=lane_w)
    return pl.pallas_call(
        kern,
        grid=(grid,),
        in_specs=[u_spec,
                  c3((nsup, cw, sw)), c3((nsup, cw, sw)),
                  c2((1, S5_NSTATE)), c2((1, S5_NSTATE)),
                  c3((nsup, sw, cw)), c3((nsup, sw, cw)),
                  c2((1, D_MODEL)),
                  c2((nbatch, S5_NSTATE)), c2((nbatch, S5_NSTATE))],
        out_specs=[y_spec, c2((nbatch, S5_NSTATE)), c2((nbatch, S5_NSTATE))],
        out_shape=[jax.ShapeDtypeStruct((nb, L, D_MODEL), BF16),
                   jax.ShapeDtypeStruct((nbatch, S5_NSTATE), F32),
                   jax.ShapeDtypeStruct((nbatch, S5_NSTATE), F32)],
        scratch_shapes=[pltpu.VMEM((blk, S5_NSTATE), F32), pltpu.VMEM((blk, S5_NSTATE), F32)],
        compiler_params=_cparams(("arbitrary",)),
        name="s5_scan",
    )(proj, wbr, wbi, abr, abi, wcr, wci, d, h0r, h0i)


def _s5_weights(abr, abi, bbr, bbi, c_re, c_im):
    nsup = S5_GROUPS // S5_SUPER
    eye = jnp.eye(S5_SUPER, dtype=F32)

    def bexp(b):
        b = b.reshape(nsup, S5_SUPER, S5_GROUP, S5_STATE)
        return (b[:, :, :, None, :] * eye[None, :, None, :, None]).reshape(
            nsup, S5_SUPER * S5_GROUP, S5_SUPER * S5_STATE).astype(BF16)

    def cexp(c):
        c = jnp.swapaxes(c, 1, 2).reshape(nsup, S5_SUPER, S5_STATE, S5_GROUP)
        return (c[:, :, :, None, :] * eye[None, :, None, :, None]).reshape(
            nsup, S5_SUPER * S5_STATE, S5_SUPER * S5_GROUP).astype(BF16)

    return (bexp(bbr), bexp(bbi), abr.reshape(1, S5_NSTATE), abi.reshape(1, S5_NSTATE),
            cexp(c_re), cexp(-c_im))


def _ssd_gate_norm(y, xs, z, dfull, nw):
    y = (y + dfull * xs) * _silu(z)
    return _group_rms(y, D_MODEL // SSD_GROUPS) * nw


CONV_PAD = 16
CONV_ROWS = 2 * CHUNK


def _conv_shift_mats():
    i = jnp.arange(CHUNK)[:, None]
    j = jnp.arange(CONV_ROWS)[None, :]
    return jnp.stack([(j == CONV_PAD + i - s) for s in range(1, SSD_CONV)]).astype(BF16)


def _ssd_chunk_seq(xs_ref, bc_ref, z_ref, dt_ref, cw_ref, cb_ref, b16_ref, al16_ref,
                   dfull_ref, nw_ref, eh_ref, sh_ref, o_ref, xpad, st):
    q = CHUNK
    pad = CONV_PAD
    xpad[0:pad, :] = xpad[q:q + pad, :]

    xpad[pad:pad + q, :D_MODEL] = xs_ref[...]
    xpad[pad:pad + q, D_MODEL:] = bc_ref[...]
    staged = xpad[...]
    acc = cb_ref[...] + xpad[pad:pad + q, :].astype(F32) * cw_ref[SSD_CONV - 1:SSD_CONV, :]
    for s in range(1, SSD_CONV):
        acc = acc + _dot(sh_ref[s - 1], staged) * cw_ref[SSD_CONV - 1 - s:SSD_CONV - s, :]
    xc = _silu(acc)
    xs = xc[:, :D_MODEL]
    nbc = SSD_GROUPS * SSD_STATE
    bs = xc[:, D_MODEL:D_MODEL + nbc]
    cs = xc[:, D_MODEL + nbc:]

    delta16 = _softplus(dt_ref[...] + b16_ref[...])
    acum16 = _cumsum_rows(delta16 * (-jnp.exp(al16_ref[...])))
    acum_t = acum16.T
    delta_t = delta16.T
    last16 = acum16[q - 1:q, :]
    ea_f = _split_dot(jnp.exp(acum16), eh_ref[...])
    dw_f = _split_dot(delta16 * jnp.exp(last16 - acum16), eh_ref[...])
    xw = (xs * dw_f).astype(BF16)

    ri = lax.broadcasted_iota(jnp.int32, (q, q), 0)
    ci = lax.broadcasted_iota(jnp.int32, (q, q), 1)
    causal = ri >= ci
    hw = D_MODEL // SSD_GROUPS
    hpg = SSD_HEADS // SSD_GROUPS
    xsb = xs.astype(BF16)
    ys = []
    for g in range(SSD_GROUPS):
        bg = bs[:, g * SSD_STATE:(g + 1) * SSD_STATE]
        cg = cs[:, g * SSD_STATE:(g + 1) * SSD_STATE].astype(BF16)
        bgt = bg.T.astype(BF16)
        scores = _dot(cg, bgt)
        sg = st[:, g * hw:(g + 1) * hw]
        y_inter = _dot(cg, sg.astype(BF16)) * ea_f[:, g * hw:(g + 1) * hw]
        for r in range(hpg):
            hd = g * hpg + r
            seg = acum16[:, hd:hd + 1] - acum_t[hd:hd + 1, :]
            lm = jnp.exp(jnp.where(causal, seg, -jnp.inf)) * delta_t[hd:hd + 1, :]
            p = (scores * lm).astype(BF16)
            ys.append(_dot(p, xsb[:, hd * SSD_HEAD_DIM:(hd + 1) * SSD_HEAD_DIM])
                      + y_inter[:, r * SSD_HEAD_DIM:(r + 1) * SSD_HEAD_DIM])
        st[:, g * hw:(g + 1) * hw] = (ea_f[q - 1:q, g * hw:(g + 1) * hw] * sg
                                      + _dot(bgt, xw[:, g * hw:(g + 1) * hw]))
    y = jnp.concatenate(ys, axis=-1)
    o_ref[...] = _ssd_gate_norm(y, xs, z_ref[...].astype(F32), dfull_ref[...], nw_ref[...]).astype(o_ref.dtype)


def _rotary(x, cos2, sin2):
    outs = []
    for h in range(RET_HEADS):
        xh = x[:, h * RET_QK_DIM:(h + 1) * RET_QK_DIM]
        outs.append(xh * cos2 + pltpu.roll(xh, RET_QK_DIM // 2, 1) * sin2)
    return outs


def _ret_gate_norm(y, g):
    return _group_rms(y, RET_V_DIM) * _silu(g)


def _ret_chunk_seq(q_ref, k_ref, v_ref, g_ref, cos2, sin2, lm_ref, gp_ref, we_ref, gq_ref, o_ref, st):
    qr = _rotary(q_ref[...].astype(F32), cos2, sin2)
    kr = _rotary(k_ref[...].astype(F32), cos2, sin2)
    vb = v_ref[...]
    vw = (vb.astype(F32) * we_ref[...]).astype(BF16)
    ys = []
    for h in range(RET_HEADS):
        sl = slice(h * RET_V_DIM, (h + 1) * RET_V_DIM)
        qh = qr[h].astype(BF16)
        kht = (kr[h] * (RET_QK_DIM ** -0.5)).T.astype(BF16)
        p = (_dot(qh, kht) * lm_ref[h]).astype(BF16)
        sh = st[:, sl]
        ys.append(_dot(p, vb[:, sl]) + _dot(qh, sh.astype(BF16)) * gp_ref[:, sl])
        st[:, sl] = gq_ref[:, sl] * sh + _dot(kht, vw[:, sl])
    y = jnp.concatenate(ys, axis=-1)
    o_ref[...] = _ret_gate_norm(y, g_ref[...].astype(F32)).astype(o_ref.dtype)


def _mix_chunk_kernel(xs_ref, bc_ref, z_ref, dt_ref, cw_ref, cb_ref, b16_ref, al16_ref, dfull_ref, nw_ref, eh_ref,
                      sh_ref, q_ref, k_ref, v_ref, g_ref, cos_ref, sin_ref, lm_ref, gp_ref, we_ref, gq_ref,
                      so_ref, ss_ref, ro_ref, rs_ref, xpad, st_ssd, st_ret):
    c = pl.program_id(1)
    nc = pl.num_programs(1)
    nseq = xs_ref.shape[0]

    @pl.when(c == 0)
    def _():
        xpad[...] = jnp.zeros(xpad.shape, BF16)
        st_ssd[...] = jnp.zeros(st_ssd.shape, F32)
        st_ret[...] = jnp.zeros(st_ret.shape, F32)

    cos2, sin2 = cos_ref[...], sin_ref[...]
    for bi in range(nseq):
        _ssd_chunk_seq(xs_ref.at[bi], bc_ref.at[bi], z_ref.at[bi], dt_ref.at[bi], cw_ref, cb_ref, b16_ref,
                       al16_ref, dfull_ref, nw_ref, eh_ref, sh_ref, so_ref.at[bi], xpad.at[bi], st_ssd.at[bi])
        _ret_chunk_seq(q_ref.at[bi], k_ref.at[bi], v_ref.at[bi], g_ref.at[bi], cos2, sin2, lm_ref, gp_ref, we_ref,
                       gq_ref, ro_ref.at[bi], st_ret.at[bi])

    @pl.when(c == nc - 1)
    def _():
        for bi in range(nseq):
            for k in range(D_MODEL // LANES):
                ss_ref[bi, k * LANES:(k + 1) * LANES, :] = st_ssd[bi, :, k * LANES:(k + 1) * LANES].T
                rs_ref[bi, k * LANES:(k + 1) * LANES, :] = st_ret[bi, :, k * LANES:(k + 1) * LANES].T


def _ret_consts(q):
    hidx = jnp.arange(RET_HEADS, dtype=F32)
    log_gamma = jnp.log1p(-jnp.exp2(-5.0 - hidx))
    i = jnp.arange(q, dtype=F32)
    seg = (i[:, None] - i[None, :])[None] * log_gamma[:, None, None]
    lm = jnp.exp(jnp.where((i[:, None] >= i[None, :])[None], seg, -jnp.inf))
    rep = lambda t: jnp.repeat(t, RET_V_DIM, axis=-1)
    gp = rep(jnp.exp((i[:, None] + 1.0) * log_gamma[None, :]))
    we = rep(jnp.exp((q - 1.0 - i[:, None]) * log_gamma[None, :]))
    gq = rep(jnp.exp(q * log_gamma)[None, :])
    return log_gamma, lm, gp, we, gq


def _rope_tables(pos):
    half = RET_QK_DIM // 2
    inv = ROPE_BASE ** (-jnp.arange(half, dtype=F32) / half)
    ang = pos.astype(F32)[:, None] * inv[None]
    cos, sin = jnp.cos(ang), jnp.sin(ang)
    return jnp.concatenate([cos, cos], axis=-1), jnp.concatenate([-sin, sin], axis=-1)


def _mix_prompt(proj, dt, cw, cb, b16, al16, dfull, nw, eh, cos2, sin2, lm, gp, we, gq, nseq=2):
    nb, L, _ = proj.shape
    q = CHUNK
    qkw = RET_HEADS * RET_QK_DIM
    c2 = lambda s: pl.BlockSpec(s, lambda b, c: (0, 0))
    rows = lambda w, off: pl.BlockSpec((nseq, q, w), lambda b, c: (b, c, off // w))
    state = pl.BlockSpec((nseq, D_MODEL, LANES), lambda b, c: (b, 0, 0))
    return pl.pallas_call(
        _mix_chunk_kernel,
        grid=(nb // nseq, L // q),
        in_specs=[rows(D_MODEL, OFF_XS), rows(D_MODEL, OFF_BC), rows(D_MODEL, OFF_Z), rows(LANES, 0),
                  c2((SSD_CONV, SSD_CONV_DIM)), c2((1, SSD_CONV_DIM)),
                  c2((1, LANES)), c2((1, LANES)),
                  c2((1, D_MODEL)), c2((1, D_MODEL)),
                  c2((LANES, D_MODEL)),
                  pl.BlockSpec((SSD_CONV - 1, q, CONV_ROWS), lambda b, c: (0, 0, 0)),
                  rows(qkw, OFF_Q), rows(qkw, OFF_K), rows(D_MODEL, OFF_V), rows(D_MODEL, OFF_G),
                  pl.BlockSpec((q, RET_QK_DIM), lambda b, c: (c, 0)),
                  pl.BlockSpec((q, RET_QK_DIM), lambda b, c: (c, 0)),
                  pl.BlockSpec((RET_HEADS, q, q), lambda b, c: (0, 0, 0)),
                  c2((q, D_MODEL)), c2((q, D_MODEL)), c2((1, D_MODEL))],
        out_specs=[rows(D_MODEL, 0), state, rows(D_MODEL, 0), state],
        out_shape=[jax.ShapeDtypeStruct((nb, L, D_MODEL), BF16),
                   jax.ShapeDtypeStruct((nb, D_MODEL, SSD_STATE), F32),
                   jax.ShapeDtypeStruct((nb, L, D_MODEL), BF16),
                   jax.ShapeDtypeStruct((nb, D_MODEL, RET_QK_DIM), F32)],
        scratch_shapes=[pltpu.VMEM((nseq, CONV_ROWS, SSD_CONV_DIM), BF16),
                        pltpu.VMEM((nseq, SSD_STATE, D_MODEL), F32),
                        pltpu.VMEM((nseq, RET_QK_DIM, D_MODEL), F32)],
        compiler_params=_cparams(("parallel", "arbitrary")),
        name="mix_chunk",
    )(proj, proj, proj, dt, cw, cb, b16, al16, dfull, nw, eh, _conv_shift_mats(),
      proj, proj, proj, proj, cos2, sin2, lm, gp, we, gq)


def _sample_pre_kernel(xsin_ref, bcin_ref, dt_ref, q_ref, k_ref, conv_ref, cw_ref, cb_ref, b16_ref, al16_ref,
                       eh_ref, cos_ref, sin_ref,
                       vv_ref, e16_ref, bs_ref, cs_ref, xs_ref, convn_ref, qr_ref, kr_ref):
    cd = SSD_CONV_DIM
    x = jnp.concatenate([xsin_ref[...], bcin_ref[...]], axis=-1).astype(F32)
    acc = cb_ref[...] + x * cw_ref[SSD_CONV - 1:SSD_CONV, :]
    for tap in range(SSD_CONV - 1):
        acc = acc + conv_ref[:, tap * cd:(tap + 1) * cd] * cw_ref[tap:tap + 1, :]
    convn_ref[:, 0:(SSD_CONV - 2) * cd] = conv_ref[:, cd:(SSD_CONV - 1) * cd]
    convn_ref[:, (SSD_CONV - 2) * cd:] = x
    xc = _silu(acc)
    xs = xc[:, :D_MODEL]
    nbc = SSD_GROUPS * SSD_STATE
    xs_ref[...] = xs
    bs_ref[...] = xc[:, D_MODEL:D_MODEL + nbc]
    cs_ref[...] = xc[:, D_MODEL + nbc:]
    delta16 = _softplus(dt_ref[...] + b16_ref[...])
    e16_ref[...] = jnp.exp(delta16 * (-jnp.exp(al16_ref[...])))
    vv_ref[...] = xs * _split_dot(delta16, eh_ref[...])
    cos2, sin2 = cos_ref[...], sin_ref[...]
    qr_ref[...] = jnp.concatenate(_rotary(q_ref[...].astype(F32), cos2, sin2), axis=-1)
    kr_ref[...] = jnp.concatenate(_rotary(k_ref[...].astype(F32), cos2, sin2), axis=-1) * (RET_QK_DIM ** -0.5)


def _sample_pre(proj2, dt2, conv2, cw, cb, b16, al16, eh, cos2, sin2):
    n = proj2.shape[0]
    qkw = RET_HEADS * RET_QK_DIM
    nbc = SSD_GROUPS * SSD_STATE
    full = lambda s: pl.BlockSpec(s, lambda i: (0, 0))
    col = lambda w, off: pl.BlockSpec((n, w), lambda i: (0, off // w))
    shp = lambda w: jax.ShapeDtypeStruct((n, w), F32)
    return pl.pallas_call(
        _sample_pre_kernel,
        grid=(1,),
        in_specs=[col(D_MODEL, OFF_XS), col(D_MODEL, OFF_BC), full((n, LANES)), col(qkw, OFF_Q), col(qkw, OFF_K),
                  full((n, (SSD_CONV - 1) * SSD_CONV_DIM)),
                  full((SSD_CONV, SSD_CONV_DIM)), full((1, SSD_CONV_DIM)),
                  full((1, LANES)), full((1, LANES)),
                  full((LANES, D_MODEL)), full((1, RET_QK_DIM)), full((1, RET_QK_DIM))],
        out_specs=[full((n, D_MODEL)), full((n, LANES)), full((n, nbc)), full((n, nbc)), full((n, D_MODEL)),
                   full((n, (SSD_CONV - 1) * SSD_CONV_DIM)), full((n, qkw)), full((n, qkw))],
        out_shape=[shp(D_MODEL), shp(LANES), shp(nbc), shp(nbc), shp(D_MODEL),
                   shp((SSD_CONV - 1) * SSD_CONV_DIM), shp(qkw), shp(qkw)],
        compiler_params=_cparams(("arbitrary",)),
        name="sample_pre",
    )(proj2, proj2, dt2, proj2, proj2, conv2, cw, cb, b16, al16, eh, cos2, sin2)


STATE_ROWS = D_MODEL
STATE_BLK = 64


def _state_step_kernel(e_ref, s_ref, v_ref, k_ref, q_ref, *rest, bb, layer, fill):
    sn_all, y_ref = rest[-2:]
    if fill:
        for l in range(sn_all.shape[0]):
            if l != layer:
                sn_all[l] = jnp.zeros(sn_all.shape[1:], F32)
        sn_ref = sn_all.at[layer]
    else:
        sn_ref = sn_all
    i = pl.program_id(0)
    nblk = STATE_ROWS // STATE_BLK
    gw = STATE_ROWS // 4
    ones = jnp.ones((SUBLANES, LANES), BF16)
    zpad = jnp.zeros((LANES - SUBLANES, LANES), F32)
    nt = (((1,), (1,)), ((), ()))
    for t in range(bb):
        b = i * bb + t
        vt = jnp.concatenate([v_ref[t], zpad], axis=0).T
        kk = k_ref[t]
        qq = q_ref[t]
        for cch in range(STATE_ROWS // LANES):
            prods = []
            for half in range(LANES // STATE_BLK):
                blk = cch * (LANES // STATE_BLK) + half
                r0 = blk * STATE_BLK
                g = r0 // gw
                vcol = vt[half * STATE_BLK:(half + 1) * STATE_BLK, cch:cch + 1]
                sn = e_ref[b * nblk + blk] * s_ref[t, r0:r0 + STATE_BLK, :] + vcol * kk[g:g + 1, :]
                sn_ref[t, r0:r0 + STATE_BLK, :] = sn
                prods.append(sn * qq[g:g + 1, :])
            p = jnp.concatenate(prods, axis=0).astype(BF16)
            ysum = lax.dot_general(ones, p, nt, preferred_element_type=F32)
            y_ref[t, cch:cch + 1, :] = ysum[0:1, :]


def _state_step(e_flat, s_all, layer, v3, k3, q3, prev=None, bb=8):
    depth, n = s_all.shape[:2]
    blk3 = lambda a, b_: pl.BlockSpec((bb, a, b_), lambda i: (i, 0, 0))
    sblk = pl.BlockSpec((None, bb, STATE_ROWS, LANES), lambda i: (layer, i, 0, 0))
    in_specs = [pl.BlockSpec(memory_space=pltpu.SMEM), sblk, blk3(SUBLANES, LANES), blk3(4, LANES), blk3(4, LANES)]
    args = [e_flat, s_all, v3, k3, q3]
    aliases = {}
    if prev is not None:
        in_specs.append(pl.BlockSpec(memory_space=pl.ANY))
        args.append(prev)
        aliases = {len(args) - 1: 0}
        oblk = sblk
    else:
        oblk = pl.BlockSpec((depth, bb, STATE_ROWS, LANES), lambda i: (0, i, 0, 0))
    return pl.pallas_call(
        functools.partial(_state_step_kernel, bb=bb, layer=layer, fill=prev is None),
        grid=(n // bb,),
        in_specs=in_specs,
        out_specs=[oblk, blk3(SUBLANES, LANES)],
        out_shape=[jax.ShapeDtypeStruct((depth, n, STATE_ROWS, LANES), F32),
                   jax.ShapeDtypeStruct((n, SUBLANES, LANES), F32)],
        input_output_aliases=aliases,
        compiler_params=_cparams(("parallel",)),
        name="state_step",
    )(*args)


def _sample_post_kernel(ys_ref, xs_ref, z_ref, yr_ref, g_ref, dfull_ref, nw_ref, os_ref, or_ref):
    os_ref[...] = _ssd_gate_norm(ys_ref[...], xs_ref[...], z_ref[...].astype(F32), dfull_ref[...],
                                 nw_ref[...]).astype(os_ref.dtype)
    or_ref[...] = _ret_gate_norm(yr_ref[...], g_ref[...].astype(F32)).astype(or_ref.dtype)


def _sample_post(y_ssd, xs, proj2, y_ret, dfull, nw):
    n = y_ssd.shape[0]
    full = lambda s: pl.BlockSpec(s, lambda i: (0, 0))
    col = lambda off: pl.BlockSpec((n, D_MODEL), lambda i: (0, off // D_MODEL))
    return pl.pallas_call(
        _sample_post_kernel,
        grid=(1,),
        in_specs=[full((n, D_MODEL)), full((n, D_MODEL)), col(OFF_Z), full((n, D_MODEL)), col(OFF_G),
                  full((1, D_MODEL)), full((1, D_MODEL))],
        out_specs=[full((n, D_MODEL)), full((n, D_MODEL))],
        out_shape=[jax.ShapeDtypeStruct((n, D_MODEL), BF16)] * 2,
        compiler_params=_cparams(("arbitrary",)),
        name="sample_post",
    )(y_ssd, xs, proj2, y_ret, proj2, dfull, nw)


def _merge_kernel(s5_ref, ssd_ref, ret_ref, ga_ref, gb_ref, gc_ref, h_ref, ws5_ref, wssd_ref, wret_ref, wout_ref,
                  o_ref):
    glu = _dot(s5_ref[...], ws5_ref[...])
    y_a = glu[:, :D_MODEL] * _sigmoid(glu[:, D_MODEL:])
    y_b = _dot(ssd_ref[...], wssd_ref[...])
    y_c = _dot(ret_ref[...], wret_ref[...])
    merged = (_sigmoid(ga_ref[...].astype(F32)) * y_a + _sigmoid(gb_ref[...].astype(F32)) * y_b
              + _sigmoid(gc_ref[...].astype(F32)) * y_c)
    o_ref[...] = h_ref[...] + _dot(merged.astype(BF16), wout_ref[...])


def _merge(s5_g, ssd_n, ret_n, proj, h3, ws5, wssd, wret, wout, layer, tm):
    nb, L, _ = h3.shape
    row3 = lambda off: pl.BlockSpec((None, tm, D_MODEL), lambda b, i: (b, i, off // D_MODEL))
    wfull = lambda s: pl.BlockSpec((None,) + s, lambda b, i: (layer, 0, 0), pipeline_mode=pl.Buffered(1))
    return pl.pallas_call(
        _merge_kernel,
        grid=(nb, L // tm),
        in_specs=[row3(0), row3(0), row3(0),
                  row3(OFF_GATES), row3(OFF_GATES + D_MODEL), row3(OFF_GATES + 2 * D_MODEL), row3(0),
                  wfull((D_MODEL, 2 * D_MODEL)), wfull((D_MODEL, D_MODEL)), wfull((D_MODEL, D_MODEL)),
                  wfull((D_MODEL, D_MODEL))],
        out_specs=row3(0),
        out_shape=jax.ShapeDtypeStruct((nb, L, D_MODEL), F32),
        compiler_params=_cparams(("parallel", "parallel")),
        name="merge",
    )(s5_g, ssd_n, ret_n, proj, proj, proj, h3, ws5, wssd, wret, wout)


def _mlp_kernel(x_ref, nw_ref, wup_ref, wdn_ref, fw_ref, o_ref, xn_ref, acc_ref, *, final_norm):
    k = pl.program_id(1)
    nk = pl.num_programs(1)

    @pl.when(k == 0)
    def _():
        x = x_ref[...]
        xn_ref[...] = _rmsnorm(x, nw_ref[...]).astype(BF16)
        acc_ref[...] = x

    hid = jnp.maximum(_dot(xn_ref[...], wup_ref[...]), 0.0)
    acc_ref[...] += _dot((hid * hid).astype(BF16), wdn_ref[...])

    @pl.when(k == nk - 1)
    def _():
        h = acc_ref[...]
        o_ref[...] = _rmsnorm(h, fw_ref[...]) if final_norm else h


def _mlp(h2, nw, wup, wdn, fw, layer, tm, tf, final_norm):
    rows = h2.shape[0]
    return pl.pallas_call(
        functools.partial(_mlp_kernel, final_norm=final_norm),
        grid=(rows // tm, D_FF // tf),
        in_specs=[pl.BlockSpec((tm, D_MODEL), lambda i, k: (i, 0)),
                  pl.BlockSpec((1, D_MODEL), lambda i, k: (0, 0)),
                  pl.BlockSpec((None, D_MODEL, tf), lambda i, k: (layer, 0, k)),
                  pl.BlockSpec((None, tf, D_MODEL), lambda i, k: (layer, k, 0)),
                  pl.BlockSpec((1, D_MODEL), lambda i, k: (0, 0))],
        out_specs=pl.BlockSpec((tm, D_MODEL), lambda i, k: (i, 0)),
        out_shape=jax.ShapeDtypeStruct((rows, D_MODEL), F32),
        scratch_shapes=[pltpu.VMEM((tm, D_MODEL), BF16), pltpu.VMEM((tm, D_MODEL), F32)],
        compiler_params=_cparams(("parallel", "arbitrary")),
        name="mlp",
    )(h2, nw, wup, wdn, fw)


def _prep_layer(p):
    out = {}
    row = lambda a: a.reshape(1, -1).astype(F32)
    padrow = lambda a: jnp.pad(a.astype(F32), (0, LANES - a.shape[0])).reshape(1, LANES)
    rep = lambda a: jnp.repeat(a.astype(F32), SSD_HEAD_DIM).reshape(1, D_MODEL)
    out['norm1_w'] = row(p['norm1_w'])
    out['norm2_w'] = row(p['norm2_w'])
    out['conv_w'] = p['ssd_conv_w'].astype(F32)
    out['conv_b'] = row(p['ssd_conv_b'])
    out['b16'] = padrow(p['ssd_dt_bias'])
    out['al16'] = padrow(p['ssd_a_log'])
    out['dfull'] = rep(p['ssd_d'])
    out['ssd_nw'] = row(p['ssd_norm_w'])
    out['s5_d'] = row(p['s5_d'])
    abr, abi, bbr, bbi = _s5_discretise(p['s5_a_re'].astype(F32), p['s5_a_im'].astype(F32),
                                        p['s5_log_dt'].astype(F32), p['s5_b_re'].astype(F32),
                                        p['s5_b_im'].astype(F32))
    out['s5'] = _s5_weights(abr, abi, bbr, bbi, p['s5_c_re'].astype(F32), p['s5_c_im'].astype(F32))
    return out


def _head_expand():
    e = (jnp.arange(LANES)[:, None] == (jnp.arange(D_MODEL)[None, :] // SSD_HEAD_DIM))
    return e.astype(BF16)


def _tile(n, pref):
    return pref if n % pref == 0 else n


def _prompt_trunk(x, layers, final_w):
    nb, L, _ = x.shape
    eh = _head_expand()
    _, lm, gp, we, gq = _ret_consts(CHUNK)
    cos2, sin2 = _rope_tables(jnp.arange(L, dtype=jnp.int32))
    zeros_state = jnp.zeros((nb, S5_NSTATE), F32)
    steps = 64
    h = x
    outs = {k: [] for k in ('re', 'im', 'ssd', 'conv', 'ret')}
    for li, lp in enumerate(layers):
        tm = _tile(L, 1024)
        proj, dt = _proj_main(h, lp['norm1_w'], *lp['proj_w'], tm)
        wbr, wbi, abr, abi, wcr, wci = lp['s5']
        y_s5, hr, hi = _s5_scan(proj, wbr, wbi, abr, abi, wcr, wci, lp['s5_d'],
                                zeros_state, zeros_state, steps)
        ssd_n, ssd_s, ret_n, ret_s = _mix_prompt(proj, dt, lp['conv_w'], lp['conv_b'], lp['b16'], lp['al16'],
                                                 lp['dfull'], lp['ssd_nw'], eh, cos2, sin2, lm, gp, we, gq)
        h = _merge(y_s5, ssd_n, ret_n, proj, h,
                   *lp['merge_w'], li, _tile(L, 512))
        last = li == len(layers) - 1
        h = _mlp(h.reshape(nb * L, D_MODEL), lp['norm2_w'], *lp['mlp_w'], final_w, li,
                 _tile(nb * L, 1024), 1024, last).reshape(nb, L, D_MODEL)
        outs['re'].append(hr.reshape(nb, S5_GROUPS, S5_STATE))
        outs['im'].append(hi.reshape(nb, S5_GROUPS, S5_STATE))
        outs['ssd'].append(ssd_s.reshape(nb, SSD_HEADS, SSD_HEAD_DIM, SSD_STATE))
        outs['conv'].append(proj[:, L - (SSD_CONV - 1):, OFF_XS:OFF_XS + SSD_CONV_DIM].astype(F32))
        outs['ret'].append(ret_s.reshape(nb, RET_HEADS, RET_V_DIM, RET_QK_DIM))
    return (h,) + tuple(jnp.stack(outs[k]) for k in ('re', 'im', 'ssd', 'conv', 'ret'))


def _sample_trunk(x, states, layers, final_w):
    n = x.shape[0]
    st_re, st_im, st_ssd, st_conv, st_ret = states
    eh = _head_expand()
    log_gamma, _, _, _, _ = _ret_consts(1)
    cos2, sin2 = _rope_tables(PAST_LEN + jnp.arange(1, dtype=jnp.int32))
    e_ret = jnp.broadcast_to(jnp.repeat(jnp.exp(log_gamma), RET_V_DIM // STATE_BLK)[None, :],
                             (n, STATE_ROWS // STATE_BLK)).reshape(-1)
    h = x.reshape(1, n, D_MODEL)
    depth = len(layers)
    ssd_all = st_ssd.reshape(depth, n, STATE_ROWS, SSD_STATE).astype(F32)
    ret_all = st_ret.reshape(depth, n, STATE_ROWS, RET_QK_DIM).astype(F32)
    ssd_new = ret_new = None
    outs = {k: [] for k in ('re', 'im', 'conv')}
    for li, lp in enumerate(layers):
        proj, dt = _proj_main(h, lp['norm1_w'], *lp['proj_w'], n)
        wbr, wbi, abr, abi, wcr, wci = lp['s5']
        y_s5, hr, hi = _s5_scan(proj, wbr, wbi, abr, abi, wcr, wci, lp['s5_d'],
                                st_re[li].reshape(n, S5_NSTATE).astype(F32),
                                st_im[li].reshape(n, S5_NSTATE).astype(F32), 1)
        proj2, dt2 = proj.reshape(n, N_MAIN), dt.reshape(n, LANES)
        conv2 = st_conv[li].reshape(n, (SSD_CONV - 1) * SSD_CONV_DIM).astype(F32)
        vv, e16, bs, cs, xs, conv_new, qr, kr = _sample_pre(
            proj2, dt2, conv2, lp['conv_w'], lp['conv_b'], lp['b16'], lp['al16'], eh, cos2, sin2)
        ssd_new, y_ssd = _state_step(
            e16[:, :SSD_HEADS].reshape(-1), ssd_all, li,
            vv.reshape(n, SUBLANES, LANES), bs.reshape(n, SSD_GROUPS, SSD_STATE),
            cs.reshape(n, SSD_GROUPS, SSD_STATE), prev=ssd_new)
        vret = proj2[:, OFF_V:OFF_V + D_MODEL].astype(F32)
        ret_new, y_ret = _state_step(
            e_ret, ret_all, li,
            vret.reshape(n, SUBLANES, LANES), kr.reshape(n, RET_HEADS, RET_QK_DIM),
            qr.reshape(n, RET_HEADS, RET_QK_DIM), prev=ret_new)
        ssd_n, ret_n = _sample_post(y_ssd.reshape(n, D_MODEL), xs, proj2, y_ret.reshape(n, D_MODEL),
                                    lp['dfull'], lp['ssd_nw'])
        h = _merge(y_s5, ssd_n.reshape(1, n, D_MODEL), ret_n.reshape(1, n, D_MODEL), proj, h,
                   *lp['merge_w'], li, n)
        last = li == len(layers) - 1
        h = _mlp(h.reshape(n, D_MODEL), lp['norm2_w'], *lp['mlp_w'], final_w, li,
                 n, 1024, last).reshape(1, n, D_MODEL)
        outs['re'].append(hr.reshape(n, S5_GROUPS, S5_STATE))
        outs['im'].append(hi.reshape(n, S5_GROUPS, S5_STATE))
        outs['conv'].append(conv_new.reshape(n, SSD_CONV - 1, SSD_CONV_DIM))
    return (h.reshape(n, 1, D_MODEL), jnp.stack(outs['re']), jnp.stack(outs['im']),
            ssd_new.reshape(depth, n, SSD_HEADS, SSD_HEAD_DIM, SSD_STATE), jnp.stack(outs['conv']),
            ret_new.reshape(depth, n, RET_HEADS, RET_V_DIM, RET_QK_DIM))


def kernel(x_prompt, x_sample, state_s5_re, state_s5_im, state_ssd, state_conv, state_ret, norm1_w, w_in, s5_a_re, s5_a_im, s5_log_dt, s5_b_re, s5_b_im, s5_c_re, s5_c_im, s5_d, w_s5_out, ssd_conv_w, ssd_conv_b, ssd_dt_bias, ssd_a_log, ssd_d, ssd_norm_w, w_ssd_out, w_ret_out, w_out, norm2_w, w_mlp_up, w_mlp_down, final_norm_w):
    params = dict(norm1_w=norm1_w, w_in=w_in, s5_a_re=s5_a_re, s5_a_im=s5_a_im, s5_log_dt=s5_log_dt,
                  s5_b_re=s5_b_re, s5_b_im=s5_b_im, s5_c_re=s5_c_re, s5_c_im=s5_c_im, s5_d=s5_d,
                  w_s5_out=w_s5_out, ssd_conv_w=ssd_conv_w, ssd_conv_b=ssd_conv_b, ssd_dt_bias=ssd_dt_bias,
                  ssd_a_log=ssd_a_log, ssd_d=ssd_d, ssd_norm_w=ssd_norm_w, w_ssd_out=w_ssd_out,
                  w_ret_out=w_ret_out, w_out=w_out, norm2_w=norm2_w, w_mlp_up=w_mlp_up, w_mlp_down=w_mlp_down)
    depth = w_in.shape[0]
    big = ('w_in', 'w_s5_out', 'w_ssd_out', 'w_ret_out', 'w_out', 'w_mlp_up', 'w_mlp_down')
    layers = [_prep_layer({name: arr[l] for name, arr in params.items() if name not in big}) for l in range(depth)]
    merge_w = tuple(params[name].astype(BF16) for name in ('w_s5_out', 'w_ssd_out', 'w_ret_out', 'w_out'))
    mlp_w = tuple(params[name].astype(BF16) for name in ('w_mlp_up', 'w_mlp_down'))
    w_a, w_b, w_dt = _proj_weight_slabs(w_in.astype(F32))
    for l, lp in enumerate(layers):
        lp['proj_w'] = (w_a, w_b, w_dt, l)
        lp['merge_w'] = merge_w
        lp['mlp_w'] = mlp_w
    final_w = final_norm_w.reshape(1, D_MODEL).astype(F32)
    y_p, re_p, im_p, ssd_p, conv_p, ret_p = _prompt_trunk(x_prompt.astype(F32), layers, final_w)
    y_s, re_s, im_s, ssd_s, conv_s, ret_s = _sample_trunk(
        x_sample.astype(F32), (state_s5_re, state_s5_im, state_ssd, state_conv, state_ret), layers, final_w)
    return (y_p, y_s, re_p, im_p, ssd_p, conv_p, ret_p, re_s, im_s, ssd_s, conv_s, ret_s)
```

```python
import functools
import math

import jax
import jax.numpy as jnp
from jax import lax
from jax.experimental import pallas as pl
from jax.experimental.pallas import tpu as pltpu

F32 = jnp.float32
BF16 = jnp.bfloat16

D_MODEL = 1024
S5_GROUP = 16
S5_STATE = 64
S5_GROUPS = D_MODEL // S5_GROUP
S5_NSTATE = S5_GROUPS * S5_STATE
S5_SUPER = 8
S5_TIME_SPLIT = 2
SSD_HEADS = 16
SSD_HEAD_DIM = 64
SSD_GROUPS = 4
SSD_STATE = 128
SSD_CONV = 4
SSD_CONV_DIM = D_MODEL + 2 * SSD_GROUPS * SSD_STATE
RET_HEADS = 4
RET_QK_DIM = 128
RET_V_DIM = 256
ROPE_BASE = 10000.0
D_FF = 4 * D_MODEL
CHUNK = 128
EPS = 1e-6
PAST_LEN = 16384
LANES = 128
SUBLANES = 8
VMEM_LIMIT = 52 * 1024 * 1024

OFF_U, OFF_Z, OFF_XS, OFF_BC, OFF_Q, OFF_K, OFF_V, OFF_G, OFF_GATES = (
    0, 1024, 2048, 3072, 4096, 4608, 5120, 6144, 7168)
N_MAIN = 10240
SLAB_A_COLS = OFF_Q
SLAB_B_START = 2 * D_MODEL + SSD_CONV_DIM + SSD_HEADS
PROJ_TN = 2048


def _cparams(sem):
    return pltpu.CompilerParams(dimension_semantics=sem, vmem_limit_bytes=VMEM_LIMIT)


def _sigmoid(x):
    return 1.0 / (1.0 + jnp.exp(-x))


def _silu(x):
    return x * _sigmoid(x)


def _softplus(x):
    return jnp.maximum(x, 0.0) + jnp.log1p(jnp.exp(-jnp.abs(x)))


def _gelu_tanh(x):
    return 0.5 * x * (1.0 + jnp.tanh(math.sqrt(2.0 / math.pi) * (x + 0.044715 * (x * x * x))))


def _rmsnorm(x, w):
    return x * lax.rsqrt(jnp.mean(x * x, axis=-1, keepdims=True) + EPS) * w


def _cumsum_rows(x):
    n = x.shape[0]
    row = lax.broadcasted_iota(jnp.int32, x.shape, 0)
    k = 1
    while k < n:
        x = x + jnp.where(row >= k, pltpu.roll(x, k, 0), 0.0)
        k *= 2
    return x


def _group_rms(y, width):
    outs = []
    for s in range(0, y.shape[-1], width):
        yg = y[:, s:s + width]
        outs.append(yg * lax.rsqrt(jnp.mean(yg * yg, axis=-1, keepdims=True) + EPS))
    return jnp.concatenate(outs, axis=-1)


def _dot(a, b):
    return jnp.dot(a, b, preferred_element_type=F32)


def _split_dot(x, e):
    hi = x.astype(BF16)
    lo = (x - hi.astype(F32)).astype(BF16)
    return _dot(hi, e) + _dot(lo, e)


def _proj_kernel(x_ref, nw_ref, wa_ref, wb_ref, wdt_ref, o_ref, odt_ref, xn_ref, *, na):
    j = pl.program_id(2)

    @pl.when(j == 0)
    def _():
        xn = _rmsnorm(x_ref[...], nw_ref[...]).astype(BF16)
        xn_ref[...] = xn
        odt_ref[...] = _dot(xn, wdt_ref[...])

    @pl.when(j < na)
    def _():
        o_ref[...] = _dot(xn_ref[...], wa_ref[...]).astype(o_ref.dtype)

    @pl.when(j >= na)
    def _():
        o_ref[...] = _dot(xn_ref[...], wb_ref[...]).astype(o_ref.dtype)


PREP_TN = 1024
DT_START = SLAB_B_START - SSD_HEADS
SLAB_SHIFT = SLAB_B_START % PREP_TN


def _slab_b_kernel(lo_ref, hi_ref, o_ref):
    keep = PREP_TN - SLAB_SHIFT
    lane = lax.broadcasted_iota(jnp.int32, (D_MODEL, PREP_TN), 1)
    lo = pltpu.roll(lo_ref[...].astype(F32), keep, 1)
    hi = pltpu.roll(hi_ref[...].astype(F32), keep, 1)
    o_ref[...] = jnp.where(lane < keep, lo, hi).astype(BF16)


def _proj_weight_slabs(w_in):
    depth = w_in.shape[0]
    tn = PREP_TN
    w_all = w_in.astype(BF16)
    w_dt = jnp.pad(w_in[:, :, DT_START:SLAB_B_START], ((0, 0), (0, 0), (0, LANES - SSD_HEADS))).astype(BF16)
    blk = lambda f: pl.BlockSpec((None, D_MODEL, tn), f)
    nbcols = N_MAIN - SLAB_A_COLS
    first = SLAB_B_START // tn
    w_b = pl.pallas_call(
        _slab_b_kernel,
        grid=(depth, nbcols // tn),
        in_specs=[blk(lambda l, j: (l, 0, first + j)), blk(lambda l, j: (l, 0, first + j + 1))],
        out_specs=blk(lambda l, j: (l, 0, j)),
        out_shape=jax.ShapeDtypeStruct((depth, D_MODEL, nbcols), BF16),
        compiler_params=_cparams(("parallel", "parallel")),
        name="weight_slab_b",
    )(w_all, w_all)
    return w_all, w_b, w_dt


def _proj_main(x3, nw, w_a, w_b, wdt, layer, tm):
    nb, L, _ = x3.shape
    tn = PROJ_TN
    na = SLAB_A_COLS // tn
    return pl.pallas_call(
        functools.partial(_proj_kernel, na=na),
        grid=(nb, L // tm, N_MAIN // tn),
        in_specs=[pl.BlockSpec((None, tm, D_MODEL), lambda b, i, j: (b, i, 0)),
                  pl.BlockSpec((1, D_MODEL), lambda b, i, j: (0, 0)),
                  pl.BlockSpec((None, D_MODEL, tn), lambda b, i, j: (layer, 0, jnp.minimum(j, na - 1))),
                  pl.BlockSpec((None, D_MODEL, tn), lambda b, i, j: (layer, 0, jnp.maximum(j - na, 0))),
                  pl.BlockSpec((None, D_MODEL, LANES), lambda b, i, j: (layer, 0, 0))],
        out_specs=[pl.BlockSpec((None, tm, tn), lambda b, i, j: (b, i, j)),
                   pl.BlockSpec((None, tm, LANES), lambda b, i, j: (b, i, 0))],
        out_shape=[jax.ShapeDtypeStruct((nb, L, N_MAIN), BF16), jax.ShapeDtypeStruct((nb, L, LANES), F32)],
        scratch_shapes=[pltpu.VMEM((tm, D_MODEL), BF16)],
        compiler_params=_cparams(("parallel", "parallel", "arbitrary")),
        name="proj_main",
    )(x3, nw, w_a, w_b, wdt)


def _s5_disc_kernel(are_ref, aim_ref, ldt_ref, bre_ref, bim_ref, abr_ref, abi_ref, bbr_ref, bbi_ref):
    ar, ai = are_ref[...], aim_ref[...]
    dt = jnp.exp(ldt_ref[...])
    mag = jnp.exp(ar * dt)
    abr = mag * jnp.cos(ai * dt)
    abi = mag * jnp.sin(ai * dt)
    abr_ref[...] = abr
    abi_ref[...] = abi
    nr, ni = abr - 1.0, abi
    den = 1.0 / (ar * ar + ai * ai)
    cr = (nr * ar + ni * ai) * den
    ci = (ni * ar - nr * ai) * den
    br, bi = bre_ref[...], bim_ref[...]
    bbr_ref[...] = cr * br - ci * bi
    bbi_ref[...] = cr * bi + ci * br


def _s5_discretise(a_re, a_im, log_dt, b_re, b_im):
    g, p = a_re.shape
    gs = b_re.shape[-1]
    full3 = lambda s: pl.BlockSpec(s, lambda: (0, 0, 0))
    return pl.pallas_call(
        _s5_disc_kernel,
        in_specs=[full3((g, 1, p)), full3((g, 1, p)), full3((g, 1, 1)), full3((g, gs, p)), full3((g, gs, p))],
        out_specs=[full3((g, 1, p)), full3((g, 1, p)), full3((g, gs, p)), full3((g, gs, p))],
        out_shape=[jax.ShapeDtypeStruct((g, 1, p), F32)] * 2 + [jax.ShapeDtypeStruct((g, gs, p), F32)] * 2,
        name="s5_discretise",
    )(a_re.reshape(g, 1, p), a_im.reshape(g, 1, p), log_dt.reshape(g, 1, 1),
      jnp.swapaxes(b_re, 1, 2), jnp.swapaxes(b_im, 1, 2))


def _s5_kernel(u_ref, wbr_ref, wbi_ref, ar_ref, ai_ref, wcr_ref, wci_ref, d_ref, h0r_ref, h0i_ref,
               y_ref, hr_ref, hi_ref, bur, bui, *, nbatch, steps, lane_w):
    @pl.when(pl.program_id(0) == 0)
    def _():
        hr_ref[...] = h0r_ref[...]
        hi_ref[...] = h0i_ref[...]

    cw = S5_SUPER * S5_GROUP
    sw = S5_SUPER * S5_STATE
    batch_major = steps > 1
    u = u_ref[...].astype(F32)
    if batch_major:
        u = jnp.swapaxes(u, 0, 1).reshape(steps * nbatch, D_MODEL)
    ub = u.astype(BF16)
    nsub = S5_TIME_SPLIT if steps % (2 * S5_TIME_SPLIT) == 0 else 1
    sub_steps = steps // nsub
    nsup = S5_GROUPS // S5_SUPER
    for sb in range(nsub):
        t0 = sb * sub_steps
        rs = slice(t0 * nbatch, (t0 + sub_steps) * nbatch)
        for j in range(nsup):
            uj = ub[rs, j * cw:(j + 1) * cw]
            bur[rs, j * sw:(j + 1) * sw] = _dot(uj, wbr_ref[j])
            bui[rs, j * sw:(j + 1) * sw] = _dot(uj, wbi_ref[j])

        for j in range(nsup):
            for lc in range(sw // lane_w):
                sl = slice(j * sw + lc * lane_w, j * sw + (lc + 1) * lane_w)
                are = jnp.broadcast_to(ar_ref[:, sl], (nbatch, lane_w))
                aim = jnp.broadcast_to(ai_ref[:, sl], (nbatch, lane_w))
                hr, hi = hr_ref[:, sl], hi_ref[:, sl]
                for t in range(t0, t0 + sub_steps):
                    rows = slice(t * nbatch, (t + 1) * nbatch)
                    hr, hi = are * hr - aim * hi + bur[rows, sl], are * hi + aim * hr + bui[rows, sl]
                    bur[rows, sl] = hr
                    bui[rows, sl] = hi
                hr_ref[:, sl] = hr
                hi_ref[:, sl] = hi

        for j in range(nsup):
            ch = slice(j * cw, (j + 1) * cw)
            yj = (_dot(bur[rs, j * sw:(j + 1) * sw].astype(BF16), wcr_ref[j])
                  + _dot(bui[rs, j * sw:(j + 1) * sw].astype(BF16), wci_ref[j]))
            yj = _gelu_tanh(yj + d_ref[:, ch] * u[rs, ch])
            if batch_major:
                y_ref[:, t0:t0 + sub_steps, ch] = jnp.swapaxes(
                    yj.reshape(sub_steps, nbatch, cw), 0, 1).astype(y_ref.dtype)
            else:
                y_ref[:, ch] = yj.astype(y_ref.dtype)


def _s5_scan(proj, wbr, wbi, abr, abi, wcr, wci, d, h0r, h0i, steps):
    nb, L, _ = proj.shape
    nsup = S5_GROUPS // S5_SUPER
    cw, sw = S5_SUPER * S5_GROUP, S5_SUPER * S5_STATE
    c2 = lambda s: pl.BlockSpec(s, lambda c: (0, 0))
    c3 = lambda s: pl.BlockSpec(s, lambda c: (0, 0, 0))
    if steps > 1:
        nbatch, grid = nb, L // steps
        u_spec = pl.BlockSpec((nb, steps, D_MODEL), lambda c: (0, c, OFF_U // D_MODEL))
        y_spec = pl.BlockSpec((nb, steps, D_MODEL), lambda c: (0, c, 0))
    else:
        nbatch, grid = L, 1
        u_spec = pl.BlockSpec((None, L, D_MODEL), lambda c: (0, 0, OFF_U // D_MODEL))
        y_spec = pl.BlockSpec((None, L, D_MODEL), lambda c: (0, 0, 0))
    blk = nbatch * steps
    lane_w = max(LANES, S5_NSTATE // nbatch)
    kern = functools.partial(_s5_kernel, nbatch=nbatch, steps=steps, lane_w=lane_w)
    return pl.pallas_call(
        kern,
        grid=(grid,),
        in_specs=[u_spec,
                  c3((nsup, cw, sw)), c3((nsup, cw, sw)),
                  c2((1, S5_NSTATE)), c2((1, S5_NSTATE)),
                  c3((nsup, sw, cw)), c3((nsup, sw, cw)),
                  c2((1, D_MODEL)),
                  c2((nbatch, S5_NSTATE)), c2((nbatch, S5_NSTATE))],
        out_specs=[y_spec, c2((nbatch, S5_NSTATE)), c2((nbatch, S5_NSTATE))],
        out_shape=[jax.ShapeDtypeStruct((nb, L, D_MODEL), BF16),
                   jax.ShapeDtypeStruct((nbatch, S5_NSTATE), F32),
                   jax.ShapeDtypeStruct((nbatch, S5_NSTATE), F32)],
        scratch_shapes=[pltpu.VMEM((blk, S5_NSTATE), F32), pltpu.VMEM((blk, S5_NSTATE), F32)],
        compiler_params=_cparams(("arbitrary",)),
        name="s5_scan",
    )(proj, wbr, wbi, abr, abi, wcr, wci, d, h0r, h0i)


def _s5_weights(abr, abi, bbr, bbi, c_re, c_im):
    nsup = S5_GROUPS // S5_SUPER
    eye = jnp.eye(S5_SUPER, dtype=F32)

    def bexp(b):
        b = b.reshape(nsup, S5_SUPER, S5_GROUP, S5_STATE)
        return (b[:, :, :, None, :] * eye[None, :, None, :, None]).reshape(
            nsup, S5_SUPER * S5_GROUP, S5_SUPER * S5_STATE).astype(BF16)

    def cexp(c):
        c = jnp.swapaxes(c, 1, 2).reshape(nsup, S5_SUPER, S5_STATE, S5_GROUP)
        return (c[:, :, :, None, :] * eye[None, :, None, :, None]).reshape(
            nsup, S5_SUPER * S5_STATE, S5_SUPER * S5_GROUP).astype(BF16)

    return (bexp(bbr), bexp(bbi), abr.reshape(1, S5_NSTATE), abi.reshape(1, S5_NSTATE),
            cexp(c_re), cexp(-c_im))


def _ssd_gate_norm(y, xs, z, dfull, nw):
    y = (y + dfull * xs) * _silu(z)
    return _group_rms(y, D_MODEL // SSD_GROUPS) * nw


CONV_PAD = 16
CONV_ROWS = 2 * CHUNK


def _conv_shift_mats():
    i = jnp.arange(CHUNK)[:, None]
    j = jnp.arange(CONV_ROWS)[None, :]
    return jnp.stack([(j == CONV_PAD + i - s) for s in range(1, SSD_CONV)]).astype(BF16)


def _ssd_chunk_seq(xs_ref, bc_ref, z_ref, dt_ref, cw_ref, cb_ref, b16_ref, al16_ref,
                   dfull_ref, nw_ref, eh_ref, sh_ref, o_ref, xpad, st):
    q = CHUNK
    pad = CONV_PAD
    xpad[0:pad, :] = xpad[q:q + pad, :]

    xpad[pad:pad + q, :D_MODEL] = xs_ref[...]
    xpad[pad:pad + q, D_MODEL:] = bc_ref[...]
    staged = xpad[...]
    acc = cb_ref[...] + xpad[pad:pad + q, :].astype(F32) * cw_ref[SSD_CONV - 1:SSD_CONV, :]
    for s in range(1, SSD_CONV):
        acc = acc + _dot(sh_ref[s - 1], staged) * cw_ref[SSD_CONV - 1 - s:SSD_CONV - s, :]
    xc = _silu(acc)
    xs = xc[:, :D_MODEL]
    nbc = SSD_GROUPS * SSD_STATE
    bs = xc[:, D_MODEL:D_MODEL + nbc]
    cs = xc[:, D_MODEL + nbc:]

    delta16 = _softplus(dt_ref[...] + b16_ref[...])
    acum16 = _cumsum_rows(delta16 * (-jnp.exp(al16_ref[...])))
    acum_t = acum16.T
    delta_t = delta16.T
    last16 = acum16[q - 1:q, :]
    ea_f = _split_dot(jnp.exp(acum16), eh_ref[...])
    dw_f = _split_dot(delta16 * jnp.exp(last16 - acum16), eh_ref[...])
    xw = (xs * dw_f).astype(BF16)

    ri = lax.broadcasted_iota(jnp.int32, (q, q), 0)
    ci = lax.broadcasted_iota(jnp.int32, (q, q), 1)
    causal = ri >= ci
    hw = D_MODEL // SSD_GROUPS
    hpg = SSD_HEADS // SSD_GROUPS
    xsb = xs.astype(BF16)
    ys = []
    for g in range(SSD_GROUPS):
        bg = bs[:, g * SSD_STATE:(g + 1) * SSD_STATE]
        cg = cs[:, g * SSD_STATE:(g + 1) * SSD_STATE].astype(BF16)
        bgt = bg.T.astype(BF16)
        scores = _dot(cg, bgt)
        sg = st[:, g * hw:(g + 1) * hw]
        y_inter = _dot(cg, sg.astype(BF16)) * ea_f[:, g * hw:(g + 1) * hw]
        for r in range(hpg):
            hd = g * hpg + r
            seg = acum16[:, hd:hd + 1] - acum_t[hd:hd + 1, :]
            lm = jnp.exp(jnp.where(causal, seg, -jnp.inf)) * delta_t[hd:hd + 1, :]
            p = (scores * lm).astype(BF16)
            ys.append(_dot(p, xsb[:, hd * SSD_HEAD_DIM:(hd + 1) * SSD_HEAD_DIM])
                      + y_inter[:, r * SSD_HEAD_DIM:(r + 1) * SSD_HEAD_DIM])
        st[:, g * hw:(g + 1) * hw] = (ea_f[q - 1:q, g * hw:(g + 1) * hw] * sg
                                      + _dot(bgt, xw[:, g * hw:(g + 1) * hw]))
    y = jnp.concatenate(ys, axis=-1)
    o_ref[...] = _ssd_gate_norm(y, xs, z_ref[...].astype(F32), dfull_ref[...], nw_ref[...]).astype(o_ref.dtype)


def _rotary(x, cos2, sin2):
    outs = []
    for h in range(RET_HEADS):
        xh = x[:, h * RET_QK_DIM:(h + 1) * RET_QK_DIM]
        outs.append(xh * cos2 + pltpu.roll(xh, RET_QK_DIM // 2, 1) * sin2)
    return outs


def _ret_gate_norm(y, g):
    return _group_rms(y, RET_V_DIM) * _silu(g)


def _ret_chunk_seq(q_ref, k_ref, v_ref, g_ref, cos2, sin2, lm_ref, gp_ref, we_ref, gq_ref, o_ref, st):
    qr = _rotary(q_ref[...].astype(F32), cos2, sin2)
    kr = _rotary(k_ref[...].astype(F32), cos2, sin2)
    vb = v_ref[...]
    vw = (vb.astype(F32) * we_ref[...]).astype(BF16)
    ys = []
    for h in range(RET_HEADS):
        sl = slice(h * RET_V_DIM, (h + 1) * RET_V_DIM)
        qh = qr[h].astype(BF16)
        kht = (kr[h] * (RET_QK_DIM ** -0.5)).T.astype(BF16)
        p = (_dot(qh, kht) * lm_ref[h]).astype(BF16)
        sh = st[:, sl]
        ys.append(_dot(p, vb[:, sl]) + _dot(qh, sh.astype(BF16)) * gp_ref[:, sl])
        st[:, sl] = gq_ref[:, sl] * sh + _dot(kht, vw[:, sl])
    y = jnp.concatenate(ys, axis=-1)
    o_ref[...] = _ret_gate_norm(y, g_ref[...].astype(F32)).astype(o_ref.dtype)


def _mix_chunk_kernel(xs_ref, bc_ref, z_ref, dt_ref, cw_ref, cb_ref, b16_ref, al16_ref, dfull_ref, nw_ref, eh_ref,
                      sh_ref, q_ref, k_ref, v_ref, g_ref, cos_ref, sin_ref, lm_ref, gp_ref, we_ref, gq_ref,
                      so_ref, ss_ref, ro_ref, rs_ref, xpad, st_ssd, st_ret):
    c = pl.program_id(1)
    nc = pl.num_programs(1)
    nseq = xs_ref.shape[0]

    @pl.when(c == 0)
    def _():
        xpad[...] = jnp.zeros(xpad.shape, BF16)
        st_ssd[...] = jnp.zeros(st_ssd.shape, F32)
        st_ret[...] = jnp.zeros(st_ret.shape, F32)

    cos2, sin2 = cos_ref[...], sin_ref[...]
    for bi in range(nseq):
        _ssd_chunk_seq(xs_ref.at[bi], bc_ref.at[bi], z_ref.at[bi], dt_ref.at[bi], cw_ref, cb_ref, b16_ref,
                       al16_ref, dfull_ref, nw_ref, eh_ref, sh_ref, so_ref.at[bi], xpad.at[bi], st_ssd.at[bi])
        _ret_chunk_seq(q_ref.at[bi], k_ref.at[bi], v_ref.at[bi], g_ref.at[bi], cos2, sin2, lm_ref, gp_ref, we_ref,
                       gq_ref, ro_ref.at[bi], st_ret.at[bi])

    @pl.when(c == nc - 1)
    def _():
        for bi in range(nseq):
            for k in range(D_MODEL // LANES):
                ss_ref[bi, k * LANES:(k + 1) * LANES, :] = st_ssd[bi, :, k * LANES:(k + 1) * LANES].T
                rs_ref[bi, k * LANES:(k + 1) * LANES, :] = st_ret[bi, :, k * LANES:(k + 1) * LANES].T


def _ret_consts(q):
    hidx = jnp.arange(RET_HEADS, dtype=F32)
    log_gamma = jnp.log1p(-jnp.exp2(-5.0 - hidx))
    i = jnp.arange(q, dtype=F32)
    seg = (i[:, None] - i[None, :])[None] * log_gamma[:, None, None]
    lm = jnp.exp(jnp.where((i[:, None] >= i[None, :])[None], seg, -jnp.inf))
    rep = lambda t: jnp.repeat(t, RET_V_DIM, axis=-1)
    gp = rep(jnp.exp((i[:, None] + 1.0) * log_gamma[None, :]))
    we = rep(jnp.exp((q - 1.0 - i[:, None]) * log_gamma[None, :]))
    gq = rep(jnp.exp(q * log_gamma)[None, :])
    return log_gamma, lm, gp, we, gq


def _rope_tables(pos):
    half = RET_QK_DIM // 2
    inv = ROPE_BASE ** (-jnp.arange(half, dtype=F32) / half)
    ang = pos.astype(F32)[:, None] * inv[None]
    cos, sin = jnp.cos(ang), jnp.sin(ang)
    return jnp.concatenate([cos, cos], axis=-1), jnp.concatenate([-sin, sin], axis=-1)


def _mix_prompt(proj, dt, cw, cb, b16, al16, dfull, nw, eh, cos2, sin2, lm, gp, we, gq, nseq=2):
    nb, L, _ = proj.shape
    q = CHUNK
    qkw = RET_HEADS * RET_QK_DIM
    c2 = lambda s: pl.BlockSpec(s, lambda b, c: (0, 0))
    rows = lambda w, off: pl.BlockSpec((nseq, q, w), lambda b, c: (b, c, off // w))
    state = pl.BlockSpec((nseq, D_MODEL, LANES), lambda b, c: (b, 0, 0))
    return pl.pallas_call(
        _mix_chunk_kernel,
        grid=(nb // nseq, L // q),
        in_specs=[rows(D_MODEL, OFF_XS), rows(D_MODEL, OFF_BC), rows(D_MODEL, OFF_Z), rows(LANES, 0),
                  c2((SSD_CONV, SSD_CONV_DIM)), c2((1, SSD_CONV_DIM)),
                  c2((1, LANES)), c2((1, LANES)),
                  c2((1, D_MODEL)), c2((1, D_MODEL)),
                  c2((LANES, D_MODEL)),
                  pl.BlockSpec((SSD_CONV - 1, q, CONV_ROWS), lambda b, c: (0, 0, 0)),
                  rows(qkw, OFF_Q), rows(qkw, OFF_K), rows(D_MODEL, OFF_V), rows(D_MODEL, OFF_G),
                  pl.BlockSpec((q, RET_QK_DIM), lambda b, c: (c, 0)),
                  pl.BlockSpec((q, RET_QK_DIM), lambda b, c: (c, 0)),
                  pl.BlockSpec((RET_HEADS, q, q), lambda b, c: (0, 0, 0)),
                  c2((q, D_MODEL)), c2((q, D_MODEL)), c2((1, D_MODEL))],
        out_specs=[rows(D_MODEL, 0), state, rows(D_MODEL, 0), state],
        out_shape=[jax.ShapeDtypeStruct((nb, L, D_MODEL), BF16),
                   jax.ShapeDtypeStruct((nb, D_MODEL, SSD_STATE), F32),
                   jax.ShapeDtypeStruct((nb, L, D_MODEL), BF16),
                   jax.ShapeDtypeStruct((nb, D_MODEL, RET_QK_DIM), F32)],
        scratch_shapes=[pltpu.VMEM((nseq, CONV_ROWS, SSD_CONV_DIM), BF16),
                        pltpu.VMEM((nseq, SSD_STATE, D_MODEL), F32),
                        pltpu.VMEM((nseq, RET_QK_DIM, D_MODEL), F32)],
        compiler_params=_cparams(("parallel", "arbitrary")),
        name="mix_chunk",
    )(proj, proj, proj, dt, cw, cb, b16, al16, dfull, nw, eh, _conv_shift_mats(),
      proj, proj, proj, proj, cos2, sin2, lm, gp, we, gq)


def _sample_pre_kernel(xsin_ref, bcin_ref, dt_ref, q_ref, k_ref, conv_ref, cw_ref, cb_ref, b16_ref, al16_ref,
                       eh_ref, cos_ref, sin_ref,
                       vv_ref, e16_ref, bs_ref, cs_ref, xs_ref, convn_ref, qr_ref, kr_ref):
    cd = SSD_CONV_DIM
    x = jnp.concatenate([xsin_ref[...], bcin_ref[...]], axis=-1).astype(F32)
    acc = cb_ref[...] + x * cw_ref[SSD_CONV - 1:SSD_CONV, :]
    for tap in range(SSD_CONV - 1):
        acc = acc + conv_ref[:, tap * cd:(tap + 1) * cd] * cw_ref[tap:tap + 1, :]
    convn_ref[:, 0:(SSD_CONV - 2) * cd] = conv_ref[:, cd:(SSD_CONV - 1) * cd]
    convn_ref[:, (SSD_CONV - 2) * cd:] = x
    xc = _silu(acc)
    xs = xc[:, :D_MODEL]
    nbc = SSD_GROUPS * SSD_STATE
    xs_ref[...] = xs
    bs_ref[...] = xc[:, D_MODEL:D_MODEL + nbc]
    cs_ref[...] = xc[:, D_MODEL + nbc:]
    delta16 = _softplus(dt_ref[...] + b16_ref[...])
    e16_ref[...] = jnp.exp(delta16 * (-jnp.exp(al16_ref[...])))
    vv_ref[...] = xs * _split_dot(delta16, eh_ref[...])
    cos2, sin2 = cos_ref[...], sin_ref[...]
    qr_ref[...] = jnp.concatenate(_rotary(q_ref[...].astype(F32), cos2, sin2), axis=-1)
    kr_ref[...] = jnp.concatenate(_rotary(k_ref[...].astype(F32), cos2, sin2), axis=-1) * (RET_QK_DIM ** -0.5)


def _sample_pre(proj2, dt2, conv2, cw, cb, b16, al16, eh, cos2, sin2):
    n = proj2.shape[0]
    qkw = RET_HEADS * RET_QK_DIM
    nbc = SSD_GROUPS * SSD_STATE
    full = lambda s: pl.BlockSpec(s, lambda i: (0, 0))
    col = lambda w, off: pl.BlockSpec((n, w), lambda i: (0, off // w))
    shp = lambda w: jax.ShapeDtypeStruct((n, w), F32)
    return pl.pallas_call(
        _sample_pre_kernel,
        grid=(1,),
        in_specs=[col(D_MODEL, OFF_XS), col(D_MODEL, OFF_BC), full((n, LANES)), col(qkw, OFF_Q), col(qkw, OFF_K),
                  full((n, (SSD_CONV - 1) * SSD_CONV_DIM)),
                  full((SSD_CONV, SSD_CONV_DIM)), full((1, SSD_CONV_DIM)),
                  full((1, LANES)), full((1, LANES)),
                  full((LANES, D_MODEL)), full((1, RET_QK_DIM)), full((1, RET_QK_DIM))],
        out_specs=[full((n, D_MODEL)), full((n, LANES)), full((n, nbc)), full((n, nbc)), full((n, D_MODEL)),
                   full((n, (SSD_CONV - 1) * SSD_CONV_DIM)), full((n, qkw)), full((n, qkw))],
        out_shape=[shp(D_MODEL), shp(LANES), shp(nbc), shp(nbc), shp(D_MODEL),
                   shp((SSD_CONV - 1) * SSD_CONV_DIM), shp(qkw), shp(qkw)],
        compiler_params=_cparams(("arbitrary",)),
        name="sample_pre",
    )(proj2, proj2, dt2, proj2, proj2, conv2, cw, cb, b16, al16, eh, cos2, sin2)


STATE_ROWS = D_MODEL
STATE_BLK = 64


def _state_step_kernel(e_ref, s_ref, v_ref, k_ref, q_ref, *rest, bb, layer, fill):
    sn_all, y_ref = rest[-2:]
    if fill:
        for l in range(sn_all.shape[0]):
            if l != layer:
                sn_all[l] = jnp.zeros(sn_all.shape[1:], F32)
        sn_ref = sn_all.at[layer]
    else:
        sn_ref = sn_all
    i = pl.program_id(0)
    nblk = STATE_ROWS // STATE_BLK
    gw = STATE_ROWS // 4
    ones = jnp.ones((SUBLANES, LANES), BF16)
    zpad = jnp.zeros((LANES - SUBLANES, LANES), F32)
    nt = (((1,), (1,)), ((), ()))
    for t in range(bb):
        b = i * bb + t
        vt = jnp.concatenate([v_ref[t], zpad], axis=0).T
        kk = k_ref[t]
        qq = q_ref[t]
        for cch in range(STATE_ROWS // LANES):
            prods = []
            for half in range(LANES // STATE_BLK):
                blk = cch * (LANES // STATE_BLK) + half
                r0 = blk * STATE_BLK
                g = r0 // gw
                vcol = vt[half * STATE_BLK:(half + 1) * STATE_BLK, cch:cch + 1]
                sn = e_ref[b * nblk + blk] * s_ref[t, r0:r0 + STATE_BLK, :] + vcol * kk[g:g + 1, :]
                sn_ref[t, r0:r0 + STATE_BLK, :] = sn
                prods.append(sn * qq[g:g + 1, :])
            p = jnp.concatenate(prods, axis=0).astype(BF16)
            ysum = lax.dot_general(ones, p, nt, preferred_element_type=F32)
            y_ref[t, cch:cch + 1, :] = ysum[0:1, :]


def _state_step(e_flat, s_all, layer, v3, k3, q3, prev=None, bb=8):
    depth, n = s_all.shape[:2]
    blk3 = lambda a, b_: pl.BlockSpec((bb, a, b_), lambda i: (i, 0, 0))
    sblk = pl.BlockSpec((None, bb, STATE_ROWS, LANES), lambda i: (layer, i, 0, 0))
    in_specs = [pl.BlockSpec(memory_space=pltpu.SMEM), sblk, blk3(SUBLANES, LANES), blk3(4, LANES), blk3(4, LANES)]
    args = [e_flat, s_all, v3, k3, q3]
    aliases = {}
    if prev is not None:
        in_specs.append(pl.BlockSpec(memory_space=pl.ANY))
        args.append(prev)
        aliases = {len(args) - 1: 0}
        oblk = sblk
    else:
        oblk = pl.BlockSpec((depth, bb, STATE_ROWS, LANES), lambda i: (0, i, 0, 0))
    return pl.pallas_call(
        functools.partial(_state_step_kernel, bb=bb, layer=layer, fill=prev is None),
        grid=(n // bb,),
        in_specs=in_specs,
        out_specs=[oblk, blk3(SUBLANES, LANES)],
        out_shape=[jax.ShapeDtypeStruct((depth, n, STATE_ROWS, LANES), F32),
                   jax.ShapeDtypeStruct((n, SUBLANES, LANES), F32)],
        input_output_aliases=aliases,
        compiler_params=_cparams(("parallel",)),
        name="state_step",
    )(*args)


def _sample_post_kernel(ys_ref, xs_ref, z_ref, yr_ref, g_ref, dfull_ref, nw_ref, os_ref, or_ref):
    os_ref[...] = _ssd_gate_norm(ys_ref[...], xs_ref[...], z_ref[...].astype(F32), dfull_ref[...],
                                 nw_ref[...]).astype(os_ref.dtype)
    or_ref[...] = _ret_gate_norm(yr_ref[...], g_ref[...].astype(F32)).astype(or_ref.dtype)


def _sample_post(y_ssd, xs, proj2, y_ret, dfull, nw):
    n = y_ssd.shape[0]
    full = lambda s: pl.BlockSpec(s, lambda i: (0, 0))
    col = lambda off: pl.BlockSpec((n, D_MODEL), lambda i: (0, off // D_MODEL))
    return pl.pallas_call(
        _sample_post_kernel,
        grid=(1,),
        in_specs=[full((n, D_MODEL)), full((n, D_MODEL)), col(OFF_Z), full((n, D_MODEL)), col(OFF_G),
                  full((1, D_MODEL)), full((1, D_MODEL))],
        out_specs=[full((n, D_MODEL)), full((n, D_MODEL))],
        out_shape=[jax.ShapeDtypeStruct((n, D_MODEL), BF16)] * 2,
        compiler_params=_cparams(("arbitrary",)),
        name="sample_post",
    )(y_ssd, xs, proj2, y_ret, proj2, dfull, nw)


def _merge_kernel(s5_ref, ssd_ref, ret_ref, ga_ref, gb_ref, gc_ref, h_ref, ws5_ref, wssd_ref, wret_ref, wout_ref,
                  o_ref):
    glu = _dot(s5_ref[...], ws5_ref[...])
    y_a = glu[:, :D_MODEL] * _sigmoid(glu[:, D_MODEL:])
    y_b = _dot(ssd_ref[...], wssd_ref[...])
    y_c = _dot(ret_ref[...], wret_ref[...])
    merged = (_sigmoid(ga_ref[...].astype(F32)) * y_a + _sigmoid(gb_ref[...].astype(F32)) * y_b
              + _sigmoid(gc_ref[...].astype(F32)) * y_c)
    o_ref[...] = h_ref[...] + _dot(merged.astype(BF16), wout_ref[...])


def _merge(s5_g, ssd_n, ret_n, proj, h3, ws5, wssd, wret, wout, layer, tm):
    nb, L, _ = h3.shape
    row3 = lambda off: pl.BlockSpec((None, tm, D_MODEL), lambda b, i: (b, i, off // D_MODEL))
    wfull = lambda s: pl.BlockSpec((None,) + s, lambda b, i: (layer, 0, 0), pipeline_mode=pl.Buffered(1))
    return pl.pallas_call(
        _merge_kernel,
        grid=(nb, L // tm),
        in_specs=[row3(0), row3(0), row3(0),
                  row3(OFF_GATES), row3(OFF_GATES + D_MODEL), row3(OFF_GATES + 2 * D_MODEL), row3(0),
                  wfull((D_MODEL, 2 * D_MODEL)), wfull((D_MODEL, D_MODEL)), wfull((D_MODEL, D_MODEL)),
                  wfull((D_MODEL, D_MODEL))],
        out_specs=row3(0),
        out_shape=jax.ShapeDtypeStruct((nb, L, D_MODEL), F32),
        compiler_params=_cparams(("parallel", "parallel")),
        name="merge",
    )(s5_g, ssd_n, ret_n, proj, proj, proj, h3, ws5, wssd, wret, wout)


def _mlp_kernel(x_ref, nw_ref, wup_ref, wdn_ref, fw_ref, o_ref, xn_ref, acc_ref, *, final_norm):
    k = pl.program_id(1)
    nk = pl.num_programs(1)

    @pl.when(k == 0)
    def _():
        x = x_ref[...]
        xn_ref[...] = _rmsnorm(x, nw_ref[...]).astype(BF16)
        acc_ref[...] = x

    hid = jnp.maximum(_dot(xn_ref[...], wup_ref[...]), 0.0)
    acc_ref[...] += _dot((hid * hid).astype(BF16), wdn_ref[...])

    @pl.when(k == nk - 1)
    def _():
        h = acc_ref[...]
        o_ref[...] = _rmsnorm(h, fw_ref[...]) if final_norm else h


def _mlp(h2, nw, wup, wdn, fw, layer, tm, tf, final_norm):
    rows = h2.shape[0]
    return pl.pallas_call(
        functools.partial(_mlp_kernel, final_norm=final_norm),
        grid=(rows // tm, D_FF // tf),
        in_specs=[pl.BlockSpec((tm, D_MODEL), lambda i, k: (i, 0)),
                  pl.BlockSpec((1, D_MODEL), lambda i, k: (0, 0)),
                  pl.BlockSpec((None, D_MODEL, tf), lambda i, k: (layer, 0, k)),
                  pl.BlockSpec((None, tf, D_MODEL), lambda i, k: (layer, k, 0)),
                  pl.BlockSpec((1, D_MODEL), lambda i, k: (0, 0))],
        out_specs=pl.BlockSpec((tm, D_MODEL), lambda i, k: (i, 0)),
        out_shape=jax.ShapeDtypeStruct((rows, D_MODEL), F32),
        scratch_shapes=[pltpu.VMEM((tm, D_MODEL), BF16), pltpu.VMEM((tm, D_MODEL), F32)],
        compiler_params=_cparams(("parallel", "arbitrary")),
        name="mlp",
    )(h2, nw, wup, wdn, fw)


def _prep_layer(p):
    out = {}
    row = lambda a: a.reshape(1, -1).astype(F32)
    padrow = lambda a: jnp.pad(a.astype(F32), (0, LANES - a.shape[0])).reshape(1, LANES)
    rep = lambda a: jnp.repeat(a.astype(F32), SSD_HEAD_DIM).reshape(1, D_MODEL)
    out['norm1_w'] = row(p['norm1_w'])
    out['norm2_w'] = row(p['norm2_w'])
    out['conv_w'] = p['ssd_conv_w'].astype(F32)
    out['conv_b'] = row(p['ssd_conv_b'])
    out['b16'] = padrow(p['ssd_dt_bias'])
    out['al16'] = padrow(p['ssd_a_log'])
    out['dfull'] = rep(p['ssd_d'])
    out['ssd_nw'] = row(p['ssd_norm_w'])
    out['s5_d'] = row(p['s5_d'])
    abr, abi, bbr, bbi = _s5_discretise(p['s5_a_re'].astype(F32), p['s5_a_im'].astype(F32),
                                        p['s5_log_dt'].astype(F32), p['s5_b_re'].astype(F32),
                                        p['s5_b_im'].astype(F32))
    out['s5'] = _s5_weights(abr, abi, bbr, bbi, p['s5_c_re'].astype(F32), p['s5_c_im'].astype(F32))
    return out


def _head_expand():
    e = (jnp.arange(LANES)[:, None] == (jnp.arange(D_MODEL)[None, :] // SSD_HEAD_DIM))
    return e.astype(BF16)


def _tile(n, pref):
    return pref if n % pref == 0 else n


def _prompt_trunk(x, layers, final_w):
    nb, L, _ = x.shape
    eh = _head_expand()
    _, lm, gp, we, gq = _ret_consts(CHUNK)
    cos2, sin2 = _rope_tables(jnp.arange(L, dtype=jnp.int32))
    zeros_state = jnp.zeros((nb, S5_NSTATE), F32)
    steps = 64
    h = x
    outs = {k: [] for k in ('re', 'im', 'ssd', 'conv', 'ret')}
    for li, lp in enumerate(layers):
        tm = _tile(L, 1024)
        proj, dt = _proj_main(h, lp['norm1_w'], *lp['proj_w'], tm)
        wbr, wbi, abr, abi, wcr, wci = lp['s5']
        y_s5, hr, hi = _s5_scan(proj, wbr, wbi, abr, abi, wcr, wci, lp['s5_d'],
                                zeros_state, zeros_state, steps)
        ssd_n, ssd_s, ret_n, ret_s = _mix_prompt(proj, dt, lp['conv_w'], lp['conv_b'], lp['b16'], lp['al16'],
                                                 lp['dfull'], lp['ssd_nw'], eh, cos2, sin2, lm, gp, we, gq)
        h = _merge(y_s5, ssd_n, ret_n, proj, h,
                   *lp['merge_w'], li, _tile(L, 512))
        last = li == len(layers) - 1
        h = _mlp(h.reshape(nb * L, D_MODEL), lp['norm2_w'], *lp['mlp_w'], final_w, li,
                 _tile(nb * L, 1024), 1024, last).reshape(nb, L, D_MODEL)
        outs['re'].append(hr.reshape(nb, S5_GROUPS, S5_STATE))
        outs['im'].append(hi.reshape(nb, S5_GROUPS, S5_STATE))
        outs['ssd'].append(ssd_s.reshape(nb, SSD_HEADS, SSD_HEAD_DIM, SSD_STATE))
        outs['conv'].append(proj[:, L - (SSD_CONV - 1):, OFF_XS:OFF_XS + SSD_CONV_DIM].astype(F32))
        outs['ret'].append(ret_s.reshape(nb, RET_HEADS, RET_V_DIM, RET_QK_DIM))
    return (h,) + tuple(jnp.stack(outs[k]) for k in ('re', 'im', 'ssd', 'conv', 'ret'))


def _sample_trunk(x, states, layers, final_w):
    n = x.shape[0]
    st_re, st_im, st_ssd, st_conv, st_ret = states
    eh = _head_expand()
    log_gamma, _, _, _, _ = _ret_consts(1)
    cos2, sin2 = _rope_tables(PAST_LEN + jnp.arange(1, dtype=jnp.int32))
    e_ret = jnp.broadcast_to(jnp.repeat(jnp.exp(log_gamma), RET_V_DIM // STATE_BLK)[None, :],
                             (n, STATE_ROWS // STATE_BLK)).reshape(-1)
    h = x.reshape(1, n, D_MODEL)
    depth = len(layers)
    ssd_all = st_ssd.reshape(depth, n, STATE_ROWS, SSD_STATE).astype(F32)
    ret_all = st_ret.reshape(depth, n, STATE_ROWS, RET_QK_DIM).astype(F32)
    ssd_new = ret_new = None
    outs = {k: [] for k in ('re', 'im', 'conv')}
    for li, lp in enumerate(layers):
        proj, dt = _proj_main(h, lp['norm1_w'], *lp['proj_w'], n)
        wbr, wbi, abr, abi, wcr, wci = lp['s5']
        y_s5, hr, hi = _s5_scan(proj, wbr, wbi, abr, abi, wcr, wci, lp['s5_d'],
                                st_re[li].reshape(n, S5_NSTATE).astype(F32),
                                st_im[li].reshape(n, S5_NSTATE).astype(F32), 1)
        proj2, dt2 = proj.reshape(n, N_MAIN), dt.reshape(n, LANES)
        conv2 = st_conv[li].reshape(n, (SSD_CONV - 1) * SSD_CONV_DIM).astype(F32)
        vv, e16, bs, cs, xs, conv_new, qr, kr = _sample_pre(
            proj2, dt2, conv2, lp['conv_w'], lp['conv_b'], lp['b16'], lp['al16'], eh, cos2, sin2)
        ssd_new, y_ssd = _state_step(
            e16[:, :SSD_HEADS].reshape(-1), ssd_all, li,
            vv.reshape(n, SUBLANES, LANES), bs.reshape(n, SSD_GROUPS, SSD_STATE),
            cs.reshape(n, SSD_GROUPS, SSD_STATE), prev=ssd_new)
        vret = proj2[:, OFF_V:OFF_V + D_MODEL].astype(F32)
        ret_new, y_ret = _state_step(
            e_ret, ret_all, li,
            vret.reshape(n, SUBLANES, LANES), kr.reshape(n, RET_HEADS, RET_QK_DIM),
            qr.reshape(n, RET_HEADS, RET_QK_DIM), prev=ret_new)
        ssd_n, ret_n = _sample_post(y_ssd.reshape(n, D_MODEL), xs, proj2, y_ret.reshape(n, D_MODEL),
                                    lp['dfull'], lp['ssd_nw'])
        h = _merge(y_s5, ssd_n.reshape(1, n, D_MODEL), ret_n.reshape(1, n, D_MODEL), proj, h,
                   *lp['merge_w'], li, n)
        last = li == len(layers) - 1
        h = _mlp(h.reshape(n, D_MODEL), lp['norm2_w'], *lp['mlp_w'], final_w, li,
                 n, 1024, last).reshape(1, n, D_MODEL)
        outs['re'].append(hr.reshape(n, S5_GROUPS, S5_STATE))
        outs['im'].append(hi.reshape(n, S5_GROUPS, S5_STATE))
        outs['conv'].append(conv_new.reshape(n, SSD_CONV - 1, SSD_CONV_DIM))
    return (h.reshape(n, 1, D_MODEL), jnp.stack(outs['re']), jnp.stack(outs['im']),
            ssd_new.reshape(depth, n, SSD_HEADS, SSD_HEAD_DIM, SSD_STATE), jnp.stack(outs['conv']),
            ret_new.reshape(depth, n, RET_HEADS, RET_V_DIM, RET_QK_DIM))


def kernel(x_prompt, x_sample, state_s5_re, state_s5_im, state_ssd, state_conv, state_ret, norm1_w, w_in, s5_a_re, s5_a_im, s5_log_dt, s5_b_re, s5_b_im, s5_c_re, s5_c_im, s5_d, w_s5_out, ssd_conv_w, ssd_conv_b, ssd_dt_bias, ssd_a_log, ssd_d, ssd_norm_w, w_ssd_out, w_ret_out, w_out, norm2_w, w_mlp_up, w_mlp_down, final_norm_w):
    params = dict(norm1_w=norm1_w, w_in=w_in, s5_a_re=s5_a_re, s5_a_im=s5_a_im, s5_log_dt=s5_log_dt,
                  s5_b_re=s5_b_re, s5_b_im=s5_b_im, s5_c_re=s5_c_re, s5_c_im=s5_c_im, s5_d=s5_d,
                  w_s5_out=w_s5_out, ssd_conv_w=ssd_conv_w, ssd_conv_b=ssd_conv_b, ssd_dt_bias=ssd_dt_bias,
                  ssd_a_log=ssd_a_log, ssd_d=ssd_d, ssd_norm_w=ssd_norm_w, w_ssd_out=w_ssd_out,
                  w_ret_out=w_ret_out, w_out=w_out, norm2_w=norm2_w, w_mlp_up=w_mlp_up, w_mlp_down=w_mlp_down)
    depth = w_in.shape[0]
    big = ('w_in', 'w_s5_out', 'w_ssd_out', 'w_ret_out', 'w_out', 'w_mlp_up', 'w_mlp_down')
    layers = [_prep_layer({name: arr[l] for name, arr in params.items() if name not in big}) for l in range(depth)]
    merge_w = tuple(params[name].astype(BF16) for name in ('w_s5_out', 'w_ssd_out', 'w_ret_out', 'w_out'))
    mlp_w = tuple(params[name].astype(BF16) for name in ('w_mlp_up', 'w_mlp_down'))
    w_a, w_b, w_dt = _proj_weight_slabs(w_in.astype(F32))
    for l, lp in enumerate(layers):
        lp['proj_w'] = (w_a, w_b, w_dt, l)
        lp['merge_w'] = merge_w
        lp['mlp_w'] = mlp_w
    final_w = final_norm_w.reshape(1, D_MODEL).astype(F32)
    y_p, re_p, im_p, ssd_p, conv_p, ret_p = _prompt_trunk(x_prompt.astype(F32), layers, final_w)
    y_s, re_s, im_s, ssd_s, conv_s, ret_s = _sample_trunk(
        x_sample.astype(F32), (state_s5_re, state_s5_im, state_ssd, state_conv, state_ret), layers, final_w)
    return (y_p, y_s, re_p, im_p, ssd_p, conv_p, ret_p, re_s, im_s, ssd_s, conv_s, ret_s)
```

```python
import functools
import math

import jax
import jax.numpy as jnp
from jax import lax
from jax.experimental import pallas as pl
from jax.experimental.pallas import tpu as pltpu

F32 = jnp.float32
BF16 = jnp.bfloat16

D_MODEL = 1024
S5_GROUP = 16
S5_STATE = 64
S5_GROUPS = D_MODEL // S5_GROUP
S5_NSTATE = S5_GROUPS * S5_STATE
S5_SUPER = 8
S5_TIME_SPLIT = 2
SSD_HEADS = 16
SSD_HEAD_DIM = 64
SSD_GROUPS = 4
SSD_STATE = 128
SSD_CONV = 4
SSD_CONV_DIM = D_MODEL + 2 * SSD_GROUPS * SSD_STATE
RET_HEADS = 4
RET_QK_DIM = 128
RET_V_DIM = 256
ROPE_BASE = 10000.0
D_FF = 4 * D_MODEL
CHUNK = 128
EPS = 1e-6
PAST_LEN = 16384
LANES = 128
SUBLANES = 8
VMEM_LIMIT = 52 * 1024 * 1024

OFF_U, OFF_Z, OFF_XS, OFF_BC, OFF_Q, OFF_K, OFF_V, OFF_G, OFF_GATES = (
    0, 1024, 2048, 3072, 4096, 4608, 5120, 6144, 7168)
N_MAIN = 10240
SLAB_A_COLS = OFF_Q
SLAB_B_START = 2 * D_MODEL + SSD_CONV_DIM + SSD_HEADS

PROJ_TM, PROJ_TN = 1024, 2048
MERGE_TM = 512
MLP_TM, MLP_TF = 1024, 1024
S5_STEPS = 64
MIX_NSEQ = 2
STATE_TOKENS = 8


def _cparams(sem):
    return pltpu.CompilerParams(dimension_semantics=sem, vmem_limit_bytes=VMEM_LIMIT)


def _sigmoid(x):
    return 1.0 / (1.0 + jnp.exp(-x))


def _silu(x):
    return x * _sigmoid(x)


def _softplus(x):
    return jnp.maximum(x, 0.0) + jnp.log1p(jnp.exp(-jnp.abs(x)))


def _gelu_tanh(x):
    c = math.sqrt(2.0 / math.pi)
    half = 0.5 * x
    return half + half * jnp.tanh(x * (c + (c * 0.044715) * (x * x)))


def _rmsnorm(x, w):
    return x * lax.rsqrt(jnp.mean(x * x, axis=-1, keepdims=True) + EPS) * w


def _cumsum_rows(x):
    n = x.shape[0]
    row = lax.broadcasted_iota(jnp.int32, x.shape, 0)
    k = 1
    while k < n:
        x = x + jnp.where(row >= k, pltpu.roll(x, k, 0), 0.0)
        k *= 2
    return x


def _group_rms(y, width):
    outs = []
    for s in range(0, y.shape[-1], width):
        yg = y[:, s:s + width]
        outs.append(yg * lax.rsqrt(jnp.mean(yg * yg, axis=-1, keepdims=True) + EPS))
    return jnp.concatenate(outs, axis=-1)


def _dot(a, b):
    return jnp.dot(a, b, preferred_element_type=F32)


def _split_dot(x, e):
    hi = x.astype(BF16)
    lo = (x - hi.astype(F32)).astype(BF16)
    return _dot(hi, e) + _dot(lo, e)


def _proj_kernel(x_ref, nw_ref, wa_ref, wb_ref, wdt_ref, o_ref, odt_ref, xn_ref, *, na):
    j = pl.program_id(2)

    @pl.when(j == 0)
    def _():
        xn = _rmsnorm(x_ref[...], nw_ref[...]).astype(BF16)
        xn_ref[...] = xn
        odt_ref[...] = _dot(xn, wdt_ref[...])

    @pl.when(j < na)
    def _():
        o_ref[...] = _dot(xn_ref[...], wa_ref[...]).astype(o_ref.dtype)

    @pl.when(j >= na)
    def _():
        o_ref[...] = _dot(xn_ref[...], wb_ref[...]).astype(o_ref.dtype)


PREP_TN = 1024
DT_START = SLAB_B_START - SSD_HEADS
SLAB_SHIFT = SLAB_B_START % PREP_TN


def _slab_b_kernel(lo_ref, hi_ref, o_ref):
    keep = PREP_TN - SLAB_SHIFT
    lane = lax.broadcasted_iota(jnp.int32, (D_MODEL, PREP_TN), 1)
    lo = pltpu.roll(lo_ref[...].astype(F32), keep, 1)
    hi = pltpu.roll(hi_ref[...].astype(F32), keep, 1)
    o_ref[...] = jnp.where(lane < keep, lo, hi).astype(BF16)


def _proj_weight_slabs(w_in):
    depth = w_in.shape[0]
    tn = PREP_TN
    w_all = w_in.astype(BF16)
    w_dt = jnp.pad(w_in[:, :, DT_START:SLAB_B_START], ((0, 0), (0, 0), (0, LANES - SSD_HEADS))).astype(BF16)
    blk = lambda f: pl.BlockSpec((None, D_MODEL, tn), f)
    nbcols = N_MAIN - SLAB_A_COLS
    first = SLAB_B_START // tn
    w_b = pl.pallas_call(
        _slab_b_kernel,
        grid=(depth, nbcols // tn),
        in_specs=[blk(lambda l, j: (l, 0, first + j)), blk(lambda l, j: (l, 0, first + j + 1))],
        out_specs=blk(lambda l, j: (l, 0, j)),
        out_shape=jax.ShapeDtypeStruct((depth, D_MODEL, nbcols), BF16),
        compiler_params=_cparams(("parallel", "parallel")),
        name="weight_slab_b",
    )(w_all, w_all)
    return w_all, w_b, w_dt


def _proj_main(x3, nw, w_a, w_b, wdt, layer, tm):
    nb, L, _ = x3.shape
    tn = PROJ_TN
    na = SLAB_A_COLS // tn
    return pl.pallas_call(
        functools.partial(_proj_kernel, na=na),
        grid=(nb, L // tm, N_MAIN // tn),
        in_specs=[pl.BlockSpec((None, tm, D_MODEL), lambda b, i, j: (b, i, 0)),
                  pl.BlockSpec((1, D_MODEL), lambda b, i, j: (0, 0)),
                  pl.BlockSpec((None, D_MODEL, tn), lambda b, i, j: (layer, 0, jnp.minimum(j, na - 1))),
                  pl.BlockSpec((None, D_MODEL, tn), lambda b, i, j: (layer, 0, jnp.maximum(j - na, 0))),
                  pl.BlockSpec((None, D_MODEL, LANES), lambda b, i, j: (layer, 0, 0))],
        out_specs=[pl.BlockSpec((None, tm, tn), lambda b, i, j: (b, i, j)),
                   pl.BlockSpec((None, tm, LANES), lambda b, i, j: (b, i, 0))],
        out_shape=[jax.ShapeDtypeStruct((nb, L, N_MAIN), BF16), jax.ShapeDtypeStruct((nb, L, LANES), F32)],
        scratch_shapes=[pltpu.VMEM((tm, D_MODEL), BF16)],
        compiler_params=_cparams(("parallel", "parallel", "arbitrary")),
        name="proj_main",
    )(x3, nw, w_a, w_b, wdt)


def _s5_disc_kernel(are_ref, aim_ref, ldt_ref, bre_ref, bim_ref, abr_ref, abi_ref, bbr_ref, bbi_ref):
    ar, ai = are_ref[...], aim_ref[...]
    dt = jnp.exp(ldt_ref[...])
    mag = jnp.exp(ar * dt)
    abr = mag * jnp.cos(ai * dt)
    abi = mag * jnp.sin(ai * dt)
    abr_ref[...] = abr
    abi_ref[...] = abi
    nr, ni = abr - 1.0, abi
    den = 1.0 / (ar * ar + ai * ai)
    cr = (nr * ar + ni * ai) * den
    ci = (ni * ar - nr * ai) * den
    br, bi = bre_ref[...], bim_ref[...]
    bbr_ref[...] = cr * br - ci * bi
    bbi_ref[...] = cr * bi + ci * br


def _s5_discretise(a_re, a_im, log_dt, b_re, b_im):
    g, p = a_re.shape
    gs = b_re.shape[-1]
    full3 = lambda s: pl.BlockSpec(s, lambda: (0, 0, 0))
    return pl.pallas_call(
        _s5_disc_kernel,
        in_specs=[full3((g, 1, p)), full3((g, 1, p)), full3((g, 1, 1)), full3((g, gs, p)), full3((g, gs, p))],
        out_specs=[full3((g, 1, p)), full3((g, 1, p)), full3((g, gs, p)), full3((g, gs, p))],
        out_shape=[jax.ShapeDtypeStruct((g, 1, p), F32)] * 2 + [jax.ShapeDtypeStruct((g, gs, p), F32)] * 2,
        name="s5_discretise",
    )(a_re.reshape(g, 1, p), a_im.reshape(g, 1, p), log_dt.reshape(g, 1, 1),
      jnp.swapaxes(b_re, 1, 2), jnp.swapaxes(b_im, 1, 2))


def _s5_kernel(u_ref, wbr_ref, wbi_ref, ar_ref, ai_ref, wcr_ref, wci_ref, d_ref, h0r_ref, h0i_ref,
               y_ref, hr_ref, hi_ref, bur, bui, *, nbatch, steps, lane_w):
    @pl.when(pl.program_id(0) == 0)
    def _():
        hr_ref[...] = h0r_ref[...]
        hi_ref[...] = h0i_ref[...]

    cw = S5_SUPER * S5_GROUP
    sw = S5_SUPER * S5_STATE
    batch_major = steps > 1
    u = u_ref[...].astype(F32)
    if batch_major:
        u = jnp.swapaxes(u, 0, 1).reshape(steps * nbatch, D_MODEL)
    ub = u.astype(BF16)
    nsub = S5_TIME_SPLIT if steps % (2 * S5_TIME_SPLIT) == 0 else 1
    sub_steps = steps // nsub
    nsup = S5_GROUPS // S5_SUPER
    for sb in range(nsub):
        t0 = sb * sub_steps
        rs = slice(t0 * nbatch, (t0 + sub_steps) * nbatch)
        for j in range(nsup):
            uj = ub[rs, j * cw:(j + 1) * cw]
            bur[rs, j * sw:(j + 1) * sw] = _dot(uj, wbr_ref[j])
            bui[rs, j * sw:(j + 1) * sw] = _dot(uj, wbi_ref[j])

        for j in range(nsup):
            for lc in range(sw // lane_w):
                sl = slice(j * sw + lc * lane_w, j * sw + (lc + 1) * lane_w)
                are = jnp.broadcast_to(ar_ref[:, sl], (nbatch, lane_w))
                aim = jnp.broadcast_to(ai_ref[:, sl], (nbatch, lane_w))
                hr, hi = hr_ref[:, sl], hi_ref[:, sl]
                for t in range(t0, t0 + sub_steps):
                    rows = slice(t * nbatch, (t + 1) * nbatch)
                    hr, hi = are * hr - aim * hi + bur[rows, sl], are * hi + aim * hr + bui[rows, sl]
                    bur[rows, sl] = hr
                    bui[rows, sl] = hi
                hr_ref[:, sl] = hr
                hi_ref[:, sl] = hi

        for j in range(nsup):
            ch = slice(j * cw, (j + 1) * cw)
            yj = (_dot(bur[rs, j * sw:(j + 1) * sw].astype(BF16), wcr_ref[j])
                  + _dot(bui[rs, j * sw:(j + 1) * sw].astype(BF16), wci_ref[j]))
            yj = _gelu_tanh(yj + d_ref[:, ch] * u[rs, ch])
            if batch_major:
                y_ref[:, t0:t0 + sub_steps, ch] = jnp.swapaxes(
                    yj.reshape(sub_steps, nbatch, cw), 0, 1).astype(y_ref.dtype)
            else:
                y_ref[:, ch] = yj.astype(y_ref.dtype)


def _s5_scan(proj, wbr, wbi, abr, abi, wcr, wci, d, h0r, h0i, steps):
    nb, L, _ = proj.shape
    nsup = S5_GROUPS // S5_SUPER
    cw, sw = S5_SUPER * S5_GROUP, S5_SUPER * S5_STATE
    c2 = lambda s: pl.BlockSpec(s, lambda c: (0, 0))
    c3 = lambda s: pl.BlockSpec(s, lambda c: (0, 0, 0))
    if steps > 1:
        nbatch, grid = nb, L // steps
        u_spec = pl.BlockSpec((nb, steps, D_MODEL), lambda c: (0, c, OFF_U // D_MODEL))
        y_spec = pl.BlockSpec((nb, steps, D_MODEL), lambda c: (0, c, 0))
    else:
        nbatch, grid = L, 1
        u_spec = pl.BlockSpec((None, L, D_MODEL), lambda c: (0, 0, OFF_U // D_MODEL))
        y_spec = pl.BlockSpec((None, L, D_MODEL), lambda c: (0, 0, 0))
    blk = nbatch * steps
    lane_w = max(LANES, S5_NSTATE // nbatch)
    kern = functools.partial(_s5_kernel, nbatch=nbatch, steps=steps, lane_w=lane_w)
    return pl.pallas_call(
        kern,
        grid=(grid,),
        in_specs=[u_spec,
                  c3((nsup, cw, sw)), c3((nsup, cw, sw)),
                  c2((1, S5_NSTATE)), c2((1, S5_NSTATE)),
                  c3((nsup, sw, cw)), c3((nsup, sw, cw)),
                  c2((1, D_MODEL)),
                  c2((nbatch, S5_NSTATE)), c2((nbatch, S5_NSTATE))],
        out_specs=[y_spec, c2((nbatch, S5_NSTATE)), c2((nbatch, S5_NSTATE))],
        out_shape=[jax.ShapeDtypeStruct((nb, L, D_MODEL), BF16),
                   jax.ShapeDtypeStruct((nbatch, S5_NSTATE), F32),
                   jax.ShapeDtypeStruct((nbatch, S5_NSTATE), F32)],
        scratch_shapes=[pltpu.VMEM((blk, S5_NSTATE), F32), pltpu.VMEM((blk, S5_NSTATE), F32)],
        compiler_params=_cparams(("arbitrary",)),
        name="s5_scan",
    )(proj, wbr, wbi, abr, abi, wcr, wci, d, h0r, h0i)


def _s5_weights(abr, abi, bbr, bbi, c_re, c_im):
    nsup = S5_GROUPS // S5_SUPER
    eye = jnp.eye(S5_SUPER, dtype=F32)

    def bexp(b):
        b = b.reshape(nsup, S5_SUPER, S5_GROUP, S5_STATE)
        return (b[:, :, :, None, :] * eye[None, :, None, :, None]).reshape(
            nsup, S5_SUPER * S5_GROUP, S5_SUPER * S5_STATE).astype(BF16)

    def cexp(c):
        c = jnp.swapaxes(c, 1, 2).reshape(nsup, S5_SUPER, S5_STATE, S5_GROUP)
        return (c[:, :, :, None, :] * eye[None, :, None, :, None]).reshape(
            nsup, S5_SUPER * S5_STATE, S5_SUPER * S5_GROUP).astype(BF16)

    return (bexp(bbr), bexp(bbi), abr.reshape(1, S5_NSTATE), abi.reshape(1, S5_NSTATE),
            cexp(c_re), cexp(-c_im))


def _ssd_gate_norm(y, xs, z, dfull, nw):
    y = (y + dfull * xs) * _silu(z)
    return _group_rms(y, D_MODEL // SSD_GROUPS) * nw


CONV_PAD = 16
CONV_ROWS = 2 * CHUNK


def _conv_shift_mats():
    i = jnp.arange(CHUNK)[:, None]
    j = jnp.arange(CONV_ROWS)[None, :]
    return jnp.stack([(j == CONV_PAD + i - s) for s in range(1, SSD_CONV)]).astype(BF16)


def _ssd_chunk_seq(xs_ref, bc_ref, z_ref, dt_ref, cw_ref, cb_ref, b16_ref, al16_ref,
                   dfull_ref, nw_ref, eh_ref, sh_ref, o_ref, xpad, st):
    q = CHUNK
    pad = CONV_PAD
    xpad[0:pad, :] = xpad[q:q + pad, :]

    xpad[pad:pad + q, :D_MODEL] = xs_ref[...]
    xpad[pad:pad + q, D_MODEL:] = bc_ref[...]
    staged = xpad[...]
    acc = cb_ref[...] + xpad[pad:pad + q, :].astype(F32) * cw_ref[SSD_CONV - 1:SSD_CONV, :]
    for s in range(1, SSD_CONV):
        acc = acc + _dot(sh_ref[s - 1], staged) * cw_ref[SSD_CONV - 1 - s:SSD_CONV - s, :]
    xc = _silu(acc)
    xs = xc[:, :D_MODEL]
    nbc = SSD_GROUPS * SSD_STATE
    bs = xc[:, D_MODEL:D_MODEL + nbc]
    cs = xc[:, D_MODEL + nbc:]

    delta16 = _softplus(dt_ref[...] + b16_ref[...])
    acum16 = _cumsum_rows(delta16 * (-jnp.exp(al16_ref[...])))
    acum_log2 = acum16 * math.log2(math.e)
    acum_log2_t = acum_log2.T
    delta_t = delta16.T
    last16 = acum16[q - 1:q, :]
    ea_f = _split_dot(jnp.exp(acum16), eh_ref[...])
    dw_f = _split_dot(delta16 * jnp.exp(last16 - acum16), eh_ref[...])
    xw = (xs * dw_f).astype(BF16)

    ri = lax.broadcasted_iota(jnp.int32, (q, q), 0)
    ci = lax.broadcasted_iota(jnp.int32, (q, q), 1)
    causal = ri >= ci
    hw = D_MODEL // SSD_GROUPS
    hpg = SSD_HEADS // SSD_GROUPS
    xsb = xs.astype(BF16)
    ys = []
    for g in range(SSD_GROUPS):
        bg = bs[:, g * SSD_STATE:(g + 1) * SSD_STATE]
        cg = cs[:, g * SSD_STATE:(g + 1) * SSD_STATE].astype(BF16)
        bgt = bg.T.astype(BF16)
        scores = _dot(cg, bgt)
        sg = st[:, g * hw:(g + 1) * hw]
        y_inter = _dot(cg, sg.astype(BF16)) * ea_f[:, g * hw:(g + 1) * hw]
        for r in range(hpg):
            hd = g * hpg + r
            seg = acum_log2[:, hd:hd + 1] - acum_log2_t[hd:hd + 1, :]
            lm = jnp.exp2(jnp.where(causal, seg, -jnp.inf)) * delta_t[hd:hd + 1, :]
            p = (scores * lm).astype(BF16)
            ys.append(_dot(p, xsb[:, hd * SSD_HEAD_DIM:(hd + 1) * SSD_HEAD_DIM])
                      + y_inter[:, r * SSD_HEAD_DIM:(r + 1) * SSD_HEAD_DIM])
        st[:, g * hw:(g + 1) * hw] = (ea_f[q - 1:q, g * hw:(g + 1) * hw] * sg
                                      + _dot(bgt, xw[:, g * hw:(g + 1) * hw]))
    y = jnp.concatenate(ys, axis=-1)
    o_ref[...] = _ssd_gate_norm(y, xs, z_ref[...].astype(F32), dfull_ref[...], nw_ref[...]).astype(o_ref.dtype)


def _rotary(x, cos2, sin2):
    outs = []
    for h in range(RET_HEADS):
        xh = x[:, h * RET_QK_DIM:(h + 1) * RET_QK_DIM]
        outs.append(xh * cos2 + pltpu.roll(xh, RET_QK_DIM // 2, 1) * sin2)
    return outs


def _ret_gate_norm(y, g):
    return _group_rms(y, RET_V_DIM) * _silu(g)


def _ret_chunk_seq(q_ref, k_ref, v_ref, g_ref, cos2, sin2, lm_ref, gp_ref, we_ref, gq_ref, o_ref, st):
    qr = _rotary(q_ref[...].astype(F32), cos2, sin2)
    kr = _rotary(k_ref[...].astype(F32), cos2, sin2)
    vb = v_ref[...]
    vw = (vb.astype(F32) * we_ref[...]).astype(BF16)
    ys = []
    for h in range(RET_HEADS):
        sl = slice(h * RET_V_DIM, (h + 1) * RET_V_DIM)
        qh = qr[h].astype(BF16)
        kht = (kr[h] * (RET_QK_DIM ** -0.5)).T.astype(BF16)
        p = (_dot(qh, kht) * lm_ref[h]).astype(BF16)
        sh = st[:, sl]
        ys.append(_dot(p, vb[:, sl]) + _dot(qh, sh.astype(BF16)) * gp_ref[:, sl])
        st[:, sl] = gq_ref[:, sl] * sh + _dot(kht, vw[:, sl])
    y = jnp.concatenate(ys, axis=-1)
    o_ref[...] = _ret_gate_norm(y, g_ref[...].astype(F32)).astype(o_ref.dtype)


def _mix_chunk_kernel(xs_ref, bc_ref, z_ref, dt_ref, cw_ref, cb_ref, b16_ref, al16_ref, dfull_ref, nw_ref, eh_ref,
                      sh_ref, q_ref, k_ref, v_ref, g_ref, cos_ref, sin_ref, lm_ref, gp_ref, we_ref, gq_ref,
                      so_ref, ss_ref, ro_ref, rs_ref, xpad, st_ssd, st_ret):
    c = pl.program_id(1)
    nc = pl.num_programs(1)
    nseq = xs_ref.shape[0]

    @pl.when(c == 0)
    def _():
        xpad[...] = jnp.zeros(xpad.shape, BF16)
        st_ssd[...] = jnp.zeros(st_ssd.shape, F32)
        st_ret[...] = jnp.zeros(st_ret.shape, F32)

    cos2, sin2 = cos_ref[...], sin_ref[...]
    for bi in range(nseq):
        _ssd_chunk_seq(xs_ref.at[bi], bc_ref.at[bi], z_ref.at[bi], dt_ref.at[bi], cw_ref, cb_ref, b16_ref,
                       al16_ref, dfull_ref, nw_ref, eh_ref, sh_ref, so_ref.at[bi], xpad.at[bi], st_ssd.at[bi])
        _ret_chunk_seq(q_ref.at[bi], k_ref.at[bi], v_ref.at[bi], g_ref.at[bi], cos2, sin2, lm_ref, gp_ref, we_ref,
                       gq_ref, ro_ref.at[bi], st_ret.at[bi])

    @pl.when(c == nc - 1)
    def _():
        for bi in range(nseq):
            for k in range(D_MODEL // LANES):
                ss_ref[bi, k * LANES:(k + 1) * LANES, :] = st_ssd[bi, :, k * LANES:(k + 1) * LANES].T
                rs_ref[bi, k * LANES:(k + 1) * LANES, :] = st_ret[bi, :, k * LANES:(k + 1) * LANES].T


def _ret_consts(q):
    hidx = jnp.arange(RET_HEADS, dtype=F32)
    log_gamma = jnp.log1p(-jnp.exp2(-5.0 - hidx))
    i = jnp.arange(q, dtype=F32)
    seg = (i[:, None] - i[None, :])[None] * log_gamma[:, None, None]
    lm = jnp.exp(jnp.where((i[:, None] >= i[None, :])[None], seg, -jnp.inf))
    rep = lambda t: jnp.repeat(t, RET_V_DIM, axis=-1)
    gp = rep(jnp.exp((i[:, None] + 1.0) * log_gamma[None, :]))
    we = rep(jnp.exp((q - 1.0 - i[:, None]) * log_gamma[None, :]))
    gq = rep(jnp.exp(q * log_gamma)[None, :])
    return log_gamma, lm, gp, we, gq


def _rope_tables(pos):
    half = RET_QK_DIM // 2
    inv = ROPE_BASE ** (-jnp.arange(half, dtype=F32) / half)
    ang = pos.astype(F32)[:, None] * inv[None]
    cos, sin = jnp.cos(ang), jnp.sin(ang)
    return jnp.concatenate([cos, cos], axis=-1), jnp.concatenate([-sin, sin], axis=-1)


def _mix_prompt(proj, dt, cw, cb, b16, al16, dfull, nw, eh, cos2, sin2, lm, gp, we, gq, nseq=MIX_NSEQ):
    nb, L, _ = proj.shape
    q = CHUNK
    qkw = RET_HEADS * RET_QK_DIM
    c2 = lambda s: pl.BlockSpec(s, lambda b, c: (0, 0))
    rows = lambda w, off: pl.BlockSpec((nseq, q, w), lambda b, c: (b, c, off // w))
    state = pl.BlockSpec((nseq, D_MODEL, LANES), lambda b, c: (b, 0, 0))
    return pl.pallas_call(
        _mix_chunk_kernel,
        grid=(nb // nseq, L // q),
        in_specs=[rows(D_MODEL, OFF_XS), rows(D_MODEL, OFF_BC), rows(D_MODEL, OFF_Z), rows(LANES, 0),
                  c2((SSD_CONV, SSD_CONV_DIM)), c2((1, SSD_CONV_DIM)),
                  c2((1, LANES)), c2((1, LANES)),
                  c2((1, D_MODEL)), c2((1, D_MODEL)),
                  c2((LANES, D_MODEL)),
                  pl.BlockSpec((SSD_CONV - 1, q, CONV_ROWS), lambda b, c: (0, 0, 0)),
                  rows(qkw, OFF_Q), rows(qkw, OFF_K), rows(D_MODEL, OFF_V), rows(D_MODEL, OFF_G),
                  pl.BlockSpec((q, RET_QK_DIM), lambda b, c: (c, 0)),
                  pl.BlockSpec((q, RET_QK_DIM), lambda b, c: (c, 0)),
                  pl.BlockSpec((RET_HEADS, q, q), lambda b, c: (0, 0, 0)),
                  c2((q, D_MODEL)), c2((q, D_MODEL)), c2((1, D_MODEL))],
        out_specs=[rows(D_MODEL, 0), state, rows(D_MODEL, 0), state],
        out_shape=[jax.ShapeDtypeStruct((nb, L, D_MODEL), BF16),
                   jax.ShapeDtypeStruct((nb, D_MODEL, SSD_STATE), F32),
                   jax.ShapeDtypeStruct((nb, L, D_MODEL), BF16),
                   jax.ShapeDtypeStruct((nb, D_MODEL, RET_QK_DIM), F32)],
        scratch_shapes=[pltpu.VMEM((nseq, CONV_ROWS, SSD_CONV_DIM), BF16),
                        pltpu.VMEM((nseq, SSD_STATE, D_MODEL), F32),
                        pltpu.VMEM((nseq, RET_QK_DIM, D_MODEL), F32)],
        compiler_params=_cparams(("parallel", "arbitrary")),
        name="mix_chunk",
    )(proj, proj, proj, dt, cw, cb, b16, al16, dfull, nw, eh, _conv_shift_mats(),
      proj, proj, proj, proj, cos2, sin2, lm, gp, we, gq)


def _sample_pre_kernel(xsin_ref, bcin_ref, dt_ref, q_ref, k_ref, conv_ref, cw_ref, cb_ref, b16_ref, al16_ref,
                       eh_ref, cos_ref, sin_ref,
                       vv_ref, e16_ref, bs_ref, cs_ref, xs_ref, convn_ref, qr_ref, kr_ref):
    cd = SSD_CONV_DIM
    x = jnp.concatenate([xsin_ref[...], bcin_ref[...]], axis=-1).astype(F32)
    acc = cb_ref[...] + x * cw_ref[SSD_CONV - 1:SSD_CONV, :]
    for tap in range(SSD_CONV - 1):
        acc = acc + conv_ref[:, tap * cd:(tap + 1) * cd] * cw_ref[tap:tap + 1, :]
    convn_ref[:, 0:(SSD_CONV - 2) * cd] = conv_ref[:, cd:(SSD_CONV - 1) * cd]
    convn_ref[:, (SSD_CONV - 2) * cd:] = x
    xc = _silu(acc)
    xs = xc[:, :D_MODEL]
    nbc = SSD_GROUPS * SSD_STATE
    xs_ref[...] = xs
    bs_ref[...] = xc[:, D_MODEL:D_MODEL + nbc]
    cs_ref[...] = xc[:, D_MODEL + nbc:]
    delta16 = _softplus(dt_ref[...] + b16_ref[...])
    e16_ref[...] = jnp.exp(delta16 * (-jnp.exp(al16_ref[...])))
    vv_ref[...] = xs * _split_dot(delta16, eh_ref[...])
    cos2, sin2 = cos_ref[...], sin_ref[...]
    qr_ref[...] = jnp.concatenate(_rotary(q_ref[...].astype(F32), cos2, sin2), axis=-1)
    kr_ref[...] = jnp.concatenate(_rotary(k_ref[...].astype(F32), cos2, sin2), axis=-1) * (RET_QK_DIM ** -0.5)


def _sample_pre(proj2, dt2, conv2, cw, cb, b16, al16, eh, cos2, sin2):
    n = proj2.shape[0]
    qkw = RET_HEADS * RET_QK_DIM
    nbc = SSD_GROUPS * SSD_STATE
    full = lambda s: pl.BlockSpec(s, lambda i: (0, 0))
    col = lambda w, off: pl.BlockSpec((n, w), lambda i: (0, off // w))
    shp = lambda w: jax.ShapeDtypeStruct((n, w), F32)
    return pl.pallas_call(
        _sample_pre_kernel,
        grid=(1,),
        in_specs=[col(D_MODEL, OFF_XS), col(D_MODEL, OFF_BC), full((n, LANES)), col(qkw, OFF_Q), col(qkw, OFF_K),
                  full((n, (SSD_CONV - 1) * SSD_CONV_DIM)),
                  full((SSD_CONV, SSD_CONV_DIM)), full((1, SSD_CONV_DIM)),
                  full((1, LANES)), full((1, LANES)),
                  full((LANES, D_MODEL)), full((1, RET_QK_DIM)), full((1, RET_QK_DIM))],
        out_specs=[full((n, D_MODEL)), full((n, LANES)), full((n, nbc)), full((n, nbc)), full((n, D_MODEL)),
                   full((n, (SSD_CONV - 1) * SSD_CONV_DIM)), full((n, qkw)), full((n, qkw))],
        out_shape=[shp(D_MODEL), shp(LANES), shp(nbc), shp(nbc), shp(D_MODEL),
                   shp((SSD_CONV - 1) * SSD_CONV_DIM), shp(qkw), shp(qkw)],
        compiler_params=_cparams(("arbitrary",)),
        name="sample_pre",
    )(proj2, proj2, dt2, proj2, proj2, conv2, cw, cb, b16, al16, eh, cos2, sin2)


STATE_ROWS = D_MODEL
STATE_BLK = 64


def _state_step_kernel(e_ref, s_ref, v_ref, k_ref, q_ref, *rest, bb, layer, fill):
    sn_all, y_ref = rest[-2:]
    if fill:
        for l in range(sn_all.shape[0]):
            if l != layer:
                sn_all[l] = jnp.zeros(sn_all.shape[1:], F32)
        sn_ref = sn_all.at[layer]
    else:
        sn_ref = sn_all
    i = pl.program_id(0)
    nblk = STATE_ROWS // STATE_BLK
    gw = STATE_ROWS // 4
    ones = jnp.ones((SUBLANES, LANES), BF16)
    zpad = jnp.zeros((LANES - SUBLANES, LANES), F32)
    nt = (((1,), (1,)), ((), ()))
    for t in range(bb):
        b = i * bb + t
        vt = jnp.concatenate([v_ref[t], zpad], axis=0).T
        kk = k_ref[t]
        qq = q_ref[t]
        for cch in range(STATE_ROWS // LANES):
            prods = []
            for half in range(LANES // STATE_BLK):
                blk = cch * (LANES // STATE_BLK) + half
                r0 = blk * STATE_BLK
                g = r0 // gw
                vcol = vt[half * STATE_BLK:(half + 1) * STATE_BLK, cch:cch + 1]
                sn = e_ref[b * nblk + blk] * s_ref[t, r0:r0 + STATE_BLK, :] + vcol * kk[g:g + 1, :]
                sn_ref[t, r0:r0 + STATE_BLK, :] = sn
                prods.append(sn * qq[g:g + 1, :])
            p = jnp.concatenate(prods, axis=0).astype(BF16)
            ysum = lax.dot_general(ones, p, nt, preferred_element_type=F32)
            y_ref[t, cch:cch + 1, :] = ysum[0:1, :]


def _state_step(e_flat, s_all, layer, v3, k3, q3, prev=None, bb=STATE_TOKENS):
    depth, n = s_all.shape[:2]
    blk3 = lambda a, b_: pl.BlockSpec((bb, a, b_), lambda i: (i, 0, 0))
    sblk = pl.BlockSpec((None, bb, STATE_ROWS, LANES), lambda i: (layer, i, 0, 0))
    in_specs = [pl.BlockSpec(memory_space=pltpu.SMEM), sblk, blk3(SUBLANES, LANES), blk3(4, LANES), blk3(4, LANES)]
    args = [e_flat, s_all, v3, k3, q3]
    aliases = {}
    if prev is not None:
        in_specs.append(pl.BlockSpec(memory_space=pl.ANY))
        args.append(prev)
        aliases = {len(args) - 1: 0}
        oblk = sblk
    else:
        oblk = pl.BlockSpec((depth, bb, STATE_ROWS, LANES), lambda i: (0, i, 0, 0))
    return pl.pallas_call(
        functools.partial(_state_step_kernel, bb=bb, layer=layer, fill=prev is None),
        grid=(n // bb,),
        in_specs=in_specs,
        out_specs=[oblk, blk3(SUBLANES, LANES)],
        out_shape=[jax.ShapeDtypeStruct((depth, n, STATE_ROWS, LANES), F32),
                   jax.ShapeDtypeStruct((n, SUBLANES, LANES), F32)],
        input_output_aliases=aliases,
        compiler_params=_cparams(("parallel",)),
        name="state_step",
    )(*args)


def _sample_post_kernel(ys_ref, xs_ref, z_ref, yr_ref, g_ref, dfull_ref, nw_ref, os_ref, or_ref):
    os_ref[...] = _ssd_gate_norm(ys_ref[...], xs_ref[...], z_ref[...].astype(F32), dfull_ref[...],
                                 nw_ref[...]).astype(os_ref.dtype)
    or_ref[...] = _ret_gate_norm(yr_ref[...], g_ref[...].astype(F32)).astype(or_ref.dtype)


def _sample_post(y_ssd, xs, proj2, y_ret, dfull, nw):
    n = y_ssd.shape[0]
    full = lambda s: pl.BlockSpec(s, lambda i: (0, 0))
    col = lambda off: pl.BlockSpec((n, D_MODEL), lambda i: (0, off // D_MODEL))
    return pl.pallas_call(
        _sample_post_kernel,
        grid=(1,),
        in_specs=[full((n, D_MODEL)), full((n, D_MODEL)), col(OFF_Z), full((n, D_MODEL)), col(OFF_G),
                  full((1, D_MODEL)), full((1, D_MODEL))],
        out_specs=[full((n, D_MODEL)), full((n, D_MODEL))],
        out_shape=[jax.ShapeDtypeStruct((n, D_MODEL), BF16)] * 2,
        compiler_params=_cparams(("arbitrary",)),
        name="sample_post",
    )(y_ssd, xs, proj2, y_ret, proj2, dfull, nw)


def _merge_kernel(s5_ref, ssd_ref, ret_ref, ga_ref, gb_ref, gc_ref, h_ref, ws5_ref, wssd_ref, wret_ref, wout_ref,
                  o_ref):
    glu = _dot(s5_ref[...], ws5_ref[...])
    y_a = glu[:, :D_MODEL] * _sigmoid(glu[:, D_MODEL:])
    y_b = _dot(ssd_ref[...], wssd_ref[...])
    y_c = _dot(ret_ref[...], wret_ref[...])
    merged = (_sigmoid(ga_ref[...].astype(F32)) * y_a + _sigmoid(gb_ref[...].astype(F32)) * y_b
              + _sigmoid(gc_ref[...].astype(F32)) * y_c)
    o_ref[...] = h_ref[...] + _dot(merged.astype(BF16), wout_ref[...])


def _merge(s5_g, ssd_n, ret_n, proj, h3, ws5, wssd, wret, wout, layer, tm):
    nb, L, _ = h3.shape
    row3 = lambda off: pl.BlockSpec((None, tm, D_MODEL), lambda b, i: (b, i, off // D_MODEL))
    wfull = lambda s: pl.BlockSpec((None,) + s, lambda b, i: (layer, 0, 0), pipeline_mode=pl.Buffered(1))
    return pl.pallas_call(
        _merge_kernel,
        grid=(nb, L // tm),
        in_specs=[row3(0), row3(0), row3(0),
                  row3(OFF_GATES), row3(OFF_GATES + D_MODEL), row3(OFF_GATES + 2 * D_MODEL), row3(0),
                  wfull((D_MODEL, 2 * D_MODEL)), wfull((D_MODEL, D_MODEL)), wfull((D_MODEL, D_MODEL)),
                  wfull((D_MODEL, D_MODEL))],
        out_specs=row3(0),
        out_shape=jax.ShapeDtypeStruct((nb, L, D_MODEL), F32),
        compiler_params=_cparams(("parallel", "parallel")),
        name="merge",
    )(s5_g, ssd_n, ret_n, proj, proj, proj, h3, ws5, wssd, wret, wout)


def _mlp_kernel(x_ref, nw_ref, wup_ref, wdn_ref, fw_ref, o_ref, xn_ref, acc_ref, *, final_norm):
    k = pl.program_id(1)
    nk = pl.num_programs(1)

    @pl.when(k == 0)
    def _():
        x = x_ref[...]
        xn_ref[...] = _rmsnorm(x, nw_ref[...]).astype(BF16)
        acc_ref[...] = x

    hid = jnp.maximum(_dot(xn_ref[...], wup_ref[...]), 0.0)
    acc_ref[...] += _dot((hid * hid).astype(BF16), wdn_ref[...])

    @pl.when(k == nk - 1)
    def _():
        h = acc_ref[...]
        o_ref[...] = _rmsnorm(h, fw_ref[...]) if final_norm else h


def _mlp(h2, nw, wup, wdn, fw, layer, tm, tf, final_norm):
    rows = h2.shape[0]
    return pl.pallas_call(
        functools.partial(_mlp_kernel, final_norm=final_norm),
        grid=(rows // tm, D_FF // tf),
        in_specs=[pl.BlockSpec((tm, D_MODEL), lambda i, k: (i, 0)),
                  pl.BlockSpec((1, D_MODEL), lambda i, k: (0, 0)),
                  pl.BlockSpec((None, D_MODEL, tf), lambda i, k: (layer, 0, k)),
                  pl.BlockSpec((None, tf, D_MODEL), lambda i, k: (layer, k, 0)),
                  pl.BlockSpec((1, D_MODEL), lambda i, k: (0, 0))],
        out_specs=pl.BlockSpec((tm, D_MODEL), lambda i, k: (i, 0)),
        out_shape=jax.ShapeDtypeStruct((rows, D_MODEL), F32),
        scratch_shapes=[pltpu.VMEM((tm, D_MODEL), BF16), pltpu.VMEM((tm, D_MODEL), F32)],
        compiler_params=_cparams(("parallel", "arbitrary")),
        name="mlp",
    )(h2, nw, wup, wdn, fw)


def _prep_layer(p):
    out = {}
    row = lambda a: a.reshape(1, -1).astype(F32)
    padrow = lambda a: jnp.pad(a.astype(F32), (0, LANES - a.shape[0])).reshape(1, LANES)
    rep = lambda a: jnp.repeat(a.astype(F32), SSD_HEAD_DIM).reshape(1, D_MODEL)
    out['norm1_w'] = row(p['norm1_w'])
    out['norm2_w'] = row(p['norm2_w'])
    out['conv_w'] = p['ssd_conv_w'].astype(F32)
    out['conv_b'] = row(p['ssd_conv_b'])
    out['b16'] = padrow(p['ssd_dt_bias'])
    out['al16'] = padrow(p['ssd_a_log'])
    out['dfull'] = rep(p['ssd_d'])
    out['ssd_nw'] = row(p['ssd_norm_w'])
    out['s5_d'] = row(p['s5_d'])
    abr, abi, bbr, bbi = _s5_discretise(p['s5_a_re'].astype(F32), p['s5_a_im'].astype(F32),
                                        p['s5_log_dt'].astype(F32), p['s5_b_re'].astype(F32),
                                        p['s5_b_im'].astype(F32))
    out['s5'] = _s5_weights(abr, abi, bbr, bbi, p['s5_c_re'].astype(F32), p['s5_c_im'].astype(F32))
    return out


def _head_expand():
    e = (jnp.arange(LANES)[:, None] == (jnp.arange(D_MODEL)[None, :] // SSD_HEAD_DIM))
    return e.astype(BF16)


def _tile(n, pref):
    return pref if n % pref == 0 else n


def _prompt_trunk(x, layers, final_w):
    nb, L, _ = x.shape
    eh = _head_expand()
    _, lm, gp, we, gq = _ret_consts(CHUNK)
    cos2, sin2 = _rope_tables(jnp.arange(L, dtype=jnp.int32))
    zeros_state = jnp.zeros((nb, S5_NSTATE), F32)
    h = x
    outs = {k: [] for k in ('re', 'im', 'ssd', 'conv', 'ret')}
    for li, lp in enumerate(layers):
        proj, dt = _proj_main(h, lp['norm1_w'], *lp['proj_w'], _tile(L, PROJ_TM))
        wbr, wbi, abr, abi, wcr, wci = lp['s5']
        y_s5, hr, hi = _s5_scan(proj, wbr, wbi, abr, abi, wcr, wci, lp['s5_d'],
                                zeros_state, zeros_state, S5_STEPS)
        ssd_n, ssd_s, ret_n, ret_s = _mix_prompt(proj, dt, lp['conv_w'], lp['conv_b'], lp['b16'], lp['al16'],
                                                 lp['dfull'], lp['ssd_nw'], eh, cos2, sin2, lm, gp, we, gq)
        h = _merge(y_s5, ssd_n, ret_n, proj, h,
                   *lp['merge_w'], li, _tile(L, MERGE_TM))
        last = li == len(layers) - 1
        h = _mlp(h.reshape(nb * L, D_MODEL), lp['norm2_w'], *lp['mlp_w'], final_w, li,
                 _tile(nb * L, MLP_TM), MLP_TF, last).reshape(nb, L, D_MODEL)
        outs['re'].append(hr.reshape(nb, S5_GROUPS, S5_STATE))
        outs['im'].append(hi.reshape(nb, S5_GROUPS, S5_STATE))
        outs['ssd'].append(ssd_s.reshape(nb, SSD_HEADS, SSD_HEAD_DIM, SSD_STATE))
        outs['conv'].append(proj[:, L - (SSD_CONV - 1):, OFF_XS:OFF_XS + SSD_CONV_DIM].astype(F32))
        outs['ret'].append(ret_s.reshape(nb, RET_HEADS, RET_V_DIM, RET_QK_DIM))
    return (h,) + tuple(jnp.stack(outs[k]) for k in ('re', 'im', 'ssd', 'conv', 'ret'))


def _sample_trunk(x, states, layers, final_w):
    n = x.shape[0]
    st_re, st_im, st_ssd, st_conv, st_ret = states
    eh = _head_expand()
    log_gamma, _, _, _, _ = _ret_consts(1)
    cos2, sin2 = _rope_tables(PAST_LEN + jnp.arange(1, dtype=jnp.int32))
    e_ret = jnp.broadcast_to(jnp.repeat(jnp.exp(log_gamma), RET_V_DIM // STATE_BLK)[None, :],
                             (n, STATE_ROWS // STATE_BLK)).reshape(-1)
    h = x.reshape(1, n, D_MODEL)
    depth = len(layers)
    ssd_all = st_ssd.reshape(depth, n, STATE_ROWS, SSD_STATE).astype(F32)
    ret_all = st_ret.reshape(depth, n, STATE_ROWS, RET_QK_DIM).astype(F32)
    ssd_new = ret_new = None
    outs = {k: [] for k in ('re', 'im', 'conv')}
    for li, lp in enumerate(layers):
        proj, dt = _proj_main(h, lp['norm1_w'], *lp['proj_w'], n)
        wbr, wbi, abr, abi, wcr, wci = lp['s5']
        y_s5, hr, hi = _s5_scan(proj, wbr, wbi, abr, abi, wcr, wci, lp['s5_d'],
                                st_re[li].reshape(n, S5_NSTATE).astype(F32),
                                st_im[li].reshape(n, S5_NSTATE).astype(F32), 1)
        proj2, dt2 = proj.reshape(n, N_MAIN), dt.reshape(n, LANES)
        conv2 = st_conv[li].reshape(n, (SSD_CONV - 1) * SSD_CONV_DIM).astype(F32)
        vv, e16, bs, cs, xs, conv_new, qr, kr = _sample_pre(
            proj2, dt2, conv2, lp['conv_w'], lp['conv_b'], lp['b16'], lp['al16'], eh, cos2, sin2)
        ssd_new, y_ssd = _state_step(
            e16[:, :SSD_HEADS].reshape(-1), ssd_all, li,
            vv.reshape(n, SUBLANES, LANES), bs.reshape(n, SSD_GROUPS, SSD_STATE),
            cs.reshape(n, SSD_GROUPS, SSD_STATE), prev=ssd_new)
        vret = proj2[:, OFF_V:OFF_V + D_MODEL].astype(F32)
        ret_new, y_ret = _state_step(
            e_ret, ret_all, li,
            vret.reshape(n, SUBLANES, LANES), kr.reshape(n, RET_HEADS, RET_QK_DIM),
            qr.reshape(n, RET_HEADS, RET_QK_DIM), prev=ret_new)
        ssd_n, ret_n = _sample_post(y_ssd.reshape(n, D_MODEL), xs, proj2, y_ret.reshape(n, D_MODEL),
                                    lp['dfull'], lp['ssd_nw'])
        h = _merge(y_s5, ssd_n.reshape(1, n, D_MODEL), ret_n.reshape(1, n, D_MODEL), proj, h,
                   *lp['merge_w'], li, n)
        last = li == len(layers) - 1
        h = _mlp(h.reshape(n, D_MODEL), lp['norm2_w'], *lp['mlp_w'], final_w, li,
                 n, MLP_TF, last).reshape(1, n, D_MODEL)
        outs['re'].append(hr.reshape(n, S5_GROUPS, S5_STATE))
        outs['im'].append(hi.reshape(n, S5_GROUPS, S5_STATE))
        outs['conv'].append(conv_new.reshape(n, SSD_CONV - 1, SSD_CONV_DIM))
    return (h.reshape(n, 1, D_MODEL), jnp.stack(outs['re']), jnp.stack(outs['im']),
            ssd_new.reshape(depth, n, SSD_HEADS, SSD_HEAD_DIM, SSD_STATE), jnp.stack(outs['conv']),
            ret_new.reshape(depth, n, RET_HEADS, RET_V_DIM, RET_QK_DIM))


def kernel(x_prompt, x_sample, state_s5_re, state_s5_im, state_ssd, state_conv, state_ret, norm1_w, w_in, s5_a_re, s5_a_im, s5_log_dt, s5_b_re, s5_b_im, s5_c_re, s5_c_im, s5_d, w_s5_out, ssd_conv_w, ssd_conv_b, ssd_dt_bias, ssd_a_log, ssd_d, ssd_norm_w, w_ssd_out, w_ret_out, w_out, norm2_w, w_mlp_up, w_mlp_down, final_norm_w):
    params = dict(norm1_w=norm1_w, w_in=w_in, s5_a_re=s5_a_re, s5_a_im=s5_a_im, s5_log_dt=s5_log_dt,
                  s5_b_re=s5_b_re, s5_b_im=s5_b_im, s5_c_re=s5_c_re, s5_c_im=s5_c_im, s5_d=s5_d,
                  w_s5_out=w_s5_out, ssd_conv_w=ssd_conv_w, ssd_conv_b=ssd_conv_b, ssd_dt_bias=ssd_dt_bias,
                  ssd_a_log=ssd_a_log, ssd_d=ssd_d, ssd_norm_w=ssd_norm_w, w_ssd_out=w_ssd_out,
                  w_ret_out=w_ret_out, w_out=w_out, norm2_w=norm2_w, w_mlp_up=w_mlp_up, w_mlp_down=w_mlp_down)
    depth = w_in.shape[0]
    big = ('w_in', 'w_s5_out', 'w_ssd_out', 'w_ret_out', 'w_out', 'w_mlp_up', 'w_mlp_down')
    layers = [_prep_layer({name: arr[l] for name, arr in params.items() if name not in big}) for l in range(depth)]
    merge_w = tuple(params[name].astype(BF16) for name in ('w_s5_out', 'w_ssd_out', 'w_ret_out', 'w_out'))
    mlp_w = tuple(params[name].astype(BF16) for name in ('w_mlp_up', 'w_mlp_down'))
    w_a, w_b, w_dt = _proj_weight_slabs(w_in.astype(F32))
    for l, lp in enumerate(layers):
        lp['proj_w'] = (w_a, w_b, w_dt, l)
        lp['merge_w'] = merge_w
        lp['mlp_w'] = mlp_w
    final_w = final_norm_w.reshape(1, D_MODEL).astype(F32)
    y_p, re_p, im_p, ssd_p, conv_p, ret_p = _prompt_trunk(x_prompt.astype(F32), layers, final_w)
    y_s, re_s, im_s, ssd_s, conv_s, ret_s = _sample_trunk(
        x_sample.astype(F32), (state_s5_re, state_s5_im, state_ssd, state_conv, state_ret), layers, final_w)
    return (y_p, y_s, re_p, im_p, ssd_p, conv_p, ret_p, re_s, im_s, ssd_s, conv_s, ret_s)
```

```python
import functools
import math

import jax
import jax.numpy as jnp
from jax import lax
from jax.experimental import pallas as pl
from jax.experimental.pallas import tpu as pltpu

F32 = jnp.float32
BF16 = jnp.bfloat16

D_MODEL = 1024
S5_GROUP = 16
S5_STATE = 64
S5_GROUPS = D_MODEL // S5_GROUP
S5_NSTATE = S5_GROUPS * S5_STATE
S5_SUPER = 8
S5_TIME_SPLIT = 2
SSD_HEADS = 16
SSD_HEAD_DIM = 64
SSD_GROUPS = 4
SSD_STATE = 128
SSD_CONV = 4
SSD_CONV_DIM = D_MODEL + 2 * SSD_GROUPS * SSD_STATE
RET_HEADS = 4
RET_QK_DIM = 128
RET_V_DIM = 256
ROPE_BASE = 10000.0
D_FF = 4 * D_MODEL
CHUNK = 128
EPS = 1e-6
PAST_LEN = 16384
LANES = 128
SUBLANES = 8
VMEM_LIMIT = 52 * 1024 * 1024

OFF_U, OFF_Z, OFF_XS, OFF_BC, OFF_Q, OFF_K, OFF_V, OFF_G, OFF_GATES = (
    0, 1024, 2048, 3072, 4096, 4608, 5120, 6144, 7168)
N_MAIN = 10240
SLAB_A_COLS = OFF_Q
SLAB_B_START = 2 * D_MODEL + SSD_CONV_DIM + SSD_HEADS

PROJ_TM, PROJ_TN = 1024, 2048
MERGE_TM = 512
MLP_TM, MLP_TF = 1024, 1024
S5_STEPS = 64
MIX_NSEQ = 2
STATE_TOKENS = 8


def _cparams(sem):
    return pltpu.CompilerParams(dimension_semantics=sem, vmem_limit_bytes=VMEM_LIMIT)


def _sigmoid(x):
    return 1.0 / (1.0 + jnp.exp(-x))


def _silu(x):
    return x * _sigmoid(x)


def _softplus(x):
    return jnp.maximum(x, 0.0) + jnp.log1p(jnp.exp(-jnp.abs(x)))


def _gelu_tanh(x):
    c = math.sqrt(2.0 / math.pi)
    half = 0.5 * x
    return half + half * jnp.tanh(x * (c + (c * 0.044715) * (x * x)))


def _rmsnorm(x, w):
    return x * lax.rsqrt(jnp.mean(x * x, axis=-1, keepdims=True) + EPS) * w


def _cumsum_rows(x):
    n = x.shape[0]
    row = lax.broadcasted_iota(jnp.int32, x.shape, 0)
    k = 1
    while k < n:
        x = x + jnp.where(row >= k, pltpu.roll(x, k, 0), 0.0)
        k *= 2
    return x


def _group_rms(y, width):
    outs = []
    for s in range(0, y.shape[-1], width):
        yg = y[:, s:s + width]
        outs.append(yg * lax.rsqrt(jnp.mean(yg * yg, axis=-1, keepdims=True) + EPS))
    return jnp.concatenate(outs, axis=-1)


def _dot(a, b):
    return jnp.dot(a, b, preferred_element_type=F32)


def _split_dot(x, e):
    hi = x.astype(BF16)
    lo = (x - hi.astype(F32)).astype(BF16)
    return _dot(hi, e) + _dot(lo, e)


def _proj_kernel(x_ref, nw_ref, wa_ref, wb_ref, wdt_ref, o_ref, odt_ref, xn_ref, *, na):
    j = pl.program_id(2)

    @pl.when(j == 0)
    def _():
        xn = _rmsnorm(x_ref[...], nw_ref[...]).astype(BF16)
        xn_ref[...] = xn
        odt_ref[...] = _dot_nt(xn, wdt_ref[...])

    @pl.when(j < na)
    def _():
        o_ref[...] = _dot_nt(xn_ref[...], wa_ref[...]).astype(o_ref.dtype)

    @pl.when(j >= na)
    def _():
        o_ref[...] = _dot_nt(xn_ref[...], wb_ref[...]).astype(o_ref.dtype)


DT_START = SLAB_B_START - SSD_HEADS


def _proj_weight_slabs(w_in):
    wt = jnp.swapaxes(w_in, 1, 2)
    w_a = wt[:, :SLAB_A_COLS].astype(BF16)
    w_b = wt[:, SLAB_B_START:].astype(BF16)
    w_dt = jnp.pad(wt[:, DT_START:SLAB_B_START], ((0, 0), (0, LANES - SSD_HEADS), (0, 0))).astype(BF16)
    return w_a, w_b, w_dt


def _dot_nt(a, b):
    return lax.dot_general(a, b, (((1,), (1,)), ((), ())), preferred_element_type=F32)


def _proj_main(x3, nw, w_a, w_b, wdt, layer, tm):
    nb, L, _ = x3.shape
    tn = PROJ_TN
    na = SLAB_A_COLS // tn
    return pl.pallas_call(
        functools.partial(_proj_kernel, na=na),
        grid=(nb, L // tm, N_MAIN // tn),
        in_specs=[pl.BlockSpec((None, tm, D_MODEL), lambda b, i, j: (b, i, 0)),
                  pl.BlockSpec((1, D_MODEL), lambda b, i, j: (0, 0)),
                  pl.BlockSpec((None, tn, D_MODEL), lambda b, i, j: (layer, jnp.minimum(j, na - 1), 0)),
                  pl.BlockSpec((None, tn, D_MODEL), lambda b, i, j: (layer, jnp.maximum(j - na, 0), 0)),
                  pl.BlockSpec((None, LANES, D_MODEL), lambda b, i, j: (layer, 0, 0))],
        out_specs=[pl.BlockSpec((None, tm, tn), lambda b, i, j: (b, i, j)),
                   pl.BlockSpec((None, tm, LANES), lambda b, i, j: (b, i, 0))],
        out_shape=[jax.ShapeDtypeStruct((nb, L, N_MAIN), BF16), jax.ShapeDtypeStruct((nb, L, LANES), F32)],
        scratch_shapes=[pltpu.VMEM((tm, D_MODEL), BF16)],
        compiler_params=_cparams(("parallel", "parallel", "arbitrary")),
        name="proj_main",
    )(x3, nw, w_a, w_b, wdt)


def _s5_disc_kernel(are_ref, aim_ref, ldt_ref, bre_ref, bim_ref, abr_ref, abi_ref, bbr_ref, bbi_ref):
    ar, ai = are_ref[...], aim_ref[...]
    dt = jnp.exp(ldt_ref[...])
    mag = jnp.exp(ar * dt)
    abr = mag * jnp.cos(ai * dt)
    abi = mag * jnp.sin(ai * dt)
    abr_ref[...] = abr
    abi_ref[...] = abi
    nr, ni = abr - 1.0, abi
    den = 1.0 / (ar * ar + ai * ai)
    cr = (nr * ar + ni * ai) * den
    ci = (ni * ar - nr * ai) * den
    br, bi = bre_ref[...], bim_ref[...]
    bbr_ref[...] = cr * br - ci * bi
    bbi_ref[...] = cr * bi + ci * br


def _s5_discretise(a_re, a_im, log_dt, b_re, b_im):
    g, p = a_re.shape
    gs = b_re.shape[-1]
    full3 = lambda s: pl.BlockSpec(s, lambda: (0, 0, 0))
    return pl.pallas_call(
        _s5_disc_kernel,
        in_specs=[full3((g, 1, p)), full3((g, 1, p)), full3((g, 1, 1)), full3((g, gs, p)), full3((g, gs, p))],
        out_specs=[full3((g, 1, p)), full3((g, 1, p)), full3((g, gs, p)), full3((g, gs, p))],
        out_shape=[jax.ShapeDtypeStruct((g, 1, p), F32)] * 2 + [jax.ShapeDtypeStruct((g, gs, p), F32)] * 2,
        name="s5_discretise",
    )(a_re.reshape(g, 1, p), a_im.reshape(g, 1, p), log_dt.reshape(g, 1, 1),
      jnp.swapaxes(b_re, 1, 2), jnp.swapaxes(b_im, 1, 2))


def _s5_kernel(u_ref, wbr_ref, wbi_ref, ar_ref, ai_ref, wcr_ref, wci_ref, d_ref, h0r_ref, h0i_ref,
               y_ref, hr_ref, hi_ref, bur, bui, *, nbatch, steps, lane_w):
    @pl.when(pl.program_id(0) == 0)
    def _():
        hr_ref[...] = h0r_ref[...]
        hi_ref[...] = h0i_ref[...]

    cw = S5_SUPER * S5_GROUP
    sw = S5_SUPER * S5_STATE
    batch_major = steps > 1
    u = u_ref[...].astype(F32)
    if batch_major:
        u = jnp.swapaxes(u, 0, 1).reshape(steps * nbatch, D_MODEL)
    ub = u.astype(BF16)
    nsub = S5_TIME_SPLIT if steps % (2 * S5_TIME_SPLIT) == 0 else 1
    sub_steps = steps // nsub
    nsup = S5_GROUPS // S5_SUPER
    for sb in range(nsub):
        t0 = sb * sub_steps
        rs = slice(t0 * nbatch, (t0 + sub_steps) * nbatch)
        for j in range(nsup):
            uj = ub[rs, j * cw:(j + 1) * cw]
            bur[rs, j * sw:(j + 1) * sw] = _dot(uj, wbr_ref[j])
            bui[rs, j * sw:(j + 1) * sw] = _dot(uj, wbi_ref[j])

        for j in range(nsup):
            for lc in range(sw // lane_w):
                sl = slice(j * sw + lc * lane_w, j * sw + (lc + 1) * lane_w)
                are = jnp.broadcast_to(ar_ref[:, sl], (nbatch, lane_w))
                aim = jnp.broadcast_to(ai_ref[:, sl], (nbatch, lane_w))
                hr, hi = hr_ref[:, sl], hi_ref[:, sl]
                for t in range(t0, t0 + sub_steps):
                    rows = slice(t * nbatch, (t + 1) * nbatch)
                    hr, hi = are * hr - aim * hi + bur[rows, sl], are * hi + aim * hr + bui[rows, sl]
                    bur[rows, sl] = hr
                    bui[rows, sl] = hi
                hr_ref[:, sl] = hr
                hi_ref[:, sl] = hi

        for j in range(nsup):
            ch = slice(j * cw, (j + 1) * cw)
            yj = (_dot(bur[rs, j * sw:(j + 1) * sw].astype(BF16), wcr_ref[j])
                  + _dot(bui[rs, j * sw:(j + 1) * sw].astype(BF16), wci_ref[j]))
            yj = _gelu_tanh(yj + d_ref[:, ch] * u[rs, ch])
            if batch_major:
                y_ref[:, t0:t0 + sub_steps, ch] = jnp.swapaxes(
                    yj.reshape(sub_steps, nbatch, cw), 0, 1).astype(y_ref.dtype)
            else:
                y_ref[:, ch] = yj.astype(y_ref.dtype)


def _s5_scan(proj, wbr, wbi, abr, abi, wcr, wci, d, h0r, h0i, steps):
    nb, L, _ = proj.shape
    nsup = S5_GROUPS // S5_SUPER
    cw, sw = S5_SUPER * S5_GROUP, S5_SUPER * S5_STATE
    c2 = lambda s: pl.BlockSpec(s, lambda c: (0, 0))
    c3 = lambda s: pl.BlockSpec(s, lambda c: (0, 0, 0))
    if steps > 1:
        nbatch, grid = nb, L // steps
        u_spec = pl.BlockSpec((nb, steps, D_MODEL), lambda c: (0, c, OFF_U // D_MODEL))
        y_spec = pl.BlockSpec((nb, steps, D_MODEL), lambda c: (0, c, 0))
    else:
        nbatch, grid = L, 1
        u_spec = pl.BlockSpec((None, L, D_MODEL), lambda c: (0, 0, OFF_U // D_MODEL))
        y_spec = pl.BlockSpec((None, L, D_MODEL), lambda c: (0, 0, 0))
    blk = nbatch * steps
    lane_w = max(LANES, S5_NSTATE // nbatch)
    kern = functools.partial(_s5_kernel, nbatch=nbatch, steps=steps, lane_w=lane_w)
    return pl.pallas_call(
        kern,
        grid=(grid,),
        in_specs=[u_spec,
                  c3((nsup, cw, sw)), c3((nsup, cw, sw)),
                  c2((1, S5_NSTATE)), c2((1, S5_NSTATE)),
                  c3((nsup, sw, cw)), c3((nsup, sw, cw)),
                  c2((1, D_MODEL)),
                  c2((nbatch, S5_NSTATE)), c2((nbatch, S5_NSTATE))],
        out_specs=[y_spec, c2((nbatch, S5_NSTATE)), c2((nbatch, S5_NSTATE))],
        out_shape=[jax.ShapeDtypeStruct((nb, L, D_MODEL), BF16),
                   jax.ShapeDtypeStruct((nbatch, S5_NSTATE), F32),
                   jax.ShapeDtypeStruct((nbatch, S5_NSTATE), F32)],
        scratch_shapes=[pltpu.VMEM((blk, S5_NSTATE), F32), pltpu.VMEM((blk, S5_NSTATE), F32)],
        compiler_params=_cparams(("arbitrary",)),
        name="s5_scan",
    )(proj, wbr, wbi, abr, abi, wcr, wci, d, h0r, h0i)


def _s5_weights(abr, abi, bbr, bbi, c_re, c_im):
    nsup = S5_GROUPS // S5_SUPER
    eye = jnp.eye(S5_SUPER, dtype=F32)

    def bexp(b):
        b = b.reshape(nsup, S5_SUPER, S5_GROUP, S5_STATE)
        return (b[:, :, :, None, :] * eye[None, :, None, :, None]).reshape(
            nsup, S5_SUPER * S5_GROUP, S5_SUPER * S5_STATE).astype(BF16)

    def cexp(c):
        c = jnp.swapaxes(c, 1, 2).reshape(nsup, S5_SUPER, S5_STATE, S5_GROUP)
        return (c[:, :, :, None, :] * eye[None, :, None, :, None]).reshape(
            nsup, S5_SUPER * S5_STATE, S5_SUPER * S5_GROUP).astype(BF16)

    return (bexp(bbr), bexp(bbi), abr.reshape(1, S5_NSTATE), abi.reshape(1, S5_NSTATE),
            cexp(c_re), cexp(-c_im))


def _ssd_gate_norm(y, xs, z, dfull, nw):
    y = (y + dfull * xs) * _silu(z)
    return _group_rms(y, D_MODEL // SSD_GROUPS) * nw


CONV_PAD = 16
CONV_ROWS = 2 * CHUNK


def _conv_shift_mats():
    i = jnp.arange(CHUNK)[:, None]
    j = jnp.arange(CONV_ROWS)[None, :]
    return jnp.stack([(j == CONV_PAD + i - s) for s in range(1, SSD_CONV)]).astype(BF16)


def _ssd_chunk_seq(xs_ref, bc_ref, z_ref, dt_ref, cw_ref, cb_ref, b16_ref, al16_ref,
                   dfull_ref, nw_ref, eh_ref, sh_ref, o_ref, xpad, st):
    q = CHUNK
    pad = CONV_PAD
    xpad[0:pad, :] = xpad[q:q + pad, :]

    xpad[pad:pad + q, :D_MODEL] = xs_ref[...]
    xpad[pad:pad + q, D_MODEL:] = bc_ref[...]
    staged = xpad[...]
    acc = cb_ref[...] + xpad[pad:pad + q, :].astype(F32) * cw_ref[SSD_CONV - 1:SSD_CONV, :]
    for s in range(1, SSD_CONV):
        acc = acc + _dot(sh_ref[s - 1], staged) * cw_ref[SSD_CONV - 1 - s:SSD_CONV - s, :]
    xc = _silu(acc)
    xs = xc[:, :D_MODEL]
    nbc = SSD_GROUPS * SSD_STATE
    bs = xc[:, D_MODEL:D_MODEL + nbc]
    cs = xc[:, D_MODEL + nbc:]

    delta16 = _softplus(dt_ref[...] + b16_ref[...])
    acum16 = _cumsum_rows(delta16 * (-jnp.exp(al16_ref[...])))
    acum_log2 = acum16 * math.log2(math.e)
    acum_log2_t = acum_log2.T
    delta_t = delta16.T
    last16 = acum16[q - 1:q, :]
    ea_f = _split_dot(jnp.exp(acum16), eh_ref[...])
    dw_f = _split_dot(delta16 * jnp.exp(last16 - acum16), eh_ref[...])
    xw = (xs * dw_f).astype(BF16)

    ri = lax.broadcasted_iota(jnp.int32, (q, q), 0)
    ci = lax.broadcasted_iota(jnp.int32, (q, q), 1)
    causal = ri >= ci
    hw = D_MODEL // SSD_GROUPS
    hpg = SSD_HEADS // SSD_GROUPS
    xsb = xs.astype(BF16)
    ys = []
    for g in range(SSD_GROUPS):
        bg = bs[:, g * SSD_STATE:(g + 1) * SSD_STATE]
        cg = cs[:, g * SSD_STATE:(g + 1) * SSD_STATE].astype(BF16)
        bgt = bg.T.astype(BF16)
        scores = _dot(cg, bgt)
        sg = st[:, g * hw:(g + 1) * hw]
        y_inter = _dot(cg, sg.astype(BF16)) * ea_f[:, g * hw:(g + 1) * hw]
        for r in range(hpg):
            hd = g * hpg + r
            seg = acum_log2[:, hd:hd + 1] - acum_log2_t[hd:hd + 1, :]
            lm = jnp.exp2(jnp.where(causal, seg, -jnp.inf)) * delta_t[hd:hd + 1, :]
            p = (scores * lm).astype(BF16)
            ys.append(_dot(p, xsb[:, hd * SSD_HEAD_DIM:(hd + 1) * SSD_HEAD_DIM])
                      + y_inter[:, r * SSD_HEAD_DIM:(r + 1) * SSD_HEAD_DIM])
        st[:, g * hw:(g + 1) * hw] = (ea_f[q - 1:q, g * hw:(g + 1) * hw] * sg
                                      + _dot(bgt, xw[:, g * hw:(g + 1) * hw]))
    y = jnp.concatenate(ys, axis=-1)
    o_ref[...] = _ssd_gate_norm(y, xs, z_ref[...].astype(F32), dfull_ref[...], nw_ref[...]).astype(o_ref.dtype)


def _rotary(x, cos2, sin2):
    outs = []
    for h in range(RET_HEADS):
        xh = x[:, h * RET_QK_DIM:(h + 1) * RET_QK_DIM]
        outs.append(xh * cos2 + pltpu.roll(xh, RET_QK_DIM // 2, 1) * sin2)
    return outs


def _ret_gate_norm(y, g):
    return _group_rms(y, RET_V_DIM) * _silu(g)


def _ret_chunk_seq(q_ref, k_ref, v_ref, g_ref, cos2, sin2, lm_ref, gp_ref, we_ref, gq_ref, o_ref, st):
    qr = _rotary(q_ref[...].astype(F32), cos2, sin2)
    kr = _rotary(k_ref[...].astype(F32), cos2, sin2)
    vb = v_ref[...]
    vw = (vb.astype(F32) * we_ref[...]).astype(BF16)
    ys = []
    for h in range(RET_HEADS):
        sl = slice(h * RET_V_DIM, (h + 1) * RET_V_DIM)
        qh = qr[h].astype(BF16)
        kht = (kr[h] * (RET_QK_DIM ** -0.5)).T.astype(BF16)
        p = (_dot(qh, kht) * lm_ref[h]).astype(BF16)
        sh = st[:, sl]
        ys.append(_dot(p, vb[:, sl]) + _dot(qh, sh.astype(BF16)) * gp_ref[:, sl])
        st[:, sl] = gq_ref[:, sl] * sh + _dot(kht, vw[:, sl])
    y = jnp.concatenate(ys, axis=-1)
    o_ref[...] = _ret_gate_norm(y, g_ref[...].astype(F32)).astype(o_ref.dtype)


def _mix_chunk_kernel(xs_ref, bc_ref, z_ref, dt_ref, cw_ref, cb_ref, b16_ref, al16_ref, dfull_ref, nw_ref, eh_ref,
                      sh_ref, q_ref, k_ref, v_ref, g_ref, cos_ref, sin_ref, lm_ref, gp_ref, we_ref, gq_ref,
                      so_ref, ss_ref, ro_ref, rs_ref, xpad, st_ssd, st_ret):
    c = pl.program_id(1)
    nc = pl.num_programs(1)
    nseq = xs_ref.shape[0]

    @pl.when(c == 0)
    def _():
        xpad[...] = jnp.zeros(xpad.shape, BF16)
        st_ssd[...] = jnp.zeros(st_ssd.shape, F32)
        st_ret[...] = jnp.zeros(st_ret.shape, F32)

    cos2, sin2 = cos_ref[...], sin_ref[...]
    for bi in range(nseq):
        _ssd_chunk_seq(xs_ref.at[bi], bc_ref.at[bi], z_ref.at[bi], dt_ref.at[bi], cw_ref, cb_ref, b16_ref,
                       al16_ref, dfull_ref, nw_ref, eh_ref, sh_ref, so_ref.at[bi], xpad.at[bi], st_ssd.at[bi])
        _ret_chunk_seq(q_ref.at[bi], k_ref.at[bi], v_ref.at[bi], g_ref.at[bi], cos2, sin2, lm_ref, gp_ref, we_ref,
                       gq_ref, ro_ref.at[bi], st_ret.at[bi])

    @pl.when(c == nc - 1)
    def _():
        for bi in range(nseq):
            for k in range(D_MODEL // LANES):
                ss_ref[bi, k * LANES:(k + 1) * LANES, :] = st_ssd[bi, :, k * LANES:(k + 1) * LANES].T
                rs_ref[bi, k * LANES:(k + 1) * LANES, :] = st_ret[bi, :, k * LANES:(k + 1) * LANES].T


def _ret_consts(q):
    hidx = jnp.arange(RET_HEADS, dtype=F32)
    log_gamma = jnp.log1p(-jnp.exp2(-5.0 - hidx))
    i = jnp.arange(q, dtype=F32)
    seg = (i[:, None] - i[None, :])[None] * log_gamma[:, None, None]
    lm = jnp.exp(jnp.where((i[:, None] >= i[None, :])[None], seg, -jnp.inf))
    rep = lambda t: jnp.repeat(t, RET_V_DIM, axis=-1)
    gp = rep(jnp.exp((i[:, None] + 1.0) * log_gamma[None, :]))
    we = rep(jnp.exp((q - 1.0 - i[:, None]) * log_gamma[None, :]))
    gq = rep(jnp.exp(q * log_gamma)[None, :])
    return log_gamma, lm, gp, we, gq


def _rope_tables(pos):
    half = RET_QK_DIM // 2
    inv = ROPE_BASE ** (-jnp.arange(half, dtype=F32) / half)
    ang = pos.astype(F32)[:, None] * inv[None]
    cos, sin = jnp.cos(ang), jnp.sin(ang)
    return jnp.concatenate([cos, cos], axis=-1), jnp.concatenate([-sin, sin], axis=-1)


def _mix_prompt(proj, dt, cw, cb, b16, al16, dfull, nw, eh, cos2, sin2, lm, gp, we, gq, nseq=MIX_NSEQ):
    nb, L, _ = proj.shape
    q = CHUNK
    qkw = RET_HEADS * RET_QK_DIM
    c2 = lambda s: pl.BlockSpec(s, lambda b, c: (0, 0))
    rows = lambda w, off: pl.BlockSpec((nseq, q, w), lambda b, c: (b, c, off // w))
    state = pl.BlockSpec((nseq, D_MODEL, LANES), lambda b, c: (b, 0, 0))
    return pl.pallas_call(
        _mix_chunk_kernel,
        grid=(nb // nseq, L // q),
        in_specs=[rows(D_MODEL, OFF_XS), rows(D_MODEL, OFF_BC), rows(D_MODEL, OFF_Z), rows(LANES, 0),
                  c2((SSD_CONV, SSD_CONV_DIM)), c2((1, SSD_CONV_DIM)),
                  c2((1, LANES)), c2((1, LANES)),
                  c2((1, D_MODEL)), c2((1, D_MODEL)),
                  c2((LANES, D_MODEL)),
                  pl.BlockSpec((SSD_CONV - 1, q, CONV_ROWS), lambda b, c: (0, 0, 0)),
                  rows(qkw, OFF_Q), rows(qkw, OFF_K), rows(D_MODEL, OFF_V), rows(D_MODEL, OFF_G),
                  pl.BlockSpec((q, RET_QK_DIM), lambda b, c: (c, 0)),
                  pl.BlockSpec((q, RET_QK_DIM), lambda b, c: (c, 0)),
                  pl.BlockSpec((RET_HEADS, q, q), lambda b, c: (0, 0, 0)),
                  c2((q, D_MODEL)), c2((q, D_MODEL)), c2((1, D_MODEL))],
        out_specs=[rows(D_MODEL, 0), state, rows(D_MODEL, 0), state],
        out_shape=[jax.ShapeDtypeStruct((nb, L, D_MODEL), BF16),
                   jax.ShapeDtypeStruct((nb, D_MODEL, SSD_STATE), F32),
                   jax.ShapeDtypeStruct((nb, L, D_MODEL), BF16),
                   jax.ShapeDtypeStruct((nb, D_MODEL, RET_QK_DIM), F32)],
        scratch_shapes=[pltpu.VMEM((nseq, CONV_ROWS, SSD_CONV_DIM), BF16),
                        pltpu.VMEM((nseq, SSD_STATE, D_MODEL), F32),
                        pltpu.VMEM((nseq, RET_QK_DIM, D_MODEL), F32)],
        compiler_params=_cparams(("parallel", "arbitrary")),
        name="mix_chunk",
    )(proj, proj, proj, dt, cw, cb, b16, al16, dfull, nw, eh, _conv_shift_mats(),
      proj, proj, proj, proj, cos2, sin2, lm, gp, we, gq)


def _sample_pre_kernel(xsin_ref, bcin_ref, dt_ref, q_ref, k_ref, conv_ref, cw_ref, cb_ref, b16_ref, al16_ref,
                       eh_ref, cos_ref, sin_ref,
                       vv_ref, e16_ref, bs_ref, cs_ref, xs_ref, convn_ref, qr_ref, kr_ref):
    cd = SSD_CONV_DIM
    x = jnp.concatenate([xsin_ref[...], bcin_ref[...]], axis=-1).astype(F32)
    acc = cb_ref[...] + x * cw_ref[SSD_CONV - 1:SSD_CONV, :]
    for tap in range(SSD_CONV - 1):
        acc = acc + conv_ref[:, tap * cd:(tap + 1) * cd] * cw_ref[tap:tap + 1, :]
    convn_ref[:, 0:(SSD_CONV - 2) * cd] = conv_ref[:, cd:(SSD_CONV - 1) * cd]
    convn_ref[:, (SSD_CONV - 2) * cd:] = x
    xc = _silu(acc)
    xs = xc[:, :D_MODEL]
    nbc = SSD_GROUPS * SSD_STATE
    xs_ref[...] = xs
    bs_ref[...] = xc[:, D_MODEL:D_MODEL + nbc]
    cs_ref[...] = xc[:, D_MODEL + nbc:]
    delta16 = _softplus(dt_ref[...] + b16_ref[...])
    e16_ref[...] = jnp.exp(delta16 * (-jnp.exp(al16_ref[...])))
    vv_ref[...] = xs * _split_dot(delta16, eh_ref[...])
    cos2, sin2 = cos_ref[...], sin_ref[...]
    qr_ref[...] = jnp.concatenate(_rotary(q_ref[...].astype(F32), cos2, sin2), axis=-1)
    kr_ref[...] = jnp.concatenate(_rotary(k_ref[...].astype(F32), cos2, sin2), axis=-1) * (RET_QK_DIM ** -0.5)


def _sample_pre(proj2, dt2, conv2, cw, cb, b16, al16, eh, cos2, sin2):
    n = proj2.shape[0]
    qkw = RET_HEADS * RET_QK_DIM
    nbc = SSD_GROUPS * SSD_STATE
    full = lambda s: pl.BlockSpec(s, lambda i: (0, 0))
    col = lambda w, off: pl.BlockSpec((n, w), lambda i: (0, off // w))
    shp = lambda w: jax.ShapeDtypeStruct((n, w), F32)
    return pl.pallas_call(
        _sample_pre_kernel,
        grid=(1,),
        in_specs=[col(D_MODEL, OFF_XS), col(D_MODEL, OFF_BC), full((n, LANES)), col(qkw, OFF_Q), col(qkw, OFF_K),
                  full((n, (SSD_CONV - 1) * SSD_CONV_DIM)),
                  full((SSD_CONV, SSD_CONV_DIM)), full((1, SSD_CONV_DIM)),
                  full((1, LANES)), full((1, LANES)),
                  full((LANES, D_MODEL)), full((1, RET_QK_DIM)), full((1, RET_QK_DIM))],
        out_specs=[full((n, D_MODEL)), full((n, LANES)), full((n, nbc)), full((n, nbc)), full((n, D_MODEL)),
                   full((n, (SSD_CONV - 1) * SSD_CONV_DIM)), full((n, qkw)), full((n, qkw))],
        out_shape=[shp(D_MODEL), shp(LANES), shp(nbc), shp(nbc), shp(D_MODEL),
                   shp((SSD_CONV - 1) * SSD_CONV_DIM), shp(qkw), shp(qkw)],
        compiler_params=_cparams(("arbitrary",)),
        name="sample_pre",
    )(proj2, proj2, dt2, proj2, proj2, conv2, cw, cb, b16, al16, eh, cos2, sin2)


STATE_ROWS = D_MODEL
STATE_BLK = 64


def _state_step_kernel(e_ref, s_ref, v_ref, k_ref, q_ref, *rest, bb, layer, fill):
    sn_all, y_ref = rest[-2:]
    if fill:
        for l in range(sn_all.shape[0]):
            if l != layer:
                sn_all[l] = jnp.zeros(sn_all.shape[1:], F32)
        sn_ref = sn_all.at[layer]
    else:
        sn_ref = sn_all
    i = pl.program_id(0)
    nblk = STATE_ROWS // STATE_BLK
    gw = STATE_ROWS // 4
    ones = jnp.ones((SUBLANES, LANES), BF16)
    zpad = jnp.zeros((LANES - SUBLANES, LANES), F32)
    nt = (((1,), (1,)), ((), ()))
    for t in range(bb):
        b = i * bb + t
        vt = jnp.concatenate([v_ref[t], zpad], axis=0).T
        kk = k_ref[t]
        qq = q_ref[t]
        for cch in range(STATE_ROWS // LANES):
            prods = []
            for half in range(LANES // STATE_BLK):
                blk = cch * (LANES // STATE_BLK) + half
                r0 = blk * STATE_BLK
                g = r0 // gw
                vcol = vt[half * STATE_BLK:(half + 1) * STATE_BLK, cch:cch + 1]
                sn = e_ref[b * nblk + blk] * s_ref[t, r0:r0 + STATE_BLK, :] + vcol * kk[g:g + 1, :]
                sn_ref[t, r0:r0 + STATE_BLK, :] = sn
                prods.append(sn * qq[g:g + 1, :])
            p = jnp.concatenate(prods, axis=0).astype(BF16)
            ysum = lax.dot_general(ones, p, nt, preferred_element_type=F32)
            y_ref[t, cch:cch + 1, :] = ysum[0:1, :]


def _state_step(e_flat, s_all, layer, v3, k3, q3, prev=None, bb=STATE_TOKENS):
    depth, n = s_all.shape[:2]
    blk3 = lambda a, b_: pl.BlockSpec((bb, a, b_), lambda i: (i, 0, 0))
    sblk = pl.BlockSpec((None, bb, STATE_ROWS, LANES), lambda i: (layer, i, 0, 0))
    in_specs = [pl.BlockSpec(memory_space=pltpu.SMEM), sblk, blk3(SUBLANES, LANES), blk3(4, LANES), blk3(4, LANES)]
    args = [e_flat, s_all, v3, k3, q3]
    aliases = {}
    if prev is not None:
        in_specs.append(pl.BlockSpec(memory_space=pl.ANY))
        args.append(prev)
        aliases = {len(args) - 1: 0}
        oblk = sblk
    else:
        oblk = pl.BlockSpec((depth, bb, STATE_ROWS, LANES), lambda i: (0, i, 0, 0))
    return pl.pallas_call(
        functools.partial(_state_step_kernel, bb=bb, layer=layer, fill=prev is None),
        grid=(n // bb,),
        in_specs=in_specs,
        out_specs=[oblk, blk3(SUBLANES, LANES)],
        out_shape=[jax.ShapeDtypeStruct((depth, n, STATE_ROWS, LANES), F32),
                   jax.ShapeDtypeStruct((n, SUBLANES, LANES), F32)],
        input_output_aliases=aliases,
        compiler_params=_cparams(("parallel",)),
        name="state_step",
    )(*args)


def _sample_post_kernel(ys_ref, xs_ref, z_ref, yr_ref, g_ref, dfull_ref, nw_ref, os_ref, or_ref):
    os_ref[...] = _ssd_gate_norm(ys_ref[...], xs_ref[...], z_ref[...].astype(F32), dfull_ref[...],
                                 nw_ref[...]).astype(os_ref.dtype)
    or_ref[...] = _ret_gate_norm(yr_ref[...], g_ref[...].astype(F32)).astype(or_ref.dtype)


def _sample_post(y_ssd, xs, proj2, y_ret, dfull, nw):
    n = y_ssd.shape[0]
    full = lambda s: pl.BlockSpec(s, lambda i: (0, 0))
    col = lambda off: pl.BlockSpec((n, D_MODEL), lambda i: (0, off // D_MODEL))
    return pl.pallas_call(
        _sample_post_kernel,
        grid=(1,),
        in_specs=[full((n, D_MODEL)), full((n, D_MODEL)), col(OFF_Z), full((n, D_MODEL)), col(OFF_G),
                  full((1, D_MODEL)), full((1, D_MODEL))],
        out_specs=[full((n, D_MODEL)), full((n, D_MODEL))],
        out_shape=[jax.ShapeDtypeStruct((n, D_MODEL), BF16)] * 2,
        compiler_params=_cparams(("arbitrary",)),
        name="sample_post",
    )(y_ssd, xs, proj2, y_ret, proj2, dfull, nw)


def _merge_kernel(s5_ref, ssd_ref, ret_ref, ga_ref, gb_ref, gc_ref, h_ref, ws5_ref, wssd_ref, wret_ref, wout_ref,
                  o_ref):
    glu = _dot(s5_ref[...], ws5_ref[...])
    y_a = glu[:, :D_MODEL] * _sigmoid(glu[:, D_MODEL:])
    y_b = _dot(ssd_ref[...], wssd_ref[...])
    y_c = _dot(ret_ref[...], wret_ref[...])
    merged = (_sigmoid(ga_ref[...].astype(F32)) * y_a + _sigmoid(gb_ref[...].astype(F32)) * y_b
              + _sigmoid(gc_ref[...].astype(F32)) * y_c)
    o_ref[...] = h_ref[...] + _dot(merged.astype(BF16), wout_ref[...])


def _merge(s5_g, ssd_n, ret_n, proj, h3, ws5, wssd, wret, wout, layer, tm):
    nb, L, _ = h3.shape
    row3 = lambda off: pl.BlockSpec((None, tm, D_MODEL), lambda b, i: (b, i, off // D_MODEL))
    wfull = lambda s: pl.BlockSpec((None,) + s, lambda b, i: (layer, 0, 0), pipeline_mode=pl.Buffered(1))
    return pl.pallas_call(
        _merge_kernel,
        grid=(nb, L // tm),
        in_specs=[row3(0), row3(0), row3(0),
                  row3(OFF_GATES), row3(OFF_GATES + D_MODEL), row3(OFF_GATES + 2 * D_MODEL), row3(0),
                  wfull((D_MODEL, 2 * D_MODEL)), wfull((D_MODEL, D_MODEL)), wfull((D_MODEL, D_MODEL)),
                  wfull((D_MODEL, D_MODEL))],
        out_specs=row3(0),
        out_shape=jax.ShapeDtypeStruct((nb, L, D_MODEL), F32),
        compiler_params=_cparams(("parallel", "parallel")),
        name="merge",
    )(s5_g, ssd_n, ret_n, proj, proj, proj, h3, ws5, wssd, wret, wout)


def _mlp_kernel(x_ref, nw_ref, wup_ref, wdn_ref, fw_ref, o_ref, xn_ref, acc_ref, *, final_norm):
    k = pl.program_id(1)
    nk = pl.num_programs(1)

    @pl.when(k == 0)
    def _():
        x = x_ref[...]
        xn_ref[...] = _rmsnorm(x, nw_ref[...]).astype(BF16)
        acc_ref[...] = x

    hid = jnp.maximum(_dot(xn_ref[...], wup_ref[...]), 0.0)
    acc_ref[...] += _dot((hid * hid).astype(BF16), wdn_ref[...])

    @pl.when(k == nk - 1)
    def _():
        h = acc_ref[...]
        o_ref[...] = _rmsnorm(h, fw_ref[...]) if final_norm else h


def _mlp(h2, nw, wup, wdn, fw, layer, tm, tf, final_norm):
    rows = h2.shape[0]
    return pl.pallas_call(
        functools.partial(_mlp_kernel, final_norm=final_norm),
        grid=(rows // tm, D_FF // tf),
        in_specs=[pl.BlockSpec((tm, D_MODEL), lambda i, k: (i, 0)),
                  pl.BlockSpec((1, D_MODEL), lambda i, k: (0, 0)),
                  pl.BlockSpec((None, D_MODEL, tf), lambda i, k: (layer, 0, k)),
                  pl.BlockSpec((None, tf, D_MODEL), lambda i, k: (layer, k, 0)),
                  pl.BlockSpec((1, D_MODEL), lambda i, k: (0, 0))],
        out_specs=pl.BlockSpec((tm, D_MODEL), lambda i, k: (i, 0)),
        out_shape=jax.ShapeDtypeStruct((rows, D_MODEL), F32),
        scratch_shapes=[pltpu.VMEM((tm, D_MODEL), BF16), pltpu.VMEM((tm, D_MODEL), F32)],
        compiler_params=_cparams(("parallel", "arbitrary")),
        name="mlp",
    )(h2, nw, wup, wdn, fw)


def _prep_layer(p):
    out = {}
    row = lambda a: a.reshape(1, -1).astype(F32)
    padrow = lambda a: jnp.pad(a.astype(F32), (0, LANES - a.shape[0])).reshape(1, LANES)
    rep = lambda a: jnp.repeat(a.astype(F32), SSD_HEAD_DIM).reshape(1, D_MODEL)
    out['norm1_w'] = row(p['norm1_w'])
    out['norm2_w'] = row(p['norm2_w'])
    out['conv_w'] = p['ssd_conv_w'].astype(F32)
    out['conv_b'] = row(p['ssd_conv_b'])
    out['b16'] = padrow(p['ssd_dt_bias'])
    out['al16'] = padrow(p['ssd_a_log'])
    out['dfull'] = rep(p['ssd_d'])
    out['ssd_nw'] = row(p['ssd_norm_w'])
    out['s5_d'] = row(p['s5_d'])
    abr, abi, bbr, bbi = _s5_discretise(p['s5_a_re'].astype(F32), p['s5_a_im'].astype(F32),
                                        p['s5_log_dt'].astype(F32), p['s5_b_re'].astype(F32),
                                        p['s5_b_im'].astype(F32))
    out['s5'] = _s5_weights(abr, abi, bbr, bbi, p['s5_c_re'].astype(F32), p['s5_c_im'].astype(F32))
    return out


def _head_expand():
    e = (jnp.arange(LANES)[:, None] == (jnp.arange(D_MODEL)[None, :] // SSD_HEAD_DIM))
    return e.astype(BF16)


def _tile(n, pref):
    return pref if n % pref == 0 else n


def _prompt_trunk(x, layers, final_w):
    nb, L, _ = x.shape
    eh = _head_expand()
    _, lm, gp, we, gq = _ret_consts(CHUNK)
    cos2, sin2 = _rope_tables(jnp.arange(L, dtype=jnp.int32))
    zeros_state = jnp.zeros((nb, S5_NSTATE), F32)
    h = x
    outs = {k: [] for k in ('re', 'im', 'ssd', 'conv', 'ret')}
    for li, lp in enumerate(layers):
        proj, dt = _proj_main(h, lp['norm1_w'], *lp['proj_w'], _tile(L, PROJ_TM))
        wbr, wbi, abr, abi, wcr, wci = lp['s5']
        y_s5, hr, hi = _s5_scan(proj, wbr, wbi, abr, abi, wcr, wci, lp['s5_d'],
                                zeros_state, zeros_state, S5_STEPS)
        ssd_n, ssd_s, ret_n, ret_s = _mix_prompt(proj, dt, lp['conv_w'], lp['conv_b'], lp['b16'], lp['al16'],
                                                 lp['dfull'], lp['ssd_nw'], eh, cos2, sin2, lm, gp, we, gq)
        h = _merge(y_s5, ssd_n, ret_n, proj, h,
                   *lp['merge_w'], li, _tile(L, MERGE_TM))
        last = li == len(layers) - 1
        h = _mlp(h.reshape(nb * L, D_MODEL), lp['norm2_w'], *lp['mlp_w'], final_w, li,
                 _tile(nb * L, MLP_TM), MLP_TF, last).reshape(nb, L, D_MODEL)
        outs['re'].append(hr.reshape(nb, S5_GROUPS, S5_STATE))
        outs['im'].append(hi.reshape(nb, S5_GROUPS, S5_STATE))
        outs['ssd'].append(ssd_s.reshape(nb, SSD_HEADS, SSD_HEAD_DIM, SSD_STATE))
        outs['conv'].append(proj[:, L - (SSD_CONV - 1):, OFF_XS:OFF_XS + SSD_CONV_DIM].astype(F32))
        outs['ret'].append(ret_s.reshape(nb, RET_HEADS, RET_V_DIM, RET_QK_DIM))
    return (h,) + tuple(jnp.stack(outs[k]) for k in ('re', 'im', 'ssd', 'conv', 'ret'))


def _sample_trunk(x, states, layers, final_w):
    n = x.shape[0]
    st_re, st_im, st_ssd, st_conv, st_ret = states
    eh = _head_expand()
    log_gamma, _, _, _, _ = _ret_consts(1)
    cos2, sin2 = _rope_tables(PAST_LEN + jnp.arange(1, dtype=jnp.int32))
    e_ret = jnp.broadcast_to(jnp.repeat(jnp.exp(log_gamma), RET_V_DIM // STATE_BLK)[None, :],
                             (n, STATE_ROWS // STATE_BLK)).reshape(-1)
    h = x.reshape(1, n, D_MODEL)
    depth = len(layers)
    ssd_all = st_ssd.reshape(depth, n, STATE_ROWS, SSD_STATE).astype(F32)
    ret_all = st_ret.reshape(depth, n, STATE_ROWS, RET_QK_DIM).astype(F32)
    ssd_new = ret_new = None
    outs = {k: [] for k in ('re', 'im', 'conv')}
    for li, lp in enumerate(layers):
        proj, dt = _proj_main(h, lp['norm1_w'], *lp['proj_w'], n)
        wbr, wbi, abr, abi, wcr, wci = lp['s5']
        y_s5, hr, hi = _s5_scan(proj, wbr, wbi, abr, abi, wcr, wci, lp['s5_d'],
                                st_re[li].reshape(n, S5_NSTATE).astype(F32),
                                st_im[li].reshape(n, S5_NSTATE).astype(F32), 1)
        proj2, dt2 = proj.reshape(n, N_MAIN), dt.reshape(n, LANES)
        conv2 = st_conv[li].reshape(n, (SSD_CONV - 1) * SSD_CONV_DIM).astype(F32)
        vv, e16, bs, cs, xs, conv_new, qr, kr = _sample_pre(
            proj2, dt2, conv2, lp['conv_w'], lp['conv_b'], lp['b16'], lp['al16'], eh, cos2, sin2)
        ssd_new, y_ssd = _state_step(
            e16[:, :SSD_HEADS].reshape(-1), ssd_all, li,
            vv.reshape(n, SUBLANES, LANES), bs.reshape(n, SSD_GROUPS, SSD_STATE),
            cs.reshape(n, SSD_GROUPS, SSD_STATE), prev=ssd_new)
        vret = proj2[:, OFF_V:OFF_V + D_MODEL].astype(F32)
        ret_new, y_ret = _state_step(
            e_ret, ret_all, li,
            vret.reshape(n, SUBLANES, LANES), kr.reshape(n, RET_HEADS, RET_QK_DIM),
            qr.reshape(n, RET_HEADS, RET_QK_DIM), prev=ret_new)
        ssd_n, ret_n = _sample_post(y_ssd.reshape(n, D_MODEL), xs, proj2, y_ret.reshape(n, D_MODEL),
                                    lp['dfull'], lp['ssd_nw'])
        h = _merge(y_s5, ssd_n.reshape(1, n, D_MODEL), ret_n.reshape(1, n, D_MODEL), proj, h,
                   *lp['merge_w'], li, n)
        last = li == len(layers) - 1
        h = _mlp(h.reshape(n, D_MODEL), lp['norm2_w'], *lp['mlp_w'], final_w, li,
                 n, MLP_TF, last).reshape(1, n, D_MODEL)
        outs['re'].append(hr.reshape(n, S5_GROUPS, S5_STATE))
        outs['im'].append(hi.reshape(n, S5_GROUPS, S5_STATE))
        outs['conv'].append(conv_new.reshape(n, SSD_CONV - 1, SSD_CONV_DIM))
    return (h.reshape(n, 1, D_MODEL), jnp.stack(outs['re']), jnp.stack(outs['im']),
            ssd_new.reshape(depth, n, SSD_HEADS, SSD_HEAD_DIM, SSD_STATE), jnp.stack(outs['conv']),
            ret_new.reshape(depth, n, RET_HEADS, RET_V_DIM, RET_QK_DIM))


def kernel(x_prompt, x_sample, state_s5_re, state_s5_im, state_ssd, state_conv, state_ret, norm1_w, w_in, s5_a_re, s5_a_im, s5_log_dt, s5_b_re, s5_b_im, s5_c_re, s5_c_im, s5_d, w_s5_out, ssd_conv_w, ssd_conv_b, ssd_dt_bias, ssd_a_log, ssd_d, ssd_norm_w, w_ssd_out, w_ret_out, w_out, norm2_w, w_mlp_up, w_mlp_down, final_norm_w):
    params = dict(norm1_w=norm1_w, w_in=w_in, s5_a_re=s5_a_re, s5_a_im=s5_a_im, s5_log_dt=s5_log_dt,
                  s5_b_re=s5_b_re, s5_b_im=s5_b_im, s5_c_re=s5_c_re, s5_c_im=s5_c_im, s5_d=s5_d,
                  w_s5_out=w_s5_out, ssd_conv_w=ssd_conv_w, ssd_conv_b=ssd_conv_b, ssd_dt_bias=ssd_dt_bias,
                  ssd_a_log=ssd_a_log, ssd_d=ssd_d, ssd_norm_w=ssd_norm_w, w_ssd_out=w_ssd_out,
                  w_ret_out=w_ret_out, w_out=w_out, norm2_w=norm2_w, w_mlp_up=w_mlp_up, w_mlp_down=w_mlp_down)
    depth = w_in.shape[0]
    big = ('w_in', 'w_s5_out', 'w_ssd_out', 'w_ret_out', 'w_out', 'w_mlp_up', 'w_mlp_down')
    layers = [_prep_layer({name: arr[l] for name, arr in params.items() if name not in big}) for l in range(depth)]
    merge_w = tuple(params[name].astype(BF16) for name in ('w_s5_out', 'w_ssd_out', 'w_ret_out', 'w_out'))
    mlp_w = tuple(params[name].astype(BF16) for name in ('w_mlp_up', 'w_mlp_down'))
    w_a, w_b, w_dt = _proj_weight_slabs(w_in.astype(F32))
    for l, lp in enumerate(layers):
        lp['proj_w'] = (w_a, w_b, w_dt, l)
        lp['merge_w'] = merge_w
        lp['mlp_w'] = mlp_w
    final_w = final_norm_w.reshape(1, D_MODEL).astype(F32)
    y_p, re_p, im_p, ssd_p, conv_p, ret_p = _prompt_trunk(x_prompt.astype(F32), layers, final_w)
    y_s, re_s, im_s, ssd_s, conv_s, ret_s = _sample_trunk(
        x_sample.astype(F32), (state_s5_re, state_s5_im, state_ssd, state_conv, state_ret), layers, final_w)
    return (y_p, y_s, re_p, im_p, ssd_p, conv_p, ret_p, re_s, im_s, ssd_s, conv_s, ret_s)
```

```python
import functools
import math

import jax
import jax.numpy as jnp
from jax import lax
from jax.experimental import pallas as pl
from jax.experimental.pallas import tpu as pltpu

F32 = jnp.float32
BF16 = jnp.bfloat16

D_MODEL = 1024
S5_GROUP = 16
S5_STATE = 64
S5_GROUPS = D_MODEL // S5_GROUP
S5_NSTATE = S5_GROUPS * S5_STATE
S5_SUPER = 8
S5_TIME_SPLIT = 2
SSD_HEADS = 16
SSD_HEAD_DIM = 64
SSD_GROUPS = 4
SSD_STATE = 128
SSD_CONV = 4
SSD_CONV_DIM = D_MODEL + 2 * SSD_GROUPS * SSD_STATE
RET_HEADS = 4
RET_QK_DIM = 128
RET_V_DIM = 256
ROPE_BASE = 10000.0
D_FF = 4 * D_MODEL
CHUNK = 128
EPS = 1e-6
PAST_LEN = 16384
LANES = 128
SUBLANES = 8
VMEM_LIMIT = 52 * 1024 * 1024

OFF_U, OFF_Z, OFF_XS, OFF_BC, OFF_Q, OFF_K, OFF_V, OFF_G, OFF_GATES = (
    0, 1024, 2048, 3072, 4096, 4608, 5120, 6144, 7168)
N_MAIN = 10240
SLAB_A_COLS = OFF_Q
SLAB_B_START = 2 * D_MODEL + SSD_CONV_DIM + SSD_HEADS

PROJ_TM, PROJ_TN = 1024, 2048
MERGE_TM = 512
MLP_TM, MLP_TF = 1024, 1024
S5_STEPS = 64
MIX_NSEQ = 2
STATE_TOKENS = 8


def _cparams(sem):
    return pltpu.CompilerParams(dimension_semantics=sem, vmem_limit_bytes=VMEM_LIMIT)


def _sigmoid(x):
    return 1.0 / (1.0 + jnp.exp(-x))


def _silu(x):
    return x * _sigmoid(x)


def _softplus(x):
    return jnp.maximum(x, 0.0) + jnp.log1p(jnp.exp(-jnp.abs(x)))


def _gelu_tanh(x):
    c = math.sqrt(2.0 / math.pi)
    half = 0.5 * x
    return half + half * jnp.tanh(x * (c + (c * 0.044715) * (x * x)))


def _rmsnorm(x, w):
    return x * lax.rsqrt(jnp.mean(x * x, axis=-1, keepdims=True) + EPS) * w


def _cumsum_rows(x):
    n = x.shape[0]
    row = lax.broadcasted_iota(jnp.int32, x.shape, 0)
    k = 1
    while k < n:
        x = x + jnp.where(row >= k, pltpu.roll(x, k, 0), 0.0)
        k *= 2
    return x


def _group_rms(y, width):
    outs = []
    for s in range(0, y.shape[-1], width):
        yg = y[:, s:s + width]
        outs.append(yg * lax.rsqrt(jnp.mean(yg * yg, axis=-1, keepdims=True) + EPS))
    return jnp.concatenate(outs, axis=-1)


def _dot(a, b):
    return jnp.dot(a, b, preferred_element_type=F32)


def _split_dot(x, e):
    hi = x.astype(BF16)
    lo = (x - hi.astype(F32)).astype(BF16)
    return _dot(hi, e) + _dot(lo, e)


def _proj_kernel(x_ref, nw_ref, wa_ref, wb_ref, wdt_ref, o_ref, odt_ref, xn_ref, *, na):
    j = pl.program_id(2)

    @pl.when(j == 0)
    def _():
        xn = _rmsnorm(x_ref[...], nw_ref[...]).astype(BF16)
        xn_ref[...] = xn
        odt_ref[...] = _dot_nt(xn, wdt_ref[...])

    @pl.when(j < na)
    def _():
        o_ref[...] = _dot_nt(xn_ref[...], wa_ref[...]).astype(o_ref.dtype)

    @pl.when(j >= na)
    def _():
        o_ref[...] = _dot_nt(xn_ref[...], wb_ref[0]).astype(o_ref.dtype)


DT_START = SLAB_B_START - SSD_HEADS
assert SLAB_B_START % (2 * SUBLANES) == 0


def _proj_weight_slabs(w_in):
    wt = jnp.swapaxes(w_in, 1, 2).astype(BF16)
    w_dt = jnp.pad(wt[:, DT_START:SLAB_B_START], ((0, 0), (0, LANES - SSD_HEADS), (0, 0)))
    return wt, w_dt


def _dot_nt(a, b):
    return lax.dot_general(a, b, (((1,), (1,)), ((), ())), preferred_element_type=F32)


def _proj_main(x3, nw, wt, wdt, layer, tm):
    nb, L, _ = x3.shape
    tn = PROJ_TN
    na = SLAB_A_COLS // tn
    return pl.pallas_call(
        functools.partial(_proj_kernel, na=na),
        grid=(nb, L // tm, N_MAIN // tn),
        in_specs=[pl.BlockSpec((None, tm, D_MODEL), lambda b, i, j: (b, i, 0)),
                  pl.BlockSpec((1, D_MODEL), lambda b, i, j: (0, 0)),
                  pl.BlockSpec((None, tn, D_MODEL), lambda b, i, j: (layer, jnp.minimum(j, na - 1), 0)),
                  pl.BlockSpec((pl.Element(1), pl.Element(tn), pl.Element(D_MODEL)),
                               lambda b, i, j: (layer, pl.multiple_of(
                                   SLAB_B_START + jnp.maximum(j - na, 0) * tn, 2 * SUBLANES), 0)),
                  pl.BlockSpec((None, LANES, D_MODEL), lambda b, i, j: (layer, 0, 0))],
        out_specs=[pl.BlockSpec((None, tm, tn), lambda b, i, j: (b, i, j)),
                   pl.BlockSpec((None, tm, LANES), lambda b, i, j: (b, i, 0))],
        out_shape=[jax.ShapeDtypeStruct((nb, L, N_MAIN), BF16), jax.ShapeDtypeStruct((nb, L, LANES), F32)],
        scratch_shapes=[pltpu.VMEM((tm, D_MODEL), BF16)],
        compiler_params=_cparams(("parallel", "parallel", "arbitrary")),
        name="proj_main",
    )(x3, nw, wt, wt, wdt)


def _s5_disc_kernel(are_ref, aim_ref, ldt_ref, bre_ref, bim_ref, abr_ref, abi_ref, bbr_ref, bbi_ref):
    ar, ai = are_ref[...], aim_ref[...]
    dt = jnp.exp(ldt_ref[...])
    mag = jnp.exp(ar * dt)
    abr = mag * jnp.cos(ai * dt)
    abi = mag * jnp.sin(ai * dt)
    abr_ref[...] = abr
    abi_ref[...] = abi
    nr, ni = abr - 1.0, abi
    den = 1.0 / (ar * ar + ai * ai)
    cr = (nr * ar + ni * ai) * den
    ci = (ni * ar - nr * ai) * den
    br, bi = bre_ref[...], bim_ref[...]
    bbr_ref[...] = cr * br - ci * bi
    bbi_ref[...] = cr * bi + ci * br


def _s5_discretise(a_re, a_im, log_dt, b_re, b_im):
    g, p = a_re.shape
    gs = b_re.shape[-1]
    full3 = lambda s: pl.BlockSpec(s, lambda: (0, 0, 0))
    return pl.pallas_call(
        _s5_disc_kernel,
        in_specs=[full3((g, 1, p)), full3((g, 1, p)), full3((g, 1, 1)), full3((g, gs, p)), full3((g, gs, p))],
        out_specs=[full3((g, 1, p)), full3((g, 1, p)), full3((g, gs, p)), full3((g, gs, p))],
        out_shape=[jax.ShapeDtypeStruct((g, 1, p), F32)] * 2 + [jax.ShapeDtypeStruct((g, gs, p), F32)] * 2,
        name="s5_discretise",
    )(a_re.reshape(g, 1, p), a_im.reshape(g, 1, p), log_dt.reshape(g, 1, 1),
      jnp.swapaxes(b_re, 1, 2), jnp.swapaxes(b_im, 1, 2))


def _s5_kernel(u_ref, wbr_ref, wbi_ref, ar_ref, ai_ref, wcr_ref, wci_ref, d_ref, h0r_ref, h0i_ref,
               y_ref, hr_ref, hi_ref, bur, bui, *, nbatch, steps, lane_w):
    @pl.when(pl.program_id(0) == 0)
    def _():
        hr_ref[...] = h0r_ref[...]
        hi_ref[...] = h0i_ref[...]

    cw = S5_SUPER * S5_GROUP
    sw = S5_SUPER * S5_STATE
    batch_major = steps > 1
    u = u_ref[...].astype(F32)
    if batch_major:
        u = jnp.swapaxes(u, 0, 1).reshape(steps * nbatch, D_MODEL)
    ub = u.astype(BF16)
    nsub = S5_TIME_SPLIT if steps % (2 * S5_TIME_SPLIT) == 0 else 1
    sub_steps = steps // nsub
    nsup = S5_GROUPS // S5_SUPER
    for sb in range(nsub):
        t0 = sb * sub_steps
        rs = slice(t0 * nbatch, (t0 + sub_steps) * nbatch)
        for j in range(nsup):
            uj = ub[rs, j * cw:(j + 1) * cw]
            bur[rs, j * sw:(j + 1) * sw] = _dot(uj, wbr_ref[j])
            bui[rs, j * sw:(j + 1) * sw] = _dot(uj, wbi_ref[j])

        for j in range(nsup):
            for lc in range(sw // lane_w):
                sl = slice(j * sw + lc * lane_w, j * sw + (lc + 1) * lane_w)
                are = jnp.broadcast_to(ar_ref[:, sl], (nbatch, lane_w))
                aim = jnp.broadcast_to(ai_ref[:, sl], (nbatch, lane_w))
                hr, hi = hr_ref[:, sl], hi_ref[:, sl]
                for t in range(t0, t0 + sub_steps):
                    rows = slice(t * nbatch, (t + 1) * nbatch)
                    hr, hi = are * hr - aim * hi + bur[rows, sl], are * hi + aim * hr + bui[rows, sl]
                    bur[rows, sl] = hr
                    bui[rows, sl] = hi
                hr_ref[:, sl] = hr
                hi_ref[:, sl] = hi

        for j in range(nsup):
            ch = slice(j * cw, (j + 1) * cw)
            yj = (_dot(bur[rs, j * sw:(j + 1) * sw].astype(BF16), wcr_ref[j])
                  + _dot(bui[rs, j * sw:(j + 1) * sw].astype(BF16), wci_ref[j]))
            yj = _gelu_tanh(yj + d_ref[:, ch] * u[rs, ch])
            if batch_major:
                y_ref[:, t0:t0 + sub_steps, ch] = jnp.swapaxes(
                    yj.reshape(sub_steps, nbatch, cw), 0, 1).astype(y_ref.dtype)
            else:
                y_ref[:, ch] = yj.astype(y_ref.dtype)


def _s5_scan(proj, wbr, wbi, abr, abi, wcr, wci, d, h0r, h0i, steps):
    nb, L, _ = proj.shape
    nsup = S5_GROUPS // S5_SUPER
    cw, sw = S5_SUPER * S5_GROUP, S5_SUPER * S5_STATE
    c2 = lambda s: pl.BlockSpec(s, lambda c: (0, 0))
    c3 = lambda s: pl.BlockSpec(s, lambda c: (0, 0, 0))
    if steps > 1:
        nbatch, grid = nb, L // steps
        u_spec = pl.BlockSpec((nb, steps, D_MODEL), lambda c: (0, c, OFF_U // D_MODEL))
        y_spec = pl.BlockSpec((nb, steps, D_MODEL), lambda c: (0, c, 0))
    else:
        nbatch, grid = L, 1
        u_spec = pl.BlockSpec((None, L, D_MODEL), lambda c: (0, 0, OFF_U // D_MODEL))
        y_spec = pl.BlockSpec((None, L, D_MODEL), lambda c: (0, 0, 0))
    blk = nbatch * steps
    lane_w = max(LANES, S5_NSTATE // nbatch)
    kern = functools.partial(_s5_kernel, nbatch=nbatch, steps=steps, lane_w=lane_w)
    return pl.pallas_call(
        kern,
        grid=(grid,),
        in_specs=[u_spec,
                  c3((nsup, cw, sw)), c3((nsup, cw, sw)),
                  c2((1, S5_NSTATE)), c2((1, S5_NSTATE)),
                  c3((nsup, sw, cw)), c3((nsup, sw, cw)),
                  c2((1, D_MODEL)),
                  c2((nbatch, S5_NSTATE)), c2((nbatch, S5_NSTATE))],
        out_specs=[y_spec, c2((nbatch, S5_NSTATE)), c2((nbatch, S5_NSTATE))],
        out_shape=[jax.ShapeDtypeStruct((nb, L, D_MODEL), BF16),
                   jax.ShapeDtypeStruct((nbatch, S5_NSTATE), F32),
                   jax.ShapeDtypeStruct((nbatch, S5_NSTATE), F32)],
        scratch_shapes=[pltpu.VMEM((blk, S5_NSTATE), F32), pltpu.VMEM((blk, S5_NSTATE), F32)],
        compiler_params=_cparams(("arbitrary",)),
        name="s5_scan",
    )(proj, wbr, wbi, abr, abi, wcr, wci, d, h0r, h0i)


def _s5_weights(abr, abi, bbr, bbi, c_re, c_im):
    nsup = S5_GROUPS // S5_SUPER
    eye = jnp.eye(S5_SUPER, dtype=F32)

    def bexp(b):
        b = b.reshape(nsup, S5_SUPER, S5_GROUP, S5_STATE)
        return (b[:, :, :, None, :] * eye[None, :, None, :, None]).reshape(
            nsup, S5_SUPER * S5_GROUP, S5_SUPER * S5_STATE).astype(BF16)

    def cexp(c):
        c = jnp.swapaxes(c, 1, 2).reshape(nsup, S5_SUPER, S5_STATE, S5_GROUP)
        return (c[:, :, :, None, :] * eye[None, :, None, :, None]).reshape(
            nsup, S5_SUPER * S5_STATE, S5_SUPER * S5_GROUP).astype(BF16)

    return (bexp(bbr), bexp(bbi), abr.reshape(1, S5_NSTATE), abi.reshape(1, S5_NSTATE),
            cexp(c_re), cexp(-c_im))


def _ssd_gate_norm(y, xs, z, dfull, nw):
    y = (y + dfull * xs) * _silu(z)
    return _group_rms(y, D_MODEL // SSD_GROUPS) * nw


CONV_PAD = 16
CONV_ROWS = 2 * CHUNK


def _conv_shift_mats():
    i = jnp.arange(CHUNK)[:, None]
    j = jnp.arange(CONV_ROWS)[None, :]
    return jnp.stack([(j == CONV_PAD + i - s) for s in range(1, SSD_CONV)]).astype(BF16)


def _ssd_chunk_seq(xs_ref, bc_ref, z_ref, dt_ref, cw_ref, cb_ref, b16_ref, al16_ref,
                   dfull_ref, nw_ref, eh_ref, sh_ref, o_ref, xpad, st):
    q = CHUNK
    pad = CONV_PAD
    xpad[0:pad, :] = xpad[q:q + pad, :]

    xpad[pad:pad + q, :D_MODEL] = xs_ref[...]
    xpad[pad:pad + q, D_MODEL:] = bc_ref[...]
    staged = xpad[...]
    acc = cb_ref[...] + xpad[pad:pad + q, :].astype(F32) * cw_ref[SSD_CONV - 1:SSD_CONV, :]
    for s in range(1, SSD_CONV):
        acc = acc + _dot(sh_ref[s - 1], staged) * cw_ref[SSD_CONV - 1 - s:SSD_CONV - s, :]
    xc = _silu(acc)
    xs = xc[:, :D_MODEL]
    nbc = SSD_GROUPS * SSD_STATE
    bs = xc[:, D_MODEL:D_MODEL + nbc]
    cs = xc[:, D_MODEL + nbc:]

    delta16 = _softplus(dt_ref[...] + b16_ref[...])
    acum16 = _cumsum_rows(delta16 * (-jnp.exp(al16_ref[...])))
    acum_log2 = acum16 * math.log2(math.e)
    acum_log2_t = acum_log2.T
    delta_t = delta16.T
    last16 = acum16[q - 1:q, :]
    ea_f = _split_dot(jnp.exp(acum16), eh_ref[...])
    dw_f = _split_dot(delta16 * jnp.exp(last16 - acum16), eh_ref[...])
    xw = (xs * dw_f).astype(BF16)

    ri = lax.broadcasted_iota(jnp.int32, (q, q), 0)
    ci = lax.broadcasted_iota(jnp.int32, (q, q), 1)
    causal = ri >= ci
    hw = D_MODEL // SSD_GROUPS
    hpg = SSD_HEADS // SSD_GROUPS
    xsb = xs.astype(BF16)
    ys = []
    for g in range(SSD_GROUPS):
        bg = bs[:, g * SSD_STATE:(g + 1) * SSD_STATE]
        cg = cs[:, g * SSD_STATE:(g + 1) * SSD_STATE].astype(BF16)
        bgt = bg.T.astype(BF16)
        scores = _dot(cg, bgt)
        sg = st[:, g * hw:(g + 1) * hw]
        y_inter = _dot(cg, sg.astype(BF16)) * ea_f[:, g * hw:(g + 1) * hw]
        for r in range(hpg):
            hd = g * hpg + r
            seg = acum_log2[:, hd:hd + 1] - acum_log2_t[hd:hd + 1, :]
            lm = jnp.exp2(jnp.where(causal, seg, -jnp.inf)) * delta_t[hd:hd + 1, :]
            p = (scores * lm).astype(BF16)
            ys.append(_dot(p, xsb[:, hd * SSD_HEAD_DIM:(hd + 1) * SSD_HEAD_DIM])
                      + y_inter[:, r * SSD_HEAD_DIM:(r + 1) * SSD_HEAD_DIM])
        st[:, g * hw:(g + 1) * hw] = (ea_f[q - 1:q, g * hw:(g + 1) * hw] * sg
                                      + _dot(bgt, xw[:, g * hw:(g + 1) * hw]))
    y = jnp.concatenate(ys, axis=-1)
    o_ref[...] = _ssd_gate_norm(y, xs, z_ref[...].astype(F32), dfull_ref[...], nw_ref[...]).astype(o_ref.dtype)


def _rotary(x, cos2, sin2):
    outs = []
    for h in range(RET_HEADS):
        xh = x[:, h * RET_QK_DIM:(h + 1) * RET_QK_DIM]
        outs.append(xh * cos2 + pltpu.roll(xh, RET_QK_DIM // 2, 1) * sin2)
    return outs


def _ret_gate_norm(y, g):
    return _group_rms(y, RET_V_DIM) * _silu(g)


def _ret_chunk_seq(q_ref, k_ref, v_ref, g_ref, cos2, sin2, lm_ref, gp_ref, we_ref, gq_ref, o_ref, st):
    qr = _rotary(q_ref[...].astype(F32), cos2, sin2)
    kr = _rotary(k_ref[...].astype(F32), cos2, sin2)
    vb = v_ref[...]
    vw = (vb.astype(F32) * we_ref[...]).astype(BF16)
    ys = []
    for h in range(RET_HEADS):
        sl = slice(h * RET_V_DIM, (h + 1) * RET_V_DIM)
        qh = qr[h].astype(BF16)
        kht = (kr[h] * (RET_QK_DIM ** -0.5)).T.astype(BF16)
        p = (_dot(qh, kht) * lm_ref[h]).astype(BF16)
        sh = st[:, sl]
        ys.append(_dot(p, vb[:, sl]) + _dot(qh, sh.astype(BF16)) * gp_ref[:, sl])
        st[:, sl] = gq_ref[:, sl] * sh + _dot(kht, vw[:, sl])
    y = jnp.concatenate(ys, axis=-1)
    o_ref[...] = _ret_gate_norm(y, g_ref[...].astype(F32)).astype(o_ref.dtype)


def _mix_chunk_kernel(xs_ref, bc_ref, z_ref, dt_ref, cw_ref, cb_ref, b16_ref, al16_ref, dfull_ref, nw_ref, eh_ref,
                      sh_ref, q_ref, k_ref, v_ref, g_ref, cos_ref, sin_ref, lm_ref, gp_ref, we_ref, gq_ref,
                      so_ref, ss_ref, ro_ref, rs_ref, xpad, st_ssd, st_ret):
    c = pl.program_id(1)
    nc = pl.num_programs(1)
    nseq = xs_ref.shape[0]

    @pl.when(c == 0)
    def _():
        xpad[...] = jnp.zeros(xpad.shape, BF16)
        st_ssd[...] = jnp.zeros(st_ssd.shape, F32)
        st_ret[...] = jnp.zeros(st_ret.shape, F32)

    cos2, sin2 = cos_ref[...], sin_ref[...]
    for bi in range(nseq):
        _ssd_chunk_seq(xs_ref.at[bi], bc_ref.at[bi], z_ref.at[bi], dt_ref.at[bi], cw_ref, cb_ref, b16_ref,
                       al16_ref, dfull_ref, nw_ref, eh_ref, sh_ref, so_ref.at[bi], xpad.at[bi], st_ssd.at[bi])
        _ret_chunk_seq(q_ref.at[bi], k_ref.at[bi], v_ref.at[bi], g_ref.at[bi], cos2, sin2, lm_ref, gp_ref, we_ref,
                       gq_ref, ro_ref.at[bi], st_ret.at[bi])

    @pl.when(c == nc - 1)
    def _():
        for bi in range(nseq):
            for k in range(D_MODEL // LANES):
                ss_ref[bi, k * LANES:(k + 1) * LANES, :] = st_ssd[bi, :, k * LANES:(k + 1) * LANES].T
                rs_ref[bi, k * LANES:(k + 1) * LANES, :] = st_ret[bi, :, k * LANES:(k + 1) * LANES].T


def _ret_consts(q):
    hidx = jnp.arange(RET_HEADS, dtype=F32)
    log_gamma = jnp.log1p(-jnp.exp2(-5.0 - hidx))
    i = jnp.arange(q, dtype=F32)
    seg = (i[:, None] - i[None, :])[None] * log_gamma[:, None, None]
    lm = jnp.exp(jnp.where((i[:, None] >= i[None, :])[None], seg, -jnp.inf))
    rep = lambda t: jnp.repeat(t, RET_V_DIM, axis=-1)
    gp = rep(jnp.exp((i[:, None] + 1.0) * log_gamma[None, :]))
    we = rep(jnp.exp((q - 1.0 - i[:, None]) * log_gamma[None, :]))
    gq = rep(jnp.exp(q * log_gamma)[None, :])
    return log_gamma, lm, gp, we, gq


def _rope_tables(pos):
    half = RET_QK_DIM // 2
    inv = ROPE_BASE ** (-jnp.arange(half, dtype=F32) / half)
    ang = pos.astype(F32)[:, None] * inv[None]
    cos, sin = jnp.cos(ang), jnp.sin(ang)
    return jnp.concatenate([cos, cos], axis=-1), jnp.concatenate([-sin, sin], axis=-1)


def _mix_prompt(proj, dt, cw, cb, b16, al16, dfull, nw, eh, cos2, sin2, lm, gp, we, gq, nseq=MIX_NSEQ):
    nb, L, _ = proj.shape
    q = CHUNK
    qkw = RET_HEADS * RET_QK_DIM
    c2 = lambda s: pl.BlockSpec(s, lambda b, c: (0, 0))
    rows = lambda w, off: pl.BlockSpec((nseq, q, w), lambda b, c: (b, c, off // w))
    state = pl.BlockSpec((nseq, D_MODEL, LANES), lambda b, c: (b, 0, 0))
    return pl.pallas_call(
        _mix_chunk_kernel,
        grid=(nb // nseq, L // q),
        in_specs=[rows(D_MODEL, OFF_XS), rows(D_MODEL, OFF_BC), rows(D_MODEL, OFF_Z), rows(LANES, 0),
                  c2((SSD_CONV, SSD_CONV_DIM)), c2((1, SSD_CONV_DIM)),
                  c2((1, LANES)), c2((1, LANES)),
                  c2((1, D_MODEL)), c2((1, D_MODEL)),
                  c2((LANES, D_MODEL)),
                  pl.BlockSpec((SSD_CONV - 1, q, CONV_ROWS), lambda b, c: (0, 0, 0)),
                  rows(qkw, OFF_Q), rows(qkw, OFF_K), rows(D_MODEL, OFF_V), rows(D_MODEL, OFF_G),
                  pl.BlockSpec((q, RET_QK_DIM), lambda b, c: (c, 0)),
                  pl.BlockSpec((q, RET_QK_DIM), lambda b, c: (c, 0)),
                  pl.BlockSpec((RET_HEADS, q, q), lambda b, c: (0, 0, 0)),
                  c2((q, D_MODEL)), c2((q, D_MODEL)), c2((1, D_MODEL))],
        out_specs=[rows(D_MODEL, 0), state, rows(D_MODEL, 0), state],
        out_shape=[jax.ShapeDtypeStruct((nb, L, D_MODEL), BF16),
                   jax.ShapeDtypeStruct((nb, D_MODEL, SSD_STATE), F32),
                   jax.ShapeDtypeStruct((nb, L, D_MODEL), BF16),
                   jax.ShapeDtypeStruct((nb, D_MODEL, RET_QK_DIM), F32)],
        scratch_shapes=[pltpu.VMEM((nseq, CONV_ROWS, SSD_CONV_DIM), BF16),
                        pltpu.VMEM((nseq, SSD_STATE, D_MODEL), F32),
                        pltpu.VMEM((nseq, RET_QK_DIM, D_MODEL), F32)],
        compiler_params=_cparams(("parallel", "arbitrary")),
        name="mix_chunk",
    )(proj, proj, proj, dt, cw, cb, b16, al16, dfull, nw, eh, _conv_shift_mats(),
      proj, proj, proj, proj, cos2, sin2, lm, gp, we, gq)


def _sample_pre_kernel(xsin_ref, bcin_ref, dt_ref, q_ref, k_ref, conv_ref, cw_ref, cb_ref, b16_ref, al16_ref,
                       eh_ref, cos_ref, sin_ref,
                       vv_ref, e16_ref, bs_ref, cs_ref, xs_ref, convn_ref, qr_ref, kr_ref):
    cd = SSD_CONV_DIM
    x = jnp.concatenate([xsin_ref[...], bcin_ref[...]], axis=-1).astype(F32)
    acc = cb_ref[...] + x * cw_ref[SSD_CONV - 1:SSD_CONV, :]
    for tap in range(SSD_CONV - 1):
        acc = acc + conv_ref[:, tap * cd:(tap + 1) * cd] * cw_ref[tap:tap + 1, :]
    convn_ref[:, 0:(SSD_CONV - 2) * cd] = conv_ref[:, cd:(SSD_CONV - 1) * cd]
    convn_ref[:, (SSD_CONV - 2) * cd:] = x
    xc = _silu(acc)
    xs = xc[:, :D_MODEL]
    nbc = SSD_GROUPS * SSD_STATE
    xs_ref[...] = xs
    bs_ref[...] = xc[:, D_MODEL:D_MODEL + nbc]
    cs_ref[...] = xc[:, D_MODEL + nbc:]
    delta16 = _softplus(dt_ref[...] + b16_ref[...])
    e16_ref[...] = jnp.exp(delta16 * (-jnp.exp(al16_ref[...])))
    vv_ref[...] = xs * _split_dot(delta16, eh_ref[...])
    cos2, sin2 = cos_ref[...], sin_ref[...]
    qr_ref[...] = jnp.concatenate(_rotary(q_ref[...].astype(F32), cos2, sin2), axis=-1)
    kr_ref[...] = jnp.concatenate(_rotary(k_ref[...].astype(F32), cos2, sin2), axis=-1) * (RET_QK_DIM ** -0.5)


def _sample_pre(proj2, dt2, conv2, cw, cb, b16, al16, eh, cos2, sin2):
    n = proj2.shape[0]
    qkw = RET_HEADS * RET_QK_DIM
    nbc = SSD_GROUPS * SSD_STATE
    full = lambda s: pl.BlockSpec(s, lambda i: (0, 0))
    col = lambda w, off: pl.BlockSpec((n, w), lambda i: (0, off // w))
    shp = lambda w: jax.ShapeDtypeStruct((n, w), F32)
    return pl.pallas_call(
        _sample_pre_kernel,
        grid=(1,),
        in_specs=[col(D_MODEL, OFF_XS), col(D_MODEL, OFF_BC), full((n, LANES)), col(qkw, OFF_Q), col(qkw, OFF_K),
                  full((n, (SSD_CONV - 1) * SSD_CONV_DIM)),
                  full((SSD_CONV, SSD_CONV_DIM)), full((1, SSD_CONV_DIM)),
                  full((1, LANES)), full((1, LANES)),
                  full((LANES, D_MODEL)), full((1, RET_QK_DIM)), full((1, RET_QK_DIM))],
        out_specs=[full((n, D_MODEL)), full((n, LANES)), full((n, nbc)), full((n, nbc)), full((n, D_MODEL)),
                   full((n, (SSD_CONV - 1) * SSD_CONV_DIM)), full((n, qkw)), full((n, qkw))],
        out_shape=[shp(D_MODEL), shp(LANES), shp(nbc), shp(nbc), shp(D_MODEL),
                   shp((SSD_CONV - 1) * SSD_CONV_DIM), shp(qkw), shp(qkw)],
        compiler_params=_cparams(("arbitrary",)),
        name="sample_pre",
    )(proj2, proj2, dt2, proj2, proj2, conv2, cw, cb, b16, al16, eh, cos2, sin2)


STATE_ROWS = D_MODEL
STATE_BLK = 64


def _state_step_kernel(e_ref, s_ref, v_ref, k_ref, q_ref, *rest, bb, layer, fill):
    sn_all, y_ref = rest[-2:]
    if fill:
        for l in range(sn_all.shape[0]):
            if l != layer:
                sn_all[l] = jnp.zeros(sn_all.shape[1:], F32)
        sn_ref = sn_all.at[layer]
    else:
        sn_ref = sn_all
    i = pl.program_id(0)
    nblk = STATE_ROWS // STATE_BLK
    gw = STATE_ROWS // 4
    ones = jnp.ones((SUBLANES, LANES), BF16)
    zpad = jnp.zeros((LANES - SUBLANES, LANES), F32)
    nt = (((1,), (1,)), ((), ()))
    for t in range(bb):
        b = i * bb + t
        vt = jnp.concatenate([v_ref[t], zpad], axis=0).T
        kk = k_ref[t]
        qq = q_ref[t]
        for cch in range(STATE_ROWS // LANES):
            prods = []
            for half in range(LANES // STATE_BLK):
                blk = cch * (LANES // STATE_BLK) + half
                r0 = blk * STATE_BLK
                g = r0 // gw
                vcol = vt[half * STATE_BLK:(half + 1) * STATE_BLK, cch:cch + 1]
                sn = e_ref[b * nblk + blk] * s_ref[t, r0:r0 + STATE_BLK, :] + vcol * kk[g:g + 1, :]
                sn_ref[t, r0:r0 + STATE_BLK, :] = sn
                prods.append(sn * qq[g:g + 1, :])
            p = jnp.concatenate(prods, axis=0).astype(BF16)
            ysum = lax.dot_general(ones, p, nt, preferred_element_type=F32)
            y_ref[t, cch:cch + 1, :] = ysum[0:1, :]


def _state_step(e_flat, s_all, layer, v3, k3, q3, prev=None, bb=STATE_TOKENS):
    depth, n = s_all.shape[:2]
    blk3 = lambda a, b_: pl.BlockSpec((bb, a, b_), lambda i: (i, 0, 0))
    sblk = pl.BlockSpec((None, bb, STATE_ROWS, LANES), lambda i: (layer, i, 0, 0))
    in_specs = [pl.BlockSpec(memory_space=pltpu.SMEM), sblk, blk3(SUBLANES, LANES), blk3(4, LANES), blk3(4, LANES)]
    args = [e_flat, s_all, v3, k3, q3]
    aliases = {}
    if prev is not None:
        in_specs.append(pl.BlockSpec(memory_space=pl.ANY))
        args.append(prev)
        aliases = {len(args) - 1: 0}
        oblk = sblk
    else:
        oblk = pl.BlockSpec((depth, bb, STATE_ROWS, LANES), lambda i: (0, i, 0, 0))
    return pl.pallas_call(
        functools.partial(_state_step_kernel, bb=bb, layer=layer, fill=prev is None),
        grid=(n // bb,),
        in_specs=in_specs,
        out_specs=[oblk, blk3(SUBLANES, LANES)],
        out_shape=[jax.ShapeDtypeStruct((depth, n, STATE_ROWS, LANES), F32),
                   jax.ShapeDtypeStruct((n, SUBLANES, LANES), F32)],
        input_output_aliases=aliases,
        compiler_params=_cparams(("parallel",)),
        name="state_step",
    )(*args)


def _sample_post_kernel(ys_ref, xs_ref, z_ref, yr_ref, g_ref, dfull_ref, nw_ref, os_ref, or_ref):
    os_ref[...] = _ssd_gate_norm(ys_ref[...], xs_ref[...], z_ref[...].astype(F32), dfull_ref[...],
                                 nw_ref[...]).astype(os_ref.dtype)
    or_ref[...] = _ret_gate_norm(yr_ref[...], g_ref[...].astype(F32)).astype(or_ref.dtype)


def _sample_post(y_ssd, xs, proj2, y_ret, dfull, nw):
    n = y_ssd.shape[0]
    full = lambda s: pl.BlockSpec(s, lambda i: (0, 0))
    col = lambda off: pl.BlockSpec((n, D_MODEL), lambda i: (0, off // D_MODEL))
    return pl.pallas_call(
        _sample_post_kernel,
        grid=(1,),
        in_specs=[full((n, D_MODEL)), full((n, D_MODEL)), col(OFF_Z), full((n, D_MODEL)), col(OFF_G),
                  full((1, D_MODEL)), full((1, D_MODEL))],
        out_specs=[full((n, D_MODEL)), full((n, D_MODEL))],
        out_shape=[jax.ShapeDtypeStruct((n, D_MODEL), BF16)] * 2,
        compiler_params=_cparams(("arbitrary",)),
        name="sample_post",
    )(y_ssd, xs, proj2, y_ret, proj2, dfull, nw)


def _merge_kernel(s5_ref, ssd_ref, ret_ref, ga_ref, gb_ref, gc_ref, h_ref, ws5_ref, wssd_ref, wret_ref, wout_ref,
                  o_ref):
    glu = _dot(s5_ref[...], ws5_ref[...])
    y_a = glu[:, :D_MODEL] * _sigmoid(glu[:, D_MODEL:])
    y_b = _dot(ssd_ref[...], wssd_ref[...])
    y_c = _dot(ret_ref[...], wret_ref[...])
    merged = (_sigmoid(ga_ref[...].astype(F32)) * y_a + _sigmoid(gb_ref[...].astype(F32)) * y_b
              + _sigmoid(gc_ref[...].astype(F32)) * y_c)
    o_ref[...] = h_ref[...] + _dot(merged.astype(BF16), wout_ref[...])


def _merge(s5_g, ssd_n, ret_n, proj, h3, ws5, wssd, wret, wout, layer, tm):
    nb, L, _ = h3.shape
    row3 = lambda off: pl.BlockSpec((None, tm, D_MODEL), lambda b, i: (b, i, off // D_MODEL))
    wfull = lambda s: pl.BlockSpec((None,) + s, lambda b, i: (layer, 0, 0), pipeline_mode=pl.Buffered(1))
    return pl.pallas_call(
        _merge_kernel,
        grid=(nb, L // tm),
        in_specs=[row3(0), row3(0), row3(0),
                  row3(OFF_GATES), row3(OFF_GATES + D_MODEL), row3(OFF_GATES + 2 * D_MODEL), row3(0),
                  wfull((D_MODEL, 2 * D_MODEL)), wfull((D_MODEL, D_MODEL)), wfull((D_MODEL, D_MODEL)),
                  wfull((D_MODEL, D_MODEL))],
        out_specs=row3(0),
        out_shape=jax.ShapeDtypeStruct((nb, L, D_MODEL), F32),
        compiler_params=_cparams(("parallel", "parallel")),
        name="merge",
    )(s5_g, ssd_n, ret_n, proj, proj, proj, h3, ws5, wssd, wret, wout)


def _mlp_kernel(x_ref, nw_ref, wup_ref, wdn_ref, fw_ref, o_ref, xn_ref, acc_ref, *, final_norm):
    k = pl.program_id(1)
    nk = pl.num_programs(1)

    @pl.when(k == 0)
    def _():
        x = x_ref[...]
        xn_ref[...] = _rmsnorm(x, nw_ref[...]).astype(BF16)
        acc_ref[...] = x

    hid = jnp.maximum(_dot(xn_ref[...], wup_ref[...]), 0.0)
    acc_ref[...] += _dot((hid * hid).astype(BF16), wdn_ref[...])

    @pl.when(k == nk - 1)
    def _():
        h = acc_ref[...]
        o_ref[...] = _rmsnorm(h, fw_ref[...]) if final_norm else h


def _mlp(h2, nw, wup, wdn, fw, layer, tm, tf, final_norm):
    rows = h2.shape[0]
    return pl.pallas_call(
        functools.partial(_mlp_kernel, final_norm=final_norm),
        grid=(rows // tm, D_FF // tf),
        in_specs=[pl.BlockSpec((tm, D_MODEL), lambda i, k: (i, 0)),
                  pl.BlockSpec((1, D_MODEL), lambda i, k: (0, 0)),
                  pl.BlockSpec((None, D_MODEL, tf), lambda i, k: (layer, 0, k)),
                  pl.BlockSpec((None, tf, D_MODEL), lambda i, k: (layer, k, 0)),
                  pl.BlockSpec((1, D_MODEL), lambda i, k: (0, 0))],
        out_specs=pl.BlockSpec((tm, D_MODEL), lambda i, k: (i, 0)),
        out_shape=jax.ShapeDtypeStruct((rows, D_MODEL), F32),
        scratch_shapes=[pltpu.VMEM((tm, D_MODEL), BF16), pltpu.VMEM((tm, D_MODEL), F32)],
        compiler_params=_cparams(("parallel", "arbitrary")),
        name="mlp",
    )(h2, nw, wup, wdn, fw)


def _prep_layer(p):
    out = {}
    row = lambda a: a.reshape(1, -1).astype(F32)
    padrow = lambda a: jnp.pad(a.astype(F32), (0, LANES - a.shape[0])).reshape(1, LANES)
    rep = lambda a: jnp.repeat(a.astype(F32), SSD_HEAD_DIM).reshape(1, D_MODEL)
    out['norm1_w'] = row(p['norm1_w'])
    out['norm2_w'] = row(p['norm2_w'])
    out['conv_w'] = p['ssd_conv_w'].astype(F32)
    out['conv_b'] = row(p['ssd_conv_b'])
    out['b16'] = padrow(p['ssd_dt_bias'])
    out['al16'] = padrow(p['ssd_a_log'])
    out['dfull'] = rep(p['ssd_d'])
    out['ssd_nw'] = row(p['ssd_norm_w'])
    out['s5_d'] = row(p['s5_d'])
    abr, abi, bbr, bbi = _s5_discretise(p['s5_a_re'].astype(F32), p['s5_a_im'].astype(F32),
                                        p['s5_log_dt'].astype(F32), p['s5_b_re'].astype(F32),
                                        p['s5_b_im'].astype(F32))
    out['s5'] = _s5_weights(abr, abi, bbr, bbi, p['s5_c_re'].astype(F32), p['s5_c_im'].astype(F32))
    return out


def _head_expand():
    e = (jnp.arange(LANES)[:, None] == (jnp.arange(D_MODEL)[None, :] // SSD_HEAD_DIM))
    return e.astype(BF16)


def _tile(n, pref):
    return pref if n % pref == 0 else n


def _prompt_trunk(x, layers, final_w):
    nb, L, _ = x.shape
    eh = _head_expand()
    _, lm, gp, we, gq = _ret_consts(CHUNK)
    cos2, sin2 = _rope_tables(jnp.arange(L, dtype=jnp.int32))
    zeros_state = jnp.zeros((nb, S5_NSTATE), F32)
    h = x
    outs = {k: [] for k in ('re', 'im', 'ssd', 'conv', 'ret')}
    for li, lp in enumerate(layers):
        proj, dt = _proj_main(h, lp['norm1_w'], *lp['proj_w'], _tile(L, PROJ_TM))
        wbr, wbi, abr, abi, wcr, wci = lp['s5']
        y_s5, hr, hi = _s5_scan(proj, wbr, wbi, abr, abi, wcr, wci, lp['s5_d'],
                                zeros_state, zeros_state, S5_STEPS)
        ssd_n, ssd_s, ret_n, ret_s = _mix_prompt(proj, dt, lp['conv_w'], lp['conv_b'], lp['b16'], lp['al16'],
                                                 lp['dfull'], lp['ssd_nw'], eh, cos2, sin2, lm, gp, we, gq)
        h = _merge(y_s5, ssd_n, ret_n, proj, h,
                   *lp['merge_w'], li, _tile(L, MERGE_TM))
        last = li == len(layers) - 1
        h = _mlp(h.reshape(nb * L, D_MODEL), lp['norm2_w'], *lp['mlp_w'], final_w, li,
                 _tile(nb * L, MLP_TM), MLP_TF, last).reshape(nb, L, D_MODEL)
        outs['re'].append(hr.reshape(nb, S5_GROUPS, S5_STATE))
        outs['im'].append(hi.reshape(nb, S5_GROUPS, S5_STATE))
        outs['ssd'].append(ssd_s.reshape(nb, SSD_HEADS, SSD_HEAD_DIM, SSD_STATE))
        outs['conv'].append(proj[:, L - (SSD_CONV - 1):, OFF_XS:OFF_XS + SSD_CONV_DIM].astype(F32))
        outs['ret'].append(ret_s.reshape(nb, RET_HEADS, RET_V_DIM, RET_QK_DIM))
    return (h,) + tuple(jnp.stack(outs[k]) for k in ('re', 'im', 'ssd', 'conv', 'ret'))


def _sample_trunk(x, states, layers, final_w):
    n = x.shape[0]
    st_re, st_im, st_ssd, st_conv, st_ret = states
    eh = _head_expand()
    log_gamma, _, _, _, _ = _ret_consts(1)
    cos2, sin2 = _rope_tables(PAST_LEN + jnp.arange(1, dtype=jnp.int32))
    e_ret = jnp.broadcast_to(jnp.repeat(jnp.exp(log_gamma), RET_V_DIM // STATE_BLK)[None, :],
                             (n, STATE_ROWS // STATE_BLK)).reshape(-1)
    h = x.reshape(1, n, D_MODEL)
    depth = len(layers)
    ssd_all = st_ssd.reshape(depth, n, STATE_ROWS, SSD_STATE).astype(F32)
    ret_all = st_ret.reshape(depth, n, STATE_ROWS, RET_QK_DIM).astype(F32)
    ssd_new = ret_new = None
    outs = {k: [] for k in ('re', 'im', 'conv')}
    for li, lp in enumerate(layers):
        proj, dt = _proj_main(h, lp['norm1_w'], *lp['proj_w'], n)
        wbr, wbi, abr, abi, wcr, wci = lp['s5']
        y_s5, hr, hi = _s5_scan(proj, wbr, wbi, abr, abi, wcr, wci, lp['s5_d'],
                                st_re[li].reshape(n, S5_NSTATE).astype(F32),
                                st_im[li].reshape(n, S5_NSTATE).astype(F32), 1)
        proj2, dt2 = proj.reshape(n, N_MAIN), dt.reshape(n, LANES)
        conv2 = st_conv[li].reshape(n, (SSD_CONV - 1) * SSD_CONV_DIM).astype(F32)
        vv, e16, bs, cs, xs, conv_new, qr, kr = _sample_pre(
            proj2, dt2, conv2, lp['conv_w'], lp['conv_b'], lp['b16'], lp['al16'], eh, cos2, sin2)
        ssd_new, y_ssd = _state_step(
            e16[:, :SSD_HEADS].reshape(-1), ssd_all, li,
            vv.reshape(n, SUBLANES, LANES), bs.reshape(n, SSD_GROUPS, SSD_STATE),
            cs.reshape(n, SSD_GROUPS, SSD_STATE), prev=ssd_new)
        vret = proj2[:, OFF_V:OFF_V + D_MODEL].astype(F32)
        ret_new, y_ret = _state_step(
            e_ret, ret_all, li,
            vret.reshape(n, SUBLANES, LANES), kr.reshape(n, RET_HEADS, RET_QK_DIM),
            qr.reshape(n, RET_HEADS, RET_QK_DIM), prev=ret_new)
        ssd_n, ret_n = _sample_post(y_ssd.reshape(n, D_MODEL), xs, proj2, y_ret.reshape(n, D_MODEL),
                                    lp['dfull'], lp['ssd_nw'])
        h = _merge(y_s5, ssd_n.reshape(1, n, D_MODEL), ret_n.reshape(1, n, D_MODEL), proj, h,
                   *lp['merge_w'], li, n)
        last = li == len(layers) - 1
        h = _mlp(h.reshape(n, D_MODEL), lp['norm2_w'], *lp['mlp_w'], final_w, li,
                 n, MLP_TF, last).reshape(1, n, D_MODEL)
        outs['re'].append(hr.reshape(n, S5_GROUPS, S5_STATE))
        outs['im'].append(hi.reshape(n, S5_GROUPS, S5_STATE))
        outs['conv'].append(conv_new.reshape(n, SSD_CONV - 1, SSD_CONV_DIM))
    return (h.reshape(n, 1, D_MODEL), jnp.stack(outs['re']), jnp.stack(outs['im']),
            ssd_new.reshape(depth, n, SSD_HEADS, SSD_HEAD_DIM, SSD_STATE), jnp.stack(outs['conv']),
            ret_new.reshape(depth, n, RET_HEADS, RET_V_DIM, RET_QK_DIM))


def kernel(x_prompt, x_sample, state_s5_re, state_s5_im, state_ssd, state_conv, state_ret, norm1_w, w_in, s5_a_re, s5_a_im, s5_log_dt, s5_b_re, s5_b_im, s5_c_re, s5_c_im, s5_d, w_s5_out, ssd_conv_w, ssd_conv_b, ssd_dt_bias, ssd_a_log, ssd_d, ssd_norm_w, w_ssd_out, w_ret_out, w_out, norm2_w, w_mlp_up, w_mlp_down, final_norm_w):
    params = dict(norm1_w=norm1_w, w_in=w_in, s5_a_re=s5_a_re, s5_a_im=s5_a_im, s5_log_dt=s5_log_dt,
                  s5_b_re=s5_b_re, s5_b_im=s5_b_im, s5_c_re=s5_c_re, s5_c_im=s5_c_im, s5_d=s5_d,
                  w_s5_out=w_s5_out, ssd_conv_w=ssd_conv_w, ssd_conv_b=ssd_conv_b, ssd_dt_bias=ssd_dt_bias,
                  ssd_a_log=ssd_a_log, ssd_d=ssd_d, ssd_norm_w=ssd_norm_w, w_ssd_out=w_ssd_out,
                  w_ret_out=w_ret_out, w_out=w_out, norm2_w=norm2_w, w_mlp_up=w_mlp_up, w_mlp_down=w_mlp_down)
    depth = w_in.shape[0]
    big = ('w_in', 'w_s5_out', 'w_ssd_out', 'w_ret_out', 'w_out', 'w_mlp_up', 'w_mlp_down')
    layers = [_prep_layer({name: arr[l] for name, arr in params.items() if name not in big}) for l in range(depth)]
    merge_w = tuple(params[name].astype(BF16) for name in ('w_s5_out', 'w_ssd_out', 'w_ret_out', 'w_out'))
    mlp_w = tuple(params[name].astype(BF16) for name in ('w_mlp_up', 'w_mlp_down'))
    wt, w_dt = _proj_weight_slabs(w_in)
    for l, lp in enumerate(layers):
        lp['proj_w'] = (wt, w_dt, l)
        lp['merge_w'] = merge_w
        lp['mlp_w'] = mlp_w
    final_w = final_norm_w.reshape(1, D_MODEL).astype(F32)
    y_p, re_p, im_p, ssd_p, conv_p, ret_p = _prompt_trunk(x_prompt.astype(F32), layers, final_w)
    y_s, re_s, im_s, ssd_s, conv_s, ret_s = _sample_trunk(
        x_sample.astype(F32), (state_s5_re, state_s5_im, state_ssd, state_conv, state_ret), layers, final_w)
    return (y_p, y_s, re_p, im_p, ssd_p, conv_p, ret_p, re_s, im_s, ssd_s, conv_s, ret_s)
```

```python
import functools
import math

import jax
import jax.numpy as jnp
from jax import lax
from jax.experimental import pallas as pl
from jax.experimental.pallas import tpu as pltpu

F32 = jnp.float32
BF16 = jnp.bfloat16

D_MODEL = 1024
S5_GROUP = 16
S5_STATE = 64
S5_GROUPS = D_MODEL // S5_GROUP
S5_NSTATE = S5_GROUPS * S5_STATE
S5_SUPER = 8
S5_TIME_SPLIT = 4
SSD_HEADS = 16
SSD_HEAD_DIM = 64
SSD_GROUPS = 4
SSD_STATE = 128
SSD_CONV = 4
SSD_CONV_DIM = D_MODEL + 2 * SSD_GROUPS * SSD_STATE
RET_HEADS = 4
RET_QK_DIM = 128
RET_V_DIM = 256
ROPE_BASE = 10000.0
D_FF = 4 * D_MODEL
CHUNK = 128
EPS = 1e-6
PAST_LEN = 16384
LANES = 128
SUBLANES = 8
VMEM_LIMIT = 52 * 1024 * 1024

OFF_U, OFF_Z, OFF_XS, OFF_BC, OFF_Q, OFF_K, OFF_V, OFF_G, OFF_GATES = (
    0, 1024, 2048, 3072, 4096, 4608, 5120, 6144, 7168)
N_MAIN = 10240
SLAB_A_COLS = OFF_Q
SLAB_B_START = 2 * D_MODEL + SSD_CONV_DIM + SSD_HEADS

PROJ_TM, PROJ_TN = 1024, 2048
MERGE_TM = 512
MLP_TM, MLP_TF = 1024, 1024
S5_STEPS = 64
MIX_NSEQ = 2
STATE_TOKENS = 8


def _cparams(sem):
    return pltpu.CompilerParams(dimension_semantics=sem, vmem_limit_bytes=VMEM_LIMIT)


def _sigmoid(x):
    return 1.0 / (1.0 + jnp.exp(-x))


def _silu(x):
    return x * _sigmoid(x)


def _softplus(x):
    return jnp.maximum(x, 0.0) + jnp.log1p(jnp.exp(-jnp.abs(x)))


def _gelu_tanh(x):
    c = math.sqrt(2.0 / math.pi)
    half = 0.5 * x
    return half + half * jnp.tanh(x * (c + (c * 0.044715) * (x * x)))


def _rmsnorm(x, w):
    return x * lax.rsqrt(jnp.mean(x * x, axis=-1, keepdims=True) + EPS) * w


def _cumsum_rows(x):
    n = x.shape[0]
    row = lax.broadcasted_iota(jnp.int32, x.shape, 0)
    k = 1
    while k < n:
        x = x + jnp.where(row >= k, pltpu.roll(x, k, 0), 0.0)
        k *= 2
    return x


def _group_rms(y, width):
    outs = []
    for s in range(0, y.shape[-1], width):
        yg = y[:, s:s + width]
        outs.append(yg * lax.rsqrt(jnp.mean(yg * yg, axis=-1, keepdims=True) + EPS))
    return jnp.concatenate(outs, axis=-1)


def _dot(a, b):
    return jnp.dot(a, b, preferred_element_type=F32)


def _split_dot(x, e):
    hi = x.astype(BF16)
    lo = (x - hi.astype(F32)).astype(BF16)
    return _dot(hi, e) + _dot(lo, e)


def _proj_kernel(x_ref, nw_ref, wa_ref, wb_ref, wdt_ref, o_ref, odt_ref, xn_ref, *, na):
    j = pl.program_id(2)

    @pl.when(j == 0)
    def _():
        xn = _rmsnorm(x_ref[...], nw_ref[...]).astype(BF16)
        xn_ref[...] = xn
        odt_ref[...] = _dot_nt(xn, wdt_ref[...])

    @pl.when(j < na)
    def _():
        o_ref[...] = _dot_nt(xn_ref[...], wa_ref[...]).astype(o_ref.dtype)

    @pl.when(j >= na)
    def _():
        o_ref[...] = _dot_nt(xn_ref[...], wb_ref[0]).astype(o_ref.dtype)


DT_START = SLAB_B_START - SSD_HEADS
assert SLAB_B_START % (2 * SUBLANES) == 0


def _proj_weight_slabs(w_in):
    wt = jnp.swapaxes(w_in, 1, 2).astype(BF16)
    w_dt = jnp.pad(wt[:, DT_START:SLAB_B_START], ((0, 0), (0, LANES - SSD_HEADS), (0, 0)))
    return wt, w_dt


def _dot_nt(a, b):
    return lax.dot_general(a, b, (((1,), (1,)), ((), ())), preferred_element_type=F32)


def _proj_main(x3, nw, wt, wdt, layer, tm):
    nb, L, _ = x3.shape
    tn = PROJ_TN
    na = SLAB_A_COLS // tn
    return pl.pallas_call(
        functools.partial(_proj_kernel, na=na),
        grid=(nb, L // tm, N_MAIN // tn),
        in_specs=[pl.BlockSpec((None, tm, D_MODEL), lambda b, i, j: (b, i, 0)),
                  pl.BlockSpec((1, D_MODEL), lambda b, i, j: (0, 0)),
                  pl.BlockSpec((None, tn, D_MODEL), lambda b, i, j: (layer, jnp.minimum(j, na - 1), 0)),
                  pl.BlockSpec((pl.Element(1), pl.Element(tn), pl.Element(D_MODEL)),
                               lambda b, i, j: (layer, pl.multiple_of(
                                   SLAB_B_START + jnp.maximum(j - na, 0) * tn, 2 * SUBLANES), 0)),
                  pl.BlockSpec((None, LANES, D_MODEL), lambda b, i, j: (layer, 0, 0))],
        out_specs=[pl.BlockSpec((None, tm, tn), lambda b, i, j: (b, i, j)),
                   pl.BlockSpec((None, tm, LANES), lambda b, i, j: (b, i, 0))],
        out_shape=[jax.ShapeDtypeStruct((nb, L, N_MAIN), BF16), jax.ShapeDtypeStruct((nb, L, LANES), F32)],
        scratch_shapes=[pltpu.VMEM((tm, D_MODEL), BF16)],
        compiler_params=_cparams(("parallel", "parallel", "arbitrary")),
        name="proj_main",
    )(x3, nw, wt, wt, wdt)


def _s5_disc_kernel(are_ref, aim_ref, ldt_ref, bre_ref, bim_ref, abr_ref, abi_ref, bbr_ref, bbi_ref):
    ar, ai = are_ref[...], aim_ref[...]
    dt = jnp.exp(ldt_ref[...])
    mag = jnp.exp(ar * dt)
    abr = mag * jnp.cos(ai * dt)
    abi = mag * jnp.sin(ai * dt)
    abr_ref[...] = abr
    abi_ref[...] = abi
    nr, ni = abr - 1.0, abi
    den = 1.0 / (ar * ar + ai * ai)
    cr = (nr * ar + ni * ai) * den
    ci = (ni * ar - nr * ai) * den
    br, bi = bre_ref[...], bim_ref[...]
    bbr_ref[...] = cr * br - ci * bi
    bbi_ref[...] = cr * bi + ci * br


def _s5_discretise(a_re, a_im, log_dt, b_re, b_im):
    g, p = a_re.shape
    gs = b_re.shape[-1]
    full3 = lambda s: pl.BlockSpec(s, lambda: (0, 0, 0))
    return pl.pallas_call(
        _s5_disc_kernel,
        in_specs=[full3((g, 1, p)), full3((g, 1, p)), full3((g, 1, 1)), full3((g, gs, p)), full3((g, gs, p))],
        out_specs=[full3((g, 1, p)), full3((g, 1, p)), full3((g, gs, p)), full3((g, gs, p))],
        out_shape=[jax.ShapeDtypeStruct((g, 1, p), F32)] * 2 + [jax.ShapeDtypeStruct((g, gs, p), F32)] * 2,
        name="s5_discretise",
    )(a_re.reshape(g, 1, p), a_im.reshape(g, 1, p), log_dt.reshape(g, 1, 1),
      jnp.swapaxes(b_re, 1, 2), jnp.swapaxes(b_im, 1, 2))


def _s5_kernel(u_ref, wbr_ref, wbi_ref, ar_ref, ai_ref, wcr_ref, wci_ref, d_ref, h0r_ref, h0i_ref,
               y_ref, hr_ref, hi_ref, bur, bui, *, nbatch, steps, lane_w):
    @pl.when(pl.program_id(0) == 0)
    def _():
        hr_ref[...] = h0r_ref[...]
        hi_ref[...] = h0i_ref[...]

    cw = S5_SUPER * S5_GROUP
    sw = S5_SUPER * S5_STATE
    batch_major = steps > 1
    u = u_ref[...].astype(F32)
    if batch_major:
        u = jnp.swapaxes(u, 0, 1).reshape(steps * nbatch, D_MODEL)
    ub = u.astype(BF16)
    nsub = S5_TIME_SPLIT if steps % (2 * S5_TIME_SPLIT) == 0 else 1
    sub_steps = steps // nsub
    nsup = S5_GROUPS // S5_SUPER
    for sb in range(nsub):
        t0 = sb * sub_steps
        rs = slice(t0 * nbatch, (t0 + sub_steps) * nbatch)
        for j in range(nsup):
            uj = ub[rs, j * cw:(j + 1) * cw]
            bur[rs, j * sw:(j + 1) * sw] = _dot(uj, wbr_ref[j])
            bui[rs, j * sw:(j + 1) * sw] = _dot(uj, wbi_ref[j])

        for j in range(nsup):
            for lc in range(sw // lane_w):
                sl = slice(j * sw + lc * lane_w, j * sw + (lc + 1) * lane_w)
                are = jnp.broadcast_to(ar_ref[:, sl], (nbatch, lane_w))
                aim = jnp.broadcast_to(ai_ref[:, sl], (nbatch, lane_w))
                hr, hi = hr_ref[:, sl], hi_ref[:, sl]
                for t in range(t0, t0 + sub_steps):
                    rows = slice(t * nbatch, (t + 1) * nbatch)
                    hr, hi = are * hr - aim * hi + bur[rows, sl], are * hi + aim * hr + bui[rows, sl]
                    bur[rows, sl] = hr
                    bui[rows, sl] = hi
                hr_ref[:, sl] = hr
                hi_ref[:, sl] = hi

        for j in range(nsup):
            ch = slice(j * cw, (j + 1) * cw)
            yj = (_dot(bur[rs, j * sw:(j + 1) * sw].astype(BF16), wcr_ref[j])
                  + _dot(bui[rs, j * sw:(j + 1) * sw].astype(BF16), wci_ref[j]))
            yj = _gelu_tanh(yj + d_ref[:, ch] * u[rs, ch])
            if batch_major:
                y_ref[:, t0:t0 + sub_steps, ch] = jnp.swapaxes(
                    yj.reshape(sub_steps, nbatch, cw), 0, 1).astype(y_ref.dtype)
            else:
                y_ref[:, ch] = yj.astype(y_ref.dtype)


def _s5_scan(proj, wbr, wbi, abr, abi, wcr, wci, d, h0r, h0i, steps):
    nb, L, _ = proj.shape
    nsup = S5_GROUPS // S5_SUPER
    cw, sw = S5_SUPER * S5_GROUP, S5_SUPER * S5_STATE
    c2 = lambda s: pl.BlockSpec(s, lambda c: (0, 0))
    c3 = lambda s: pl.BlockSpec(s, lambda c: (0, 0, 0))
    if steps > 1:
        nbatch, grid = nb, L // steps
        u_spec = pl.BlockSpec((nb, steps, D_MODEL), lambda c: (0, c, OFF_U // D_MODEL))
        y_spec = pl.BlockSpec((nb, steps, D_MODEL), lambda c: (0, c, 0))
    else:
        nbatch, grid = L, 1
        u_spec = pl.BlockSpec((None, L, D_MODEL), lambda c: (0, 0, OFF_U // D_MODEL))
        y_spec = pl.BlockSpec((None, L, D_MODEL), lambda c: (0, 0, 0))
    blk = nbatch * steps
    lane_w = max(LANES, S5_NSTATE // nbatch)
    kern = functools.partial(_s5_kernel, nbatch=nbatch, steps=steps, lane_w=lane_w)
    return pl.pallas_call(
        kern,
        grid=(grid,),
        in_specs=[u_spec,
                  c3((nsup, cw, sw)), c3((nsup, cw, sw)),
                  c2((1, S5_NSTATE)), c2((1, S5_NSTATE)),
                  c3((nsup, sw, cw)), c3((nsup, sw, cw)),
                  c2((1, D_MODEL)),
                  c2((nbatch, S5_NSTATE)), c2((nbatch, S5_NSTATE))],
        out_specs=[y_spec, c2((nbatch, S5_NSTATE)), c2((nbatch, S5_NSTATE))],
        out_shape=[jax.ShapeDtypeStruct((nb, L, D_MODEL), BF16),
                   jax.ShapeDtypeStruct((nbatch, S5_NSTATE), F32),
                   jax.ShapeDtypeStruct((nbatch, S5_NSTATE), F32)],
        scratch_shapes=[pltpu.VMEM((blk, S5_NSTATE), F32), pltpu.VMEM((blk, S5_NSTATE), F32)],
        compiler_params=_cparams(("arbitrary",)),
        name="s5_scan",
    )(proj, wbr, wbi, abr, abi, wcr, wci, d, h0r, h0i)


def _s5_weights(abr, abi, bbr, bbi, c_re, c_im):
    nsup = S5_GROUPS // S5_SUPER
    eye = jnp.eye(S5_SUPER, dtype=F32)

    def bexp(b):
        b = b.reshape(nsup, S5_SUPER, S5_GROUP, S5_STATE)
        return (b[:, :, :, None, :] * eye[None, :, None, :, None]).reshape(
            nsup, S5_SUPER * S5_GROUP, S5_SUPER * S5_STATE).astype(BF16)

    def cexp(c):
        c = jnp.swapaxes(c, 1, 2).reshape(nsup, S5_SUPER, S5_STATE, S5_GROUP)
        return (c[:, :, :, None, :] * eye[None, :, None, :, None]).reshape(
            nsup, S5_SUPER * S5_STATE, S5_SUPER * S5_GROUP).astype(BF16)

    return (bexp(bbr), bexp(bbi), abr.reshape(1, S5_NSTATE), abi.reshape(1, S5_NSTATE),
            cexp(c_re), cexp(-c_im))


def _ssd_gate_norm(y, xs, z, dfull, nw):
    y = (y + dfull * xs) * _silu(z)
    return _group_rms(y, D_MODEL // SSD_GROUPS) * nw


CONV_PAD = 16
CONV_ROWS = 2 * CHUNK


def _conv_shift_mats():
    i = jnp.arange(CHUNK)[:, None]
    j = jnp.arange(CONV_ROWS)[None, :]
    return jnp.stack([(j == CONV_PAD + i - s) for s in range(1, SSD_CONV)]).astype(BF16)


def _ssd_chunk_seq(xs_ref, bc_ref, z_ref, dt_ref, cw_ref, cb_ref, b16_ref, al16_ref,
                   dfull_ref, nw_ref, eh_ref, sh_ref, o_ref, xpad, st):
    q = CHUNK
    pad = CONV_PAD
    xpad[0:pad, :] = xpad[q:q + pad, :]

    xpad[pad:pad + q, :D_MODEL] = xs_ref[...]
    xpad[pad:pad + q, D_MODEL:] = bc_ref[...]
    staged = xpad[...]
    acc = cb_ref[...] + xpad[pad:pad + q, :].astype(F32) * cw_ref[SSD_CONV - 1:SSD_CONV, :]
    for s in range(1, SSD_CONV):
        acc = acc + _dot(sh_ref[s - 1], staged) * cw_ref[SSD_CONV - 1 - s:SSD_CONV - s, :]
    xc = _silu(acc)
    xs = xc[:, :D_MODEL]
    nbc = SSD_GROUPS * SSD_STATE
    bs = xc[:, D_MODEL:D_MODEL + nbc]
    cs = xc[:, D_MODEL + nbc:]

    delta16 = _softplus(dt_ref[...] + b16_ref[...])
    acum16 = _cumsum_rows(delta16 * (-jnp.exp(al16_ref[...])))
    acum_log2 = acum16 * math.log2(math.e)
    acum_log2_t = acum_log2.T
    delta_t = delta16.T
    last16 = acum16[q - 1:q, :]
    ea_f = _split_dot(jnp.exp(acum16), eh_ref[...])
    dw_f = _split_dot(delta16 * jnp.exp(last16 - acum16), eh_ref[...])
    xw = (xs * dw_f).astype(BF16)

    ri = lax.broadcasted_iota(jnp.int32, (q, q), 0)
    ci = lax.broadcasted_iota(jnp.int32, (q, q), 1)
    causal = ri >= ci
    hw = D_MODEL // SSD_GROUPS
    hpg = SSD_HEADS // SSD_GROUPS
    xsb = xs.astype(BF16)
    ys = []
    for g in range(SSD_GROUPS):
        bg = bs[:, g * SSD_STATE:(g + 1) * SSD_STATE]
        cg = cs[:, g * SSD_STATE:(g + 1) * SSD_STATE].astype(BF16)
        bgt = bg.T.astype(BF16)
        scores = _dot(cg, bgt)
        sg = st[:, g * hw:(g + 1) * hw]
        y_inter = _dot(cg, sg.astype(BF16)) * ea_f[:, g * hw:(g + 1) * hw]
        for r in range(hpg):
            hd = g * hpg + r
            seg = acum_log2[:, hd:hd + 1] - acum_log2_t[hd:hd + 1, :]
            lm = jnp.exp2(jnp.where(causal, seg, -jnp.inf)) * delta_t[hd:hd + 1, :]
            p = (scores * lm).astype(BF16)
            ys.append(_dot(p, xsb[:, hd * SSD_HEAD_DIM:(hd + 1) * SSD_HEAD_DIM])
                      + y_inter[:, r * SSD_HEAD_DIM:(r + 1) * SSD_HEAD_DIM])
        st[:, g * hw:(g + 1) * hw] = (ea_f[q - 1:q, g * hw:(g + 1) * hw] * sg
                                      + _dot(bgt, xw[:, g * hw:(g + 1) * hw]))
    y = jnp.concatenate(ys, axis=-1)
    o_ref[...] = _ssd_gate_norm(y, xs, z_ref[...].astype(F32), dfull_ref[...], nw_ref[...]).astype(o_ref.dtype)


def _rotary(x, cos2, sin2):
    outs = []
    for h in range(RET_HEADS):
        xh = x[:, h * RET_QK_DIM:(h + 1) * RET_QK_DIM]
        outs.append(xh * cos2 + pltpu.roll(xh, RET_QK_DIM // 2, 1) * sin2)
    return outs


def _ret_gate_norm(y, g):
    return _group_rms(y, RET_V_DIM) * _silu(g)


def _ret_chunk_seq(q_ref, k_ref, v_ref, g_ref, cos2, sin2, lm_ref, gp_ref, we_ref, gq_ref, o_ref, st):
    qr = _rotary(q_ref[...].astype(F32), cos2, sin2)
    kr = _rotary(k_ref[...].astype(F32), cos2, sin2)
    vb = v_ref[...]
    vw = (vb.astype(F32) * we_ref[...]).astype(BF16)
    ys = []
    for h in range(RET_HEADS):
        sl = slice(h * RET_V_DIM, (h + 1) * RET_V_DIM)
        qh = qr[h].astype(BF16)
        kht = (kr[h] * (RET_QK_DIM ** -0.5)).T.astype(BF16)
        p = (_dot(qh, kht) * lm_ref[h]).astype(BF16)
        sh = st[:, sl]
        ys.append(_dot(p, vb[:, sl]) + _dot(qh, sh.astype(BF16)) * gp_ref[:, sl])
        st[:, sl] = gq_ref[:, sl] * sh + _dot(kht, vw[:, sl])
    y = jnp.concatenate(ys, axis=-1)
    o_ref[...] = _ret_gate_norm(y, g_ref[...].astype(F32)).astype(o_ref.dtype)


def _mix_chunk_kernel(xs_ref, bc_ref, z_ref, dt_ref, cw_ref, cb_ref, b16_ref, al16_ref, dfull_ref, nw_ref, eh_ref,
                      sh_ref, q_ref, k_ref, v_ref, g_ref, cos_ref, sin_ref, lm_ref, gp_ref, we_ref, gq_ref,
                      so_ref, ss_ref, ro_ref, rs_ref, xpad, st_ssd, st_ret):
    c = pl.program_id(1)
    nc = pl.num_programs(1)
    nseq = xs_ref.shape[0]

    @pl.when(c == 0)
    def _():
        xpad[...] = jnp.zeros(xpad.shape, BF16)
        st_ssd[...] = jnp.zeros(st_ssd.shape, F32)
        st_ret[...] = jnp.zeros(st_ret.shape, F32)

    cos2, sin2 = cos_ref[...], sin_ref[...]
    for bi in range(nseq):
        _ssd_chunk_seq(xs_ref.at[bi], bc_ref.at[bi], z_ref.at[bi], dt_ref.at[bi], cw_ref, cb_ref, b16_ref,
                       al16_ref, dfull_ref, nw_ref, eh_ref, sh_ref, so_ref.at[bi], xpad.at[bi], st_ssd.at[bi])
        _ret_chunk_seq(q_ref.at[bi], k_ref.at[bi], v_ref.at[bi], g_ref.at[bi], cos2, sin2, lm_ref, gp_ref, we_ref,
                       gq_ref, ro_ref.at[bi], st_ret.at[bi])

    @pl.when(c == nc - 1)
    def _():
        for bi in range(nseq):
            for k in range(D_MODEL // LANES):
                ss_ref[bi, k * LANES:(k + 1) * LANES, :] = st_ssd[bi, :, k * LANES:(k + 1) * LANES].T
                rs_ref[bi, k * LANES:(k + 1) * LANES, :] = st_ret[bi, :, k * LANES:(k + 1) * LANES].T


def _ret_consts(q):
    hidx = jnp.arange(RET_HEADS, dtype=F32)
    log_gamma = jnp.log1p(-jnp.exp2(-5.0 - hidx))
    i = jnp.arange(q, dtype=F32)
    seg = (i[:, None] - i[None, :])[None] * log_gamma[:, None, None]
    lm = jnp.exp(jnp.where((i[:, None] >= i[None, :])[None], seg, -jnp.inf))
    rep = lambda t: jnp.repeat(t, RET_V_DIM, axis=-1)
    gp = rep(jnp.exp((i[:, None] + 1.0) * log_gamma[None, :]))
    we = rep(jnp.exp((q - 1.0 - i[:, None]) * log_gamma[None, :]))
    gq = rep(jnp.exp(q * log_gamma)[None, :])
    return log_gamma, lm, gp, we, gq


def _rope_tables(pos):
    half = RET_QK_DIM // 2
    inv = ROPE_BASE ** (-jnp.arange(half, dtype=F32) / half)
    ang = pos.astype(F32)[:, None] * inv[None]
    cos, sin = jnp.cos(ang), jnp.sin(ang)
    return jnp.concatenate([cos, cos], axis=-1), jnp.concatenate([-sin, sin], axis=-1)


def _mix_prompt(proj, dt, cw, cb, b16, al16, dfull, nw, eh, cos2, sin2, lm, gp, we, gq, nseq=MIX_NSEQ):
    nb, L, _ = proj.shape
    q = CHUNK
    qkw = RET_HEADS * RET_QK_DIM
    c2 = lambda s: pl.BlockSpec(s, lambda b, c: (0, 0))
    rows = lambda w, off: pl.BlockSpec((nseq, q, w), lambda b, c: (b, c, off // w))
    state = pl.BlockSpec((nseq, D_MODEL, LANES), lambda b, c: (b, 0, 0))
    return pl.pallas_call(
        _mix_chunk_kernel,
        grid=(nb // nseq, L // q),
        in_specs=[rows(D_MODEL, OFF_XS), rows(D_MODEL, OFF_BC), rows(D_MODEL, OFF_Z), rows(LANES, 0),
                  c2((SSD_CONV, SSD_CONV_DIM)), c2((1, SSD_CONV_DIM)),
                  c2((1, LANES)), c2((1, LANES)),
                  c2((1, D_MODEL)), c2((1, D_MODEL)),
                  c2((LANES, D_MODEL)),
                  pl.BlockSpec((SSD_CONV - 1, q, CONV_ROWS), lambda b, c: (0, 0, 0)),
                  rows(qkw, OFF_Q), rows(qkw, OFF_K), rows(D_MODEL, OFF_V), rows(D_MODEL, OFF_G),
                  pl.BlockSpec((q, RET_QK_DIM), lambda b, c: (c, 0)),
                  pl.BlockSpec((q, RET_QK_DIM), lambda b, c: (c, 0)),
                  pl.BlockSpec((RET_HEADS, q, q), lambda b, c: (0, 0, 0)),
                  c2((q, D_MODEL)), c2((q, D_MODEL)), c2((1, D_MODEL))],
        out_specs=[rows(D_MODEL, 0), state, rows(D_MODEL, 0), state],
        out_shape=[jax.ShapeDtypeStruct((nb, L, D_MODEL), BF16),
                   jax.ShapeDtypeStruct((nb, D_MODEL, SSD_STATE), F32),
                   jax.ShapeDtypeStruct((nb, L, D_MODEL), BF16),
                   jax.ShapeDtypeStruct((nb, D_MODEL, RET_QK_DIM), F32)],
        scratch_shapes=[pltpu.VMEM((nseq, CONV_ROWS, SSD_CONV_DIM), BF16),
                        pltpu.VMEM((nseq, SSD_STATE, D_MODEL), F32),
                        pltpu.VMEM((nseq, RET_QK_DIM, D_MODEL), F32)],
        compiler_params=_cparams(("parallel", "arbitrary")),
        name="mix_chunk",
    )(proj, proj, proj, dt, cw, cb, b16, al16, dfull, nw, eh, _conv_shift_mats(),
      proj, proj, proj, proj, cos2, sin2, lm, gp, we, gq)


def _sample_pre_kernel(xsin_ref, bcin_ref, dt_ref, q_ref, k_ref, conv_ref, cw_ref, cb_ref, b16_ref, al16_ref,
                       eh_ref, cos_ref, sin_ref,
                       vv_ref, e16_ref, bs_ref, cs_ref, xs_ref, convn_ref, qr_ref, kr_ref):
    cd = SSD_CONV_DIM
    x = jnp.concatenate([xsin_ref[...], bcin_ref[...]], axis=-1).astype(F32)
    acc = cb_ref[...] + x * cw_ref[SSD_CONV - 1:SSD_CONV, :]
    for tap in range(SSD_CONV - 1):
        acc = acc + conv_ref[:, tap * cd:(tap + 1) * cd] * cw_ref[tap:tap + 1, :]
    convn_ref[:, 0:(SSD_CONV - 2) * cd] = conv_ref[:, cd:(SSD_CONV - 1) * cd]
    convn_ref[:, (SSD_CONV - 2) * cd:] = x
    xc = _silu(acc)
    xs = xc[:, :D_MODEL]
    nbc = SSD_GROUPS * SSD_STATE
    xs_ref[...] = xs
    bs_ref[...] = xc[:, D_MODEL:D_MODEL + nbc]
    cs_ref[...] = xc[:, D_MODEL + nbc:]
    delta16 = _softplus(dt_ref[...] + b16_ref[...])
    e16_ref[...] = jnp.exp(delta16 * (-jnp.exp(al16_ref[...])))
    vv_ref[...] = xs * _split_dot(delta16, eh_ref[...])
    cos2, sin2 = cos_ref[...], sin_ref[...]
    qr_ref[...] = jnp.concatenate(_rotary(q_ref[...].astype(F32), cos2, sin2), axis=-1)
    kr_ref[...] = jnp.concatenate(_rotary(k_ref[...].astype(F32), cos2, sin2), axis=-1) * (RET_QK_DIM ** -0.5)


def _sample_pre(proj2, dt2, conv2, cw, cb, b16, al16, eh, cos2, sin2):
    n = proj2.shape[0]
    qkw = RET_HEADS * RET_QK_DIM
    nbc = SSD_GROUPS * SSD_STATE
    full = lambda s: pl.BlockSpec(s, lambda i: (0, 0))
    col = lambda w, off: pl.BlockSpec((n, w), lambda i: (0, off // w))
    shp = lambda w: jax.ShapeDtypeStruct((n, w), F32)
    return pl.pallas_call(
        _sample_pre_kernel,
        grid=(1,),
        in_specs=[col(D_MODEL, OFF_XS), col(D_MODEL, OFF_BC), full((n, LANES)), col(qkw, OFF_Q), col(qkw, OFF_K),
                  full((n, (SSD_CONV - 1) * SSD_CONV_DIM)),
                  full((SSD_CONV, SSD_CONV_DIM)), full((1, SSD_CONV_DIM)),
                  full((1, LANES)), full((1, LANES)),
                  full((LANES, D_MODEL)), full((1, RET_QK_DIM)), full((1, RET_QK_DIM))],
        out_specs=[full((n, D_MODEL)), full((n, LANES)), full((n, nbc)), full((n, nbc)), full((n, D_MODEL)),
                   full((n, (SSD_CONV - 1) * SSD_CONV_DIM)), full((n, qkw)), full((n, qkw))],
        out_shape=[shp(D_MODEL), shp(LANES), shp(nbc), shp(nbc), shp(D_MODEL),
                   shp((SSD_CONV - 1) * SSD_CONV_DIM), shp(qkw), shp(qkw)],
        compiler_params=_cparams(("arbitrary",)),
        name="sample_pre",
    )(proj2, proj2, dt2, proj2, proj2, conv2, cw, cb, b16, al16, eh, cos2, sin2)


STATE_ROWS = D_MODEL
STATE_BLK = 64


def _state_step_kernel(e_ref, s_ref, v_ref, k_ref, q_ref, *rest, bb, layer, fill):
    sn_all, y_ref = rest[-2:]
    if fill:
        for l in range(sn_all.shape[0]):
            if l != layer:
                sn_all[l] = jnp.zeros(sn_all.shape[1:], F32)
        sn_ref = sn_all.at[layer]
    else:
        sn_ref = sn_all
    i = pl.program_id(0)
    nblk = STATE_ROWS // STATE_BLK
    gw = STATE_ROWS // 4
    ones = jnp.ones((SUBLANES, LANES), BF16)
    zpad = jnp.zeros((LANES - SUBLANES, LANES), F32)
    nt = (((1,), (1,)), ((), ()))
    for t in range(bb):
        b = i * bb + t
        vt = jnp.concatenate([v_ref[t], zpad], axis=0).T
        kk = k_ref[t]
        qq = q_ref[t]
        for cch in range(STATE_ROWS // LANES):
            prods = []
            for half in range(LANES // STATE_BLK):
                blk = cch * (LANES // STATE_BLK) + half
                r0 = blk * STATE_BLK
                g = r0 // gw
                vcol = vt[half * STATE_BLK:(half + 1) * STATE_BLK, cch:cch + 1]
                sn = e_ref[b * nblk + blk] * s_ref[t, r0:r0 + STATE_BLK, :] + vcol * kk[g:g + 1, :]
                sn_ref[t, r0:r0 + STATE_BLK, :] = sn
                prods.append(sn * qq[g:g + 1, :])
            p = jnp.concatenate(prods, axis=0).astype(BF16)
            ysum = lax.dot_general(ones, p, nt, preferred_element_type=F32)
            y_ref[t, cch:cch + 1, :] = ysum[0:1, :]


def _state_step(e_flat, s_all, layer, v3, k3, q3, prev=None, bb=STATE_TOKENS):
    depth, n = s_all.shape[:2]
    blk3 = lambda a, b_: pl.BlockSpec((bb, a, b_), lambda i: (i, 0, 0))
    sblk = pl.BlockSpec((None, bb, STATE_ROWS, LANES), lambda i: (layer, i, 0, 0))
    in_specs = [pl.BlockSpec(memory_space=pltpu.SMEM), sblk, blk3(SUBLANES, LANES), blk3(4, LANES), blk3(4, LANES)]
    args = [e_flat, s_all, v3, k3, q3]
    aliases = {}
    if prev is not None:
        in_specs.append(pl.BlockSpec(memory_space=pl.ANY))
        args.append(prev)
        aliases = {len(args) - 1: 0}
        oblk = sblk
    else:
        oblk = pl.BlockSpec((depth, bb, STATE_ROWS, LANES), lambda i: (0, i, 0, 0))
    return pl.pallas_call(
        functools.partial(_state_step_kernel, bb=bb, layer=layer, fill=prev is None),
        grid=(n // bb,),
        in_specs=in_specs,
        out_specs=[oblk, blk3(SUBLANES, LANES)],
        out_shape=[jax.ShapeDtypeStruct((depth, n, STATE_ROWS, LANES), F32),
                   jax.ShapeDtypeStruct((n, SUBLANES, LANES), F32)],
        input_output_aliases=aliases,
        compiler_params=_cparams(("parallel",)),
        name="state_step",
    )(*args)


def _sample_post_kernel(ys_ref, xs_ref, z_ref, yr_ref, g_ref, dfull_ref, nw_ref, os_ref, or_ref):
    os_ref[...] = _ssd_gate_norm(ys_ref[...], xs_ref[...], z_ref[...].astype(F32), dfull_ref[...],
                                 nw_ref[...]).astype(os_ref.dtype)
    or_ref[...] = _ret_gate_norm(yr_ref[...], g_ref[...].astype(F32)).astype(or_ref.dtype)


def _sample_post(y_ssd, xs, proj2, y_ret, dfull, nw):
    n = y_ssd.shape[0]
    full = lambda s: pl.BlockSpec(s, lambda i: (0, 0))
    col = lambda off: pl.BlockSpec((n, D_MODEL), lambda i: (0, off // D_MODEL))
    return pl.pallas_call(
        _sample_post_kernel,
        grid=(1,),
        in_specs=[full((n, D_MODEL)), full((n, D_MODEL)), col(OFF_Z), full((n, D_MODEL)), col(OFF_G),
                  full((1, D_MODEL)), full((1, D_MODEL))],
        out_specs=[full((n, D_MODEL)), full((n, D_MODEL))],
        out_shape=[jax.ShapeDtypeStruct((n, D_MODEL), BF16)] * 2,
        compiler_params=_cparams(("arbitrary",)),
        name="sample_post",
    )(y_ssd, xs, proj2, y_ret, proj2, dfull, nw)


def _merge_kernel(s5_ref, ssd_ref, ret_ref, ga_ref, gb_ref, gc_ref, h_ref, ws5_ref, wssd_ref, wret_ref, wout_ref,
                  o_ref):
    glu = _dot(s5_ref[...], ws5_ref[...])
    y_a = glu[:, :D_MODEL] * _sigmoid(glu[:, D_MODEL:])
    y_b = _dot(ssd_ref[...], wssd_ref[...])
    y_c = _dot(ret_ref[...], wret_ref[...])
    merged = (_sigmoid(ga_ref[...].astype(F32)) * y_a + _sigmoid(gb_ref[...].astype(F32)) * y_b
              + _sigmoid(gc_ref[...].astype(F32)) * y_c)
    o_ref[...] = h_ref[...] + _dot(merged.astype(BF16), wout_ref[...])


def _merge(s5_g, ssd_n, ret_n, proj, h3, ws5, wssd, wret, wout, layer, tm):
    nb, L, _ = h3.shape
    row3 = lambda off: pl.BlockSpec((None, tm, D_MODEL), lambda b, i: (b, i, off // D_MODEL))
    wfull = lambda s: pl.BlockSpec((None,) + s, lambda b, i: (layer, 0, 0), pipeline_mode=pl.Buffered(1))
    return pl.pallas_call(
        _merge_kernel,
        grid=(nb, L // tm),
        in_specs=[row3(0), row3(0), row3(0),
                  row3(OFF_GATES), row3(OFF_GATES + D_MODEL), row3(OFF_GATES + 2 * D_MODEL), row3(0),
                  wfull((D_MODEL, 2 * D_MODEL)), wfull((D_MODEL, D_MODEL)), wfull((D_MODEL, D_MODEL)),
                  wfull((D_MODEL, D_MODEL))],
        out_specs=row3(0),
        out_shape=jax.ShapeDtypeStruct((nb, L, D_MODEL), F32),
        compiler_params=_cparams(("parallel", "parallel")),
        name="merge",
    )(s5_g, ssd_n, ret_n, proj, proj, proj, h3, ws5, wssd, wret, wout)


def _mlp_kernel(x_ref, nw_ref, wup_ref, wdn_ref, fw_ref, o_ref, xn_ref, acc_ref, *, final_norm):
    k = pl.program_id(1)
    nk = pl.num_programs(1)

    @pl.when(k == 0)
    def _():
        x = x_ref[...]
        xn_ref[...] = _rmsnorm(x, nw_ref[...]).astype(BF16)
        acc_ref[...] = x

    hid = jnp.maximum(_dot(xn_ref[...], wup_ref[...]), 0.0)
    acc_ref[...] += _dot((hid * hid).astype(BF16), wdn_ref[...])

    @pl.when(k == nk - 1)
    def _():
        h = acc_ref[...]
        o_ref[...] = _rmsnorm(h, fw_ref[...]) if final_norm else h


def _mlp(h2, nw, wup, wdn, fw, layer, tm, tf, final_norm):
    rows = h2.shape[0]
    return pl.pallas_call(
        functools.partial(_mlp_kernel, final_norm=final_norm),
        grid=(rows // tm, D_FF // tf),
        in_specs=[pl.BlockSpec((tm, D_MODEL), lambda i, k: (i, 0)),
                  pl.BlockSpec((1, D_MODEL), lambda i, k: (0, 0)),
                  pl.BlockSpec((None, D_MODEL, tf), lambda i, k: (layer, 0, k)),
                  pl.BlockSpec((None, tf, D_MODEL), lambda i, k: (layer, k, 0)),
                  pl.BlockSpec((1, D_MODEL), lambda i, k: (0, 0))],
        out_specs=pl.BlockSpec((tm, D_MODEL), lambda i, k: (i, 0)),
        out_shape=jax.ShapeDtypeStruct((rows, D_MODEL), F32),
        scratch_shapes=[pltpu.VMEM((tm, D_MODEL), BF16), pltpu.VMEM((tm, D_MODEL), F32)],
        compiler_params=_cparams(("parallel", "arbitrary")),
        name="mlp",
    )(h2, nw, wup, wdn, fw)


def _prep_layer(p):
    out = {}
    row = lambda a: a.reshape(1, -1).astype(F32)
    padrow = lambda a: jnp.pad(a.astype(F32), (0, LANES - a.shape[0])).reshape(1, LANES)
    rep = lambda a: jnp.repeat(a.astype(F32), SSD_HEAD_DIM).reshape(1, D_MODEL)
    out['norm1_w'] = row(p['norm1_w'])
    out['norm2_w'] = row(p['norm2_w'])
    out['conv_w'] = p['ssd_conv_w'].astype(F32)
    out['conv_b'] = row(p['ssd_conv_b'])
    out['b16'] = padrow(p['ssd_dt_bias'])
    out['al16'] = padrow(p['ssd_a_log'])
    out['dfull'] = rep(p['ssd_d'])
    out['ssd_nw'] = row(p['ssd_norm_w'])
    out['s5_d'] = row(p['s5_d'])
    abr, abi, bbr, bbi = _s5_discretise(p['s5_a_re'].astype(F32), p['s5_a_im'].astype(F32),
                                        p['s5_log_dt'].astype(F32), p['s5_b_re'].astype(F32),
                                        p['s5_b_im'].astype(F32))
    out['s5'] = _s5_weights(abr, abi, bbr, bbi, p['s5_c_re'].astype(F32), p['s5_c_im'].astype(F32))
    return out


def _head_expand():
    e = (jnp.arange(LANES)[:, None] == (jnp.arange(D_MODEL)[None, :] // SSD_HEAD_DIM))
    return e.astype(BF16)


def _tile(n, pref):
    return pref if n % pref == 0 else n


def _prompt_trunk(x, layers, final_w):
    nb, L, _ = x.shape
    eh = _head_expand()
    _, lm, gp, we, gq = _ret_consts(CHUNK)
    cos2, sin2 = _rope_tables(jnp.arange(L, dtype=jnp.int32))
    zeros_state = jnp.zeros((nb, S5_NSTATE), F32)
    h = x
    outs = {k: [] for k in ('re', 'im', 'ssd', 'conv', 'ret')}
    for li, lp in enumerate(layers):
        proj, dt = _proj_main(h, lp['norm1_w'], *lp['proj_w'], _tile(L, PROJ_TM))
        wbr, wbi, abr, abi, wcr, wci = lp['s5']
        y_s5, hr, hi = _s5_scan(proj, wbr, wbi, abr, abi, wcr, wci, lp['s5_d'],
                                zeros_state, zeros_state, S5_STEPS)
        ssd_n, ssd_s, ret_n, ret_s = _mix_prompt(proj, dt, lp['conv_w'], lp['conv_b'], lp['b16'], lp['al16'],
                                                 lp['dfull'], lp['ssd_nw'], eh, cos2, sin2, lm, gp, we, gq)
        h = _merge(y_s5, ssd_n, ret_n, proj, h,
                   *lp['merge_w'], li, _tile(L, MERGE_TM))
        last = li == len(layers) - 1
        h = _mlp(h.reshape(nb * L, D_MODEL), lp['norm2_w'], *lp['mlp_w'], final_w, li,
                 _tile(nb * L, MLP_TM), MLP_TF, last).reshape(nb, L, D_MODEL)
        outs['re'].append(hr.reshape(nb, S5_GROUPS, S5_STATE))
        outs['im'].append(hi.reshape(nb, S5_GROUPS, S5_STATE))
        outs['ssd'].append(ssd_s.reshape(nb, SSD_HEADS, SSD_HEAD_DIM, SSD_STATE))
        outs['conv'].append(proj[:, L - (SSD_CONV - 1):, OFF_XS:OFF_XS + SSD_CONV_DIM].astype(F32))
        outs['ret'].append(ret_s.reshape(nb, RET_HEADS, RET_V_DIM, RET_QK_DIM))
    return (h,) + tuple(jnp.stack(outs[k]) for k in ('re', 'im', 'ssd', 'conv', 'ret'))


def _sample_trunk(x, states, layers, final_w):
    n = x.shape[0]
    st_re, st_im, st_ssd, st_conv, st_ret = states
    eh = _head_expand()
    log_gamma, _, _, _, _ = _ret_consts(1)
    cos2, sin2 = _rope_tables(PAST_LEN + jnp.arange(1, dtype=jnp.int32))
    e_ret = jnp.broadcast_to(jnp.repeat(jnp.exp(log_gamma), RET_V_DIM // STATE_BLK)[None, :],
                             (n, STATE_ROWS // STATE_BLK)).reshape(-1)
    h = x.reshape(1, n, D_MODEL)
    depth = len(layers)
    ssd_all = st_ssd.reshape(depth, n, STATE_ROWS, SSD_STATE).astype(F32)
    ret_all = st_ret.reshape(depth, n, STATE_ROWS, RET_QK_DIM).astype(F32)
    ssd_new = ret_new = None
    outs = {k: [] for k in ('re', 'im', 'conv')}
    for li, lp in enumerate(layers):
        proj, dt = _proj_main(h, lp['norm1_w'], *lp['proj_w'], n)
        wbr, wbi, abr, abi, wcr, wci = lp['s5']
        y_s5, hr, hi = _s5_scan(proj, wbr, wbi, abr, abi, wcr, wci, lp['s5_d'],
                                st_re[li].reshape(n, S5_NSTATE).astype(F32),
                                st_im[li].reshape(n, S5_NSTATE).astype(F32), 1)
        proj2, dt2 = proj.reshape(n, N_MAIN), dt.reshape(n, LANES)
        conv2 = st_conv[li].reshape(n, (SSD_CONV - 1) * SSD_CONV_DIM).astype(F32)
        vv, e16, bs, cs, xs, conv_new, qr, kr = _sample_pre(
            proj2, dt2, conv2, lp['conv_w'], lp['conv_b'], lp['b16'], lp['al16'], eh, cos2, sin2)
        ssd_new, y_ssd = _state_step(
            e16[:, :SSD_HEADS].reshape(-1), ssd_all, li,
            vv.reshape(n, SUBLANES, LANES), bs.reshape(n, SSD_GROUPS, SSD_STATE),
            cs.reshape(n, SSD_GROUPS, SSD_STATE), prev=ssd_new)
        vret = proj2[:, OFF_V:OFF_V + D_MODEL].astype(F32)
        ret_new, y_ret = _state_step(
            e_ret, ret_all, li,
            vret.reshape(n, SUBLANES, LANES), kr.reshape(n, RET_HEADS, RET_QK_DIM),
            qr.reshape(n, RET_HEADS, RET_QK_DIM), prev=ret_new)
        ssd_n, ret_n = _sample_post(y_ssd.reshape(n, D_MODEL), xs, proj2, y_ret.reshape(n, D_MODEL),
                                    lp['dfull'], lp['ssd_nw'])
        h = _merge(y_s5, ssd_n.reshape(1, n, D_MODEL), ret_n.reshape(1, n, D_MODEL), proj, h,
                   *lp['merge_w'], li, n)
        last = li == len(layers) - 1
        h = _mlp(h.reshape(n, D_MODEL), lp['norm2_w'], *lp['mlp_w'], final_w, li,
                 n, MLP_TF, last).reshape(1, n, D_MODEL)
        outs['re'].append(hr.reshape(n, S5_GROUPS, S5_STATE))
        outs['im'].append(hi.reshape(n, S5_GROUPS, S5_STATE))
        outs['conv'].append(conv_new.reshape(n, SSD_CONV - 1, SSD_CONV_DIM))
    return (h.reshape(n, 1, D_MODEL), jnp.stack(outs['re']), jnp.stack(outs['im']),
            ssd_new.reshape(depth, n, SSD_HEADS, SSD_HEAD_DIM, SSD_STATE), jnp.stack(outs['conv']),
            ret_new.reshape(depth, n, RET_HEADS, RET_V_DIM, RET_QK_DIM))


def kernel(x_prompt, x_sample, state_s5_re, state_s5_im, state_ssd, state_conv, state_ret, norm1_w, w_in, s5_a_re, s5_a_im, s5_log_dt, s5_b_re, s5_b_im, s5_c_re, s5_c_im, s5_d, w_s5_out, ssd_conv_w, ssd_conv_b, ssd_dt_bias, ssd_a_log, ssd_d, ssd_norm_w, w_ssd_out, w_ret_out, w_out, norm2_w, w_mlp_up, w_mlp_down, final_norm_w):
    params = dict(norm1_w=norm1_w, w_in=w_in, s5_a_re=s5_a_re, s5_a_im=s5_a_im, s5_log_dt=s5_log_dt,
                  s5_b_re=s5_b_re, s5_b_im=s5_b_im, s5_c_re=s5_c_re, s5_c_im=s5_c_im, s5_d=s5_d,
                  w_s5_out=w_s5_out, ssd_conv_w=ssd_conv_w, ssd_conv_b=ssd_conv_b, ssd_dt_bias=ssd_dt_bias,
                  ssd_a_log=ssd_a_log, ssd_d=ssd_d, ssd_norm_w=ssd_norm_w, w_ssd_out=w_ssd_out,
                  w_ret_out=w_ret_out, w_out=w_out, norm2_w=norm2_w, w_mlp_up=w_mlp_up, w_mlp_down=w_mlp_down)
    depth = w_in.shape[0]
    big = ('w_in', 'w_s5_out', 'w_ssd_out', 'w_ret_out', 'w_out', 'w_mlp_up', 'w_mlp_down')
    layers = [_prep_layer({name: arr[l] for name, arr in params.items() if name not in big}) for l in range(depth)]
    merge_w = tuple(params[name].astype(BF16) for name in ('w_s5_out', 'w_ssd_out', 'w_ret_out', 'w_out'))
    mlp_w = tuple(params[name].astype(BF16) for name in ('w_mlp_up', 'w_mlp_down'))
    wt, w_dt = _proj_weight_slabs(w_in)
    for l, lp in enumerate(layers):
        lp['proj_w'] = (wt, w_dt, l)
        lp['merge_w'] = merge_w
        lp['mlp_w'] = mlp_w
    final_w = final_norm_w.reshape(1, D_MODEL).astype(F32)
    y_p, re_p, im_p, ssd_p, conv_p, ret_p = _prompt_trunk(x_prompt.astype(F32), layers, final_w)
    y_s, re_s, im_s, ssd_s, conv_s, ret_s = _sample_trunk(
        x_sample.astype(F32), (state_s5_re, state_s5_im, state_ssd, state_conv, state_ret), layers, final_w)
    return (y_p, y_s, re_p, im_p, ssd_p, conv_p, ret_p, re_s, im_s, ssd_s, conv_s, ret_s)
```

```python
import functools
import math

import jax
import jax.numpy as jnp
from jax import lax
from jax.experimental import pallas as pl
from jax.experimental.pallas import tpu as pltpu

F32 = jnp.float32
BF16 = jnp.bfloat16

D_MODEL = 1024
S5_GROUP = 16
S5_STATE = 64
S5_GROUPS = D_MODEL // S5_GROUP
S5_NSTATE = S5_GROUPS * S5_STATE
S5_SUPER = 8
S5_TIME_SPLIT = 4
SSD_HEADS = 16
SSD_HEAD_DIM = 64
SSD_GROUPS = 4
SSD_STATE = 128
SSD_CONV = 4
SSD_CONV_DIM = D_MODEL + 2 * SSD_GROUPS * SSD_STATE
RET_HEADS = 4
RET_QK_DIM = 128
RET_V_DIM = 256
ROPE_BASE = 10000.0
D_FF = 4 * D_MODEL
CHUNK = 128
EPS = 1e-6
PAST_LEN = 16384
LANES = 128
SUBLANES = 8
VMEM_LIMIT = 52 * 1024 * 1024

OFF_U, OFF_Z, OFF_XS, OFF_BC, OFF_Q, OFF_K, OFF_V, OFF_G, OFF_GATES = (
    0, 1024, 2048, 3072, 4096, 4608, 5120, 6144, 7168)
N_MAIN = 10240
SLAB_A_COLS = OFF_Q
SLAB_B_START = 2 * D_MODEL + SSD_CONV_DIM + SSD_HEADS

PROJ_TM, PROJ_TN = 1024, 2048
MERGE_TM = 512
MLP_TM, MLP_TF = 1024, 2048
S5_STEPS = 64
MIX_NSEQ = 2
STATE_TOKENS = 8


def _cparams(sem):
    return pltpu.CompilerParams(dimension_semantics=sem, vmem_limit_bytes=VMEM_LIMIT)


def _sigmoid(x):
    return 1.0 / (1.0 + jnp.exp(-x))


def _silu(x):
    return x * _sigmoid(x)


def _softplus(x):
    return jnp.maximum(x, 0.0) + jnp.log1p(jnp.exp(-jnp.abs(x)))


def _gelu_tanh(x):
    c = math.sqrt(2.0 / math.pi)
    half = 0.5 * x
    return half + half * jnp.tanh(x * (c + (c * 0.044715) * (x * x)))


def _rmsnorm(x, w):
    return x * lax.rsqrt(jnp.mean(x * x, axis=-1, keepdims=True) + EPS) * w


def _cumsum_rows(x):
    n = x.shape[0]
    row = lax.broadcasted_iota(jnp.int32, x.shape, 0)
    k = 1
    while k < n:
        x = x + jnp.where(row >= k, pltpu.roll(x, k, 0), 0.0)
        k *= 2
    return x


def _group_rms(y, width):
    outs = []
    for s in range(0, y.shape[-1], width):
        yg = y[:, s:s + width]
        outs.append(yg * lax.rsqrt(jnp.mean(yg * yg, axis=-1, keepdims=True) + EPS))
    return jnp.concatenate(outs, axis=-1)


def _dot(a, b):
    return jnp.dot(a, b, preferred_element_type=F32)


def _split_dot(x, e):
    hi = x.astype(BF16)
    lo = (x - hi.astype(F32)).astype(BF16)
    return _dot(hi, e) + _dot(lo, e)


def _proj_kernel(x_ref, nw_ref, wa_ref, wb_ref, wdt_ref, o_ref, odt_ref, xn_ref, *, na):
    j = pl.program_id(2)

    @pl.when(j == 0)
    def _():
        xn = _rmsnorm(x_ref[...], nw_ref[...]).astype(BF16)
        xn_ref[...] = xn
        odt_ref[...] = _dot_nt(xn, wdt_ref[...])

    @pl.when(j < na)
    def _():
        o_ref[...] = _dot_nt(xn_ref[...], wa_ref[...]).astype(o_ref.dtype)

    @pl.when(j >= na)
    def _():
        o_ref[...] = _dot_nt(xn_ref[...], wb_ref[0]).astype(o_ref.dtype)


DT_START = SLAB_B_START - SSD_HEADS
assert SLAB_B_START % (2 * SUBLANES) == 0


def _proj_weight_slabs(w_in):
    wt = jnp.swapaxes(w_in, 1, 2).astype(BF16)
    w_dt = jnp.pad(wt[:, DT_START:SLAB_B_START], ((0, 0), (0, LANES - SSD_HEADS), (0, 0)))
    return wt, w_dt


def _dot_nt(a, b):
    return lax.dot_general(a, b, (((1,), (1,)), ((), ())), preferred_element_type=F32)


def _proj_main(x3, nw, wt, wdt, layer, tm):
    nb, L, _ = x3.shape
    tn = PROJ_TN
    na = SLAB_A_COLS // tn
    return pl.pallas_call(
        functools.partial(_proj_kernel, na=na),
        grid=(nb, L // tm, N_MAIN // tn),
        in_specs=[pl.BlockSpec((None, tm, D_MODEL), lambda b, i, j: (b, i, 0)),
                  pl.BlockSpec((1, D_MODEL), lambda b, i, j: (0, 0)),
                  pl.BlockSpec((None, tn, D_MODEL), lambda b, i, j: (layer, jnp.minimum(j, na - 1), 0)),
                  pl.BlockSpec((pl.Element(1), pl.Element(tn), pl.Element(D_MODEL)),
                               lambda b, i, j: (layer, pl.multiple_of(
                                   SLAB_B_START + jnp.maximum(j - na, 0) * tn, 2 * SUBLANES), 0)),
                  pl.BlockSpec((None, LANES, D_MODEL), lambda b, i, j: (layer, 0, 0))],
        out_specs=[pl.BlockSpec((None, tm, tn), lambda b, i, j: (b, i, j)),
                   pl.BlockSpec((None, tm, LANES), lambda b, i, j: (b, i, 0))],
        out_shape=[jax.ShapeDtypeStruct((nb, L, N_MAIN), BF16), jax.ShapeDtypeStruct((nb, L, LANES), F32)],
        scratch_shapes=[pltpu.VMEM((tm, D_MODEL), BF16)],
        compiler_params=_cparams(("parallel", "parallel", "arbitrary")),
        name="proj_main",
    )(x3, nw, wt, wt, wdt)


def _s5_disc_kernel(are_ref, aim_ref, ldt_ref, bre_ref, bim_ref, abr_ref, abi_ref, bbr_ref, bbi_ref):
    ar, ai = are_ref[...], aim_ref[...]
    dt = jnp.exp(ldt_ref[...])
    mag = jnp.exp(ar * dt)
    abr = mag * jnp.cos(ai * dt)
    abi = mag * jnp.sin(ai * dt)
    abr_ref[...] = abr
    abi_ref[...] = abi
    nr, ni = abr - 1.0, abi
    den = 1.0 / (ar * ar + ai * ai)
    cr = (nr * ar + ni * ai) * den
    ci = (ni * ar - nr * ai) * den
    br, bi = bre_ref[...], bim_ref[...]
    bbr_ref[...] = cr * br - ci * bi
    bbi_ref[...] = cr * bi + ci * br


def _s5_discretise(a_re, a_im, log_dt, b_re, b_im):
    g, p = a_re.shape
    gs = b_re.shape[-1]
    full3 = lambda s: pl.BlockSpec(s, lambda: (0, 0, 0))
    return pl.pallas_call(
        _s5_disc_kernel,
        in_specs=[full3((g, 1, p)), full3((g, 1, p)), full3((g, 1, 1)), full3((g, gs, p)), full3((g, gs, p))],
        out_specs=[full3((g, 1, p)), full3((g, 1, p)), full3((g, gs, p)), full3((g, gs, p))],
        out_shape=[jax.ShapeDtypeStruct((g, 1, p), F32)] * 2 + [jax.ShapeDtypeStruct((g, gs, p), F32)] * 2,
        name="s5_discretise",
    )(a_re.reshape(g, 1, p), a_im.reshape(g, 1, p), log_dt.reshape(g, 1, 1),
      jnp.swapaxes(b_re, 1, 2), jnp.swapaxes(b_im, 1, 2))


def _s5_kernel(u_ref, wbr_ref, wbi_ref, ar_ref, ai_ref, wcr_ref, wci_ref, d_ref, h0r_ref, h0i_ref,
               y_ref, hr_ref, hi_ref, bur, bui, *, nbatch, steps, lane_w):
    @pl.when(pl.program_id(0) == 0)
    def _():
        hr_ref[...] = h0r_ref[...]
        hi_ref[...] = h0i_ref[...]

    cw = S5_SUPER * S5_GROUP
    sw = S5_SUPER * S5_STATE
    batch_major = steps > 1
    u = u_ref[...].astype(F32)
    if batch_major:
        u = jnp.swapaxes(u, 0, 1).reshape(steps * nbatch, D_MODEL)
    ub = u.astype(BF16)
    nsub = S5_TIME_SPLIT if steps % (2 * S5_TIME_SPLIT) == 0 else 1
    sub_steps = steps // nsub
    nsup = S5_GROUPS // S5_SUPER
    for sb in range(nsub):
        t0 = sb * sub_steps
        rs = slice(t0 * nbatch, (t0 + sub_steps) * nbatch)
        for j in range(nsup):
            uj = ub[rs, j * cw:(j + 1) * cw]
            bur[rs, j * sw:(j + 1) * sw] = _dot(uj, wbr_ref[j])
            bui[rs, j * sw:(j + 1) * sw] = _dot(uj, wbi_ref[j])

        for j in range(nsup):
            for lc in range(sw // lane_w):
                sl = slice(j * sw + lc * lane_w, j * sw + (lc + 1) * lane_w)
                are = jnp.broadcast_to(ar_ref[:, sl], (nbatch, lane_w))
                aim = jnp.broadcast_to(ai_ref[:, sl], (nbatch, lane_w))
                hr, hi = hr_ref[:, sl], hi_ref[:, sl]
                for t in range(t0, t0 + sub_steps):
                    rows = slice(t * nbatch, (t + 1) * nbatch)
                    hr, hi = are * hr - aim * hi + bur[rows, sl], are * hi + aim * hr + bui[rows, sl]
                    bur[rows, sl] = hr
                    bui[rows, sl] = hi
                hr_ref[:, sl] = hr
                hi_ref[:, sl] = hi

        for j in range(nsup):
            ch = slice(j * cw, (j + 1) * cw)
            yj = (_dot(bur[rs, j * sw:(j + 1) * sw].astype(BF16), wcr_ref[j])
                  + _dot(bui[rs, j * sw:(j + 1) * sw].astype(BF16), wci_ref[j]))
            yj = _gelu_tanh(yj + d_ref[:, ch] * u[rs, ch])
            if batch_major:
                y_ref[:, t0:t0 + sub_steps, ch] = jnp.swapaxes(
                    yj.reshape(sub_steps, nbatch, cw), 0, 1).astype(y_ref.dtype)
            else:
                y_ref[:, ch] = yj.astype(y_ref.dtype)


def _s5_scan(proj, wbr, wbi, abr, abi, wcr, wci, d, h0r, h0i, steps):
    nb, L, _ = proj.shape
    nsup = S5_GROUPS // S5_SUPER
    cw, sw = S5_SUPER * S5_GROUP, S5_SUPER * S5_STATE
    c2 = lambda s: pl.BlockSpec(s, lambda c: (0, 0))
    c3 = lambda s: pl.BlockSpec(s, lambda c: (0, 0, 0))
    if steps > 1:
        nbatch, grid = nb, L // steps
        u_spec = pl.BlockSpec((nb, steps, D_MODEL), lambda c: (0, c, OFF_U // D_MODEL))
        y_spec = pl.BlockSpec((nb, steps, D_MODEL), lambda c: (0, c, 0))
    else:
        nbatch, grid = L, 1
        u_spec = pl.BlockSpec((None, L, D_MODEL), lambda c: (0, 0, OFF_U // D_MODEL))
        y_spec = pl.BlockSpec((None, L, D_MODEL), lambda c: (0, 0, 0))
    blk = nbatch * steps
    lane_w = max(LANES, S5_NSTATE // nbatch)
    kern = functools.partial(_s5_kernel, nbatch=nbatch, steps=steps, lane_w=lane_w)
    return pl.pallas_call(
        kern,
        grid=(grid,),
        in_specs=[u_spec,
                  c3((nsup, cw, sw)), c3((nsup, cw, sw)),
                  c2((1, S5_NSTATE)), c2((1, S5_NSTATE)),
                  c3((nsup, sw, cw)), c3((nsup, sw, cw)),
                  c2((1, D_MODEL)),
                  c2((nbatch, S5_NSTATE)), c2((nbatch, S5_NSTATE))],
        out_specs=[y_spec, c2((nbatch, S5_NSTATE)), c2((nbatch, S5_NSTATE))],
        out_shape=[jax.ShapeDtypeStruct((nb, L, D_MODEL), BF16),
                   jax.ShapeDtypeStruct((nbatch, S5_NSTATE), F32),
                   jax.ShapeDtypeStruct((nbatch, S5_NSTATE), F32)],
        scratch_shapes=[pltpu.VMEM((blk, S5_NSTATE), F32), pltpu.VMEM((blk, S5_NSTATE), F32)],
        compiler_params=_cparams(("arbitrary",)),
        name="s5_scan",
    )(proj, wbr, wbi, abr, abi, wcr, wci, d, h0r, h0i)


def _s5_weights(abr, abi, bbr, bbi, c_re, c_im):
    nsup = S5_GROUPS // S5_SUPER
    eye = jnp.eye(S5_SUPER, dtype=F32)

    def bexp(b):
        b = b.reshape(nsup, S5_SUPER, S5_GROUP, S5_STATE)
        return (b[:, :, :, None, :] * eye[None, :, None, :, None]).reshape(
            nsup, S5_SUPER * S5_GROUP, S5_SUPER * S5_STATE).astype(BF16)

    def cexp(c):
        c = jnp.swapaxes(c, 1, 2).reshape(nsup, S5_SUPER, S5_STATE, S5_GROUP)
        return (c[:, :, :, None, :] * eye[None, :, None, :, None]).reshape(
            nsup, S5_SUPER * S5_STATE, S5_SUPER * S5_GROUP).astype(BF16)

    return (bexp(bbr), bexp(bbi), abr.reshape(1, S5_NSTATE), abi.reshape(1, S5_NSTATE),
            cexp(c_re), cexp(-c_im))


def _ssd_gate_norm(y, xs, z, dfull, nw):
    y = (y + dfull * xs) * _silu(z)
    return _group_rms(y, D_MODEL // SSD_GROUPS) * nw


CONV_PAD = 16
CONV_ROWS = 2 * CHUNK


def _conv_shift_mats():
    i = jnp.arange(CHUNK)[:, None]
    j = jnp.arange(CONV_ROWS)[None, :]
    return jnp.stack([(j == CONV_PAD + i - s) for s in range(1, SSD_CONV)]).astype(BF16)


def _ssd_chunk_seq(xs_ref, bc_ref, z_ref, dt_ref, cw_ref, cb_ref, b16_ref, al16_ref,
                   dfull_ref, nw_ref, eh_ref, sh_ref, o_ref, xpad, st):
    q = CHUNK
    pad = CONV_PAD
    xpad[0:pad, :] = xpad[q:q + pad, :]

    xpad[pad:pad + q, :D_MODEL] = xs_ref[...]
    xpad[pad:pad + q, D_MODEL:] = bc_ref[...]
    staged = xpad[...]
    acc = cb_ref[...] + xpad[pad:pad + q, :].astype(F32) * cw_ref[SSD_CONV - 1:SSD_CONV, :]
    for s in range(1, SSD_CONV):
        acc = acc + _dot(sh_ref[s - 1], staged) * cw_ref[SSD_CONV - 1 - s:SSD_CONV - s, :]
    xc = _silu(acc)
    xs = xc[:, :D_MODEL]
    nbc = SSD_GROUPS * SSD_STATE
    bs = xc[:, D_MODEL:D_MODEL + nbc]
    cs = xc[:, D_MODEL + nbc:]

    delta16 = _softplus(dt_ref[...] + b16_ref[...])
    acum16 = _cumsum_rows(delta16 * (-jnp.exp(al16_ref[...])))
    acum_log2 = acum16 * math.log2(math.e)
    acum_log2_t = acum_log2.T
    delta_t = delta16.T
    last16 = acum16[q - 1:q, :]
    ea_f = _split_dot(jnp.exp(acum16), eh_ref[...])
    dw_f = _split_dot(delta16 * jnp.exp(last16 - acum16), eh_ref[...])
    xw = (xs * dw_f).astype(BF16)

    ri = lax.broadcasted_iota(jnp.int32, (q, q), 0)
    ci = lax.broadcasted_iota(jnp.int32, (q, q), 1)
    causal = ri >= ci
    hw = D_MODEL // SSD_GROUPS
    hpg = SSD_HEADS // SSD_GROUPS
    xsb = xs.astype(BF16)
    ys = []
    for g in range(SSD_GROUPS):
        bg = bs[:, g * SSD_STATE:(g + 1) * SSD_STATE]
        cg = cs[:, g * SSD_STATE:(g + 1) * SSD_STATE].astype(BF16)
        bgt = bg.T.astype(BF16)
        scores = _dot(cg, bgt)
        sg = st[:, g * hw:(g + 1) * hw]
        y_inter = _dot(cg, sg.astype(BF16)) * ea_f[:, g * hw:(g + 1) * hw]
        for r in range(hpg):
            hd = g * hpg + r
            seg = acum_log2[:, hd:hd + 1] - acum_log2_t[hd:hd + 1, :]
            lm = jnp.exp2(jnp.where(causal, seg, -jnp.inf)) * delta_t[hd:hd + 1, :]
            p = (scores * lm).astype(BF16)
            ys.append(_dot(p, xsb[:, hd * SSD_HEAD_DIM:(hd + 1) * SSD_HEAD_DIM])
                      + y_inter[:, r * SSD_HEAD_DIM:(r + 1) * SSD_HEAD_DIM])
        st[:, g * hw:(g + 1) * hw] = (ea_f[q - 1:q, g * hw:(g + 1) * hw] * sg
                                      + _dot(bgt, xw[:, g * hw:(g + 1) * hw]))
    y = jnp.concatenate(ys, axis=-1)
    o_ref[...] = _ssd_gate_norm(y, xs, z_ref[...].astype(F32), dfull_ref[...], nw_ref[...]).astype(o_ref.dtype)


def _rotary(x, cos2, sin2):
    outs = []
    for h in range(RET_HEADS):
        xh = x[:, h * RET_QK_DIM:(h + 1) * RET_QK_DIM]
        outs.append(xh * cos2 + pltpu.roll(xh, RET_QK_DIM // 2, 1) * sin2)
    return outs


def _ret_gate_norm(y, g):
    return _group_rms(y, RET_V_DIM) * _silu(g)


def _ret_chunk_seq(q_ref, k_ref, v_ref, g_ref, cos2, sin2, lm_ref, gp_ref, we_ref, gq_ref, o_ref, st):
    qr = _rotary(q_ref[...].astype(F32), cos2, sin2)
    kr = _rotary(k_ref[...].astype(F32), cos2, sin2)
    vb = v_ref[...]
    vw = (vb.astype(F32) * we_ref[...]).astype(BF16)
    ys = []
    for h in range(RET_HEADS):
        sl = slice(h * RET_V_DIM, (h + 1) * RET_V_DIM)
        qh = qr[h].astype(BF16)
        kht = (kr[h] * (RET_QK_DIM ** -0.5)).T.astype(BF16)
        p = (_dot(qh, kht) * lm_ref[h]).astype(BF16)
        sh = st[:, sl]
        ys.append(_dot(p, vb[:, sl]) + _dot(qh, sh.astype(BF16)) * gp_ref[:, sl])
        st[:, sl] = gq_ref[:, sl] * sh + _dot(kht, vw[:, sl])
    y = jnp.concatenate(ys, axis=-1)
    o_ref[...] = _ret_gate_norm(y, g_ref[...].astype(F32)).astype(o_ref.dtype)


def _mix_chunk_kernel(xs_ref, bc_ref, z_ref, dt_ref, cw_ref, cb_ref, b16_ref, al16_ref, dfull_ref, nw_ref, eh_ref,
                      sh_ref, q_ref, k_ref, v_ref, g_ref, cos_ref, sin_ref, lm_ref, gp_ref, we_ref, gq_ref,
                      so_ref, ss_ref, ro_ref, rs_ref, xpad, st_ssd, st_ret):
    c = pl.program_id(1)
    nc = pl.num_programs(1)
    nseq = xs_ref.shape[0]

    @pl.when(c == 0)
    def _():
        xpad[...] = jnp.zeros(xpad.shape, BF16)
        st_ssd[...] = jnp.zeros(st_ssd.shape, F32)
        st_ret[...] = jnp.zeros(st_ret.shape, F32)

    cos2, sin2 = cos_ref[...], sin_ref[...]
    for bi in range(nseq):
        _ssd_chunk_seq(xs_ref.at[bi], bc_ref.at[bi], z_ref.at[bi], dt_ref.at[bi], cw_ref, cb_ref, b16_ref,
                       al16_ref, dfull_ref, nw_ref, eh_ref, sh_ref, so_ref.at[bi], xpad.at[bi], st_ssd.at[bi])
        _ret_chunk_seq(q_ref.at[bi], k_ref.at[bi], v_ref.at[bi], g_ref.at[bi], cos2, sin2, lm_ref, gp_ref, we_ref,
                       gq_ref, ro_ref.at[bi], st_ret.at[bi])

    @pl.when(c == nc - 1)
    def _():
        for bi in range(nseq):
            for k in range(D_MODEL // LANES):
                ss_ref[bi, k * LANES:(k + 1) * LANES, :] = st_ssd[bi, :, k * LANES:(k + 1) * LANES].T
                rs_ref[bi, k * LANES:(k + 1) * LANES, :] = st_ret[bi, :, k * LANES:(k + 1) * LANES].T


def _ret_consts(q):
    hidx = jnp.arange(RET_HEADS, dtype=F32)
    log_gamma = jnp.log1p(-jnp.exp2(-5.0 - hidx))
    i = jnp.arange(q, dtype=F32)
    seg = (i[:, None] - i[None, :])[None] * log_gamma[:, None, None]
    lm = jnp.exp(jnp.where((i[:, None] >= i[None, :])[None], seg, -jnp.inf))
    rep = lambda t: jnp.repeat(t, RET_V_DIM, axis=-1)
    gp = rep(jnp.exp((i[:, None] + 1.0) * log_gamma[None, :]))
    we = rep(jnp.exp((q - 1.0 - i[:, None]) * log_gamma[None, :]))
    gq = rep(jnp.exp(q * log_gamma)[None, :])
    return log_gamma, lm, gp, we, gq


def _rope_tables(pos):
    half = RET_QK_DIM // 2
    inv = ROPE_BASE ** (-jnp.arange(half, dtype=F32) / half)
    ang = pos.astype(F32)[:, None] * inv[None]
    cos, sin = jnp.cos(ang), jnp.sin(ang)
    return jnp.concatenate([cos, cos], axis=-1), jnp.concatenate([-sin, sin], axis=-1)


def _mix_prompt(proj, dt, cw, cb, b16, al16, dfull, nw, eh, cos2, sin2, lm, gp, we, gq, nseq=MIX_NSEQ):
    nb, L, _ = proj.shape
    q = CHUNK
    qkw = RET_HEADS * RET_QK_DIM
    c2 = lambda s: pl.BlockSpec(s, lambda b, c: (0, 0))
    rows = lambda w, off: pl.BlockSpec((nseq, q, w), lambda b, c: (b, c, off // w))
    state = pl.BlockSpec((nseq, D_MODEL, LANES), lambda b, c: (b, 0, 0))
    return pl.pallas_call(
        _mix_chunk_kernel,
        grid=(nb // nseq, L // q),
        in_specs=[rows(D_MODEL, OFF_XS), rows(D_MODEL, OFF_BC), rows(D_MODEL, OFF_Z), rows(LANES, 0),
                  c2((SSD_CONV, SSD_CONV_DIM)), c2((1, SSD_CONV_DIM)),
                  c2((1, LANES)), c2((1, LANES)),
                  c2((1, D_MODEL)), c2((1, D_MODEL)),
                  c2((LANES, D_MODEL)),
                  pl.BlockSpec((SSD_CONV - 1, q, CONV_ROWS), lambda b, c: (0, 0, 0)),
                  rows(qkw, OFF_Q), rows(qkw, OFF_K), rows(D_MODEL, OFF_V), rows(D_MODEL, OFF_G),
                  pl.BlockSpec((q, RET_QK_DIM), lambda b, c: (c, 0)),
                  pl.BlockSpec((q, RET_QK_DIM), lambda b, c: (c, 0)),
                  pl.BlockSpec((RET_HEADS, q, q), lambda b, c: (0, 0, 0)),
                  c2((q, D_MODEL)), c2((q, D_MODEL)), c2((1, D_MODEL))],
        out_specs=[rows(D_MODEL, 0), state, rows(D_MODEL, 0), state],
        out_shape=[jax.ShapeDtypeStruct((nb, L, D_MODEL), BF16),
                   jax.ShapeDtypeStruct((nb, D_MODEL, SSD_STATE), F32),
                   jax.ShapeDtypeStruct((nb, L, D_MODEL), BF16),
                   jax.ShapeDtypeStruct((nb, D_MODEL, RET_QK_DIM), F32)],
        scratch_shapes=[pltpu.VMEM((nseq, CONV_ROWS, SSD_CONV_DIM), BF16),
                        pltpu.VMEM((nseq, SSD_STATE, D_MODEL), F32),
                        pltpu.VMEM((nseq, RET_QK_DIM, D_MODEL), F32)],
        compiler_params=_cparams(("parallel", "arbitrary")),
        name="mix_chunk",
    )(proj, proj, proj, dt, cw, cb, b16, al16, dfull, nw, eh, _conv_shift_mats(),
      proj, proj, proj, proj, cos2, sin2, lm, gp, we, gq)


def _sample_pre_kernel(xsin_ref, bcin_ref, dt_ref, q_ref, k_ref, conv_ref, cw_ref, cb_ref, b16_ref, al16_ref,
                       eh_ref, cos_ref, sin_ref,
                       vv_ref, e16_ref, bs_ref, cs_ref, xs_ref, convn_ref, qr_ref, kr_ref):
    cd = SSD_CONV_DIM
    x = jnp.concatenate([xsin_ref[...], bcin_ref[...]], axis=-1).astype(F32)
    acc = cb_ref[...] + x * cw_ref[SSD_CONV - 1:SSD_CONV, :]
    for tap in range(SSD_CONV - 1):
        acc = acc + conv_ref[:, tap * cd:(tap + 1) * cd] * cw_ref[tap:tap + 1, :]
    convn_ref[:, 0:(SSD_CONV - 2) * cd] = conv_ref[:, cd:(SSD_CONV - 1) * cd]
    convn_ref[:, (SSD_CONV - 2) * cd:] = x
    xc = _silu(acc)
    xs = xc[:, :D_MODEL]
    nbc = SSD_GROUPS * SSD_STATE
    xs_ref[...] = xs
    bs_ref[...] = xc[:, D_MODEL:D_MODEL + nbc]
    cs_ref[...] = xc[:, D_MODEL + nbc:]
    delta16 = _softplus(dt_ref[...] + b16_ref[...])
    e16_ref[...] = jnp.exp(delta16 * (-jnp.exp(al16_ref[...])))
    vv_ref[...] = xs * _split_dot(delta16, eh_ref[...])
    cos2, sin2 = cos_ref[...], sin_ref[...]
    qr_ref[...] = jnp.concatenate(_rotary(q_ref[...].astype(F32), cos2, sin2), axis=-1)
    kr_ref[...] = jnp.concatenate(_rotary(k_ref[...].astype(F32), cos2, sin2), axis=-1) * (RET_QK_DIM ** -0.5)


def _sample_pre(proj2, dt2, conv2, cw, cb, b16, al16, eh, cos2, sin2):
    n = proj2.shape[0]
    qkw = RET_HEADS * RET_QK_DIM
    nbc = SSD_GROUPS * SSD_STATE
    full = lambda s: pl.BlockSpec(s, lambda i: (0, 0))
    col = lambda w, off: pl.BlockSpec((n, w), lambda i: (0, off // w))
    shp = lambda w: jax.ShapeDtypeStruct((n, w), F32)
    return pl.pallas_call(
        _sample_pre_kernel,
        grid=(1,),
        in_specs=[col(D_MODEL, OFF_XS), col(D_MODEL, OFF_BC), full((n, LANES)), col(qkw, OFF_Q), col(qkw, OFF_K),
                  full((n, (SSD_CONV - 1) * SSD_CONV_DIM)),
                  full((SSD_CONV, SSD_CONV_DIM)), full((1, SSD_CONV_DIM)),
                  full((1, LANES)), full((1, LANES)),
                  full((LANES, D_MODEL)), full((1, RET_QK_DIM)), full((1, RET_QK_DIM))],
        out_specs=[full((n, D_MODEL)), full((n, LANES)), full((n, nbc)), full((n, nbc)), full((n, D_MODEL)),
                   full((n, (SSD_CONV - 1) * SSD_CONV_DIM)), full((n, qkw)), full((n, qkw))],
        out_shape=[shp(D_MODEL), shp(LANES), shp(nbc), shp(nbc), shp(D_MODEL),
                   shp((SSD_CONV - 1) * SSD_CONV_DIM), shp(qkw), shp(qkw)],
        compiler_params=_cparams(("arbitrary",)),
        name="sample_pre",
    )(proj2, proj2, dt2, proj2, proj2, conv2, cw, cb, b16, al16, eh, cos2, sin2)


STATE_ROWS = D_MODEL
STATE_BLK = 64


def _state_step_kernel(e_ref, s_ref, v_ref, k_ref, q_ref, *rest, bb, layer, fill):
    sn_all, y_ref = rest[-2:]
    if fill:
        for l in range(sn_all.shape[0]):
            if l != layer:
                sn_all[l] = jnp.zeros(sn_all.shape[1:], F32)
        sn_ref = sn_all.at[layer]
    else:
        sn_ref = sn_all
    i = pl.program_id(0)
    nblk = STATE_ROWS // STATE_BLK
    gw = STATE_ROWS // 4
    ones = jnp.ones((SUBLANES, LANES), BF16)
    zpad = jnp.zeros((LANES - SUBLANES, LANES), F32)
    nt = (((1,), (1,)), ((), ()))
    for t in range(bb):
        b = i * bb + t
        vt = jnp.concatenate([v_ref[t], zpad], axis=0).T
        kk = k_ref[t]
        qq = q_ref[t]
        for cch in range(STATE_ROWS // LANES):
            prods = []
            for half in range(LANES // STATE_BLK):
                blk = cch * (LANES // STATE_BLK) + half
                r0 = blk * STATE_BLK
                g = r0 // gw
                vcol = vt[half * STATE_BLK:(half + 1) * STATE_BLK, cch:cch + 1]
                sn = e_ref[b * nblk + blk] * s_ref[t, r0:r0 + STATE_BLK, :] + vcol * kk[g:g + 1, :]
                sn_ref[t, r0:r0 + STATE_BLK, :] = sn
                prods.append(sn * qq[g:g + 1, :])
            p = jnp.concatenate(prods, axis=0).astype(BF16)
            ysum = lax.dot_general(ones, p, nt, preferred_element_type=F32)
            y_ref[t, cch:cch + 1, :] = ysum[0:1, :]


def _state_step(e_flat, s_all, layer, v3, k3, q3, prev=None, bb=STATE_TOKENS):
    depth, n = s_all.shape[:2]
    blk3 = lambda a, b_: pl.BlockSpec((bb, a, b_), lambda i: (i, 0, 0))
    sblk = pl.BlockSpec((None, bb, STATE_ROWS, LANES), lambda i: (layer, i, 0, 0))
    in_specs = [pl.BlockSpec(memory_space=pltpu.SMEM), sblk, blk3(SUBLANES, LANES), blk3(4, LANES), blk3(4, LANES)]
    args = [e_flat, s_all, v3, k3, q3]
    aliases = {}
    if prev is not None:
        in_specs.append(pl.BlockSpec(memory_space=pl.ANY))
        args.append(prev)
        aliases = {len(args) - 1: 0}
        oblk = sblk
    else:
        oblk = pl.BlockSpec((depth, bb, STATE_ROWS, LANES), lambda i: (0, i, 0, 0))
    return pl.pallas_call(
        functools.partial(_state_step_kernel, bb=bb, layer=layer, fill=prev is None),
        grid=(n // bb,),
        in_specs=in_specs,
        out_specs=[oblk, blk3(SUBLANES, LANES)],
        out_shape=[jax.ShapeDtypeStruct((depth, n, STATE_ROWS, LANES), F32),
                   jax.ShapeDtypeStruct((n, SUBLANES, LANES), F32)],
        input_output_aliases=aliases,
        compiler_params=_cparams(("parallel",)),
        name="state_step",
    )(*args)


def _sample_post_kernel(ys_ref, xs_ref, z_ref, yr_ref, g_ref, dfull_ref, nw_ref, os_ref, or_ref):
    os_ref[...] = _ssd_gate_norm(ys_ref[...], xs_ref[...], z_ref[...].astype(F32), dfull_ref[...],
                                 nw_ref[...]).astype(os_ref.dtype)
    or_ref[...] = _ret_gate_norm(yr_ref[...], g_ref[...].astype(F32)).astype(or_ref.dtype)


def _sample_post(y_ssd, xs, proj2, y_ret, dfull, nw):
    n = y_ssd.shape[0]
    full = lambda s: pl.BlockSpec(s, lambda i: (0, 0))
    col = lambda off: pl.BlockSpec((n, D_MODEL), lambda i: (0, off // D_MODEL))
    return pl.pallas_call(
        _sample_post_kernel,
        grid=(1,),
        in_specs=[full((n, D_MODEL)), full((n, D_MODEL)), col(OFF_Z), full((n, D_MODEL)), col(OFF_G),
                  full((1, D_MODEL)), full((1, D_MODEL))],
        out_specs=[full((n, D_MODEL)), full((n, D_MODEL))],
        out_shape=[jax.ShapeDtypeStruct((n, D_MODEL), BF16)] * 2,
        compiler_params=_cparams(("arbitrary",)),
        name="sample_post",
    )(y_ssd, xs, proj2, y_ret, proj2, dfull, nw)


def _merge_kernel(s5_ref, ssd_ref, ret_ref, ga_ref, gb_ref, gc_ref, h_ref, ws5_ref, wssd_ref, wret_ref, wout_ref,
                  o_ref):
    glu = _dot(s5_ref[...], ws5_ref[...])
    y_a = glu[:, :D_MODEL] * _sigmoid(glu[:, D_MODEL:])
    y_b = _dot(ssd_ref[...], wssd_ref[...])
    y_c = _dot(ret_ref[...], wret_ref[...])
    merged = (_sigmoid(ga_ref[...].astype(F32)) * y_a + _sigmoid(gb_ref[...].astype(F32)) * y_b
              + _sigmoid(gc_ref[...].astype(F32)) * y_c)
    o_ref[...] = h_ref[...] + _dot(merged.astype(BF16), wout_ref[...])


def _merge(s5_g, ssd_n, ret_n, proj, h3, ws5, wssd, wret, wout, layer, tm):
    nb, L, _ = h3.shape
    row3 = lambda off: pl.BlockSpec((None, tm, D_MODEL), lambda b, i: (b, i, off // D_MODEL))
    wfull = lambda s: pl.BlockSpec((None,) + s, lambda b, i: (layer, 0, 0), pipeline_mode=pl.Buffered(1))
    return pl.pallas_call(
        _merge_kernel,
        grid=(nb, L // tm),
        in_specs=[row3(0), row3(0), row3(0),
                  row3(OFF_GATES), row3(OFF_GATES + D_MODEL), row3(OFF_GATES + 2 * D_MODEL), row3(0),
                  wfull((D_MODEL, 2 * D_MODEL)), wfull((D_MODEL, D_MODEL)), wfull((D_MODEL, D_MODEL)),
                  wfull((D_MODEL, D_MODEL))],
        out_specs=row3(0),
        out_shape=jax.ShapeDtypeStruct((nb, L, D_MODEL), F32),
        compiler_params=_cparams(("parallel", "parallel")),
        name="merge",
    )(s5_g, ssd_n, ret_n, proj, proj, proj, h3, ws5, wssd, wret, wout)


def _mlp_kernel(x_ref, nw_ref, wup_ref, wdn_ref, fw_ref, o_ref, xn_ref, acc_ref, *, final_norm):
    k = pl.program_id(1)
    nk = pl.num_programs(1)

    @pl.when(k == 0)
    def _():
        x = x_ref[...]
        xn_ref[...] = _rmsnorm(x, nw_ref[...]).astype(BF16)
        acc_ref[...] = x

    hid = jnp.maximum(_dot(xn_ref[...], wup_ref[...]), 0.0)
    acc_ref[...] += _dot((hid * hid).astype(BF16), wdn_ref[...])

    @pl.when(k == nk - 1)
    def _():
        h = acc_ref[...]
        o_ref[...] = _rmsnorm(h, fw_ref[...]) if final_norm else h


def _mlp(h2, nw, wup, wdn, fw, layer, tm, tf, final_norm):
    rows = h2.shape[0]
    return pl.pallas_call(
        functools.partial(_mlp_kernel, final_norm=final_norm),
        grid=(rows // tm, D_FF // tf),
        in_specs=[pl.BlockSpec((tm, D_MODEL), lambda i, k: (i, 0)),
                  pl.BlockSpec((1, D_MODEL), lambda i, k: (0, 0)),
                  pl.BlockSpec((None, D_MODEL, tf), lambda i, k: (layer, 0, k)),
                  pl.BlockSpec((None, tf, D_MODEL), lambda i, k: (layer, k, 0)),
                  pl.BlockSpec((1, D_MODEL), lambda i, k: (0, 0))],
        out_specs=pl.BlockSpec((tm, D_MODEL), lambda i, k: (i, 0)),
        out_shape=jax.ShapeDtypeStruct((rows, D_MODEL), F32),
        scratch_shapes=[pltpu.VMEM((tm, D_MODEL), BF16), pltpu.VMEM((tm, D_MODEL), F32)],
        compiler_params=_cparams(("parallel", "arbitrary")),
        name="mlp",
    )(h2, nw, wup, wdn, fw)


def _prep_layer(p):
    out = {}
    row = lambda a: a.reshape(1, -1).astype(F32)
    padrow = lambda a: jnp.pad(a.astype(F32), (0, LANES - a.shape[0])).reshape(1, LANES)
    rep = lambda a: jnp.repeat(a.astype(F32), SSD_HEAD_DIM).reshape(1, D_MODEL)
    out['norm1_w'] = row(p['norm1_w'])
    out['norm2_w'] = row(p['norm2_w'])
    out['conv_w'] = p['ssd_conv_w'].astype(F32)
    out['conv_b'] = row(p['ssd_conv_b'])
    out['b16'] = padrow(p['ssd_dt_bias'])
    out['al16'] = padrow(p['ssd_a_log'])
    out['dfull'] = rep(p['ssd_d'])
    out['ssd_nw'] = row(p['ssd_norm_w'])
    out['s5_d'] = row(p['s5_d'])
    abr, abi, bbr, bbi = _s5_discretise(p['s5_a_re'].astype(F32), p['s5_a_im'].astype(F32),
                                        p['s5_log_dt'].astype(F32), p['s5_b_re'].astype(F32),
                                        p['s5_b_im'].astype(F32))
    out['s5'] = _s5_weights(abr, abi, bbr, bbi, p['s5_c_re'].astype(F32), p['s5_c_im'].astype(F32))
    return out


def _head_expand():
    e = (jnp.arange(LANES)[:, None] == (jnp.arange(D_MODEL)[None, :] // SSD_HEAD_DIM))
    return e.astype(BF16)


def _tile(n, pref):
    return pref if n % pref == 0 else n


def _prompt_trunk(x, layers, final_w):
    nb, L, _ = x.shape
    eh = _head_expand()
    _, lm, gp, we, gq = _ret_consts(CHUNK)
    cos2, sin2 = _rope_tables(jnp.arange(L, dtype=jnp.int32))
    zeros_state = jnp.zeros((nb, S5_NSTATE), F32)
    h = x
    outs = {k: [] for k in ('re', 'im', 'ssd', 'conv', 'ret')}
    for li, lp in enumerate(layers):
        proj, dt = _proj_main(h, lp['norm1_w'], *lp['proj_w'], _tile(L, PROJ_TM))
        wbr, wbi, abr, abi, wcr, wci = lp['s5']
        y_s5, hr, hi = _s5_scan(proj, wbr, wbi, abr, abi, wcr, wci, lp['s5_d'],
                                zeros_state, zeros_state, S5_STEPS)
        ssd_n, ssd_s, ret_n, ret_s = _mix_prompt(proj, dt, lp['conv_w'], lp['conv_b'], lp['b16'], lp['al16'],
                                                 lp['dfull'], lp['ssd_nw'], eh, cos2, sin2, lm, gp, we, gq)
        h = _merge(y_s5, ssd_n, ret_n, proj, h,
                   *lp['merge_w'], li, _tile(L, MERGE_TM))
        last = li == len(layers) - 1
        h = _mlp(h.reshape(nb * L, D_MODEL), lp['norm2_w'], *lp['mlp_w'], final_w, li,
                 _tile(nb * L, MLP_TM), MLP_TF, last).reshape(nb, L, D_MODEL)
        outs['re'].append(hr.reshape(nb, S5_GROUPS, S5_STATE))
        outs['im'].append(hi.reshape(nb, S5_GROUPS, S5_STATE))
        outs['ssd'].append(ssd_s.reshape(nb, SSD_HEADS, SSD_HEAD_DIM, SSD_STATE))
        outs['conv'].append(proj[:, L - (SSD_CONV - 1):, OFF_XS:OFF_XS + SSD_CONV_DIM].astype(F32))
        outs['ret'].append(ret_s.reshape(nb, RET_HEADS, RET_V_DIM, RET_QK_DIM))
    return (h,) + tuple(jnp.stack(outs[k]) for k in ('re', 'im', 'ssd', 'conv', 'ret'))


def _sample_trunk(x, states, layers, final_w):
    n = x.shape[0]
    st_re, st_im, st_ssd, st_conv, st_ret = states
    eh = _head_expand()
    log_gamma, _, _, _, _ = _ret_consts(1)
    cos2, sin2 = _rope_tables(PAST_LEN + jnp.arange(1, dtype=jnp.int32))
    e_ret = jnp.broadcast_to(jnp.repeat(jnp.exp(log_gamma), RET_V_DIM // STATE_BLK)[None, :],
                             (n, STATE_ROWS // STATE_BLK)).reshape(-1)
    h = x.reshape(1, n, D_MODEL)
    depth = len(layers)
    ssd_all = st_ssd.reshape(depth, n, STATE_ROWS, SSD_STATE).astype(F32)
    ret_all = st_ret.reshape(depth, n, STATE_ROWS, RET_QK_DIM).astype(F32)
    ssd_new = ret_new = None
    outs = {k: [] for k in ('re', 'im', 'conv')}
    for li, lp in enumerate(layers):
        proj, dt = _proj_main(h, lp['norm1_w'], *lp['proj_w'], n)
        wbr, wbi, abr, abi, wcr, wci = lp['s5']
        y_s5, hr, hi = _s5_scan(proj, wbr, wbi, abr, abi, wcr, wci, lp['s5_d'],
                                st_re[li].reshape(n, S5_NSTATE).astype(F32),
                                st_im[li].reshape(n, S5_NSTATE).astype(F32), 1)
        proj2, dt2 = proj.reshape(n, N_MAIN), dt.reshape(n, LANES)
        conv2 = st_conv[li].reshape(n, (SSD_CONV - 1) * SSD_CONV_DIM).astype(F32)
        vv, e16, bs, cs, xs, conv_new, qr, kr = _sample_pre(
            proj2, dt2, conv2, lp['conv_w'], lp['conv_b'], lp['b16'], lp['al16'], eh, cos2, sin2)
        ssd_new, y_ssd = _state_step(
            e16[:, :SSD_HEADS].reshape(-1), ssd_all, li,
            vv.reshape(n, SUBLANES, LANES), bs.reshape(n, SSD_GROUPS, SSD_STATE),
            cs.reshape(n, SSD_GROUPS, SSD_STATE), prev=ssd_new)
        vret = proj2[:, OFF_V:OFF_V + D_MODEL].astype(F32)
        ret_new, y_ret = _state_step(
            e_ret, ret_all, li,
            vret.reshape(n, SUBLANES, LANES), kr.reshape(n, RET_HEADS, RET_QK_DIM),
            qr.reshape(n, RET_HEADS, RET_QK_DIM), prev=ret_new)
        ssd_n, ret_n = _sample_post(y_ssd.reshape(n, D_MODEL), xs, proj2, y_ret.reshape(n, D_MODEL),
                                    lp['dfull'], lp['ssd_nw'])
        h = _merge(y_s5, ssd_n.reshape(1, n, D_MODEL), ret_n.reshape(1, n, D_MODEL), proj, h,
                   *lp['merge_w'], li, n)
        last = li == len(layers) - 1
        h = _mlp(h.reshape(n, D_MODEL), lp['norm2_w'], *lp['mlp_w'], final_w, li,
                 n, MLP_TF, last).reshape(1, n, D_MODEL)
        outs['re'].append(hr.reshape(n, S5_GROUPS, S5_STATE))
        outs['im'].append(hi.reshape(n, S5_GROUPS, S5_STATE))
        outs['conv'].append(conv_new.reshape(n, SSD_CONV - 1, SSD_CONV_DIM))
    return (h.reshape(n, 1, D_MODEL), jnp.stack(outs['re']), jnp.stack(outs['im']),
            ssd_new.reshape(depth, n, SSD_HEADS, SSD_HEAD_DIM, SSD_STATE), jnp.stack(outs['conv']),
            ret_new.reshape(depth, n, RET_HEADS, RET_V_DIM, RET_QK_DIM))


def kernel(x_prompt, x_sample, state_s5_re, state_s5_im, state_ssd, state_conv, state_ret, norm1_w, w_in, s5_a_re, s5_a_im, s5_log_dt, s5_b_re, s5_b_im, s5_c_re, s5_c_im, s5_d, w_s5_out, ssd_conv_w, ssd_conv_b, ssd_dt_bias, ssd_a_log, ssd_d, ssd_norm_w, w_ssd_out, w_ret_out, w_out, norm2_w, w_mlp_up, w_mlp_down, final_norm_w):
    params = dict(norm1_w=norm1_w, w_in=w_in, s5_a_re=s5_a_re, s5_a_im=s5_a_im, s5_log_dt=s5_log_dt,
                  s5_b_re=s5_b_re, s5_b_im=s5_b_im, s5_c_re=s5_c_re, s5_c_im=s5_c_im, s5_d=s5_d,
                  w_s5_out=w_s5_out, ssd_conv_w=ssd_conv_w, ssd_conv_b=ssd_conv_b, ssd_dt_bias=ssd_dt_bias,
                  ssd_a_log=ssd_a_log, ssd_d=ssd_d, ssd_norm_w=ssd_norm_w, w_ssd_out=w_ssd_out,
                  w_ret_out=w_ret_out, w_out=w_out, norm2_w=norm2_w, w_mlp_up=w_mlp_up, w_mlp_down=w_mlp_down)
    depth = w_in.shape[0]
    big = ('w_in', 'w_s5_out', 'w_ssd_out', 'w_ret_out', 'w_out', 'w_mlp_up', 'w_mlp_down')
    layers = [_prep_layer({name: arr[l] for name, arr in params.items() if name not in big}) for l in range(depth)]
    merge_w = tuple(params[name].astype(BF16) for name in ('w_s5_out', 'w_ssd_out', 'w_ret_out', 'w_out'))
    mlp_w = tuple(params[name].astype(BF16) for name in ('w_mlp_up', 'w_mlp_down'))
    wt, w_dt = _proj_weight_slabs(w_in)
    for l, lp in enumerate(layers):
        lp['proj_w'] = (wt, w_dt, l)
        lp['merge_w'] = merge_w
        lp['mlp_w'] = mlp_w
    final_w = final_norm_w.reshape(1, D_MODEL).astype(F32)
    y_p, re_p, im_p, ssd_p, conv_p, ret_p = _prompt_trunk(x_prompt.astype(F32), layers, final_w)
    y_s, re_s, im_s, ssd_s, conv_s, ret_s = _sample_trunk(
        x_sample.astype(F32), (state_s5_re, state_s5_im, state_ssd, state_conv, state_ret), layers, final_w)
    return (y_p, y_s, re_p, im_p, ssd_p, conv_p, ret_p, re_s, im_s, ssd_s, conv_s, ret_s)
```
